```python
import math
import jax, jax.numpy as jnp
from jax import lax
import numpy as np

D_MODEL = 1024
BATCH = 8
SEQ = 2048
DEPTH = 1

MEM_LEN = 256
MLA_HEADS = 8
QK_NOPE_DIM = 64
QK_ROPE_DIM = 32
V_HEAD_DIM = 64
Q_LORA_RANK = 384
KV_LORA_RANK = 256
MLA_OUT = MLA_HEADS * V_HEAD_DIM
CONV_GROUPS = 8
CONV_DIM = 512
CONV_WIDTH = 3
GROUP_DIM = 64
MIX_WIDTH = MLA_OUT + CONV_DIM
IN_PROJ_COLS = Q_LORA_RANK + KV_LORA_RANK + QK_ROPE_DIM + 3 * CONV_DIM
ROPE_THETA = 10000.0
Q_BLOCK = 128
X_HEADS = 4
X_HEAD_DIM = D_MODEL // X_HEADS
PEER_HEADS = 8
PEER_KEYS = 128
PEER_EXPERTS = PEER_KEYS * PEER_KEYS
PEER_QDIM = 128
PEER_TOPK = 16
PEER_BLOCK = 128
EPS = 1e-6

kernel_name = 'hybrid_mla_shortconv_memxattn_peer'


def rmsnorm(t, g):
    tf = t.astype(jnp.float32)
    tf = tf * lax.rsqrt(jnp.mean(tf * tf, axis=-1, keepdims=True) + EPS)
    return (tf * g.astype(jnp.float32)).astype(t.dtype)


def rope_tables(positions, dim, dtype):
    inv = ROPE_THETA ** (-jnp.arange(0, dim, 2, dtype=jnp.float32) / dim)
    ang = positions.astype(jnp.float32)[..., None] * inv
    return jnp.cos(ang).astype(dtype), jnp.sin(ang).astype(dtype)


def apply_rope(t, cos, sin):
    half = t.shape[-1] // 2
    t1, t2 = t[..., :half], t[..., half:]
    return jnp.concatenate([t1 * cos - t2 * sin, t2 * cos + t1 * sin], axis=-1)


def mla_branch(c_q, c_kv, k_rope_raw, cos, sin, g_q, w_uq, g_kv, w_ukv):
    B, S, _ = c_q.shape
    q = (rmsnorm(c_q, g_q) @ w_uq).reshape(B, S, MLA_HEADS, QK_NOPE_DIM + QK_ROPE_DIM)
    q_nope = q[..., :QK_NOPE_DIM]
    q_rope = apply_rope(q[..., QK_NOPE_DIM:], cos[:, :, None, :], sin[:, :, None, :])
    kv = (rmsnorm(c_kv, g_kv) @ w_ukv).reshape(B, S, MLA_HEADS, QK_NOPE_DIM + V_HEAD_DIM)
    k_nope = kv[..., :QK_NOPE_DIM]
    v = kv[..., QK_NOPE_DIM:]
    k_rope = apply_rope(k_rope_raw, cos, sin)
    scale = 1.0 / math.sqrt(QK_NOPE_DIM + QK_ROPE_DIM)
    kpos = jnp.arange(S)

    def block(i):
        start = i * Q_BLOCK
        qn = lax.dynamic_slice_in_dim(q_nope, start, Q_BLOCK, axis=1)
        qr = lax.dynamic_slice_in_dim(q_rope, start, Q_BLOCK, axis=1)
        s = (jnp.einsum('bqhd,bkhd->bhqk', qn, k_nope)
             + jnp.einsum('bqhd,bkd->bhqk', qr, k_rope)).astype(jnp.float32) * scale
        qpos = start + jnp.arange(Q_BLOCK)
        mask = kpos[None, :] <= qpos[:, None]
        s = jnp.where(mask[None, None], s, -jnp.inf)
        p = jax.nn.softmax(s, axis=-1).astype(v.dtype)
        return jnp.einsum('bhqk,bkhd->bqhd', p, v)

    o = lax.map(block, jnp.arange(S // Q_BLOCK))
    return jnp.transpose(o, (1, 0, 2, 3, 4)).reshape(B, S, MLA_OUT)


def short_conv_branch(gate_b, gate_c, hx, conv_w):
    z = gate_c * hx
    zc = lax.conv_general_dilated(
        z, conv_w[:, None, :].astype(z.dtype), window_strides=(1,),
        padding=[(CONV_WIDTH - 1, 0)], dimension_numbers=('NWC', 'WIO', 'NWC'),
        feature_group_count=CONV_DIM)
    return gate_b * zc


def mixer_block(xn, cos, sin, w_in, g_q, w_uq, g_kv, w_ukv, conv_w, g_out, w_o):
    B, S, _ = xn.shape
    proj = xn @ w_in
    o1 = Q_LORA_RANK
    o2 = o1 + KV_LORA_RANK
    o3 = o2 + QK_ROPE_DIM
    o4 = o3 + CONV_DIM
    o5 = o4 + CONV_DIM
    y_mla = mla_branch(proj[..., :o1], proj[..., o1:o2], proj[..., o2:o3],
                       cos, sin, g_q, w_uq, g_kv, w_ukv)
    y_conv = short_conv_branch(proj[..., o3:o4], proj[..., o4:o5], proj[..., o5:], conv_w)
    y = jnp.concatenate([y_mla, y_conv], axis=-1).reshape(B, S, MIX_WIDTH // GROUP_DIM, GROUP_DIM)
    y = rmsnorm(y, g_out.reshape(MIX_WIDTH // GROUP_DIM, GROUP_DIM)).reshape(B, S, MIX_WIDTH)
    return y @ w_o


def memory_xattn(hn, memn, w_xq, w_xkv, w_xo):
    B, S, D = hn.shape
    M = memn.shape[1]
    q = (hn @ w_xq).reshape(B, S, X_HEADS, X_HEAD_DIM)
    kv = (memn @ w_xkv).reshape(B, M, 2, X_HEADS, X_HEAD_DIM)
    k, v = kv[:, :, 0], kv[:, :, 1]
    s = jnp.einsum('bshd,bmhd->bhsm', q, k).astype(jnp.float32) / math.sqrt(X_HEAD_DIM)
    p = jax.nn.softmax(s, axis=-1).astype(v.dtype)
    o = jnp.einsum('bhsm,bmhd->bshd', p, v).reshape(B, S, D)
    return o @ w_xo


def peer_ffn(hn, w_pq, sub_keys, u_experts, v_experts):
    B, S, D = hn.shape
    T = B * S
    xt = hn.reshape(T, D)
    q = (xt @ w_pq).reshape(T, PEER_HEADS, 2, PEER_QDIM // 2)
    s = jnp.einsum('thcd,hcnd->thcn', q, sub_keys).astype(jnp.float32)
    s_top, i_top = lax.top_k(s, PEER_TOPK)
    cand_s = (s_top[:, :, 0, :, None] + s_top[:, :, 1, None, :]).reshape(T, PEER_HEADS, PEER_TOPK * PEER_TOPK)
    cand_i = (i_top[:, :, 0, :, None] * PEER_KEYS + i_top[:, :, 1, None, :]).reshape(T, PEER_HEADS, PEER_TOPK * PEER_TOPK)
    best_s, best_pos = lax.top_k(cand_s, PEER_TOPK)
    experts = jnp.take_along_axis(cand_i, best_pos, axis=-1)
    gates = jax.nn.softmax(best_s, axis=-1).astype(hn.dtype)
    nb = T // PEER_BLOCK

    def block(args):
        xb, eb, gb = args
        u_sel = jnp.take(u_experts, eb, axis=0)
        a = jax.nn.gelu(jnp.einsum('td,thkd->thk', xb, u_sel)) * gb
        v_sel = jnp.take(v_experts, eb, axis=0)
        return jnp.einsum('thk,thkd->td', a, v_sel)

    out = lax.map(block, (xt.reshape(nb, PEER_BLOCK, D),
                          experts.reshape(nb, PEER_BLOCK, PEER_HEADS, PEER_TOPK),
                          gates.reshape(nb, PEER_BLOCK, PEER_HEADS, PEER_TOPK)))
    return out.reshape(B, S, D)


def setup_inputs(seed: int = 0) -> dict:
    key = jax.random.key(seed)
    ks = iter(jax.random.split(key, 32))
    f32 = jnp.float32

    def nrm(shape, scale):
        return jax.random.normal(next(ks), shape, f32) * scale

    def gain(shape):
        return 1.0 + 0.02 * jax.random.normal(next(ks), shape, f32)

    L = DEPTH
    x = jax.random.normal(next(ks), (BATCH, SEQ, D_MODEL), f32)
    mem = jax.random.normal(next(ks), (BATCH, MEM_LEN, D_MODEL), f32)
    offs = jax.random.randint(next(ks), (BATCH, 1), 0, 1024, dtype=jnp.int32)
    positions = offs + jnp.arange(SEQ, dtype=jnp.int32)[None, :]
    return {
        'x': x,
        'mem': mem,
        'positions': positions,
        'g_mix': gain((L, D_MODEL)),
        'w_in': nrm((L, D_MODEL, IN_PROJ_COLS), D_MODEL ** -0.5),
        'g_q': gain((L, Q_LORA_RANK)),
        'w_uq': nrm((L, Q_LORA_RANK, MLA_HEADS * (QK_NOPE_DIM + QK_ROPE_DIM)), Q_LORA_RANK ** -0.5),
        'g_kv': gain((L, KV_LORA_RANK)),
        'w_ukv': nrm((L, KV_LORA_RANK, MLA_HEADS * (QK_NOPE_DIM + V_HEAD_DIM)), KV_LORA_RANK ** -0.5),
        'conv_w': nrm((L, CONV_WIDTH, CONV_DIM), CONV_WIDTH ** -0.5),
        'g_out': gain((L, MIX_WIDTH)),
        'w_o': nrm((L, MIX_WIDTH, D_MODEL), MIX_WIDTH ** -0.5),
        'g_x': gain((L, D_MODEL)),
        'g_mem': gain((L, D_MODEL)),
        'w_xq': nrm((L, D_MODEL, D_MODEL), D_MODEL ** -0.5),
        'w_xkv': nrm((L, D_MODEL, 2 * D_MODEL), D_MODEL ** -0.5),
        'w_xo': nrm((L, D_MODEL, D_MODEL), D_MODEL ** -0.5),
        'g_ffn': gain((L, D_MODEL)),
        'w_pq': nrm((L, D_MODEL, PEER_HEADS * PEER_QDIM), D_MODEL ** -0.5),
        'sub_keys': nrm((L, PEER_HEADS, 2, PEER_KEYS, PEER_QDIM // 2), (PEER_QDIM // 2) ** -0.5),
        'u_experts': nrm((L, PEER_EXPERTS, D_MODEL), D_MODEL ** -0.5),
        'v_experts': nrm((L, PEER_EXPERTS, D_MODEL), PEER_HEADS ** -0.5),
        'g_final': gain((D_MODEL,)),
    }


def reference(x, mem, positions, g_mix, w_in, g_q, w_uq, g_kv, w_ukv, conv_w, g_out, w_o,
              g_x, g_mem, w_xq, w_xkv, w_xo, g_ffn, w_pq, sub_keys, u_experts, v_experts, g_final):
    cos, sin = rope_tables(positions, QK_ROPE_DIM, x.dtype)
    h = x
    for l in range(DEPTH):
        h = h + mixer_block(rmsnorm(h, g_mix[l]), cos, sin, w_in[l], g_q[l], w_uq[l],
                            g_kv[l], w_ukv[l], conv_w[l], g_out[l], w_o[l])
        h = h + memory_xattn(rmsnorm(h, g_x[l]), rmsnorm(mem, g_mem[l]),
                             w_xq[l], w_xkv[l], w_xo[l])
        h = h + peer_ffn(rmsnorm(h, g_ffn[l]), w_pq[l], sub_keys[l], u_experts[l], v_experts[l])
    return rmsnorm(h, g_final)
```

```python
import functools
import math

import jax
import jax.numpy as jnp
from jax import lax
from jax.experimental import pallas as pl
from jax.experimental.pallas import tpu as pltpu

F32 = jnp.float32
BF16 = jnp.bfloat16

EPS = 1e-6
LANES = 128
SUBLANES = 8
VMEM_LIMIT = 56 * 1024 * 1024

MLA_HEADS = 8
QK_NOPE = 64
QK_ROPE = 32
V_HEAD = 64
Q_RANK = 384
KV_RANK = 256
CONV_DIM = 512
GROUP_DIM = 64
ROPE_THETA = 10000.0
X_HEADS = 4
PEER_HEADS = 8
PEER_KEYS = 128
PEER_TOPK = 16
HALF_Q = 64

TS_IN = 512
TQ = 512
TK = 512
TS_ROUTE = 256
TT_FFN = 512
EB_FFN = 1024

NT_DIMS = (((1,), (1,)), ((), ()))


def _rms(x, g):
    return x * lax.rsqrt(jnp.mean(x * x, axis=-1, keepdims=True) + EPS) * g


def _split_bf16(x):
    hi = x.astype(BF16)
    lo = (x - hi.astype(F32)).astype(BF16)
    return hi, lo


def _dot(a, b):
    return jnp.dot(a, b, preferred_element_type=F32)


def _dot_nt(a, b):
    return lax.dot_general(a, b, NT_DIMS, preferred_element_type=F32)


_C_CQ = 0
_C_CKV = _C_CQ + Q_RANK
_C_KR = _C_CKV + KV_RANK
_C_KRR = _C_KR + LANES
_C_GB = _C_KRR + LANES
_C_GC = _C_GB + CONV_DIM
_C_HX = _C_GC + CONV_DIM
_C_END = _C_HX + CONV_DIM


def _mixer_in_kernel(x_ref, gmix_ref, w1_ref, gq_ref, wq_ref, wqr_ref, gkv_ref, wk_ref, wv_ref,
                     eplace_ref, cq_ref, sq_ref, ck_ref, sk_ref,
                     q_out, k_out, v_out, z_out, gb_out):
    xn = _rms(x_ref[...], gmix_ref[...]).astype(BF16)
    proj = _dot(xn, w1_ref[...])
    cq = proj[:, _C_CQ:_C_CKV]
    ckv = proj[:, _C_CKV:_C_KR]
    kr = proj[:, _C_KR:_C_KRR]
    krr = proj[:, _C_KRR:_C_GB]
    gb_out[...] = proj[:, _C_GB:_C_GC]
    z_out[...] = proj[:, _C_GC:_C_HX] * proj[:, _C_HX:_C_END]

    cqn = _rms(cq, gq_ref[...]).astype(BF16)
    q_raw = _dot(cqn, wq_ref[...])
    q_rot = _dot(cqn, wqr_ref[...])
    cq_t = cq_ref[...]
    sq_t = sq_ref[...]
    for h in range(MLA_HEADS):
        sl = slice(h * LANES, (h + 1) * LANES)
        q_out[:, sl] = (q_raw[:, sl] * cq_t + q_rot[:, sl] * sq_t).astype(BF16)

    ckvn = _rms(ckv, gkv_ref[...]).astype(BF16)
    kr_roped = (kr * ck_ref[...] + krr * sk_ref[...]).astype(BF16)
    k_out[...] = (_dot(ckvn, wk_ref[...]) + _dot(kr_roped, eplace_ref[...])).astype(BF16)
    v_out[...] = _dot(ckvn, wv_ref[...]).astype(BF16)


def _mixer_in(x2, gmix, w1, gq, wq, wqr, gkv, wk, wv, eplace, cq_t, sq_t, ck_t, sk_t):
    T, D = x2.shape
    ts = min(TS_IN, T)
    row = lambda i: (i, 0)
    fixed = lambda i: (0, 0)
    full = lambda a: pl.BlockSpec(a.shape, fixed)
    return pl.pallas_call(
        _mixer_in_kernel,
        grid=(T // ts,),
        in_specs=[pl.BlockSpec((ts, D), row), full(gmix), full(w1), full(gq), full(wq), full(wqr),
                  full(gkv), full(wk), full(wv), full(eplace),
                  pl.BlockSpec((ts, LANES), row), pl.BlockSpec((ts, LANES), row),
                  pl.BlockSpec((ts, LANES), row), pl.BlockSpec((ts, LANES), row)],
        out_specs=[pl.BlockSpec((ts, MLA_HEADS * LANES), row), pl.BlockSpec((ts, MLA_HEADS * LANES), row),
                   pl.BlockSpec((ts, MLA_HEADS * V_HEAD), row), pl.BlockSpec((ts, CONV_DIM), row),
                   pl.BlockSpec((ts, CONV_DIM), row)],
        out_shape=[jax.ShapeDtypeStruct((T, MLA_HEADS * LANES), BF16),
                   jax.ShapeDtypeStruct((T, MLA_HEADS * LANES), BF16),
                   jax.ShapeDtypeStruct((T, MLA_HEADS * V_HEAD), BF16),
                   jax.ShapeDtypeStruct((T, CONV_DIM), F32),
                   jax.ShapeDtypeStruct((T, CONV_DIM), F32)],
        compiler_params=pltpu.CompilerParams(dimension_semantics=("arbitrary",), vmem_limit_bytes=VMEM_LIMIT),
        name="mixer_in",
    )(x2, gmix, w1, gq, wq, wqr, gkv, wk, wv, eplace, cq_t, sq_t, ck_t, sk_t)


def _mla_attn_kernel(q_ref, k_ref, v_ref, o_ref, *, tq, tk):
    qi = pl.program_id(2)
    scale = 1.0 / math.sqrt(QK_NOPE + QK_ROPE)
    n_kt = (qi + 1) * (tq // tk)
    row_pos = qi * tq + lax.broadcasted_iota(jnp.int32, (tq, tk), 0)
    col_iota = lax.broadcasted_iota(jnp.int32, (tq, tk), 1)
    accs = []
    for hh in range(2):
        qh = q_ref[:, hh * LANES:(hh + 1) * LANES]

        def body(kt, carry, qh=qh, hh=hh):
            m, l, acc = carry
            k0 = pl.multiple_of(kt * tk, tk)
            kh = k_ref[pl.ds(k0, tk), hh * LANES:(hh + 1) * LANES]
            s = _dot_nt(qh, kh) * scale
            s = jnp.where(k0 + col_iota <= row_pos, s, -jnp.inf)
            m_new = jnp.maximum(m, jnp.max(s, axis=-1, keepdims=True))
            alpha = jnp.exp(m - m_new)
            p = jnp.exp(s - m_new)
            l_new = alpha * l + jnp.sum(p, axis=-1, keepdims=True)
            acc_new = alpha * acc + _dot(p.astype(BF16), v_ref[pl.ds(k0, tk), :])
            return m_new, l_new, acc_new

        m0 = jnp.full((tq, 1), -jnp.inf, F32)
        l0 = jnp.zeros((tq, 1), F32)
        a0 = jnp.zeros((tq, LANES), F32)
        m, l, acc = lax.fori_loop(0, n_kt, body, (m0, l0, a0))
        accs.append(acc / l)
    lane = lax.broadcasted_iota(jnp.int32, (tq, LANES), 1)
    o_ref[...] = jnp.where(lane < V_HEAD, accs[0], accs[1])


def _mla_attn(q3, k3, v3):
    B, S, _ = q3.shape
    tq = min(TQ, S)
    tk = min(TK, tq)
    return pl.pallas_call(
        functools.partial(_mla_attn_kernel, tq=tq, tk=tk),
        grid=(B, MLA_HEADS // 2, S // tq),
        in_specs=[pl.BlockSpec((None, tq, 2 * LANES), lambda b, g, i: (b, i, g)),
                  pl.BlockSpec((None, S, 2 * LANES), lambda b, g, i: (b, 0, g)),
                  pl.BlockSpec((None, S, 2 * V_HEAD), lambda b, g, i: (b, 0, g))],
        out_specs=pl.BlockSpec((None, tq, 2 * V_HEAD), lambda b, g, i: (b, i, g)),
        out_shape=jax.ShapeDtypeStruct((B, S, MLA_HEADS * V_HEAD), F32),
        compiler_params=pltpu.CompilerParams(dimension_semantics=("arbitrary", "arbitrary", "arbitrary"),
                                             vmem_limit_bytes=VMEM_LIMIT),
        name="mla_attn",
    )(q3, k3, v3)


def _mixer_out_kernel(o_ref, z_ref, zh_ref, gb_ref, x_ref, cw_ref, gout_ref, gsum_ref, gexp_ref, wo_ref,
                      h_out, *, tiles_per_seq):
    i = pl.program_id(0)
    ts = z_ref.shape[0]
    z = z_ref[...]
    halo = jnp.where(i % tiles_per_seq == 0, 0.0, zh_ref[...])
    row = lax.broadcasted_iota(jnp.int32, z.shape, 0)
    z1 = jnp.where(row == 0, halo[7:8, :], pltpu.roll(z, 1, axis=0))
    z2 = jnp.where(row == 0, halo[6:7, :], jnp.where(row == 1, halo[7:8, :], pltpu.roll(z, 2, axis=0)))
    cw = cw_ref[...]
    y_conv = gb_ref[...] * (cw[0:1, :] * z2 + cw[1:2, :] * z1 + cw[2:3, :] * z)
    y = jnp.concatenate([o_ref[...], y_conv], axis=-1)
    sq_hi, sq_lo = _split_bf16(y * y)
    gs = _dot(sq_hi, gsum_ref[...]) + _dot(sq_lo, gsum_ref[...])
    r = lax.rsqrt(gs * (1.0 / GROUP_DIM) + EPS)
    r_hi, r_lo = _split_bf16(r)
    r_full = _dot(r_hi, gexp_ref[...]) + _dot(r_lo, gexp_ref[...])
    yn = (y * r_full * gout_ref[...]).astype(BF16)
    h_out[...] = x_ref[...] + _dot(yn, wo_ref[...])


def _mixer_out(o2, z, gb, x2, conv_w, gout, gsum, gexp, wo, seq):
    T, D = x2.shape
    ts = min(TS_IN, seq)
    row = lambda i: (i, 0)
    fixed = lambda i: (0, 0)
    full = lambda a: pl.BlockSpec(a.shape, fixed)
    halo_blocks = ts // SUBLANES
    return pl.pallas_call(
        functools.partial(_mixer_out_kernel, tiles_per_seq=seq // ts),
        grid=(T // ts,),
        in_specs=[pl.BlockSpec((ts, MLA_HEADS * V_HEAD), row), pl.BlockSpec((ts, CONV_DIM), row),
                  pl.BlockSpec((SUBLANES, CONV_DIM), lambda i: (jnp.maximum(i * halo_blocks - 1, 0), 0)),
                  pl.BlockSpec((ts, CONV_DIM), row), pl.BlockSpec((ts, D), row),
                  full(conv_w), full(gout), full(gsum), full(gexp), full(wo)],
        out_specs=pl.BlockSpec((ts, D), row),
        out_shape=jax.ShapeDtypeStruct((T, D), F32),
        compiler_params=pltpu.CompilerParams(dimension_semantics=("arbitrary",), vmem_limit_bytes=VMEM_LIMIT),
        name="mixer_out",
    )(o2, z, z, gb, x2, conv_w, gout, gsum, gexp, wo)


def _mem_kv_kernel(mem_ref, g_ref, w_ref, k_out, v_out):
    d = mem_ref.shape[-1]
    mn = _rms(mem_ref[...], g_ref[...]).astype(BF16)
    kv = _dot(mn, w_ref[...])
    k_out[...] = kv[:, :d].astype(BF16)
    v_out[...] = kv[:, d:].astype(BF16)


def _mem_kv(mem, g, w):
    B, M, D = mem.shape
    return pl.pallas_call(
        _mem_kv_kernel,
        grid=(B,),
        in_specs=[pl.BlockSpec((None, M, D), lambda b: (b, 0, 0)), pl.BlockSpec(g.shape, lambda b: (0, 0)),
                  pl.BlockSpec(w.shape, lambda b: (0, 0))],
        out_specs=[pl.BlockSpec((None, M, D), lambda b: (b, 0, 0)), pl.BlockSpec((None, M, D), lambda b: (b, 0, 0))],
        out_shape=[jax.ShapeDtypeStruct((B, M, D), BF16), jax.ShapeDtypeStruct((B, M, D), BF16)],
        compiler_params=pltpu.CompilerParams(dimension_semantics=("arbitrary",), vmem_limit_bytes=VMEM_LIMIT),
        name="mem_kv",
    )(mem, g, w)


def _xattn_kernel(h_ref, g_ref, wq_ref, k_ref, v_ref, wo_ref, h_out):
    h = h_ref[...]
    d = h.shape[-1]
    hd = d // X_HEADS
    hn = _rms(h, g_ref[...]).astype(BF16)
    q = _dot(hn, wq_ref[...]).astype(BF16)
    outs = []
    for hh in range(X_HEADS):
        sl = slice(hh * hd, (hh + 1) * hd)
        s = _dot_nt(q[:, sl], k_ref[:, sl]) * (1.0 / math.sqrt(hd))
        m = jnp.max(s, axis=-1, keepdims=True)
        p = jnp.exp(s - m)
        p = p / jnp.sum(p, axis=-1, keepdims=True)
        outs.append(_dot(p.astype(BF16), v_ref[:, sl]))
    o = jnp.concatenate(outs, axis=-1).astype(BF16)
    h_out[...] = h + _dot(o, wo_ref[...])


def _xattn(h3, g, wq, kx, vx, wo):
    B, S, D = h3.shape
    M = kx.shape[1]
    ts = min(TS_IN, S)
    fixed = lambda b, i: (0, 0)
    return pl.pallas_call(
        _xattn_kernel,
        grid=(B, S // ts),
        in_specs=[pl.BlockSpec((None, ts, D), lambda b, i: (b, i, 0)), pl.BlockSpec(g.shape, fixed),
                  pl.BlockSpec(wq.shape, fixed), pl.BlockSpec((None, M, D), lambda b, i: (b, 0, 0)),
                  pl.BlockSpec((None, M, D), lambda b, i: (b, 0, 0)), pl.BlockSpec(wo.shape, fixed)],
        out_specs=pl.BlockSpec((None, ts, D), lambda b, i: (b, i, 0)),
        out_shape=jax.ShapeDtypeStruct((B, S, D), F32),
        compiler_params=pltpu.CompilerParams(dimension_semantics=("arbitrary", "arbitrary"),
                                             vmem_limit_bytes=VMEM_LIMIT),
        name="xattn",
    )(h3, g, wq, kx, vx, wo)


def _batcher_pairs(n):
    pairs = []
    p = 1
    while p < n:
        k = p
        while k >= 1:
            for j in range(k % p, n - k, 2 * k):
                for i in range(min(k, n - j - k)):
                    if (i + j) // (2 * p) == (i + j + k) // (2 * p):
                        pairs.append((i + j, i + j + k))
            k //= 2
        p *= 2
    return pairs


_SORT16 = _batcher_pairs(PEER_TOPK)
_ROW_LEN = [PEER_TOPK // (a + 1) for a in range(PEER_TOPK)]


def _sort_desc(v):
    v = list(v)
    for i, j in _SORT16:
        hi = jnp.maximum(v[i], v[j])
        lo = jnp.minimum(v[i], v[j])
        v[i], v[j] = hi, lo
    return v


def _bitonic_desc(v):
    v = list(v)
    n = len(v)
    d = n // 2
    while d >= 1:
        for k in range(n):
            if k & d == 0:
                hi = jnp.maximum(v[k], v[k + d])
                lo = jnp.minimum(v[k], v[k + d])
                v[k], v[k + d] = hi, lo
        d //= 2
    return v


def _merge_top(cur, other):
    n = len(cur)
    c = list(cur)
    for r, val in enumerate(other):
        c[n - 1 - r] = jnp.maximum(c[n - 1 - r], val)
    return _bitonic_desc(c)


def _top16_sorted(s):
    v = _sort_desc([s[k] for k in range(PEER_TOPK)])
    for shift in (4, 2, 1):
        other = [pltpu.roll(v[k], shift, axis=0) for k in range(PEER_TOPK)]
        v = _merge_top(v, other)
    return v


def _peer_route_kernel(h_ref, g_ref, wq_hi_ref, wq_lo_ref, key_hi_ref, key_lo_ref,
                       xn_out, s2_out, p_out, thr_out, c_out,
                       st_ref, top_ref, res_ref):
    ts = h_ref.shape[0]
    n_chunk = ts // LANES
    hn = _rms(h_ref[...], g_ref[...])
    hn_hi, hn_lo = _split_bf16(hn)
    xn_out[...] = hn_hi
    q = _dot(hn_hi, wq_hi_ref[...]) + _dot(hn_hi, wq_lo_ref[...]) + _dot(hn_lo, wq_hi_ref[...])
    for h in range(PEER_HEADS):
        q_hi, q_lo = _split_bf16(q[:, h * LANES:(h + 1) * LANES])
        k_hi = key_hi_ref[h]
        k_lo = key_lo_ref[h]
        st = _dot_nt(k_hi, q_hi) + _dot_nt(k_hi, q_lo) + _dot_nt(k_lo, q_hi)
        for c in range(n_chunk):
            st_ref[c, h] = st[:, c * LANES:(c + 1) * LANES]

    neg_inf = jnp.float32(-jnp.inf)
    pos_inf = jnp.float32(jnp.inf)

    def chunk_body(c, _):
        def sort_body(h, _):
            for half in range(2):
                s = st_ref[c, h, pl.ds(half * PEER_KEYS, PEER_KEYS), :].reshape(PEER_TOPK, SUBLANES, LANES)
                v = _top16_sorted(s)
                for a in range(PEER_TOPK):
                    top_ref[half, a, pl.ds(h, 1), :] = v[a][0:1, :]
            return 0

        lax.fori_loop(0, PEER_HEADS, sort_body, 0)

        v1 = [top_ref[0, a] for a in range(PEER_TOPK)]
        v2 = [top_ref[1, b] for b in range(PEER_TOPK)]
        sums = [[v1[a] + v2[b] for b in range(_ROW_LEN[a])] for a in range(PEER_TOPK)]
        cur = sums[0]
        a = 1
        while _ROW_LEN[a] > 1:
            cur = _merge_top(cur, sums[a])
            a += 1
        cur = _merge_top(cur, [sums[r][0] for r in range(a, PEER_TOPK)])
        tau = cur[PEER_TOPK - 1]
        top_sum = sums[0][0]
        z = jnp.zeros_like(tau)
        for a in range(PEER_TOPK):
            theta = jnp.full_like(tau, pos_inf)
            for b in range(_ROW_LEN[a]):
                sel = sums[a][b] >= tau
                theta = jnp.where(sel, jnp.minimum(theta, v2[b]), theta)
                z = z + jnp.where(sel, jnp.exp(sums[a][b] - top_sum), 0.0)
            res_ref[a] = theta
        res_ref[PEER_TOPK] = 1.0 / z

        def expand_body(h, _):
            s1 = st_ref[c, h, pl.ds(0, PEER_KEYS), :].reshape(PEER_TOPK, SUBLANES, LANES)
            s2 = st_ref[c, h, pl.ds(PEER_KEYS, PEER_KEYS), :]
            thr = jnp.full(s1.shape, pos_inf, F32)
            for a in range(PEER_TOPK):
                v1a = top_ref[0, a, pl.ds(h, 1), :]
                theta_a = res_ref[a, pl.ds(h, 1), :]
                thr = jnp.where(s1 == v1a, theta_a, thr)
            m1 = top_ref[0, 0, pl.ds(h, 1), :]
            m2 = top_ref[1, 0, pl.ds(h, 1), :]
            inv_z = res_ref[PEER_TOPK, pl.ds(h, 1), :]
            thr_out[c, h] = thr.reshape(PEER_KEYS, LANES)
            c_out[c, h] = (jnp.exp(s1 - m1) * inv_z).reshape(PEER_KEYS, LANES)
            s2_out[c, h] = s2
            p_out[c, h] = jnp.exp(s2 - m2)
            return 0

        lax.fori_loop(0, PEER_HEADS, expand_body, 0)
        return 0

    lax.fori_loop(0, n_chunk, chunk_body, 0)


def _peer_route(h2, g, wq_hi, wq_lo, key_hi, key_lo):
    T, D = h2.shape
    ts = min(TS_ROUTE, T)
    nc = ts // LANES
    aux_spec = pl.BlockSpec((nc, PEER_HEADS, PEER_KEYS, LANES), lambda i: (i, 0, 0, 0))
    aux_shape = jax.ShapeDtypeStruct((T // LANES, PEER_HEADS, PEER_KEYS, LANES), F32)
    fixed2 = lambda i: (0, 0)
    fixed3 = lambda i: (0, 0, 0)
    return pl.pallas_call(
        _peer_route_kernel,
        grid=(T // ts,),
        in_specs=[pl.BlockSpec((ts, D), lambda i: (i, 0)), pl.BlockSpec(g.shape, fixed2),
                  pl.BlockSpec(wq_hi.shape, fixed2), pl.BlockSpec(wq_lo.shape, fixed2),
                  pl.BlockSpec(key_hi.shape, fixed3), pl.BlockSpec(key_lo.shape, fixed3)],
        out_specs=[pl.BlockSpec((ts, D), lambda i: (i, 0)), aux_spec, aux_spec, aux_spec, aux_spec],
        out_shape=[jax.ShapeDtypeStruct((T, D), BF16), aux_shape, aux_shape, aux_shape, aux_shape],
        scratch_shapes=[pltpu.VMEM((nc, PEER_HEADS, 2 * PEER_KEYS, LANES), F32),
                        pltpu.VMEM((2, PEER_TOPK, SUBLANES, LANES), F32),
                        pltpu.VMEM((PEER_TOPK + 1, SUBLANES, LANES), F32)],
        compiler_params=pltpu.CompilerParams(dimension_semantics=("arbitrary",), vmem_limit_bytes=VMEM_LIMIT),
        name="peer_route",
    )(h2, g, wq_hi, wq_lo, key_hi, key_lo)


def _peer_ffn_kernel(xn_ref, u_ref, vt_ref, s2_ref, p_ref, thr_ref, c_ref, h_ref, gfin_ref,
                     out_ref, acc_ref, ht_ref, *, final_norm):
    e = pl.program_id(1)
    n_e = pl.num_programs(1)
    tt = xn_ref.shape[0]
    eb = u_ref.shape[0]
    n_i = eb // PEER_KEYS

    @pl.when(e == 0)
    def _():
        acc_ref[...] = jnp.zeros_like(acc_ref)

    at = _dot_nt(u_ref[...], xn_ref[...])
    for tc in range(tt // LANES):
        lanes = slice(tc * LANES, (tc + 1) * LANES)
        for ii in range(n_i):
            g = jnp.zeros((PEER_KEYS, LANES), F32)
            for h in range(PEER_HEADS):
                thr = thr_ref[tc, h, ii:ii + 1, :]
                coef = c_ref[tc, h, ii:ii + 1, :]
                g = g + jnp.where(s2_ref[tc, h] >= thr, p_ref[tc, h], 0.0) * coef
            rows = slice(ii * PEER_KEYS, (ii + 1) * PEER_KEYS)
            ht_ref[rows, lanes] = (jax.nn.gelu(at[rows, lanes]) * g).astype(BF16)
    acc_ref[...] += _dot(vt_ref[...], ht_ref[...])

    @pl.when(e == n_e - 1)
    def _():
        res = h_ref[...] + acc_ref[...].T
        out_ref[...] = _rms(res, gfin_ref[...]) if final_norm else res


def _peer_ffn(xn, u_bf, vt_bf, s2, p, thr, coef, h2, gfin, final_norm):
    T, D = h2.shape
    E = u_bf.shape[0]
    tt = min(TT_FFN, T)
    nc = tt // LANES
    eb = EB_FFN
    n_i = eb // PEER_KEYS
    tok = lambda t, e: (t, 0)
    aux_full = pl.BlockSpec((nc, PEER_HEADS, PEER_KEYS, LANES), lambda t, e: (t, 0, 0, 0))
    aux_rows = pl.BlockSpec((nc, PEER_HEADS, n_i, LANES), lambda t, e: (t, 0, e, 0))
    return pl.pallas_call(
        functools.partial(_peer_ffn_kernel, final_norm=final_norm),
        grid=(T // tt, E // eb),
        in_specs=[pl.BlockSpec((tt, D), tok), pl.BlockSpec((eb, D), lambda t, e: (e, 0)),
                  pl.BlockSpec((D, eb), lambda t, e: (0, e)), aux_full, aux_full, aux_rows, aux_rows,
                  pl.BlockSpec((tt, D), tok), pl.BlockSpec(gfin.shape, lambda t, e: (0, 0))],
        out_specs=pl.BlockSpec((tt, D), tok),
        out_shape=jax.ShapeDtypeStruct((T, D), F32),
        scratch_shapes=[pltpu.VMEM((D, tt), F32), pltpu.VMEM((eb, tt), BF16)],
        compiler_params=pltpu.CompilerParams(dimension_semantics=("arbitrary", "arbitrary"),
                                             vmem_limit_bytes=VMEM_LIMIT),
        name="peer_ffn",
    )(xn, u_bf, vt_bf, s2, p, thr, coef, h2, gfin)


def _head_blocks(w, n_heads, width, pieces):
    w3 = w.reshape(w.shape[0], n_heads, width)
    out = jnp.zeros((w.shape[0], n_heads, LANES), w.dtype)
    for s0, s1, d0 in pieces:
        out = out.at[:, :, d0:d0 + (s1 - s0)].set(w3[:, :, s0:s1])
    return out.reshape(w.shape[0], n_heads * LANES)


def kernel(x, mem, positions, g_mix, w_in, g_q, w_uq, g_kv, w_ukv, conv_w, g_out, w_o, g_x, g_mem, w_xq,
           w_xkv, w_xo, g_ffn, w_pq, sub_keys, u_experts, v_experts, g_final):
    B, S, D = x.shape
    T = B * S
    depth = g_mix.shape[0]
    half = QK_ROPE // 2

    inv = ROPE_THETA ** (-jnp.arange(0, QK_ROPE, 2, dtype=F32) / QK_ROPE)
    ang = positions.astype(F32)[..., None] * inv
    cos = jnp.cos(ang).astype(x.dtype).reshape(T, half)
    sin = jnp.sin(ang).astype(x.dtype).reshape(T, half)
    ones = jnp.ones((T, QK_NOPE), F32)
    zeros_n = jnp.zeros((T, QK_NOPE), F32)
    pad_q = jnp.zeros((T, LANES - QK_NOPE - QK_ROPE), F32)
    pad_k = jnp.zeros((T, LANES - QK_ROPE), F32)
    cq_t = jnp.concatenate([ones, cos, cos, pad_q], axis=1)
    sq_t = jnp.concatenate([zeros_n, -sin, sin, pad_q], axis=1)
    ck_t = jnp.concatenate([cos, cos, pad_k], axis=1)
    sk_t = jnp.concatenate([-sin, sin, pad_k], axis=1)

    lane = jnp.arange(LANES)
    col = jnp.arange(MLA_HEADS * LANES)
    eplace = ((col[None, :] % LANES == lane[:, None] + QK_NOPE) & (lane[:, None] < QK_ROPE)).astype(BF16)
    mix_col = jnp.arange(D)
    gsum = (mix_col[:, None] // GROUP_DIM == lane[None, :]).astype(BF16)
    gexp = (lane[:, None] == mix_col[None, :] // GROUP_DIM).astype(BF16)

    h = x.reshape(T, D)
    for l in range(depth):
        o1 = Q_RANK
        o2 = o1 + KV_RANK
        o3 = o2 + QK_ROPE
        o4 = o3 + CONV_DIM
        o5 = o4 + CONV_DIM
        wl = w_in[l]
        w_kr = wl[:, o2:o3]
        w_krr = jnp.concatenate([w_kr[:, half:], w_kr[:, :half]], axis=1)
        lane_pad = jnp.zeros((D, LANES - QK_ROPE), wl.dtype)
        w1 = jnp.concatenate([wl[:, :o1], wl[:, o1:o2], w_kr, lane_pad, w_krr, lane_pad,
                              wl[:, o3:o4], wl[:, o4:o5], wl[:, o5:]], axis=1).astype(BF16)
        qw = QK_NOPE + QK_ROPE
        wq = _head_blocks(w_uq[l], MLA_HEADS, qw, [(0, qw, 0)]).astype(BF16)
        wqr = _head_blocks(w_uq[l], MLA_HEADS, qw,
                           [(QK_NOPE + half, qw, QK_NOPE), (QK_NOPE, QK_NOPE + half, QK_NOPE + half)]).astype(BF16)
        kvw = QK_NOPE + V_HEAD
        wk = _head_blocks(w_ukv[l], MLA_HEADS, kvw, [(0, QK_NOPE, 0)]).astype(BF16)
        wv = w_ukv[l].reshape(KV_RANK, MLA_HEADS, kvw)[:, :, QK_NOPE:].reshape(KV_RANK, MLA_HEADS * V_HEAD).astype(BF16)

        q, k, v, z, gb = _mixer_in(h, g_mix[l][None, :], w1, g_q[l][None, :], wq, wqr, g_kv[l][None, :], wk, wv,
                                   eplace, cq_t, sq_t, ck_t, sk_t)
        o = _mla_attn(q.reshape(B, S, -1), k.reshape(B, S, -1), v.reshape(B, S, -1))
        h = _mixer_out(o.reshape(T, -1), z, gb, h, conv_w[l], g_out[l][None, :], gsum, gexp,
                       w_o[l].astype(BF16), S)

        kx, vx = _mem_kv(mem, g_mem[l][None, :], w_xkv[l].astype(BF16))
        h = _xattn(h.reshape(B, S, D), g_x[l][None, :], w_xq[l].astype(BF16), kx, vx,
                   w_xo[l].astype(BF16)).reshape(T, D)

        wpq_hi, wpq_lo = _split_bf16(w_pq[l])
        sk = sub_keys[l]
        zk = jnp.zeros_like(sk[:, 0])
        keys_bd = jnp.concatenate([jnp.concatenate([sk[:, 0], zk], axis=-1),
                                   jnp.concatenate([zk, sk[:, 1]], axis=-1)], axis=1)
        key_hi, key_lo = _split_bf16(keys_bd)
        xn, s2, p, thr, coef = _peer_route(h, g_ffn[l][None, :], wpq_hi, wpq_lo, key_hi, key_lo)
        h = _peer_ffn(xn, u_experts[l].astype(BF16), v_experts[l].T.astype(BF16), s2, p, thr, coef, h,
                      g_final[None, :], final_norm=(l == depth - 1))
    return h.reshape(B, S, D)
```

```python
import functools
import math

import jax
import jax.numpy as jnp
from jax import lax
from jax.experimental import pallas as pl
from jax.experimental.pallas import tpu as pltpu

F32 = jnp.float32
BF16 = jnp.bfloat16

EPS = 1e-6
LANES = 128
SUBLANES = 8
VMEM_LIMIT = 56 * 1024 * 1024

MLA_HEADS = 8
QK_NOPE = 64
QK_ROPE = 32
V_HEAD = 64
Q_RANK = 384
KV_RANK = 256
CONV_DIM = 512
GROUP_DIM = 64
ROPE_THETA = 10000.0
X_HEADS = 4
PEER_HEADS = 8
PEER_KEYS = 128
PEER_TOPK = 16
HALF_Q = 64

TS_IN = 512
TQ = 512
TK = 512
TS_ROUTE = 256
TT_FFN = 512
EB_FFN = 1024

NT_DIMS = (((1,), (1,)), ((), ()))


def _rms(x, g):
    return x * lax.rsqrt(jnp.mean(x * x, axis=-1, keepdims=True) + EPS) * g


def _split_bf16(x):
    hi = x.astype(BF16)
    lo = (x - hi.astype(F32)).astype(BF16)
    return hi, lo


def _dot(a, b):
    return jnp.dot(a, b, preferred_element_type=F32)


def _dot_nt(a, b):
    return lax.dot_general(a, b, NT_DIMS, preferred_element_type=F32)


_C_CQ = 0
_C_CKV = _C_CQ + Q_RANK
_C_KR = _C_CKV + KV_RANK
_C_KRR = _C_KR + LANES
_C_GB = _C_KRR + LANES
_C_GC = _C_GB + CONV_DIM
_C_HX = _C_GC + CONV_DIM
_C_END = _C_HX + CONV_DIM


def _mixer_in_kernel(x_ref, gmix_ref, w1_ref, gq_ref, wq_ref, wqr_ref, gkv_ref, wk_ref, wv_ref,
                     eplace_ref, cq_ref, sq_ref, ck_ref, sk_ref,
                     q_out, k_out, v_out, z_out, gb_out):
    xn = _rms(x_ref[...], gmix_ref[...]).astype(BF16)
    proj = _dot(xn, w1_ref[...])
    cq = proj[:, _C_CQ:_C_CKV]
    ckv = proj[:, _C_CKV:_C_KR]
    kr = proj[:, _C_KR:_C_KRR]
    krr = proj[:, _C_KRR:_C_GB]
    gb_out[...] = proj[:, _C_GB:_C_GC]
    z_out[...] = proj[:, _C_GC:_C_HX] * proj[:, _C_HX:_C_END]

    cqn = _rms(cq, gq_ref[...]).astype(BF16)
    q_raw = _dot(cqn, wq_ref[...])
    q_rot = _dot(cqn, wqr_ref[...])
    cq_t = cq_ref[...]
    sq_t = sq_ref[...]
    for h in range(MLA_HEADS):
        sl = slice(h * LANES, (h + 1) * LANES)
        q_out[:, sl] = (q_raw[:, sl] * cq_t + q_rot[:, sl] * sq_t).astype(BF16)

    ckvn = _rms(ckv, gkv_ref[...]).astype(BF16)
    kr_roped = (kr * ck_ref[...] + krr * sk_ref[...]).astype(BF16)
    k_out[...] = (_dot(ckvn, wk_ref[...]) + _dot(kr_roped, eplace_ref[...])).astype(BF16)
    v_out[...] = _dot(ckvn, wv_ref[...]).astype(BF16)


def _mixer_in(x2, gmix, w1, gq, wq, wqr, gkv, wk, wv, eplace, cq_t, sq_t, ck_t, sk_t):
    T, D = x2.shape
    ts = min(TS_IN, T)
    row = lambda i: (i, 0)
    fixed = lambda i: (0, 0)
    full = lambda a: pl.BlockSpec(a.shape, fixed)
    return pl.pallas_call(
        _mixer_in_kernel,
        grid=(T // ts,),
        in_specs=[pl.BlockSpec((ts, D), row), full(gmix), full(w1), full(gq), full(wq), full(wqr),
                  full(gkv), full(wk), full(wv), full(eplace),
                  pl.BlockSpec((ts, LANES), row), pl.BlockSpec((ts, LANES), row),
                  pl.BlockSpec((ts, LANES), row), pl.BlockSpec((ts, LANES), row)],
        out_specs=[pl.BlockSpec((ts, MLA_HEADS * LANES), row), pl.BlockSpec((ts, MLA_HEADS * LANES), row),
                   pl.BlockSpec((ts, MLA_HEADS * V_HEAD), row), pl.BlockSpec((ts, CONV_DIM), row),
                   pl.BlockSpec((ts, CONV_DIM), row)],
        out_shape=[jax.ShapeDtypeStruct((T, MLA_HEADS * LANES), BF16),
                   jax.ShapeDtypeStruct((T, MLA_HEADS * LANES), BF16),
                   jax.ShapeDtypeStruct((T, MLA_HEADS * V_HEAD), BF16),
                   jax.ShapeDtypeStruct((T, CONV_DIM), F32),
                   jax.ShapeDtypeStruct((T, CONV_DIM), F32)],
        compiler_params=pltpu.CompilerParams(dimension_semantics=("arbitrary",), vmem_limit_bytes=VMEM_LIMIT),
        name="mixer_in",
    )(x2, gmix, w1, gq, wq, wqr, gkv, wk, wv, eplace, cq_t, sq_t, ck_t, sk_t)


def _mla_attn_kernel(q_ref, k_ref, v_ref, o_ref, *, tq, tk):
    qi = pl.program_id(2)
    scale = 1.0 / math.sqrt(QK_NOPE + QK_ROPE)
    n_kt = (qi + 1) * (tq // tk)
    row_pos = qi * tq + lax.broadcasted_iota(jnp.int32, (tq, tk), 0)
    col_iota = lax.broadcasted_iota(jnp.int32, (tq, tk), 1)
    accs = []
    for hh in range(2):
        qh = q_ref[:, hh * LANES:(hh + 1) * LANES]

        def body(kt, carry, qh=qh, hh=hh):
            m, l, acc = carry
            k0 = pl.multiple_of(kt * tk, tk)
            kh = k_ref[pl.ds(k0, tk), hh * LANES:(hh + 1) * LANES]
            s = _dot_nt(qh, kh) * scale
            s = jnp.where(k0 + col_iota <= row_pos, s, -jnp.inf)
            m_new = jnp.maximum(m, jnp.max(s, axis=-1, keepdims=True))
            alpha = jnp.exp(m - m_new)
            p = jnp.exp(s - m_new)
            l_new = alpha * l + jnp.sum(p, axis=-1, keepdims=True)
            acc_new = alpha * acc + _dot(p.astype(BF16), v_ref[pl.ds(k0, tk), :])
            return m_new, l_new, acc_new

        m0 = jnp.full((tq, 1), -jnp.inf, F32)
        l0 = jnp.zeros((tq, 1), F32)
        a0 = jnp.zeros((tq, LANES), F32)
        m, l, acc = lax.fori_loop(0, n_kt, body, (m0, l0, a0))
        accs.append(acc / l)
    lane = lax.broadcasted_iota(jnp.int32, (tq, LANES), 1)
    o_ref[...] = jnp.where(lane < V_HEAD, accs[0], accs[1])


def _mla_attn(q3, k3, v3):
    B, S, _ = q3.shape
    tq = min(TQ, S)
    tk = min(TK, tq)
    return pl.pallas_call(
        functools.partial(_mla_attn_kernel, tq=tq, tk=tk),
        grid=(B, MLA_HEADS // 2, S // tq),
        in_specs=[pl.BlockSpec((None, tq, 2 * LANES), lambda b, g, i: (b, i, g)),
                  pl.BlockSpec((None, S, 2 * LANES), lambda b, g, i: (b, 0, g)),
                  pl.BlockSpec((None, S, 2 * V_HEAD), lambda b, g, i: (b, 0, g))],
        out_specs=pl.BlockSpec((None, tq, 2 * V_HEAD), lambda b, g, i: (b, i, g)),
        out_shape=jax.ShapeDtypeStruct((B, S, MLA_HEADS * V_HEAD), F32),
        compiler_params=pltpu.CompilerParams(dimension_semantics=("arbitrary", "arbitrary", "arbitrary"),
                                             vmem_limit_bytes=VMEM_LIMIT),
        name="mla_attn",
    )(q3, k3, v3)


def _mixer_out_kernel(o_ref, z_ref, zh_ref, gb_ref, x_ref, cw_ref, gout_ref, gsum_ref, gexp_ref, wo_ref,
                      h_out, *, tiles_per_seq):
    i = pl.program_id(0)
    ts = z_ref.shape[0]
    z = z_ref[...]
    halo = jnp.where(i % tiles_per_seq == 0, 0.0, zh_ref[...])
    row = lax.broadcasted_iota(jnp.int32, z.shape, 0)
    z1 = jnp.where(row == 0, halo[7:8, :], pltpu.roll(z, 1, axis=0))
    z2 = jnp.where(row == 0, halo[6:7, :], jnp.where(row == 1, halo[7:8, :], pltpu.roll(z, 2, axis=0)))
    cw = cw_ref[...]
    y_conv = gb_ref[...] * (cw[0:1, :] * z2 + cw[1:2, :] * z1 + cw[2:3, :] * z)
    y = jnp.concatenate([o_ref[...], y_conv], axis=-1)
    sq_hi, sq_lo = _split_bf16(y * y)
    gs = _dot(sq_hi, gsum_ref[...]) + _dot(sq_lo, gsum_ref[...])
    r = lax.rsqrt(gs * (1.0 / GROUP_DIM) + EPS)
    r_hi, r_lo = _split_bf16(r)
    r_full = _dot(r_hi, gexp_ref[...]) + _dot(r_lo, gexp_ref[...])
    yn = (y * r_full * gout_ref[...]).astype(BF16)
    h_out[...] = x_ref[...] + _dot(yn, wo_ref[...])


def _mixer_out(o2, z, gb, x2, conv_w, gout, gsum, gexp, wo, seq):
    T, D = x2.shape
    ts = min(TS_IN, seq)
    row = lambda i: (i, 0)
    fixed = lambda i: (0, 0)
    full = lambda a: pl.BlockSpec(a.shape, fixed)
    halo_blocks = ts // SUBLANES
    return pl.pallas_call(
        functools.partial(_mixer_out_kernel, tiles_per_seq=seq // ts),
        grid=(T // ts,),
        in_specs=[pl.BlockSpec((ts, MLA_HEADS * V_HEAD), row), pl.BlockSpec((ts, CONV_DIM), row),
                  pl.BlockSpec((SUBLANES, CONV_DIM), lambda i: (jnp.maximum(i * halo_blocks - 1, 0), 0)),
                  pl.BlockSpec((ts, CONV_DIM), row), pl.BlockSpec((ts, D), row),
                  full(conv_w), full(gout), full(gsum), full(gexp), full(wo)],
        out_specs=pl.BlockSpec((ts, D), row),
        out_shape=jax.ShapeDtypeStruct((T, D), F32),
        compiler_params=pltpu.CompilerParams(dimension_semantics=("arbitrary",), vmem_limit_bytes=VMEM_LIMIT),
        name="mixer_out",
    )(o2, z, z, gb, x2, conv_w, gout, gsum, gexp, wo)


def _mem_kv_kernel(mem_ref, g_ref, w_ref, k_out, v_out):
    d = mem_ref.shape[-1]
    mn = _rms(mem_ref[...], g_ref[...]).astype(BF16)
    kv = _dot(mn, w_ref[...])
    k_out[...] = kv[:, :d].astype(BF16)
    v_out[...] = kv[:, d:].astype(BF16)


def _mem_kv(mem, g, w):
    B, M, D = mem.shape
    return pl.pallas_call(
        _mem_kv_kernel,
        grid=(B,),
        in_specs=[pl.BlockSpec((None, M, D), lambda b: (b, 0, 0)), pl.BlockSpec(g.shape, lambda b: (0, 0)),
                  pl.BlockSpec(w.shape, lambda b: (0, 0))],
        out_specs=[pl.BlockSpec((None, M, D), lambda b: (b, 0, 0)), pl.BlockSpec((None, M, D), lambda b: (b, 0, 0))],
        out_shape=[jax.ShapeDtypeStruct((B, M, D), BF16), jax.ShapeDtypeStruct((B, M, D), BF16)],
        compiler_params=pltpu.CompilerParams(dimension_semantics=("arbitrary",), vmem_limit_bytes=VMEM_LIMIT),
        name="mem_kv",
    )(mem, g, w)


def _xattn_kernel(h_ref, g_ref, wq_ref, k_ref, v_ref, wo_ref, h_out):
    h = h_ref[...]
    d = h.shape[-1]
    hd = d // X_HEADS
    hn = _rms(h, g_ref[...]).astype(BF16)
    q = _dot(hn, wq_ref[...]).astype(BF16)
    outs = []
    for hh in range(X_HEADS):
        sl = slice(hh * hd, (hh + 1) * hd)
        s = _dot_nt(q[:, sl], k_ref[:, sl]) * (1.0 / math.sqrt(hd))
        m = jnp.max(s, axis=-1, keepdims=True)
        p = jnp.exp(s - m)
        p = p / jnp.sum(p, axis=-1, keepdims=True)
        outs.append(_dot(p.astype(BF16), v_ref[:, sl]))
    o = jnp.concatenate(outs, axis=-1).astype(BF16)
    h_out[...] = h + _dot(o, wo_ref[...])


def _xattn(h3, g, wq, kx, vx, wo):
    B, S, D = h3.shape
    M = kx.shape[1]
    ts = min(TS_IN, S)
    fixed = lambda b, i: (0, 0)
    return pl.pallas_call(
        _xattn_kernel,
        grid=(B, S // ts),
        in_specs=[pl.BlockSpec((None, ts, D), lambda b, i: (b, i, 0)), pl.BlockSpec(g.shape, fixed),
                  pl.BlockSpec(wq.shape, fixed), pl.BlockSpec((None, M, D), lambda b, i: (b, 0, 0)),
                  pl.BlockSpec((None, M, D), lambda b, i: (b, 0, 0)), pl.BlockSpec(wo.shape, fixed)],
        out_specs=pl.BlockSpec((None, ts, D), lambda b, i: (b, i, 0)),
        out_shape=jax.ShapeDtypeStruct((B, S, D), F32),
        compiler_params=pltpu.CompilerParams(dimension_semantics=("arbitrary", "arbitrary"),
                                             vmem_limit_bytes=VMEM_LIMIT),
        name="xattn",
    )(h3, g, wq, kx, vx, wo)


def _batcher_pairs(n):
    pairs = []
    p = 1
    while p < n:
        k = p
        while k >= 1:
            for j in range(k % p, n - k, 2 * k):
                for i in range(min(k, n - j - k)):
                    if (i + j) // (2 * p) == (i + j + k) // (2 * p):
                        pairs.append((i + j, i + j + k))
            k //= 2
        p *= 2
    return pairs


_SORT16 = _batcher_pairs(PEER_TOPK)
_ROW_LEN = [PEER_TOPK // (a + 1) for a in range(PEER_TOPK)]


def _sort_desc(v):
    v = list(v)
    for i, j in _SORT16:
        hi = jnp.maximum(v[i], v[j])
        lo = jnp.minimum(v[i], v[j])
        v[i], v[j] = hi, lo
    return v


def _bitonic_desc(v):
    v = list(v)
    n = len(v)
    d = n // 2
    while d >= 1:
        for k in range(n):
            if k & d == 0:
                hi = jnp.maximum(v[k], v[k + d])
                lo = jnp.minimum(v[k], v[k + d])
                v[k], v[k + d] = hi, lo
        d //= 2
    return v


def _merge_top(cur, other):
    n = len(cur)
    c = list(cur)
    for r, val in enumerate(other):
        c[n - 1 - r] = jnp.maximum(c[n - 1 - r], val)
    return _bitonic_desc(c)


def _top16_sorted(s):
    v = _sort_desc([s[k] for k in range(PEER_TOPK)])
    for shift in (4, 2, 1):
        other = [pltpu.roll(v[k], shift, axis=0) for k in range(PEER_TOPK)]
        v = _merge_top(v, other)
    return v


def _peer_route_kernel(h_ref, g_ref, wq_hi_ref, wq_lo_ref, key_hi_ref, key_lo_ref,
                       xn_out, r2_out, p_out, n_out, c_out,
                       st_ref, top_ref, res_ref):
    ts = h_ref.shape[0]
    n_chunk = ts // LANES
    hn = _rms(h_ref[...], g_ref[...])
    hn_hi, hn_lo = _split_bf16(hn)
    xn_out[...] = hn_hi
    q = _dot(hn_hi, wq_hi_ref[...]) + _dot(hn_hi, wq_lo_ref[...]) + _dot(hn_lo, wq_hi_ref[...])
    for h in range(PEER_HEADS):
        q_hi, q_lo = _split_bf16(q[:, h * LANES:(h + 1) * LANES])
        k_hi = key_hi_ref[h]
        k_lo = key_lo_ref[h]
        st = _dot_nt(k_hi, q_hi) + _dot_nt(k_hi, q_lo) + _dot_nt(k_lo, q_hi)
        for c in range(n_chunk):
            st_ref[c, h] = st[:, c * LANES:(c + 1) * LANES]

    neg_inf = jnp.float32(-jnp.inf)
    pos_inf = jnp.float32(jnp.inf)

    def chunk_body(c, _):
        def sort_body(h, _):
            for half in range(2):
                s = st_ref[c, h, pl.ds(half * PEER_KEYS, PEER_KEYS), :].reshape(PEER_TOPK, SUBLANES, LANES)
                v = _top16_sorted(s)
                for a in range(PEER_TOPK):
                    top_ref[half, a, pl.ds(h, 1), :] = v[a][0:1, :]
            return 0

        lax.fori_loop(0, PEER_HEADS, sort_body, 0)

        v1 = [top_ref[0, a] for a in range(PEER_TOPK)]
        v2 = [top_ref[1, b] for b in range(PEER_TOPK)]
        sums = [[v1[a] + v2[b] for b in range(_ROW_LEN[a])] for a in range(PEER_TOPK)]
        cur = sums[0]
        a = 1
        while _ROW_LEN[a] > 1:
            cur = _merge_top(cur, sums[a])
            a += 1
        cur = _merge_top(cur, [sums[r][0] for r in range(a, PEER_TOPK)])
        tau = cur[PEER_TOPK - 1]
        top_sum = sums[0][0]
        z = jnp.zeros_like(tau)
        for a in range(PEER_TOPK):
            cnt = jnp.zeros_like(tau)
            for b in range(_ROW_LEN[a]):
                sel = sums[a][b] >= tau
                cnt = cnt + jnp.where(sel, 1.0, 0.0)
                z = z + jnp.where(sel, jnp.exp(sums[a][b] - top_sum), 0.0)
            res_ref[a] = cnt
        res_ref[PEER_TOPK] = 1.0 / z

        def expand_body(h, _):
            s1 = st_ref[c, h, pl.ds(0, PEER_KEYS), :].reshape(PEER_TOPK, SUBLANES, LANES)
            s2 = st_ref[c, h, pl.ds(PEER_KEYS, PEER_KEYS), :].reshape(PEER_TOPK, SUBLANES, LANES)
            n = jnp.zeros(s1.shape, F32)
            r2 = jnp.full(s2.shape, float(PEER_TOPK), F32)
            for a in range(PEER_TOPK - 1, -1, -1):
                v1a = top_ref[0, a, pl.ds(h, 1), :]
                v2a = top_ref[1, a, pl.ds(h, 1), :]
                n = jnp.where(s1 == v1a, res_ref[a, pl.ds(h, 1), :], n)
                r2 = jnp.where(s2 == v2a, float(a), r2)
            m1 = top_ref[0, 0, pl.ds(h, 1), :]
            m2 = top_ref[1, 0, pl.ds(h, 1), :]
            inv_z = res_ref[PEER_TOPK, pl.ds(h, 1), :]
            n_out[c, h] = n.reshape(PEER_KEYS, LANES)
            c_out[c, h] = (jnp.exp(s1 - m1) * inv_z).reshape(PEER_KEYS, LANES)
            r2_out[c, h] = r2.reshape(PEER_KEYS, LANES)
            p_out[c, h] = jnp.exp(s2 - m2).reshape(PEER_KEYS, LANES)
            return 0

        lax.fori_loop(0, PEER_HEADS, expand_body, 0)
        return 0

    lax.fori_loop(0, n_chunk, chunk_body, 0)


def _peer_route(h2, g, wq_hi, wq_lo, key_hi, key_lo):
    T, D = h2.shape
    ts = min(TS_ROUTE, T)
    nc = ts // LANES
    aux_spec = pl.BlockSpec((nc, PEER_HEADS, PEER_KEYS, LANES), lambda i: (i, 0, 0, 0))
    aux_shape = jax.ShapeDtypeStruct((T // LANES, PEER_HEADS, PEER_KEYS, LANES), F32)
    fixed2 = lambda i: (0, 0)
    fixed3 = lambda i: (0, 0, 0)
    return pl.pallas_call(
        _peer_route_kernel,
        grid=(T // ts,),
        in_specs=[pl.BlockSpec((ts, D), lambda i: (i, 0)), pl.BlockSpec(g.shape, fixed2),
                  pl.BlockSpec(wq_hi.shape, fixed2), pl.BlockSpec(wq_lo.shape, fixed2),
                  pl.BlockSpec(key_hi.shape, fixed3), pl.BlockSpec(key_lo.shape, fixed3)],
        out_specs=[pl.BlockSpec((ts, D), lambda i: (i, 0)), aux_spec, aux_spec, aux_spec, aux_spec],
        out_shape=[jax.ShapeDtypeStruct((T, D), BF16), aux_shape, aux_shape, aux_shape, aux_shape],
        scratch_shapes=[pltpu.VMEM((nc, PEER_HEADS, 2 * PEER_KEYS, LANES), F32),
                        pltpu.VMEM((2, PEER_TOPK, SUBLANES, LANES), F32),
                        pltpu.VMEM((PEER_TOPK + 1, SUBLANES, LANES), F32)],
        compiler_params=pltpu.CompilerParams(dimension_semantics=("arbitrary",), vmem_limit_bytes=VMEM_LIMIT),
        name="peer_route",
    )(h2, g, wq_hi, wq_lo, key_hi, key_lo)


def _peer_ffn_kernel(xn_ref, u_ref, vt_ref, r2_ref, p_ref, n_ref, c_ref, h_ref, gfin_ref,
                     out_ref, acc_ref, ht_ref, r2s_ref, ps_ref, xs_ref, *, final_norm):
    e = pl.program_id(1)
    n_e = pl.num_programs(1)
    tt = xn_ref.shape[0]
    eb = u_ref.shape[0]
    n_i = eb // PEER_KEYS

    @pl.when(e == 0)
    def _():
        acc_ref[...] = jnp.zeros_like(acc_ref)
        for tc in range(tt // LANES):
            for h in range(PEER_HEADS):
                r2s_ref[tc, h] = r2_ref[tc, h].astype(BF16)
                ps_ref[tc, h] = p_ref[tc, h].astype(BF16)
        xs_ref[...] = xn_ref[...]

    at = _dot_nt(u_ref[...], xs_ref[...])
    pack = 2 * SUBLANES
    n_jv = PEER_KEYS // pack
    zero = jnp.zeros((pack, LANES), BF16)
    for tc in range(tt // LANES):
        lanes = slice(tc * LANES, (tc + 1) * LANES)
        for ii in range(n_i):
            g = [None] * n_jv
            for h in range(PEER_HEADS):
                n_b = jnp.broadcast_to(n_ref[tc, h, ii:ii + 1, :], (pack, LANES)).astype(BF16)
                c_b = jnp.broadcast_to(c_ref[tc, h, ii:ii + 1, :], (pack, LANES)).astype(BF16)
                for jv in range(n_jv):
                    js = slice(jv * pack, (jv + 1) * pack)
                    term = jnp.where(r2s_ref[tc, h, js, :] < n_b, ps_ref[tc, h, js, :], zero) * c_b
                    g[jv] = term if g[jv] is None else g[jv] + term
            for jv in range(n_jv):
                rows = slice(ii * PEER_KEYS + jv * pack, ii * PEER_KEYS + (jv + 1) * pack)
                ht_ref[rows, lanes] = jax.nn.gelu(at[rows, lanes]).astype(BF16) * g[jv]
    acc_ref[...] += _dot(vt_ref[...], ht_ref[...])

    @pl.when(e == n_e - 1)
    def _():
        res = h_ref[...] + acc_ref[...].T
        out_ref[...] = _rms(res, gfin_ref[...]) if final_norm else res


def _peer_ffn(xn, u_bf, vt_bf, r2, p, n, coef, h2, gfin, final_norm):
    T, D = h2.shape
    E = u_bf.shape[0]
    tt = min(TT_FFN, T)
    nc = tt // LANES
    eb = EB_FFN
    n_i = eb // PEER_KEYS
    tok = lambda t, e: (t, 0)
    aux_full = pl.BlockSpec((nc, PEER_HEADS, PEER_KEYS, LANES), lambda t, e: (t, 0, 0, 0))
    aux_rows = pl.BlockSpec((nc, PEER_HEADS, n_i, LANES), lambda t, e: (t, 0, e, 0))
    return pl.pallas_call(
        functools.partial(_peer_ffn_kernel, final_norm=final_norm),
        grid=(T // tt, E // eb),
        in_specs=[pl.BlockSpec((tt, D), tok), pl.BlockSpec((eb, D), lambda t, e: (e, 0)),
                  pl.BlockSpec((D, eb), lambda t, e: (0, e)), aux_full, aux_full, aux_rows, aux_rows,
                  pl.BlockSpec((tt, D), tok), pl.BlockSpec(gfin.shape, lambda t, e: (0, 0))],
        out_specs=pl.BlockSpec((tt, D), tok),
        out_shape=jax.ShapeDtypeStruct((T, D), F32),
        scratch_shapes=[pltpu.VMEM((D, tt), F32), pltpu.VMEM((eb, tt), BF16),
                        pltpu.VMEM((nc, PEER_HEADS, PEER_KEYS, LANES), BF16),
                        pltpu.VMEM((nc, PEER_HEADS, PEER_KEYS, LANES), BF16),
                        pltpu.VMEM((tt, D), BF16)],
        compiler_params=pltpu.CompilerParams(dimension_semantics=("arbitrary", "arbitrary"),
                                             vmem_limit_bytes=VMEM_LIMIT),
        name="peer_ffn",
    )(xn, u_bf, vt_bf, r2, p, n, coef, h2, gfin)


def _head_blocks(w, n_heads, width, pieces):
    w3 = w.reshape(w.shape[0], n_heads, width)
    out = jnp.zeros((w.shape[0], n_heads, LANES), w.dtype)
    for s0, s1, d0 in pieces:
        out = out.at[:, :, d0:d0 + (s1 - s0)].set(w3[:, :, s0:s1])
    return out.reshape(w.shape[0], n_heads * LANES)


def kernel(x, mem, positions, g_mix, w_in, g_q, w_uq, g_kv, w_ukv, conv_w, g_out, w_o, g_x, g_mem, w_xq,
           w_xkv, w_xo, g_ffn, w_pq, sub_keys, u_experts, v_experts, g_final):
    B, S, D = x.shape
    T = B * S
    depth = g_mix.shape[0]
    half = QK_ROPE // 2

    inv = ROPE_THETA ** (-jnp.arange(0, QK_ROPE, 2, dtype=F32) / QK_ROPE)
    ang = positions.astype(F32)[..., None] * inv
    cos = jnp.cos(ang).astype(x.dtype).reshape(T, half)
    sin = jnp.sin(ang).astype(x.dtype).reshape(T, half)
    ones = jnp.ones((T, QK_NOPE), F32)
    zeros_n = jnp.zeros((T, QK_NOPE), F32)
    pad_q = jnp.zeros((T, LANES - QK_NOPE - QK_ROPE), F32)
    pad_k = jnp.zeros((T, LANES - QK_ROPE), F32)
    cq_t = jnp.concatenate([ones, cos, cos, pad_q], axis=1)
    sq_t = jnp.concatenate([zeros_n, -sin, sin, pad_q], axis=1)
    ck_t = jnp.concatenate([cos, cos, pad_k], axis=1)
    sk_t = jnp.concatenate([-sin, sin, pad_k], axis=1)

    lane = jnp.arange(LANES)
    col = jnp.arange(MLA_HEADS * LANES)
    eplace = ((col[None, :] % LANES == lane[:, None] + QK_NOPE) & (lane[:, None] < QK_ROPE)).astype(BF16)
    mix_col = jnp.arange(D)
    gsum = (mix_col[:, None] // GROUP_DIM == lane[None, :]).astype(BF16)
    gexp = (lane[:, None] == mix_col[None, :] // GROUP_DIM).astype(BF16)

    h = x.reshape(T, D)
    for l in range(depth):
        o1 = Q_RANK
        o2 = o1 + KV_RANK
        o3 = o2 + QK_ROPE
        o4 = o3 + CONV_DIM
        o5 = o4 + CONV_DIM
        wl = w_in[l]
        w_kr = wl[:, o2:o3]
        w_krr = jnp.concatenate([w_kr[:, half:], w_kr[:, :half]], axis=1)
        lane_pad = jnp.zeros((D, LANES - QK_ROPE), wl.dtype)
        w1 = jnp.concatenate([wl[:, :o1], wl[:, o1:o2], w_kr, lane_pad, w_krr, lane_pad,
                              wl[:, o3:o4], wl[:, o4:o5], wl[:, o5:]], axis=1).astype(BF16)
        qw = QK_NOPE + QK_ROPE
        wq = _head_blocks(w_uq[l], MLA_HEADS, qw, [(0, qw, 0)]).astype(BF16)
        wqr = _head_blocks(w_uq[l], MLA_HEADS, qw,
                           [(QK_NOPE + half, qw, QK_NOPE), (QK_NOPE, QK_NOPE + half, QK_NOPE + half)]).astype(BF16)
        kvw = QK_NOPE + V_HEAD
        wk = _head_blocks(w_ukv[l], MLA_HEADS, kvw, [(0, QK_NOPE, 0)]).astype(BF16)
        wv = w_ukv[l].reshape(KV_RANK, MLA_HEADS, kvw)[:, :, QK_NOPE:].reshape(KV_RANK, MLA_HEADS * V_HEAD).astype(BF16)

        q, k, v, z, gb = _mixer_in(h, g_mix[l][None, :], w1, g_q[l][None, :], wq, wqr, g_kv[l][None, :], wk, wv,
                                   eplace, cq_t, sq_t, ck_t, sk_t)
        o = _mla_attn(q.reshape(B, S, -1), k.reshape(B, S, -1), v.reshape(B, S, -1))
        h = _mixer_out(o.reshape(T, -1), z, gb, h, conv_w[l], g_out[l][None, :], gsum, gexp,
                       w_o[l].astype(BF16), S)

        kx, vx = _mem_kv(mem, g_mem[l][None, :], w_xkv[l].astype(BF16))
        h = _xattn(h.reshape(B, S, D), g_x[l][None, :], w_xq[l].astype(BF16), kx, vx,
                   w_xo[l].astype(BF16)).reshape(T, D)

        wpq_hi, wpq_lo = _split_bf16(w_pq[l])
        sk = sub_keys[l]
        zk = jnp.zeros_like(sk[:, 0])
        keys_bd = jnp.concatenate([jnp.concatenate([sk[:, 0], zk], axis=-1),
                                   jnp.concatenate([zk, sk[:, 1]], axis=-1)], axis=1)
        key_hi, key_lo = _split_bf16(keys_bd)
        xn, r2, p, n, coef = _peer_route(h, g_ffn[l][None, :], wpq_hi, wpq_lo, key_hi, key_lo)
        h = _peer_ffn(xn, u_experts[l].astype(BF16), v_experts[l].T.astype(BF16), r2, p, n, coef, h,
                      g_final[None, :], final_norm=(l == depth - 1))
    return h.reshape(B, S, D)
```

```python
import functools
import math

import jax
import jax.numpy as jnp
from jax import lax
from jax.experimental import pallas as pl
from jax.experimental.pallas import tpu as pltpu

F32 = jnp.float32
BF16 = jnp.bfloat16

EPS = 1e-6
LANES = 128
SUBLANES = 8
VMEM_LIMIT = 56 * 1024 * 1024

MLA_HEADS = 8
QK_NOPE = 64
QK_ROPE = 32
V_HEAD = 64
Q_RANK = 384
KV_RANK = 256
CONV_DIM = 512
GROUP_DIM = 64
ROPE_THETA = 10000.0
X_HEADS = 4
PEER_HEADS = 8
PEER_KEYS = 128
PEER_TOPK = 16
HALF_Q = 64

TS_IN = 512
TQ = 512
TK = 512
TS_ROUTE = 256
TT_FFN = 512
EB_FFN = 1024
GATE_UNIT_SPLIT = (6, 9, 9, 8, 0)

NT_DIMS = (((1,), (1,)), ((), ()))


def _rms(x, g):
    return x * lax.rsqrt(jnp.mean(x * x, axis=-1, keepdims=True) + EPS) * g


def _split_bf16(x):
    hi = x.astype(BF16)
    lo = (x - hi.astype(F32)).astype(BF16)
    return hi, lo


def _dot(a, b):
    return jnp.dot(a, b, preferred_element_type=F32)


def _dot_nt(a, b):
    return lax.dot_general(a, b, NT_DIMS, preferred_element_type=F32)


_C_CQ = 0
_C_CKV = _C_CQ + Q_RANK
_C_KR = _C_CKV + KV_RANK
_C_KRR = _C_KR + LANES
_C_GB = _C_KRR + LANES
_C_GC = _C_GB + CONV_DIM
_C_HX = _C_GC + CONV_DIM
_C_END = _C_HX + CONV_DIM


def _mixer_in_kernel(x_ref, gmix_ref, w1_ref, gq_ref, wq_ref, wqr_ref, gkv_ref, wk_ref, wv_ref,
                     eplace_ref, cq_ref, sq_ref, ck_ref, sk_ref,
                     q_out, k_out, v_out, z_out, gb_out):
    xn = _rms(x_ref[...], gmix_ref[...]).astype(BF16)
    proj = _dot(xn, w1_ref[...])
    cq = proj[:, _C_CQ:_C_CKV]
    ckv = proj[:, _C_CKV:_C_KR]
    kr = proj[:, _C_KR:_C_KRR]
    krr = proj[:, _C_KRR:_C_GB]
    gb_out[...] = proj[:, _C_GB:_C_GC]
    z_out[...] = proj[:, _C_GC:_C_HX] * proj[:, _C_HX:_C_END]

    cqn = _rms(cq, gq_ref[...]).astype(BF16)
    q_raw = _dot(cqn, wq_ref[...])
    q_rot = _dot(cqn, wqr_ref[...])
    cq_t = cq_ref[...]
    sq_t = sq_ref[...]
    for h in range(MLA_HEADS):
        sl = slice(h * LANES, (h + 1) * LANES)
        q_out[:, sl] = (q_raw[:, sl] * cq_t + q_rot[:, sl] * sq_t).astype(BF16)

    ckvn = _rms(ckv, gkv_ref[...]).astype(BF16)
    kr_roped = (kr * ck_ref[...] + krr * sk_ref[...]).astype(BF16)
    k_out[...] = (_dot(ckvn, wk_ref[...]) + _dot(kr_roped, eplace_ref[...])).astype(BF16)
    v_out[...] = _dot(ckvn, wv_ref[...]).astype(BF16)


def _mixer_in(x2, gmix, w1, gq, wq, wqr, gkv, wk, wv, eplace, cq_t, sq_t, ck_t, sk_t):
    T, D = x2.shape
    ts = min(TS_IN, T)
    row = lambda i: (i, 0)
    fixed = lambda i: (0, 0)
    full = lambda a: pl.BlockSpec(a.shape, fixed)
    return pl.pallas_call(
        _mixer_in_kernel,
        grid=(T // ts,),
        in_specs=[pl.BlockSpec((ts, D), row), full(gmix), full(w1), full(gq), full(wq), full(wqr),
                  full(gkv), full(wk), full(wv), full(eplace),
                  pl.BlockSpec((ts, LANES), row), pl.BlockSpec((ts, LANES), row),
                  pl.BlockSpec((ts, LANES), row), pl.BlockSpec((ts, LANES), row)],
        out_specs=[pl.BlockSpec((ts, MLA_HEADS * LANES), row), pl.BlockSpec((ts, MLA_HEADS * LANES), row),
                   pl.BlockSpec((ts, MLA_HEADS * V_HEAD), row), pl.BlockSpec((ts, CONV_DIM), row),
                   pl.BlockSpec((ts, CONV_DIM), row)],
        out_shape=[jax.ShapeDtypeStruct((T, MLA_HEADS * LANES), BF16),
                   jax.ShapeDtypeStruct((T, MLA_HEADS * LANES), BF16),
                   jax.ShapeDtypeStruct((T, MLA_HEADS * V_HEAD), BF16),
                   jax.ShapeDtypeStruct((T, CONV_DIM), F32),
                   jax.ShapeDtypeStruct((T, CONV_DIM), F32)],
        compiler_params=pltpu.CompilerParams(dimension_semantics=("arbitrary",), vmem_limit_bytes=VMEM_LIMIT),
        name="mixer_in",
    )(x2, gmix, w1, gq, wq, wqr, gkv, wk, wv, eplace, cq_t, sq_t, ck_t, sk_t)


def _mla_attn_kernel(q_ref, k_ref, v_ref, o_ref, *, tq, tk):
    qi = pl.program_id(2)
    scale = 1.0 / math.sqrt(QK_NOPE + QK_ROPE)
    n_kt = (qi + 1) * (tq // tk)
    row_pos = qi * tq + lax.broadcasted_iota(jnp.int32, (tq, tk), 0)
    col_iota = lax.broadcasted_iota(jnp.int32, (tq, tk), 1)
    accs = []
    for hh in range(2):
        qh = q_ref[:, hh * LANES:(hh + 1) * LANES]

        def body(kt, carry, qh=qh, hh=hh):
            m, l, acc = carry
            k0 = pl.multiple_of(kt * tk, tk)
            kh = k_ref[pl.ds(k0, tk), hh * LANES:(hh + 1) * LANES]
            s = _dot_nt(qh, kh) * scale
            s = jnp.where(k0 + col_iota <= row_pos, s, -jnp.inf)
            m_new = jnp.maximum(m, jnp.max(s, axis=-1, keepdims=True))
            alpha = jnp.exp(m - m_new)
            p = jnp.exp(s - m_new)
            l_new = alpha * l + jnp.sum(p, axis=-1, keepdims=True)
            acc_new = alpha * acc + _dot(p.astype(BF16), v_ref[pl.ds(k0, tk), :])
            return m_new, l_new, acc_new

        m0 = jnp.full((tq, 1), -jnp.inf, F32)
        l0 = jnp.zeros((tq, 1), F32)
        a0 = jnp.zeros((tq, LANES), F32)
        m, l, acc = lax.fori_loop(0, n_kt, body, (m0, l0, a0))
        accs.append(acc / l)
    lane = lax.broadcasted_iota(jnp.int32, (tq, LANES), 1)
    o_ref[...] = jnp.where(lane < V_HEAD, accs[0], accs[1])


def _mla_attn(q3, k3, v3):
    B, S, _ = q3.shape
    tq = min(TQ, S)
    tk = min(TK, tq)
    return pl.pallas_call(
        functools.partial(_mla_attn_kernel, tq=tq, tk=tk),
        grid=(B, MLA_HEADS // 2, S // tq),
        in_specs=[pl.BlockSpec((None, tq, 2 * LANES), lambda b, g, i: (b, i, g)),
                  pl.BlockSpec((None, S, 2 * LANES), lambda b, g, i: (b, 0, g)),
                  pl.BlockSpec((None, S, 2 * V_HEAD), lambda b, g, i: (b, 0, g))],
        out_specs=pl.BlockSpec((None, tq, 2 * V_HEAD), lambda b, g, i: (b, i, g)),
        out_shape=jax.ShapeDtypeStruct((B, S, MLA_HEADS * V_HEAD), F32),
        compiler_params=pltpu.CompilerParams(dimension_semantics=("arbitrary", "arbitrary", "arbitrary"),
                                             vmem_limit_bytes=VMEM_LIMIT),
        name="mla_attn",
    )(q3, k3, v3)


def _mixer_out_kernel(o_ref, z_ref, zh_ref, gb_ref, x_ref, cw_ref, gout_ref, gsum_ref, gexp_ref, wo_ref,
                      h_out, *, tiles_per_seq):
    i = pl.program_id(0)
    ts = z_ref.shape[0]
    z = z_ref[...]
    halo = jnp.where(i % tiles_per_seq == 0, 0.0, zh_ref[...])
    row = lax.broadcasted_iota(jnp.int32, z.shape, 0)
    z1 = jnp.where(row == 0, halo[7:8, :], pltpu.roll(z, 1, axis=0))
    z2 = jnp.where(row == 0, halo[6:7, :], jnp.where(row == 1, halo[7:8, :], pltpu.roll(z, 2, axis=0)))
    cw = cw_ref[...]
    y_conv = gb_ref[...] * (cw[0:1, :] * z2 + cw[1:2, :] * z1 + cw[2:3, :] * z)
    y = jnp.concatenate([o_ref[...], y_conv], axis=-1)
    sq_hi, sq_lo = _split_bf16(y * y)
    gs = _dot(sq_hi, gsum_ref[...]) + _dot(sq_lo, gsum_ref[...])
    r = lax.rsqrt(gs * (1.0 / GROUP_DIM) + EPS)
    r_hi, r_lo = _split_bf16(r)
    r_full = _dot(r_hi, gexp_ref[...]) + _dot(r_lo, gexp_ref[...])
    yn = (y * r_full * gout_ref[...]).astype(BF16)
    h_out[...] = x_ref[...] + _dot(yn, wo_ref[...])


def _mixer_out(o2, z, gb, x2, conv_w, gout, gsum, gexp, wo, seq):
    T, D = x2.shape
    ts = min(TS_IN, seq)
    row = lambda i: (i, 0)
    fixed = lambda i: (0, 0)
    full = lambda a: pl.BlockSpec(a.shape, fixed)
    halo_blocks = ts // SUBLANES
    return pl.pallas_call(
        functools.partial(_mixer_out_kernel, tiles_per_seq=seq // ts),
        grid=(T // ts,),
        in_specs=[pl.BlockSpec((ts, MLA_HEADS * V_HEAD), row), pl.BlockSpec((ts, CONV_DIM), row),
                  pl.BlockSpec((SUBLANES, CONV_DIM), lambda i: (jnp.maximum(i * halo_blocks - 1, 0), 0)),
                  pl.BlockSpec((ts, CONV_DIM), row), pl.BlockSpec((ts, D), row),
                  full(conv_w), full(gout), full(gsum), full(gexp), full(wo)],
        out_specs=pl.BlockSpec((ts, D), row),
        out_shape=jax.ShapeDtypeStruct((T, D), F32),
        compiler_params=pltpu.CompilerParams(dimension_semantics=("arbitrary",), vmem_limit_bytes=VMEM_LIMIT),
        name="mixer_out",
    )(o2, z, z, gb, x2, conv_w, gout, gsum, gexp, wo)


def _mem_kv_kernel(mem_ref, g_ref, w_ref, k_out, v_out):
    d = mem_ref.shape[-1]
    mn = _rms(mem_ref[...], g_ref[...]).astype(BF16)
    kv = _dot(mn, w_ref[...])
    k_out[...] = kv[:, :d].astype(BF16)
    v_out[...] = kv[:, d:].astype(BF16)


def _mem_kv(mem, g, w):
    B, M, D = mem.shape
    return pl.pallas_call(
        _mem_kv_kernel,
        grid=(B,),
        in_specs=[pl.BlockSpec((None, M, D), lambda b: (b, 0, 0)), pl.BlockSpec(g.shape, lambda b: (0, 0)),
                  pl.BlockSpec(w.shape, lambda b: (0, 0))],
        out_specs=[pl.BlockSpec((None, M, D), lambda b: (b, 0, 0)), pl.BlockSpec((None, M, D), lambda b: (b, 0, 0))],
        out_shape=[jax.ShapeDtypeStruct((B, M, D), BF16), jax.ShapeDtypeStruct((B, M, D), BF16)],
        compiler_params=pltpu.CompilerParams(dimension_semantics=("arbitrary",), vmem_limit_bytes=VMEM_LIMIT),
        name="mem_kv",
    )(mem, g, w)


def _xattn_kernel(h_ref, g_ref, wq_ref, k_ref, v_ref, wo_ref, h_out):
    h = h_ref[...]
    d = h.shape[-1]
    hd = d // X_HEADS
    hn = _rms(h, g_ref[...]).astype(BF16)
    q = _dot(hn, wq_ref[...]).astype(BF16)
    outs = []
    for hh in range(X_HEADS):
        sl = slice(hh * hd, (hh + 1) * hd)
        s = _dot_nt(q[:, sl], k_ref[:, sl]) * (1.0 / math.sqrt(hd))
        m = jnp.max(s, axis=-1, keepdims=True)
        p = jnp.exp(s - m)
        p = p / jnp.sum(p, axis=-1, keepdims=True)
        outs.append(_dot(p.astype(BF16), v_ref[:, sl]))
    o = jnp.concatenate(outs, axis=-1).astype(BF16)
    h_out[...] = h + _dot(o, wo_ref[...])


def _xattn(h3, g, wq, kx, vx, wo):
    B, S, D = h3.shape
    M = kx.shape[1]
    ts = min(TS_IN, S)
    fixed = lambda b, i: (0, 0)
    return pl.pallas_call(
        _xattn_kernel,
        grid=(B, S // ts),
        in_specs=[pl.BlockSpec((None, ts, D), lambda b, i: (b, i, 0)), pl.BlockSpec(g.shape, fixed),
                  pl.BlockSpec(wq.shape, fixed), pl.BlockSpec((None, M, D), lambda b, i: (b, 0, 0)),
                  pl.BlockSpec((None, M, D), lambda b, i: (b, 0, 0)), pl.BlockSpec(wo.shape, fixed)],
        out_specs=pl.BlockSpec((None, ts, D), lambda b, i: (b, i, 0)),
        out_shape=jax.ShapeDtypeStruct((B, S, D), F32),
        compiler_params=pltpu.CompilerParams(dimension_semantics=("arbitrary", "arbitrary"),
                                             vmem_limit_bytes=VMEM_LIMIT),
        name="xattn",
    )(h3, g, wq, kx, vx, wo)


def _batcher_pairs(n):
    pairs = []
    p = 1
    while p < n:
        k = p
        while k >= 1:
            for j in range(k % p, n - k, 2 * k):
                for i in range(min(k, n - j - k)):
                    if (i + j) // (2 * p) == (i + j + k) // (2 * p):
                        pairs.append((i + j, i + j + k))
            k //= 2
        p *= 2
    return pairs


_SORT16 = _batcher_pairs(PEER_TOPK)
_ROW_LEN = [PEER_TOPK // (a + 1) for a in range(PEER_TOPK)]


def _sort_desc(v):
    v = list(v)
    for i, j in _SORT16:
        hi = jnp.maximum(v[i], v[j])
        lo = jnp.minimum(v[i], v[j])
        v[i], v[j] = hi, lo
    return v


def _bitonic_desc(v):
    v = list(v)
    n = len(v)
    d = n // 2
    while d >= 1:
        for k in range(n):
            if k & d == 0:
                hi = jnp.maximum(v[k], v[k + d])
                lo = jnp.minimum(v[k], v[k + d])
                v[k], v[k + d] = hi, lo
        d //= 2
    return v


def _merge_top(cur, other):
    n = len(cur)
    c = list(cur)
    for r, val in enumerate(other):
        c[n - 1 - r] = jnp.maximum(c[n - 1 - r], val)
    return _bitonic_desc(c)


def _top16_sorted(s):
    v = _sort_desc([s[k] for k in range(PEER_TOPK)])
    for shift in (4, 2, 1):
        other = [pltpu.roll(v[k], shift, axis=0) for k in range(PEER_TOPK)]
        v = _merge_top(v, other)
    return v


def _peer_route_kernel(h_ref, g_ref, wq_hi_ref, wq_lo_ref, key_hi_ref, key_lo_ref,
                       xn_out, r2_out, p_out, n_out, c_out,
                       st_ref, top_ref, res_ref):
    ts = h_ref.shape[0]
    n_chunk = ts // LANES
    hn = _rms(h_ref[...], g_ref[...])
    hn_hi, hn_lo = _split_bf16(hn)
    xn_out[...] = hn_hi
    q = _dot(hn_hi, wq_hi_ref[...]) + _dot(hn_hi, wq_lo_ref[...]) + _dot(hn_lo, wq_hi_ref[...])
    for h in range(PEER_HEADS):
        q_hi, q_lo = _split_bf16(q[:, h * LANES:(h + 1) * LANES])
        k_hi = key_hi_ref[h]
        k_lo = key_lo_ref[h]
        st = _dot_nt(k_hi, q_hi) + _dot_nt(k_hi, q_lo) + _dot_nt(k_lo, q_hi)
        for c in range(n_chunk):
            st_ref[c, h] = st[:, c * LANES:(c + 1) * LANES]

    neg_inf = jnp.float32(-jnp.inf)
    pos_inf = jnp.float32(jnp.inf)

    def chunk_body(c, _):
        def sort_body(h, _):
            for half in range(2):
                s = st_ref[c, h, pl.ds(half * PEER_KEYS, PEER_KEYS), :].reshape(PEER_TOPK, SUBLANES, LANES)
                v = _top16_sorted(s)
                for a in range(PEER_TOPK):
                    top_ref[half, a, pl.ds(h, 1), :] = v[a][0:1, :]
            return 0

        lax.fori_loop(0, PEER_HEADS, sort_body, 0)

        v1 = [top_ref[0, a] for a in range(PEER_TOPK)]
        v2 = [top_ref[1, b] for b in range(PEER_TOPK)]
        sums = [[v1[a] + v2[b] for b in range(_ROW_LEN[a])] for a in range(PEER_TOPK)]
        cur = sums[0]
        a = 1
        while _ROW_LEN[a] > 1:
            cur = _merge_top(cur, sums[a])
            a += 1
        cur = _merge_top(cur, [sums[r][0] for r in range(a, PEER_TOPK)])
        tau = cur[PEER_TOPK - 1]
        top_sum = sums[0][0]
        z = jnp.zeros_like(tau)
        for a in range(PEER_TOPK):
            cnt = jnp.zeros_like(tau)
            for b in range(_ROW_LEN[a]):
                sel = sums[a][b] >= tau
                cnt = cnt + jnp.where(sel, 1.0, 0.0)
                z = z + jnp.where(sel, jnp.exp(sums[a][b] - top_sum), 0.0)
            res_ref[a] = cnt
        res_ref[PEER_TOPK] = 1.0 / z

        def expand_body(h, _):
            s1 = st_ref[c, h, pl.ds(0, PEER_KEYS), :].reshape(PEER_TOPK, SUBLANES, LANES)
            s2 = st_ref[c, h, pl.ds(PEER_KEYS, PEER_KEYS), :].reshape(PEER_TOPK, SUBLANES, LANES)
            n = jnp.zeros(s1.shape, F32)
            r2 = jnp.full(s2.shape, float(PEER_TOPK), F32)
            for a in range(PEER_TOPK - 1, -1, -1):
                v1a = top_ref[0, a, pl.ds(h, 1), :]
                v2a = top_ref[1, a, pl.ds(h, 1), :]
                n = jnp.where(s1 == v1a, res_ref[a, pl.ds(h, 1), :], n)
                r2 = jnp.where(s2 == v2a, float(a), r2)
            m1 = top_ref[0, 0, pl.ds(h, 1), :]
            m2 = top_ref[1, 0, pl.ds(h, 1), :]
            inv_z = res_ref[PEER_TOPK, pl.ds(h, 1), :]
            n_out[c, h] = n.reshape(PEER_KEYS, LANES)
            c_out[c, h] = (jnp.exp(s1 - m1) * inv_z).reshape(PEER_KEYS, LANES)
            r2_out[c, h] = r2.reshape(PEER_KEYS, LANES)
            p_out[c, h] = jnp.exp(s2 - m2).reshape(PEER_KEYS, LANES)
            return 0

        lax.fori_loop(0, PEER_HEADS, expand_body, 0)
        return 0

    lax.fori_loop(0, n_chunk, chunk_body, 0)


def _peer_route(h2, g, wq_hi, wq_lo, key_hi, key_lo):
    T, D = h2.shape
    ts = min(TS_ROUTE, T)
    nc = ts // LANES
    aux_spec = pl.BlockSpec((nc, PEER_HEADS, PEER_KEYS, LANES), lambda i: (i, 0, 0, 0))
    aux_shape = jax.ShapeDtypeStruct((T // LANES, PEER_HEADS, PEER_KEYS, LANES), F32)
    fixed2 = lambda i: (0, 0)
    fixed3 = lambda i: (0, 0, 0)
    return pl.pallas_call(
        _peer_route_kernel,
        grid=(T // ts,),
        in_specs=[pl.BlockSpec((ts, D), lambda i: (i, 0)), pl.BlockSpec(g.shape, fixed2),
                  pl.BlockSpec(wq_hi.shape, fixed2), pl.BlockSpec(wq_lo.shape, fixed2),
                  pl.BlockSpec(key_hi.shape, fixed3), pl.BlockSpec(key_lo.shape, fixed3)],
        out_specs=[pl.BlockSpec((ts, D), lambda i: (i, 0)), aux_spec, aux_spec, aux_spec, aux_spec],
        out_shape=[jax.ShapeDtypeStruct((T, D), BF16), aux_shape, aux_shape, aux_shape, aux_shape],
        scratch_shapes=[pltpu.VMEM((nc, PEER_HEADS, 2 * PEER_KEYS, LANES), F32),
                        pltpu.VMEM((2, PEER_TOPK, SUBLANES, LANES), F32),
                        pltpu.VMEM((PEER_TOPK + 1, SUBLANES, LANES), F32)],
        compiler_params=pltpu.CompilerParams(dimension_semantics=("arbitrary",), vmem_limit_bytes=VMEM_LIMIT),
        name="peer_route",
    )(h2, g, wq_hi, wq_lo, key_hi, key_lo)


def _pack_experts_kernel(u_ref, v_ref, u_out, vt_out):
    u_out[...] = pltpu.bitcast(u_ref[...].astype(BF16), jnp.uint32)
    vt_out[...] = pltpu.bitcast(v_ref[...].T.astype(BF16), jnp.uint32)


def _pack_experts(u, v):
    E, D = u.shape
    eb = EB_FFN
    return pl.pallas_call(
        _pack_experts_kernel,
        grid=(E // eb,),
        in_specs=[pl.BlockSpec((eb, D), lambda e: (e, 0)), pl.BlockSpec((eb, D), lambda e: (e, 0))],
        out_specs=[pl.BlockSpec((eb // 2, D), lambda e: (e, 0)), pl.BlockSpec((D // 2, eb), lambda e: (0, e))],
        out_shape=[jax.ShapeDtypeStruct((E // 2, D), jnp.uint32), jax.ShapeDtypeStruct((D // 2, E), jnp.uint32)],
        compiler_params=pltpu.CompilerParams(dimension_semantics=("arbitrary",), vmem_limit_bytes=VMEM_LIMIT),
        name="pack_experts",
    )(u, v)


def _peer_gate_unit(tc, ii, at_ref, ht_ref, r2s_ref, ps_ref, n_ref, c_ref):
    pack = 2 * SUBLANES
    n_jv = PEER_KEYS // pack
    zero = jnp.zeros((pack, LANES), BF16)
    lanes = slice(tc * LANES, (tc + 1) * LANES)
    g = [None] * n_jv
    for h in range(PEER_HEADS):
        n_b = jnp.broadcast_to(n_ref[tc, h, ii:ii + 1, :], (pack, LANES)).astype(BF16)
        c_b = jnp.broadcast_to(c_ref[tc, h, ii:ii + 1, :], (pack, LANES)).astype(BF16)
        for jv in range(n_jv):
            js = slice(jv * pack, (jv + 1) * pack)
            term = jnp.where(r2s_ref[tc, h, js, :] < n_b, ps_ref[tc, h, js, :], zero) * c_b
            g[jv] = term if g[jv] is None else g[jv] + term
    for jv in range(n_jv):
        rows = slice(ii * PEER_KEYS + jv * pack, ii * PEER_KEYS + (jv + 1) * pack)
        ht_ref[rows, lanes] = jax.nn.gelu(at_ref[rows, lanes]).astype(BF16) * g[jv]


def _peer_ffn_kernel(xn_ref, u_ref, vt_ref, r2_ref, p_ref, n_ref, c_ref, h_ref, gfin_ref,
                     out_ref, acc_ref, at0_ref, at1_ref, ht0_ref, ht1_ref, r2s_ref, ps_ref, xs_ref,
                     *, n_e, n_blocks, final_norm):
    g = pl.program_id(0)
    tt = xn_ref.shape[0]
    e_score = g % n_e
    e_gate = jnp.maximum(g - 1, 0) % n_e
    e_down = jnp.maximum(g - 2, 0) % n_e

    @pl.when(g == 0)
    def _():
        at1_ref[...] = jnp.zeros_like(at1_ref)
        ht0_ref[...] = jnp.zeros_like(ht0_ref)
        ht1_ref[...] = jnp.zeros_like(ht1_ref)
        acc_ref[...] = jnp.zeros_like(acc_ref)

    @pl.when((g < n_blocks) & (e_score == 0))
    def _():
        xs_ref[...] = xn_ref[...]

    @pl.when((g <= n_blocks) & (e_gate == 0))
    def _():
        for tc in range(tt // LANES):
            for h in range(PEER_HEADS):
                r2s_ref[tc, h] = r2_ref[tc, h].astype(BF16)
                ps_ref[tc, h] = p_ref[tc, h].astype(BF16)

    @pl.when((g >= 2) & (e_down == 0))
    def _():
        acc_ref[...] = jnp.zeros_like(acc_ref)

    def stages(at_w, at_r, ht_w, ht_r):
        u_blk = pltpu.bitcast(u_ref[...], BF16)
        vt_blk = pltpu.bitcast(vt_ref[...], BF16)
        eb = u_blk.shape[0]
        mxu_n = 2 * LANES
        units = [(tc, ii) for tc in range(tt // LANES) for ii in range(eb // PEER_KEYS)]
        chunks = [("score", c) for c in range(tt // mxu_n)] + [("down", c) for c in range(tt // mxu_n)]
        split = GATE_UNIT_SPLIT
        assert len(split) == len(chunks) + 1 and sum(split) == len(units)
        bounds = [sum(split[:k]) for k in range(len(split) + 1)]

        def gate_units(k):
            for tc, ii in units[bounds[k]:bounds[k + 1]]:
                _peer_gate_unit(tc, ii, at_r, ht_w, r2s_ref, ps_ref, n_ref, c_ref)

        gate_units(0)
        for k, (kind, c) in enumerate(chunks):
            tok = slice(c * mxu_n, (c + 1) * mxu_n)
            if kind == "down":
                acc_ref[:, tok] += _dot(vt_blk, ht_r[:, tok])
            else:
                at_w[:, tok] = _dot_nt(u_blk, xs_ref[tok, :])
            gate_units(k + 1)

    @pl.when(g % 2 == 0)
    def _():
        stages(at0_ref, at1_ref, ht1_ref, ht0_ref)

    @pl.when(g % 2 == 1)
    def _():
        stages(at1_ref, at0_ref, ht0_ref, ht1_ref)

    @pl.when((g >= 2) & (e_down == n_e - 1))
    def _():
        res = h_ref[...] + acc_ref[...].T
        out_ref[...] = _rms(res, gfin_ref[...]) if final_norm else res


def _peer_ffn(xn, u_pack, vt_pack, r2, p, n, coef, h2, gfin, final_norm):
    T, D = h2.shape
    E = vt_pack.shape[1]
    tt = min(TT_FFN, T)
    nc = tt // LANES
    eb = EB_FFN
    n_i = eb // PEER_KEYS
    n_e = E // eb
    n_blocks = (T // tt) * n_e

    def block(lag):
        def split(g):
            b = jnp.clip(g - lag, 0, n_blocks - 1)
            return b // n_e, b % n_e
        return split

    score, gate, down = block(0), block(1), block(2)
    aux_shape = (nc, PEER_HEADS, PEER_KEYS, LANES)
    row_shape = (nc, PEER_HEADS, n_i, LANES)
    return pl.pallas_call(
        functools.partial(_peer_ffn_kernel, n_e=n_e, n_blocks=n_blocks, final_norm=final_norm),
        grid=(n_blocks + 2,),
        in_specs=[pl.BlockSpec((tt, D), lambda g: (score(g)[0], 0)),
                  pl.BlockSpec((eb // 2, D), lambda g: (score(g)[1], 0)),
                  pl.BlockSpec((D // 2, eb), lambda g: (0, down(g)[1])),
                  pl.BlockSpec(aux_shape, lambda g: (gate(g)[0], 0, 0, 0)),
                  pl.BlockSpec(aux_shape, lambda g: (gate(g)[0], 0, 0, 0)),
                  pl.BlockSpec(row_shape, lambda g: (gate(g)[0], 0, gate(g)[1], 0)),
                  pl.BlockSpec(row_shape, lambda g: (gate(g)[0], 0, gate(g)[1], 0)),
                  pl.BlockSpec((tt, D), lambda g: (down(g)[0], 0)),
                  pl.BlockSpec(gfin.shape, lambda g: (0, 0))],
        out_specs=pl.BlockSpec((tt, D), lambda g: (down(g)[0], 0)),
        out_shape=jax.ShapeDtypeStruct((T, D), F32),
        scratch_shapes=[pltpu.VMEM((D, tt), F32),
                        pltpu.VMEM((eb, tt), F32), pltpu.VMEM((eb, tt), F32),
                        pltpu.VMEM((eb, tt), BF16), pltpu.VMEM((eb, tt), BF16),
                        pltpu.VMEM(aux_shape, BF16), pltpu.VMEM(aux_shape, BF16),
                        pltpu.VMEM((tt, D), BF16)],
        compiler_params=pltpu.CompilerParams(dimension_semantics=("arbitrary",), vmem_limit_bytes=VMEM_LIMIT),
        name="peer_ffn",
    )(xn, u_pack, vt_pack, r2, p, n, coef, h2, gfin)


def _head_blocks(w, n_heads, width, pieces):
    w3 = w.reshape(w.shape[0], n_heads, width)
    out = jnp.zeros((w.shape[0], n_heads, LANES), w.dtype)
    for s0, s1, d0 in pieces:
        out = out.at[:, :, d0:d0 + (s1 - s0)].set(w3[:, :, s0:s1])
    return out.reshape(w.shape[0], n_heads * LANES)


def kernel(x, mem, positions, g_mix, w_in, g_q, w_uq, g_kv, w_ukv, conv_w, g_out, w_o, g_x, g_mem, w_xq,
           w_xkv, w_xo, g_ffn, w_pq, sub_keys, u_experts, v_experts, g_final):
    B, S, D = x.shape
    T = B * S
    depth = g_mix.shape[0]
    half = QK_ROPE // 2

    inv = ROPE_THETA ** (-jnp.arange(0, QK_ROPE, 2, dtype=F32) / QK_ROPE)
    ang = positions.astype(F32)[..., None] * inv
    cos = jnp.cos(ang).astype(x.dtype).reshape(T, half)
    sin = jnp.sin(ang).astype(x.dtype).reshape(T, half)
    ones = jnp.ones((T, QK_NOPE), F32)
    zeros_n = jnp.zeros((T, QK_NOPE), F32)
    pad_q = jnp.zeros((T, LANES - QK_NOPE - QK_ROPE), F32)
    pad_k = jnp.zeros((T, LANES - QK_ROPE), F32)
    cq_t = jnp.concatenate([ones, cos, cos, pad_q], axis=1)
    sq_t = jnp.concatenate([zeros_n, -sin, sin, pad_q], axis=1)
    ck_t = jnp.concatenate([cos, cos, pad_k], axis=1)
    sk_t = jnp.concatenate([-sin, sin, pad_k], axis=1)

    lane = jnp.arange(LANES)
    col = jnp.arange(MLA_HEADS * LANES)
    eplace = ((col[None, :] % LANES == lane[:, None] + QK_NOPE) & (lane[:, None] < QK_ROPE)).astype(BF16)
    mix_col = jnp.arange(D)
    gsum = (mix_col[:, None] // GROUP_DIM == lane[None, :]).astype(BF16)
    gexp = (lane[:, None] == mix_col[None, :] // GROUP_DIM).astype(BF16)

    h = x.reshape(T, D)
    for l in range(depth):
        o1 = Q_RANK
        o2 = o1 + KV_RANK
        o3 = o2 + QK_ROPE
        o4 = o3 + CONV_DIM
        o5 = o4 + CONV_DIM
        wl = w_in[l]
        w_kr = wl[:, o2:o3]
        w_krr = jnp.concatenate([w_kr[:, half:], w_kr[:, :half]], axis=1)
        lane_pad = jnp.zeros((D, LANES - QK_ROPE), wl.dtype)
        w1 = jnp.concatenate([wl[:, :o1], wl[:, o1:o2], w_kr, lane_pad, w_krr, lane_pad,
                              wl[:, o3:o4], wl[:, o4:o5], wl[:, o5:]], axis=1).astype(BF16)
        qw = QK_NOPE + QK_ROPE
        wq = _head_blocks(w_uq[l], MLA_HEADS, qw, [(0, qw, 0)]).astype(BF16)
        wqr = _head_blocks(w_uq[l], MLA_HEADS, qw,
                           [(QK_NOPE + half, qw, QK_NOPE), (QK_NOPE, QK_NOPE + half, QK_NOPE + half)]).astype(BF16)
        kvw = QK_NOPE + V_HEAD
        wk = _head_blocks(w_ukv[l], MLA_HEADS, kvw, [(0, QK_NOPE, 0)]).astype(BF16)
        wv = w_ukv[l].reshape(KV_RANK, MLA_HEADS, kvw)[:, :, QK_NOPE:].reshape(KV_RANK, MLA_HEADS * V_HEAD).astype(BF16)

        q, k, v, z, gb = _mixer_in(h, g_mix[l][None, :], w1, g_q[l][None, :], wq, wqr, g_kv[l][None, :], wk, wv,
                                   eplace, cq_t, sq_t, ck_t, sk_t)
        o = _mla_attn(q.reshape(B, S, -1), k.reshape(B, S, -1), v.reshape(B, S, -1))
        h = _mixer_out(o.reshape(T, -1), z, gb, h, conv_w[l], g_out[l][None, :], gsum, gexp,
                       w_o[l].astype(BF16), S)

        kx, vx = _mem_kv(mem, g_mem[l][None, :], w_xkv[l].astype(BF16))
        h = _xattn(h.reshape(B, S, D), g_x[l][None, :], w_xq[l].astype(BF16), kx, vx,
                   w_xo[l].astype(BF16)).reshape(T, D)

        wpq_hi, wpq_lo = _split_bf16(w_pq[l])
        sk = sub_keys[l]
        zk = jnp.zeros_like(sk[:, 0])
        keys_bd = jnp.concatenate([jnp.concatenate([sk[:, 0], zk], axis=-1),
                                   jnp.concatenate([zk, sk[:, 1]], axis=-1)], axis=1)
        key_hi, key_lo = _split_bf16(keys_bd)
        xn, r2, p, n, coef = _peer_route(h, g_ffn[l][None, :], wpq_hi, wpq_lo, key_hi, key_lo)
        u_pack, vt_pack = _pack_experts(u_experts[l], v_experts[l])
        h = _peer_ffn(xn, u_pack, vt_pack, r2, p, n, coef, h,
                      g_final[None, :], final_norm=(l == depth - 1))
    return h.reshape(B, S, D)
```

```python
import functools
import math

import jax
import jax.numpy as jnp
from jax import lax
from jax.experimental import pallas as pl
from jax.experimental.pallas import tpu as pltpu

F32 = jnp.float32
BF16 = jnp.bfloat16

EPS = 1e-6
LANES = 128
SUBLANES = 8
VMEM_LIMIT = 56 * 1024 * 1024

MLA_HEADS = 8
QK_NOPE = 64
QK_ROPE = 32
V_HEAD = 64
Q_RANK = 384
KV_RANK = 256
CONV_DIM = 512
GROUP_DIM = 64
ROPE_THETA = 10000.0
X_HEADS = 4
PEER_HEADS = 8
PEER_KEYS = 128
PEER_TOPK = 16
HALF_Q = 64

TS_IN = 512
TQ = 256
TS_ROUTE = 256
TT_FFN = 512
EB_FFN = 1024
NT_DIMS = (((1,), (1,)), ((), ()))


def _rms(x, g):
    return x * lax.rsqrt(jnp.mean(x * x, axis=-1, keepdims=True) + EPS) * g


def _split_bf16(x):
    hi = x.astype(BF16)
    lo = (x - hi.astype(F32)).astype(BF16)
    return hi, lo


def _gelu_tanh(x):
    c0 = math.sqrt(2.0 / math.pi)
    inner = x * (c0 + (c0 * 0.044715) * (x * x))
    return x * (0.5 + 0.5 * jnp.tanh(inner))


def _dot(a, b):
    return jnp.dot(a, b, preferred_element_type=F32)


def _dot_nt(a, b):
    return lax.dot_general(a, b, NT_DIMS, preferred_element_type=F32)


_C_CQ = 0
_C_CKV = _C_CQ + Q_RANK
_C_KR = _C_CKV + KV_RANK
_C_KRR = _C_KR + LANES
_C_GB = _C_KRR + LANES
_C_GC = _C_GB + CONV_DIM
_C_HX = _C_GC + CONV_DIM
_C_END = _C_HX + CONV_DIM


def _mixer_in_kernel(x_ref, gmix_ref, w1_ref, gq_ref, wq_ref, wqr_ref, gkv_ref, wk_ref, wv_ref, vone_ref,
                     eplace_ref, cq_ref, sq_ref, ck_ref, sk_ref,
                     q_out, k_out, v_out, z_out, gb_out):
    xn = _rms(x_ref[...], gmix_ref[...]).astype(BF16)
    proj = _dot(xn, w1_ref[...])
    cq = proj[:, _C_CQ:_C_CKV]
    ckv = proj[:, _C_CKV:_C_KR]
    kr = proj[:, _C_KR:_C_KRR]
    krr = proj[:, _C_KRR:_C_GB]
    gb_out[...] = proj[:, _C_GB:_C_GC]
    z_out[...] = proj[:, _C_GC:_C_HX] * proj[:, _C_HX:_C_END]

    cqn = _rms(cq, gq_ref[...]).astype(BF16)
    q_raw = _dot(cqn, wq_ref[...])
    q_rot = _dot(cqn, wqr_ref[...])
    cq_t = cq_ref[...]
    sq_t = sq_ref[...]
    for h in range(MLA_HEADS):
        sl = slice(h * LANES, (h + 1) * LANES)
        q_out[:, sl] = (q_raw[:, sl] * cq_t + q_rot[:, sl] * sq_t).astype(BF16)

    ckvn = _rms(ckv, gkv_ref[...]).astype(BF16)
    kr_roped = (kr * ck_ref[...] + krr * sk_ref[...]).astype(BF16)
    k_out[...] = (_dot(ckvn, wk_ref[...]) + _dot(kr_roped, eplace_ref[...])).astype(BF16)
    v_out[...] = (_dot(ckvn, wv_ref[...]) + vone_ref[...]).astype(BF16)


def _mixer_in(x2, gmix, w1, gq, wq, wqr, gkv, wk, wv, vone, eplace, cq_t, sq_t, ck_t, sk_t):
    T, D = x2.shape
    ts = min(TS_IN, T)
    row = lambda i: (i, 0)
    fixed = lambda i: (0, 0)
    full = lambda a: pl.BlockSpec(a.shape, fixed)
    return pl.pallas_call(
        _mixer_in_kernel,
        grid=(T // ts,),
        in_specs=[pl.BlockSpec((ts, D), row), full(gmix), full(w1), full(gq), full(wq), full(wqr),
                  full(gkv), full(wk), full(wv), full(vone), full(eplace),
                  pl.BlockSpec((ts, LANES), row), pl.BlockSpec((ts, LANES), row),
                  pl.BlockSpec((ts, LANES), row), pl.BlockSpec((ts, LANES), row)],
        out_specs=[pl.BlockSpec((ts, MLA_HEADS * LANES), row), pl.BlockSpec((ts, MLA_HEADS * LANES), row),
                   pl.BlockSpec((ts, MLA_HEADS * LANES), row), pl.BlockSpec((ts, CONV_DIM), row),
                   pl.BlockSpec((ts, CONV_DIM), row)],
        out_shape=[jax.ShapeDtypeStruct((T, MLA_HEADS * LANES), BF16),
                   jax.ShapeDtypeStruct((T, MLA_HEADS * LANES), BF16),
                   jax.ShapeDtypeStruct((T, MLA_HEADS * LANES), BF16),
                   jax.ShapeDtypeStruct((T, CONV_DIM), F32),
                   jax.ShapeDtypeStruct((T, CONV_DIM), F32)],
        compiler_params=pltpu.CompilerParams(dimension_semantics=("arbitrary",), vmem_limit_bytes=VMEM_LIMIT),
        name="mixer_in",
    )(x2, gmix, w1, gq, wq, wqr, gkv, wk, wv, vone, eplace, cq_t, sq_t, ck_t, sk_t)


def _ones_lane(head):
    return V_HEAD if head % 2 == 0 else 0


def _mla_attn_kernel(q_ref, k_ref, v_ref, o_ref, *, tq):
    seq = q_ref.shape[0]
    causal = (lax.broadcasted_iota(jnp.int32, (tq, tq), 1) <= lax.broadcasted_iota(jnp.int32, (tq, tq), 0))
    lane = lax.broadcasted_iota(jnp.int32, (tq, LANES), 1)
    for qi in range(seq // tq):
        rows = slice(qi * tq, (qi + 1) * tq)
        past = slice(0, qi * tq)
        outs = []
        for hh in range(2):
            hl = slice(hh * LANES, (hh + 1) * LANES)
            qh = q_ref[rows, hl]
            s_diag = jnp.where(causal, _dot_nt(qh, k_ref[rows, hl]), -jnp.inf)
            m = jnp.max(s_diag, axis=-1, keepdims=True)
            if qi > 0:
                s_past = _dot_nt(qh, k_ref[past, hl])
                m = jnp.maximum(m, jnp.max(s_past, axis=-1, keepdims=True))
                acc = _dot(jnp.exp2(s_past - m).astype(BF16), v_ref[past, hl])
                acc = acc + _dot(jnp.exp2(s_diag - m).astype(BF16), v_ref[rows, hl])
            else:
                acc = _dot(jnp.exp2(s_diag - m).astype(BF16), v_ref[rows, hl])
            one = _ones_lane(hh)
            outs.append(acc / acc[:, one:one + 1])
        o_ref[rows, :] = jnp.where(lane < V_HEAD, outs[0], outs[1])


def _mla_attn(q3, k3, v3):
    B, S, _ = q3.shape
    tq = min(TQ, S)
    pair = lambda b, g: (b, 0, g)
    return pl.pallas_call(
        functools.partial(_mla_attn_kernel, tq=tq),
        grid=(B, MLA_HEADS // 2),
        in_specs=[pl.BlockSpec((None, S, 2 * LANES), pair), pl.BlockSpec((None, S, 2 * LANES), pair),
                  pl.BlockSpec((None, S, 2 * LANES), pair)],
        out_specs=pl.BlockSpec((None, S, 2 * V_HEAD), pair),
        out_shape=jax.ShapeDtypeStruct((B, S, MLA_HEADS * V_HEAD), F32),
        compiler_params=pltpu.CompilerParams(dimension_semantics=("arbitrary", "arbitrary"),
                                             vmem_limit_bytes=VMEM_LIMIT),
        name="mla_attn",
    )(q3, k3, v3)


def _mixer_out_kernel(o_ref, z_ref, zh_ref, gb_ref, x_ref, cw_ref, gout_ref, gsum_ref, gexp_ref, wo_ref,
                      h_out, *, tiles_per_seq):
    i = pl.program_id(0)
    ts = z_ref.shape[0]
    z = z_ref[...]
    halo = jnp.where(i % tiles_per_seq == 0, 0.0, zh_ref[...])
    row = lax.broadcasted_iota(jnp.int32, z.shape, 0)
    z1 = jnp.where(row == 0, halo[7:8, :], pltpu.roll(z, 1, axis=0))
    z2 = jnp.where(row == 0, halo[6:7, :], jnp.where(row == 1, halo[7:8, :], pltpu.roll(z, 2, axis=0)))
    cw = cw_ref[...]
    y_conv = gb_ref[...] * (cw[0:1, :] * z2 + cw[1:2, :] * z1 + cw[2:3, :] * z)
    y = jnp.concatenate([o_ref[...], y_conv], axis=-1)
    sq_hi, sq_lo = _split_bf16(y * y)
    gs = _dot(sq_hi, gsum_ref[...]) + _dot(sq_lo, gsum_ref[...])
    r = lax.rsqrt(gs * (1.0 / GROUP_DIM) + EPS)
    r_hi, r_lo = _split_bf16(r)
    r_full = _dot(r_hi, gexp_ref[...]) + _dot(r_lo, gexp_ref[...])
    yn = (y * r_full * gout_ref[...]).astype(BF16)
    h_out[...] = x_ref[...] + _dot(yn, wo_ref[...])


def _mixer_out(o2, z, gb, x2, conv_w, gout, gsum, gexp, wo, seq):
    T, D = x2.shape
    ts = min(TS_IN, seq)
    row = lambda i: (i, 0)
    fixed = lambda i: (0, 0)
    full = lambda a: pl.BlockSpec(a.shape, fixed)
    halo_blocks = ts // SUBLANES
    return pl.pallas_call(
        functools.partial(_mixer_out_kernel, tiles_per_seq=seq // ts),
        grid=(T // ts,),
        in_specs=[pl.BlockSpec((ts, MLA_HEADS * V_HEAD), row), pl.BlockSpec((ts, CONV_DIM), row),
                  pl.BlockSpec((SUBLANES, CONV_DIM), lambda i: (jnp.maximum(i * halo_blocks - 1, 0), 0)),
                  pl.BlockSpec((ts, CONV_DIM), row), pl.BlockSpec((ts, D), row),
                  full(conv_w), full(gout), full(gsum), full(gexp), full(wo)],
        out_specs=pl.BlockSpec((ts, D), row),
        out_shape=jax.ShapeDtypeStruct((T, D), F32),
        compiler_params=pltpu.CompilerParams(dimension_semantics=("arbitrary",), vmem_limit_bytes=VMEM_LIMIT),
        name="mixer_out",
    )(o2, z, z, gb, x2, conv_w, gout, gsum, gexp, wo)


def _mem_kv_kernel(mem_ref, g_ref, w_ref, k_out, v_out):
    d = mem_ref.shape[-1]
    mn = _rms(mem_ref[...], g_ref[...]).astype(BF16)
    kv = _dot(mn, w_ref[...])
    k_out[...] = kv[:, :d].astype(BF16)
    v_out[...] = kv[:, d:].astype(BF16)


def _mem_kv(mem, g, w):
    B, M, D = mem.shape
    return pl.pallas_call(
        _mem_kv_kernel,
        grid=(B,),
        in_specs=[pl.BlockSpec((None, M, D), lambda b: (b, 0, 0)), pl.BlockSpec(g.shape, lambda b: (0, 0)),
                  pl.BlockSpec(w.shape, lambda b: (0, 0))],
        out_specs=[pl.BlockSpec((None, M, D), lambda b: (b, 0, 0)), pl.BlockSpec((None, M, D), lambda b: (b, 0, 0))],
        out_shape=[jax.ShapeDtypeStruct((B, M, D), BF16), jax.ShapeDtypeStruct((B, M, D), BF16)],
        compiler_params=pltpu.CompilerParams(dimension_semantics=("arbitrary",), vmem_limit_bytes=VMEM_LIMIT),
        name="mem_kv",
    )(mem, g, w)


def _xattn_kernel(h_ref, g_ref, wq_ref, k_ref, v_ref, wo_ref, h_out):
    h = h_ref[...]
    d = h.shape[-1]
    hd = d // X_HEADS
    hn = _rms(h, g_ref[...]).astype(BF16)
    q = _dot(hn, wq_ref[...]).astype(BF16)
    outs = []
    for hh in range(X_HEADS):
        sl = slice(hh * hd, (hh + 1) * hd)
        s = _dot_nt(q[:, sl], k_ref[:, sl]) * (1.0 / math.sqrt(hd))
        m = jnp.max(s, axis=-1, keepdims=True)
        p = jnp.exp(s - m)
        p = p / jnp.sum(p, axis=-1, keepdims=True)
        outs.append(_dot(p.astype(BF16), v_ref[:, sl]))
    o = jnp.concatenate(outs, axis=-1).astype(BF16)
    h_out[...] = h + _dot(o, wo_ref[...])


def _xattn(h3, g, wq, kx, vx, wo):
    B, S, D = h3.shape
    M = kx.shape[1]
    ts = min(TS_IN, S)
    fixed = lambda b, i: (0, 0)
    return pl.pallas_call(
        _xattn_kernel,
        grid=(B, S // ts),
        in_specs=[pl.BlockSpec((None, ts, D), lambda b, i: (b, i, 0)), pl.BlockSpec(g.shape, fixed),
                  pl.BlockSpec(wq.shape, fixed), pl.BlockSpec((None, M, D), lambda b, i: (b, 0, 0)),
                  pl.BlockSpec((None, M, D), lambda b, i: (b, 0, 0)), pl.BlockSpec(wo.shape, fixed)],
        out_specs=pl.BlockSpec((None, ts, D), lambda b, i: (b, i, 0)),
        out_shape=jax.ShapeDtypeStruct((B, S, D), F32),
        compiler_params=pltpu.CompilerParams(dimension_semantics=("arbitrary", "arbitrary"),
                                             vmem_limit_bytes=VMEM_LIMIT),
        name="xattn",
    )(h3, g, wq, kx, vx, wo)


def _batcher_pairs(n):
    pairs = []
    p = 1
    while p < n:
        k = p
        while k >= 1:
            for j in range(k % p, n - k, 2 * k):
                for i in range(min(k, n - j - k)):
                    if (i + j) // (2 * p) == (i + j + k) // (2 * p):
                        pairs.append((i + j, i + j + k))
            k //= 2
        p *= 2
    return pairs


_SORT16 = _batcher_pairs(PEER_TOPK)
_ROW_LEN = [PEER_TOPK // (a + 1) for a in range(PEER_TOPK)]


def _sort_desc(v):
    v = list(v)
    for i, j in _SORT16:
        hi = jnp.maximum(v[i], v[j])
        lo = jnp.minimum(v[i], v[j])
        v[i], v[j] = hi, lo
    return v


def _bitonic_desc(v):
    v = list(v)
    n = len(v)
    d = n // 2
    while d >= 1:
        for k in range(n):
            if k & d == 0:
                hi = jnp.maximum(v[k], v[k + d])
                lo = jnp.minimum(v[k], v[k + d])
                v[k], v[k + d] = hi, lo
        d //= 2
    return v


def _merge_top(cur, other):
    n = len(cur)
    c = list(cur)
    for r, val in enumerate(other):
        c[n - 1 - r] = jnp.maximum(c[n - 1 - r], val)
    return _bitonic_desc(c)


def _top16_sorted(s):
    v = _sort_desc([s[k] for k in range(PEER_TOPK)])
    for shift in (4, 2, 1):
        other = [pltpu.roll(v[k], shift, axis=0) for k in range(PEER_TOPK)]
        v = _merge_top(v, other)
    return v


def _peer_route_kernel(h_ref, g_ref, wq_hi_ref, wq_lo_ref, key_hi_ref, key_lo_ref,
                       xn_out, r2_out, p_out, n_out, c_out,
                       st_ref, top_ref, res_ref):
    ts = h_ref.shape[0]
    n_chunk = ts // LANES
    hn = _rms(h_ref[...], g_ref[...])
    hn_hi, hn_lo = _split_bf16(hn)
    xn_out[...] = hn_hi
    q = _dot(hn_hi, wq_hi_ref[...]) + _dot(hn_hi, wq_lo_ref[...]) + _dot(hn_lo, wq_hi_ref[...])
    for h in range(PEER_HEADS):
        q_hi, q_lo = _split_bf16(q[:, h * LANES:(h + 1) * LANES])
        k_hi = key_hi_ref[h]
        k_lo = key_lo_ref[h]
        st = _dot_nt(k_hi, q_hi) + _dot_nt(k_hi, q_lo) + _dot_nt(k_lo, q_hi)
        for c in range(n_chunk):
            st_ref[c, h] = st[:, c * LANES:(c + 1) * LANES]

    neg_inf = jnp.float32(-jnp.inf)
    pos_inf = jnp.float32(jnp.inf)

    def chunk_body(c, _):
        def sort_body(h, _):
            for half in range(2):
                s = st_ref[c, h, pl.ds(half * PEER_KEYS, PEER_KEYS), :].reshape(PEER_TOPK, SUBLANES, LANES)
                v = _top16_sorted(s)
                for a in range(PEER_TOPK):
                    top_ref[half, a, pl.ds(h, 1), :] = v[a][0:1, :]
            return 0

        lax.fori_loop(0, PEER_HEADS, sort_body, 0)

        v1 = [top_ref[0, a] for a in range(PEER_TOPK)]
        v2 = [top_ref[1, b] for b in range(PEER_TOPK)]
        sums = [[v1[a] + v2[b] for b in range(_ROW_LEN[a])] for a in range(PEER_TOPK)]
        cur = sums[0]
        a = 1
        while _ROW_LEN[a] > 1:
            cur = _merge_top(cur, sums[a])
            a += 1
        cur = _merge_top(cur, [sums[r][0] for r in range(a, PEER_TOPK)])
        tau = cur[PEER_TOPK - 1]
        top_sum = sums[0][0]
        z = jnp.zeros_like(tau)
        for a in range(PEER_TOPK):
            cnt = jnp.zeros_like(tau)
            for b in range(_ROW_LEN[a]):
                sel = sums[a][b] >= tau
                cnt = cnt + jnp.where(sel, 1.0, 0.0)
                z = z + jnp.where(sel, jnp.exp(sums[a][b] - top_sum), 0.0)
            res_ref[a] = cnt
        res_ref[PEER_TOPK] = 1.0 / z

        def expand_body(h, _):
            s1 = st_ref[c, h, pl.ds(0, PEER_KEYS), :].reshape(PEER_TOPK, SUBLANES, LANES)
            s2 = st_ref[c, h, pl.ds(PEER_KEYS, PEER_KEYS), :].reshape(PEER_TOPK, SUBLANES, LANES)
            n = jnp.zeros(s1.shape, F32)
            r2 = jnp.full(s2.shape, float(PEER_TOPK), F32)
            for a in range(PEER_TOPK - 1, -1, -1):
                v1a = top_ref[0, a, pl.ds(h, 1), :]
                v2a = top_ref[1, a, pl.ds(h, 1), :]
                n = jnp.where(s1 == v1a, res_ref[a, pl.ds(h, 1), :], n)
                r2 = jnp.where(s2 == v2a, float(a), r2)
            m1 = top_ref[0, 0, pl.ds(h, 1), :]
            m2 = top_ref[1, 0, pl.ds(h, 1), :]
            inv_z = res_ref[PEER_TOPK, pl.ds(h, 1), :]
            n_out[c, h] = n.reshape(PEER_KEYS, LANES)
            c_out[c, h] = (jnp.exp(s1 - m1) * inv_z).reshape(PEER_KEYS, LANES)
            r2_out[c, h] = r2.reshape(PEER_KEYS, LANES)
            p_out[c, h] = jnp.exp(s2 - m2).reshape(PEER_KEYS, LANES)
            return 0

        lax.fori_loop(0, PEER_HEADS, expand_body, 0)
        return 0

    lax.fori_loop(0, n_chunk, chunk_body, 0)


def _peer_route(h2, g, wq_hi, wq_lo, key_hi, key_lo):
    T, D = h2.shape
    ts = min(TS_ROUTE, T)
    nc = ts // LANES
    aux_spec = pl.BlockSpec((nc, PEER_HEADS, PEER_KEYS, LANES), lambda i: (i, 0, 0, 0))
    aux_shape = jax.ShapeDtypeStruct((T // LANES, PEER_HEADS, PEER_KEYS, LANES), F32)
    fixed2 = lambda i: (0, 0)
    fixed3 = lambda i: (0, 0, 0)
    return pl.pallas_call(
        _peer_route_kernel,
        grid=(T // ts,),
        in_specs=[pl.BlockSpec((ts, D), lambda i: (i, 0)), pl.BlockSpec(g.shape, fixed2),
                  pl.BlockSpec(wq_hi.shape, fixed2), pl.BlockSpec(wq_lo.shape, fixed2),
                  pl.BlockSpec(key_hi.shape, fixed3), pl.BlockSpec(key_lo.shape, fixed3)],
        out_specs=[pl.BlockSpec((ts, D), lambda i: (i, 0)), aux_spec, aux_spec, aux_spec, aux_spec],
        out_shape=[jax.ShapeDtypeStruct((T, D), BF16), aux_shape, aux_shape, aux_shape, aux_shape],
        scratch_shapes=[pltpu.VMEM((nc, PEER_HEADS, 2 * PEER_KEYS, LANES), F32),
                        pltpu.VMEM((2, PEER_TOPK, SUBLANES, LANES), F32),
                        pltpu.VMEM((PEER_TOPK + 1, SUBLANES, LANES), F32)],
        compiler_params=pltpu.CompilerParams(dimension_semantics=("arbitrary",), vmem_limit_bytes=VMEM_LIMIT),
        name="peer_route",
    )(h2, g, wq_hi, wq_lo, key_hi, key_lo)


def _pack_experts_kernel(u_ref, v_ref, u_out, vt_out):
    u_out[...] = pltpu.bitcast(u_ref[...].astype(BF16), jnp.uint32)
    vt_out[...] = pltpu.bitcast(v_ref[...].T.astype(BF16), jnp.uint32)


def _pack_experts(u, v):
    E, D = u.shape
    eb = EB_FFN
    return pl.pallas_call(
        _pack_experts_kernel,
        grid=(E // eb,),
        in_specs=[pl.BlockSpec((eb, D), lambda e: (e, 0)), pl.BlockSpec((eb, D), lambda e: (e, 0))],
        out_specs=[pl.BlockSpec((eb // 2, D), lambda e: (e, 0)), pl.BlockSpec((D // 2, eb), lambda e: (0, e))],
        out_shape=[jax.ShapeDtypeStruct((E // 2, D), jnp.uint32), jax.ShapeDtypeStruct((D // 2, E), jnp.uint32)],
        compiler_params=pltpu.CompilerParams(dimension_semantics=("arbitrary",), vmem_limit_bytes=VMEM_LIMIT),
        name="pack_experts",
    )(u, v)


def _peer_gate_unit(tc, ii, at_ref, ht_ref, r2s_ref, ps_ref, n_ref, c_ref):
    pack = 2 * SUBLANES
    n_jv = PEER_KEYS // pack
    zero = jnp.zeros((pack, LANES), BF16)
    lanes = slice(tc * LANES, (tc + 1) * LANES)
    g = [None] * n_jv
    for h in range(PEER_HEADS):
        n_b = jnp.broadcast_to(n_ref[tc, h, ii:ii + 1, :], (pack, LANES)).astype(BF16)
        c_b = jnp.broadcast_to(c_ref[tc, h, ii:ii + 1, :], (pack, LANES)).astype(BF16)
        for jv in range(n_jv):
            js = slice(jv * pack, (jv + 1) * pack)
            term = jnp.where(r2s_ref[tc, h, js, :] < n_b, ps_ref[tc, h, js, :], zero) * c_b
            g[jv] = term if g[jv] is None else g[jv] + term
    for jv in range(n_jv):
        rows = slice(ii * PEER_KEYS + jv * pack, ii * PEER_KEYS + (jv + 1) * pack)
        ht_ref[rows, lanes] = _gelu_tanh(at_ref[rows, lanes].astype(BF16)) * g[jv]


def _peer_ffn_kernel(xn_ref, u_ref, vt_ref, r2_ref, p_ref, n_ref, c_ref, h_ref, gfin_ref,
                     out_ref, acc_ref, at_ref, ht_ref, r2s_ref, ps_ref, xs_ref, *, final_norm):
    e = pl.program_id(1)
    tt = xn_ref.shape[0]

    @pl.when(e == 0)
    def _():
        acc_ref[...] = jnp.zeros_like(acc_ref)
        xs_ref[...] = xn_ref[...]
        for tc in range(tt // LANES):
            for h in range(PEER_HEADS):
                r2s_ref[tc, h] = r2_ref[tc, h].astype(BF16)
                ps_ref[tc, h] = p_ref[tc, h].astype(BF16)

    u_blk = pltpu.bitcast(u_ref[...], BF16)
    vt_blk = pltpu.bitcast(vt_ref[...], BF16)
    at_ref[...] = _dot_nt(u_blk, xs_ref[...])
    for tc in range(tt // LANES):
        for ii in range(u_blk.shape[0] // PEER_KEYS):
            _peer_gate_unit(tc, ii, at_ref, ht_ref, r2s_ref, ps_ref, n_ref, c_ref)
    acc_ref[...] += _dot(vt_blk, ht_ref[...])

    @pl.when(e == pl.num_programs(1) - 1)
    def _():
        res = h_ref[...] + acc_ref[...].T
        out_ref[...] = _rms(res, gfin_ref[...]) if final_norm else res


def _peer_ffn(xn, u_pack, vt_pack, r2, p, n, coef, h2, gfin, final_norm):
    T, D = h2.shape
    E = vt_pack.shape[1]
    tt = min(TT_FFN, T)
    nc = tt // LANES
    eb = EB_FFN
    n_i = eb // PEER_KEYS
    tok = lambda t, e: (t, 0)
    aux_shape = (nc, PEER_HEADS, PEER_KEYS, LANES)
    aux_full = pl.BlockSpec(aux_shape, lambda t, e: (t, 0, 0, 0))
    aux_rows = pl.BlockSpec((nc, PEER_HEADS, n_i, LANES), lambda t, e: (t, 0, e, 0))
    return pl.pallas_call(
        functools.partial(_peer_ffn_kernel, final_norm=final_norm),
        grid=(T // tt, E // eb),
        in_specs=[pl.BlockSpec((tt, D), tok), pl.BlockSpec((eb // 2, D), lambda t, e: (e, 0)),
                  pl.BlockSpec((D // 2, eb), lambda t, e: (0, e)), aux_full, aux_full, aux_rows, aux_rows,
                  pl.BlockSpec((tt, D), tok), pl.BlockSpec(gfin.shape, lambda t, e: (0, 0))],
        out_specs=pl.BlockSpec((tt, D), tok),
        out_shape=jax.ShapeDtypeStruct((T, D), F32),
        scratch_shapes=[pltpu.VMEM((D, tt), F32), pltpu.VMEM((eb, tt), F32), pltpu.VMEM((eb, tt), BF16),
                        pltpu.VMEM(aux_shape, BF16), pltpu.VMEM(aux_shape, BF16), pltpu.VMEM((tt, D), BF16)],
        compiler_params=pltpu.CompilerParams(dimension_semantics=("arbitrary", "arbitrary"),
                                             vmem_limit_bytes=VMEM_LIMIT),
        name="peer_ffn",
    )(xn, u_pack, vt_pack, r2, p, n, coef, h2, gfin)


def _head_blocks(w, n_heads, width, pieces):
    w3 = w.reshape(w.shape[0], n_heads, width)
    out = jnp.zeros((w.shape[0], n_heads, LANES), w.dtype)
    for s0, s1, d0 in pieces:
        out = out.at[:, :, d0:d0 + (s1 - s0)].set(w3[:, :, s0:s1])
    return out.reshape(w.shape[0], n_heads * LANES)


def kernel(x, mem, positions, g_mix, w_in, g_q, w_uq, g_kv, w_ukv, conv_w, g_out, w_o, g_x, g_mem, w_xq,
           w_xkv, w_xo, g_ffn, w_pq, sub_keys, u_experts, v_experts, g_final):
    B, S, D = x.shape
    T = B * S
    depth = g_mix.shape[0]
    half = QK_ROPE // 2

    inv = ROPE_THETA ** (-jnp.arange(0, QK_ROPE, 2, dtype=F32) / QK_ROPE)
    ang = positions.astype(F32)[..., None] * inv
    cos = jnp.cos(ang).astype(x.dtype).reshape(T, half)
    sin = jnp.sin(ang).astype(x.dtype).reshape(T, half)
    ones = jnp.ones((T, QK_NOPE), F32)
    zeros_n = jnp.zeros((T, QK_NOPE), F32)
    pad_q = jnp.zeros((T, LANES - QK_NOPE - QK_ROPE), F32)
    pad_k = jnp.zeros((T, LANES - QK_ROPE), F32)
    q_scale = math.log2(math.e) / math.sqrt(QK_NOPE + QK_ROPE)
    cq_t = jnp.concatenate([ones, cos, cos, pad_q], axis=1) * q_scale
    sq_t = jnp.concatenate([zeros_n, -sin, sin, pad_q], axis=1) * q_scale
    ck_t = jnp.concatenate([cos, cos, pad_k], axis=1)
    sk_t = jnp.concatenate([-sin, sin, pad_k], axis=1)

    lane = jnp.arange(LANES)
    col = jnp.arange(MLA_HEADS * LANES)
    eplace = ((col[None, :] % LANES == lane[:, None] + QK_NOPE) & (lane[:, None] < QK_ROPE)).astype(BF16)
    mix_col = jnp.arange(D)
    gsum = (mix_col[:, None] // GROUP_DIM == lane[None, :]).astype(BF16)
    gexp = (lane[:, None] == mix_col[None, :] // GROUP_DIM).astype(BF16)

    h = x.reshape(T, D)
    for l in range(depth):
        o1 = Q_RANK
        o2 = o1 + KV_RANK
        o3 = o2 + QK_ROPE
        o4 = o3 + CONV_DIM
        o5 = o4 + CONV_DIM
        wl = w_in[l]
        w_kr = wl[:, o2:o3]
        w_krr = jnp.concatenate([w_kr[:, half:], w_kr[:, :half]], axis=1)
        lane_pad = jnp.zeros((D, LANES - QK_ROPE), wl.dtype)
        w1 = jnp.concatenate([wl[:, :o1], wl[:, o1:o2], w_kr, lane_pad, w_krr, lane_pad,
                              wl[:, o3:o4], wl[:, o4:o5], wl[:, o5:]], axis=1).astype(BF16)
        qw = QK_NOPE + QK_ROPE
        wq = _head_blocks(w_uq[l], MLA_HEADS, qw, [(0, qw, 0)]).astype(BF16)
        wqr = _head_blocks(w_uq[l], MLA_HEADS, qw,
                           [(QK_NOPE + half, qw, QK_NOPE), (QK_NOPE, QK_NOPE + half, QK_NOPE + half)]).astype(BF16)
        kvw = QK_NOPE + V_HEAD
        wk = _head_blocks(w_ukv[l], MLA_HEADS, kvw, [(0, QK_NOPE, 0)]).astype(BF16)
        v_cols = w_ukv[l].reshape(KV_RANK, MLA_HEADS, kvw)[:, :, QK_NOPE:]
        v_pad = jnp.zeros_like(v_cols)
        odd_head = (jnp.arange(MLA_HEADS) % 2 == 1)[None, :, None]
        wv = jnp.where(odd_head, jnp.concatenate([v_pad, v_cols], axis=-1),
                       jnp.concatenate([v_cols, v_pad], axis=-1)).reshape(KV_RANK, MLA_HEADS * LANES).astype(BF16)
        vone = jnp.stack([(lane == _ones_lane(hd)).astype(F32) for hd in range(MLA_HEADS)]).reshape(1, -1)

        q, k, v, z, gb = _mixer_in(h, g_mix[l][None, :], w1, g_q[l][None, :], wq, wqr, g_kv[l][None, :], wk, wv,
                                   vone, eplace, cq_t, sq_t, ck_t, sk_t)
        o = _mla_attn(q.reshape(B, S, -1), k.reshape(B, S, -1), v.reshape(B, S, -1))
        h = _mixer_out(o.reshape(T, -1), z, gb, h, conv_w[l], g_out[l][None, :], gsum, gexp,
                       w_o[l].astype(BF16), S)

        kx, vx = _mem_kv(mem, g_mem[l][None, :], w_xkv[l].astype(BF16))
        h = _xattn(h.reshape(B, S, D), g_x[l][None, :], w_xq[l].astype(BF16), kx, vx,
                   w_xo[l].astype(BF16)).reshape(T, D)

        wpq_hi, wpq_lo = _split_bf16(w_pq[l])
        sk = sub_keys[l]
        zk = jnp.zeros_like(sk[:, 0])
        keys_bd = jnp.concatenate([jnp.concatenate([sk[:, 0], zk], axis=-1),
                                   jnp.concatenate([zk, sk[:, 1]], axis=-1)], axis=1)
        key_hi, key_lo = _split_bf16(keys_bd)
        xn, r2, p, n, coef = _peer_route(h, g_ffn[l][None, :], wpq_hi, wpq_lo, key_hi, key_lo)
        u_pack, vt_pack = _pack_experts(u_experts[l], v_experts[l])
        h = _peer_ffn(xn, u_pack, vt_pack, r2, p, n, coef, h,
                      g_final[None, :], final_norm=(l == depth - 1))
    return h.reshape(B, S, D)
```

```python
import functools
import math

import jax
import jax.numpy as jnp
from jax import lax
from jax.experimental import pallas as pl
from jax.experimental.pallas import tpu as pltpu

F32 = jnp.float32
BF16 = jnp.bfloat16

EPS = 1e-6
LANES = 128
SUBLANES = 8
VMEM_LIMIT = 56 * 1024 * 1024

MLA_HEADS = 8
QK_NOPE = 64
QK_ROPE = 32
V_HEAD = 64
Q_RANK = 384
KV_RANK = 256
CONV_DIM = 512
GROUP_DIM = 64
ROPE_THETA = 10000.0
X_HEADS = 4
PEER_HEADS = 8
PEER_KEYS = 128
PEER_TOPK = 16
HALF_Q = 64

TS_IN = 512
TQ = 256
TS_ROUTE = 256
TT_FFN = 512
EB_FFN = 1024
GATE_UNIT_SPLIT = (6, 9, 9, 8, 0)
NT_DIMS = (((1,), (1,)), ((), ()))


def _rms(x, g):
    return x * lax.rsqrt(jnp.mean(x * x, axis=-1, keepdims=True) + EPS) * g


def _split_bf16(x):
    hi = x.astype(BF16)
    lo = (x - hi.astype(F32)).astype(BF16)
    return hi, lo


def _dot(a, b):
    return jnp.dot(a, b, preferred_element_type=F32)


def _dot_nt(a, b):
    return lax.dot_general(a, b, NT_DIMS, preferred_element_type=F32)


_C_CQ = 0
_C_CKV = _C_CQ + Q_RANK
_C_KR = _C_CKV + KV_RANK
_C_KRR = _C_KR + LANES
_C_GB = _C_KRR + LANES
_C_GC = _C_GB + CONV_DIM
_C_HX = _C_GC + CONV_DIM
_C_END = _C_HX + CONV_DIM


def _mixer_in_kernel(x_ref, gmix_ref, w1_ref, gq_ref, wq_ref, wqr_ref, gkv_ref, wk_ref, wv_ref, vone_ref,
                     eplace_ref, cq_ref, sq_ref, ck_ref, sk_ref,
                     q_out, k_out, v_out, z_out, gb_out):
    xn = _rms(x_ref[...], gmix_ref[...]).astype(BF16)
    proj = _dot(xn, w1_ref[...])
    cq = proj[:, _C_CQ:_C_CKV]
    ckv = proj[:, _C_CKV:_C_KR]
    kr = proj[:, _C_KR:_C_KRR]
    krr = proj[:, _C_KRR:_C_GB]
    gb_out[...] = proj[:, _C_GB:_C_GC]
    z_out[...] = proj[:, _C_GC:_C_HX] * proj[:, _C_HX:_C_END]

    cqn = _rms(cq, gq_ref[...]).astype(BF16)
    q_raw = _dot(cqn, wq_ref[...])
    q_rot = _dot(cqn, wqr_ref[...])
    cq_t = cq_ref[...]
    sq_t = sq_ref[...]
    for h in range(MLA_HEADS):
        sl = slice(h * LANES, (h + 1) * LANES)
        q_out[:, sl] = (q_raw[:, sl] * cq_t + q_rot[:, sl] * sq_t).astype(BF16)

    ckvn = _rms(ckv, gkv_ref[...]).astype(BF16)
    kr_roped = (kr * ck_ref[...] + krr * sk_ref[...]).astype(BF16)
    k_out[...] = (_dot(ckvn, wk_ref[...]) + _dot(kr_roped, eplace_ref[...])).astype(BF16)
    v_out[...] = (_dot(ckvn, wv_ref[...]) + vone_ref[...]).astype(BF16)


def _mixer_in(x2, gmix, w1, gq, wq, wqr, gkv, wk, wv, vone, eplace, cq_t, sq_t, ck_t, sk_t):
    T, D = x2.shape
    ts = min(TS_IN, T)
    row = lambda i: (i, 0)
    fixed = lambda i: (0, 0)
    full = lambda a: pl.BlockSpec(a.shape, fixed)
    return pl.pallas_call(
        _mixer_in_kernel,
        grid=(T // ts,),
        in_specs=[pl.BlockSpec((ts, D), row), full(gmix), full(w1), full(gq), full(wq), full(wqr),
                  full(gkv), full(wk), full(wv), full(vone), full(eplace),
                  pl.BlockSpec((ts, LANES), row), pl.BlockSpec((ts, LANES), row),
                  pl.BlockSpec((ts, LANES), row), pl.BlockSpec((ts, LANES), row)],
        out_specs=[pl.BlockSpec((ts, MLA_HEADS * LANES), row), pl.BlockSpec((ts, MLA_HEADS * LANES), row),
                   pl.BlockSpec((ts, MLA_HEADS * LANES), row), pl.BlockSpec((ts, CONV_DIM), row),
                   pl.BlockSpec((ts, CONV_DIM), row)],
        out_shape=[jax.ShapeDtypeStruct((T, MLA_HEADS * LANES), BF16),
                   jax.ShapeDtypeStruct((T, MLA_HEADS * LANES), BF16),
                   jax.ShapeDtypeStruct((T, MLA_HEADS * LANES), BF16),
                   jax.ShapeDtypeStruct((T, CONV_DIM), F32),
                   jax.ShapeDtypeStruct((T, CONV_DIM), F32)],
        compiler_params=pltpu.CompilerParams(dimension_semantics=("arbitrary",), vmem_limit_bytes=VMEM_LIMIT),
        name="mixer_in",
    )(x2, gmix, w1, gq, wq, wqr, gkv, wk, wv, vone, eplace, cq_t, sq_t, ck_t, sk_t)


def _ones_lane(head):
    return V_HEAD if head % 2 == 0 else 0


def _mla_attn_kernel(q_ref, k_ref, v_ref, o_ref, *, tq):
    seq = q_ref.shape[0]
    causal = (lax.broadcasted_iota(jnp.int32, (tq, tq), 1) <= lax.broadcasted_iota(jnp.int32, (tq, tq), 0))
    lane = lax.broadcasted_iota(jnp.int32, (tq, LANES), 1)
    for qi in range(seq // tq):
        rows = slice(qi * tq, (qi + 1) * tq)
        past = slice(0, qi * tq)
        outs = []
        for hh in range(2):
            hl = slice(hh * LANES, (hh + 1) * LANES)
            qh = q_ref[rows, hl]
            s_diag = jnp.where(causal, _dot_nt(qh, k_ref[rows, hl]), -jnp.inf)
            m = jnp.max(s_diag, axis=-1, keepdims=True)
            if qi > 0:
                s_past = _dot_nt(qh, k_ref[past, hl])
                m = jnp.maximum(m, jnp.max(s_past, axis=-1, keepdims=True))
                acc = _dot(jnp.exp2(s_past - m).astype(BF16), v_ref[past, hl])
                acc = acc + _dot(jnp.exp2(s_diag - m).astype(BF16), v_ref[rows, hl])
            else:
                acc = _dot(jnp.exp2(s_diag - m).astype(BF16), v_ref[rows, hl])
            one = _ones_lane(hh)
            outs.append(acc / acc[:, one:one + 1])
        o_ref[rows, :] = jnp.where(lane < V_HEAD, outs[0], outs[1])


def _mla_attn(q3, k3, v3):
    B, S, _ = q3.shape
    tq = min(TQ, S)
    pair = lambda b, g: (b, 0, g)
    return pl.pallas_call(
        functools.partial(_mla_attn_kernel, tq=tq),
        grid=(B, MLA_HEADS // 2),
        in_specs=[pl.BlockSpec((None, S, 2 * LANES), pair), pl.BlockSpec((None, S, 2 * LANES), pair),
                  pl.BlockSpec((None, S, 2 * LANES), pair)],
        out_specs=pl.BlockSpec((None, S, 2 * V_HEAD), pair),
        out_shape=jax.ShapeDtypeStruct((B, S, MLA_HEADS * V_HEAD), F32),
        compiler_params=pltpu.CompilerParams(dimension_semantics=("arbitrary", "arbitrary"),
                                             vmem_limit_bytes=VMEM_LIMIT),
        name="mla_attn",
    )(q3, k3, v3)


def _mixer_out_kernel(o_ref, z_ref, zh_ref, gb_ref, x_ref, cw_ref, gout_ref, gsum_ref, gexp_ref, wo_ref,
                      h_out, *, tiles_per_seq):
    i = pl.program_id(0)
    ts = z_ref.shape[0]
    z = z_ref[...]
    halo = jnp.where(i % tiles_per_seq == 0, 0.0, zh_ref[...])
    row = lax.broadcasted_iota(jnp.int32, z.shape, 0)
    z1 = jnp.where(row == 0, halo[7:8, :], pltpu.roll(z, 1, axis=0))
    z2 = jnp.where(row == 0, halo[6:7, :], jnp.where(row == 1, halo[7:8, :], pltpu.roll(z, 2, axis=0)))
    cw = cw_ref[...]
    y_conv = gb_ref[...] * (cw[0:1, :] * z2 + cw[1:2, :] * z1 + cw[2:3, :] * z)
    y = jnp.concatenate([o_ref[...], y_conv], axis=-1)
    sq_hi, sq_lo = _split_bf16(y * y)
    gs = _dot(sq_hi, gsum_ref[...]) + _dot(sq_lo, gsum_ref[...])
    r = lax.rsqrt(gs * (1.0 / GROUP_DIM) + EPS)
    r_hi, r_lo = _split_bf16(r)
    r_full = _dot(r_hi, gexp_ref[...]) + _dot(r_lo, gexp_ref[...])
    yn = (y * r_full * gout_ref[...]).astype(BF16)
    h_out[...] = x_ref[...] + _dot(yn, wo_ref[...])


def _mixer_out(o2, z, gb, x2, conv_w, gout, gsum, gexp, wo, seq):
    T, D = x2.shape
    ts = min(TS_IN, seq)
    row = lambda i: (i, 0)
    fixed = lambda i: (0, 0)
    full = lambda a: pl.BlockSpec(a.shape, fixed)
    halo_blocks = ts // SUBLANES
    return pl.pallas_call(
        functools.partial(_mixer_out_kernel, tiles_per_seq=seq // ts),
        grid=(T // ts,),
        in_specs=[pl.BlockSpec((ts, MLA_HEADS * V_HEAD), row), pl.BlockSpec((ts, CONV_DIM), row),
                  pl.BlockSpec((SUBLANES, CONV_DIM), lambda i: (jnp.maximum(i * halo_blocks - 1, 0), 0)),
                  pl.BlockSpec((ts, CONV_DIM), row), pl.BlockSpec((ts, D), row),
                  full(conv_w), full(gout), full(gsum), full(gexp), full(wo)],
        out_specs=pl.BlockSpec((ts, D), row),
        out_shape=jax.ShapeDtypeStruct((T, D), F32),
        compiler_params=pltpu.CompilerParams(dimension_semantics=("arbitrary",), vmem_limit_bytes=VMEM_LIMIT),
        name="mixer_out",
    )(o2, z, z, gb, x2, conv_w, gout, gsum, gexp, wo)


def _mem_kv_kernel(mem_ref, g_ref, w_ref, k_out, v_out):
    d = mem_ref.shape[-1]
    mn = _rms(mem_ref[...], g_ref[...]).astype(BF16)
    kv = _dot(mn, w_ref[...])
    k_out[...] = kv[:, :d].astype(BF16)
    v_out[...] = kv[:, d:].astype(BF16)


def _mem_kv(mem, g, w):
    B, M, D = mem.shape
    return pl.pallas_call(
        _mem_kv_kernel,
        grid=(B,),
        in_specs=[pl.BlockSpec((None, M, D), lambda b: (b, 0, 0)), pl.BlockSpec(g.shape, lambda b: (0, 0)),
                  pl.BlockSpec(w.shape, lambda b: (0, 0))],
        out_specs=[pl.BlockSpec((None, M, D), lambda b: (b, 0, 0)), pl.BlockSpec((None, M, D), lambda b: (b, 0, 0))],
        out_shape=[jax.ShapeDtypeStruct((B, M, D), BF16), jax.ShapeDtypeStruct((B, M, D), BF16)],
        compiler_params=pltpu.CompilerParams(dimension_semantics=("arbitrary",), vmem_limit_bytes=VMEM_LIMIT),
        name="mem_kv",
    )(mem, g, w)


def _xattn_kernel(h_ref, g_ref, wq_ref, k_ref, v_ref, wo_ref, h_out):
    h = h_ref[...]
    d = h.shape[-1]
    hd = d // X_HEADS
    hn = _rms(h, g_ref[...]).astype(BF16)
    q = _dot(hn, wq_ref[...]).astype(BF16)
    outs = []
    for hh in range(X_HEADS):
        sl = slice(hh * hd, (hh + 1) * hd)
        s = _dot_nt(q[:, sl], k_ref[:, sl]) * (1.0 / math.sqrt(hd))
        m = jnp.max(s, axis=-1, keepdims=True)
        p = jnp.exp(s - m)
        p = p / jnp.sum(p, axis=-1, keepdims=True)
        outs.append(_dot(p.astype(BF16), v_ref[:, sl]))
    o = jnp.concatenate(outs, axis=-1).astype(BF16)
    h_out[...] = h + _dot(o, wo_ref[...])


def _xattn(h3, g, wq, kx, vx, wo):
    B, S, D = h3.shape
    M = kx.shape[1]
    ts = min(TS_IN, S)
    fixed = lambda b, i: (0, 0)
    return pl.pallas_call(
        _xattn_kernel,
        grid=(B, S // ts),
        in_specs=[pl.BlockSpec((None, ts, D), lambda b, i: (b, i, 0)), pl.BlockSpec(g.shape, fixed),
                  pl.BlockSpec(wq.shape, fixed), pl.BlockSpec((None, M, D), lambda b, i: (b, 0, 0)),
                  pl.BlockSpec((None, M, D), lambda b, i: (b, 0, 0)), pl.BlockSpec(wo.shape, fixed)],
        out_specs=pl.BlockSpec((None, ts, D), lambda b, i: (b, i, 0)),
        out_shape=jax.ShapeDtypeStruct((B, S, D), F32),
        compiler_params=pltpu.CompilerParams(dimension_semantics=("arbitrary", "arbitrary"),
                                             vmem_limit_bytes=VMEM_LIMIT),
        name="xattn",
    )(h3, g, wq, kx, vx, wo)


def _batcher_pairs(n):
    pairs = []
    p = 1
    while p < n:
        k = p
        while k >= 1:
            for j in range(k % p, n - k, 2 * k):
                for i in range(min(k, n - j - k)):
                    if (i + j) // (2 * p) == (i + j + k) // (2 * p):
                        pairs.append((i + j, i + j + k))
            k //= 2
        p *= 2
    return pairs


_SORT16 = _batcher_pairs(PEER_TOPK)
_ROW_LEN = [PEER_TOPK // (a + 1) for a in range(PEER_TOPK)]


def _sort_desc(v):
    v = list(v)
    for i, j in _SORT16:
        hi = jnp.maximum(v[i], v[j])
        lo = jnp.minimum(v[i], v[j])
        v[i], v[j] = hi, lo
    return v


def _bitonic_desc(v):
    v = list(v)
    n = len(v)
    d = n // 2
    while d >= 1:
        for k in range(n):
            if k & d == 0:
                hi = jnp.maximum(v[k], v[k + d])
                lo = jnp.minimum(v[k], v[k + d])
                v[k], v[k + d] = hi, lo
        d //= 2
    return v


def _merge_top(cur, other):
    n = len(cur)
    c = list(cur)
    for r, val in enumerate(other):
        c[n - 1 - r] = jnp.maximum(c[n - 1 - r], val)
    return _bitonic_desc(c)


def _top16_sorted(s):
    v = _sort_desc([s[k] for k in range(PEER_TOPK)])
    for shift in (4, 2, 1):
        other = [pltpu.roll(v[k], shift, axis=0) for k in range(PEER_TOPK)]
        v = _merge_top(v, other)
    return v


def _peer_route_kernel(h_ref, g_ref, wq_hi_ref, wq_lo_ref, key_hi_ref, key_lo_ref,
                       xn_out, r2_out, p_out, n_out, c_out,
                       st_ref, top_ref, res_ref):
    ts = h_ref.shape[0]
    n_chunk = ts // LANES
    hn = _rms(h_ref[...], g_ref[...])
    hn_hi, hn_lo = _split_bf16(hn)
    xn_out[...] = hn_hi
    q = _dot(hn_hi, wq_hi_ref[...]) + _dot(hn_hi, wq_lo_ref[...]) + _dot(hn_lo, wq_hi_ref[...])
    for h in range(PEER_HEADS):
        q_hi, q_lo = _split_bf16(q[:, h * LANES:(h + 1) * LANES])
        k_hi = key_hi_ref[h]
        k_lo = key_lo_ref[h]
        st = _dot_nt(k_hi, q_hi) + _dot_nt(k_hi, q_lo) + _dot_nt(k_lo, q_hi)
        for c in range(n_chunk):
            st_ref[c, h] = st[:, c * LANES:(c + 1) * LANES]

    neg_inf = jnp.float32(-jnp.inf)
    pos_inf = jnp.float32(jnp.inf)

    def chunk_body(c, _):
        def sort_body(h, _):
            for half in range(2):
                s = st_ref[c, h, pl.ds(half * PEER_KEYS, PEER_KEYS), :].reshape(PEER_TOPK, SUBLANES, LANES)
                v = _top16_sorted(s)
                for a in range(PEER_TOPK):
                    top_ref[half, a, pl.ds(h, 1), :] = v[a][0:1, :]
            return 0

        lax.fori_loop(0, PEER_HEADS, sort_body, 0)

        v1 = [top_ref[0, a] for a in range(PEER_TOPK)]
        v2 = [top_ref[1, b] for b in range(PEER_TOPK)]
        sums = [[v1[a] + v2[b] for b in range(_ROW_LEN[a])] for a in range(PEER_TOPK)]
        cur = sums[0]
        a = 1
        while _ROW_LEN[a] > 1:
            cur = _merge_top(cur, sums[a])
            a += 1
        cur = _merge_top(cur, [sums[r][0] for r in range(a, PEER_TOPK)])
        tau = cur[PEER_TOPK - 1]
        top_sum = sums[0][0]
        z = jnp.zeros_like(tau)
        for a in range(PEER_TOPK):
            cnt = jnp.zeros_like(tau)
            for b in range(_ROW_LEN[a]):
                sel = sums[a][b] >= tau
                cnt = cnt + jnp.where(sel, 1.0, 0.0)
                z = z + jnp.where(sel, jnp.exp(sums[a][b] - top_sum), 0.0)
            res_ref[a] = cnt
        res_ref[PEER_TOPK] = 1.0 / z

        def expand_body(h, _):
            s1 = st_ref[c, h, pl.ds(0, PEER_KEYS), :].reshape(PEER_TOPK, SUBLANES, LANES)
            s2 = st_ref[c, h, pl.ds(PEER_KEYS, PEER_KEYS), :].reshape(PEER_TOPK, SUBLANES, LANES)
            n = jnp.zeros(s1.shape, F32)
            r2 = jnp.full(s2.shape, float(PEER_TOPK), F32)
            for a in range(PEER_TOPK - 1, -1, -1):
                v1a = top_ref[0, a, pl.ds(h, 1), :]
                v2a = top_ref[1, a, pl.ds(h, 1), :]
                n = jnp.where(s1 == v1a, res_ref[a, pl.ds(h, 1), :], n)
                r2 = jnp.where(s2 == v2a, float(a), r2)
            m1 = top_ref[0, 0, pl.ds(h, 1), :]
            m2 = top_ref[1, 0, pl.ds(h, 1), :]
            inv_z = res_ref[PEER_TOPK, pl.ds(h, 1), :]
            n_out[c, h] = n.reshape(PEER_KEYS, LANES)
            c_out[c, h] = (jnp.exp(s1 - m1) * inv_z).reshape(PEER_KEYS, LANES)
            r2_out[c, h] = r2.reshape(PEER_KEYS, LANES)
            p_out[c, h] = jnp.exp(s2 - m2).reshape(PEER_KEYS, LANES)
            return 0

        lax.fori_loop(0, PEER_HEADS, expand_body, 0)
        return 0

    lax.fori_loop(0, n_chunk, chunk_body, 0)


def _peer_route(h2, g, wq_hi, wq_lo, key_hi, key_lo):
    T, D = h2.shape
    ts = min(TS_ROUTE, T)
    nc = ts // LANES
    aux_spec = pl.BlockSpec((nc, PEER_HEADS, PEER_KEYS, LANES), lambda i: (i, 0, 0, 0))
    aux_shape = jax.ShapeDtypeStruct((T // LANES, PEER_HEADS, PEER_KEYS, LANES), F32)
    fixed2 = lambda i: (0, 0)
    fixed3 = lambda i: (0, 0, 0)
    return pl.pallas_call(
        _peer_route_kernel,
        grid=(T // ts,),
        in_specs=[pl.BlockSpec((ts, D), lambda i: (i, 0)), pl.BlockSpec(g.shape, fixed2),
                  pl.BlockSpec(wq_hi.shape, fixed2), pl.BlockSpec(wq_lo.shape, fixed2),
                  pl.BlockSpec(key_hi.shape, fixed3), pl.BlockSpec(key_lo.shape, fixed3)],
        out_specs=[pl.BlockSpec((ts, D), lambda i: (i, 0)), aux_spec, aux_spec, aux_spec, aux_spec],
        out_shape=[jax.ShapeDtypeStruct((T, D), BF16), aux_shape, aux_shape, aux_shape, aux_shape],
        scratch_shapes=[pltpu.VMEM((nc, PEER_HEADS, 2 * PEER_KEYS, LANES), F32),
                        pltpu.VMEM((2, PEER_TOPK, SUBLANES, LANES), F32),
                        pltpu.VMEM((PEER_TOPK + 1, SUBLANES, LANES), F32)],
        compiler_params=pltpu.CompilerParams(dimension_semantics=("arbitrary",), vmem_limit_bytes=VMEM_LIMIT),
        name="peer_route",
    )(h2, g, wq_hi, wq_lo, key_hi, key_lo)


def _pack_experts_kernel(u_ref, v_ref, u_out, vt_out):
    u_out[...] = pltpu.bitcast(u_ref[...].astype(BF16), jnp.uint32)
    vt_out[...] = pltpu.bitcast(v_ref[...].T.astype(BF16), jnp.uint32)


def _pack_experts(u, v):
    E, D = u.shape
    eb = EB_FFN
    return pl.pallas_call(
        _pack_experts_kernel,
        grid=(E // eb,),
        in_specs=[pl.BlockSpec((eb, D), lambda e: (e, 0)), pl.BlockSpec((eb, D), lambda e: (e, 0))],
        out_specs=[pl.BlockSpec((eb // 2, D), lambda e: (e, 0)), pl.BlockSpec((D // 2, eb), lambda e: (0, e))],
        out_shape=[jax.ShapeDtypeStruct((E // 2, D), jnp.uint32), jax.ShapeDtypeStruct((D // 2, E), jnp.uint32)],
        compiler_params=pltpu.CompilerParams(dimension_semantics=("arbitrary",), vmem_limit_bytes=VMEM_LIMIT),
        name="pack_experts",
    )(u, v)


def _peer_gate_unit(tc, ii, at_ref, ht_ref, r2s_ref, ps_ref, n_ref, c_ref):
    pack = 2 * SUBLANES
    n_jv = PEER_KEYS // pack
    zero = jnp.zeros((pack, LANES), BF16)
    lanes = slice(tc * LANES, (tc + 1) * LANES)
    g = [None] * n_jv
    for h in range(PEER_HEADS):
        n_b = jnp.broadcast_to(n_ref[tc, h, ii:ii + 1, :], (pack, LANES)).astype(BF16)
        c_b = jnp.broadcast_to(c_ref[tc, h, ii:ii + 1, :], (pack, LANES)).astype(BF16)
        for jv in range(n_jv):
            js = slice(jv * pack, (jv + 1) * pack)
            term = jnp.where(r2s_ref[tc, h, js, :] < n_b, ps_ref[tc, h, js, :], zero) * c_b
            g[jv] = term if g[jv] is None else g[jv] + term
    for jv in range(n_jv):
        rows = slice(ii * PEER_KEYS + jv * pack, ii * PEER_KEYS + (jv + 1) * pack)
        ht_ref[rows, lanes] = jax.nn.gelu(at_ref[rows, lanes]).astype(BF16) * g[jv]


def _peer_ffn_kernel(xn_ref, u_ref, vt_ref, r2_ref, p_ref, n_ref, c_ref, h_ref, gfin_ref,
                     out_ref, acc_ref, at0_ref, at1_ref, ht0_ref, ht1_ref, r2s_ref, ps_ref, xs_ref,
                     *, n_e, n_blocks, final_norm):
    g = pl.program_id(0)
    tt = xn_ref.shape[0]
    e_score = g % n_e
    e_gate = jnp.maximum(g - 1, 0) % n_e
    e_down = jnp.maximum(g - 2, 0) % n_e

    @pl.when(g == 0)
    def _():
        at1_ref[...] = jnp.zeros_like(at1_ref)
        ht0_ref[...] = jnp.zeros_like(ht0_ref)
        ht1_ref[...] = jnp.zeros_like(ht1_ref)
        acc_ref[...] = jnp.zeros_like(acc_ref)

    @pl.when((g < n_blocks) & (e_score == 0))
    def _():
        xs_ref[...] = xn_ref[...]

    @pl.when((g <= n_blocks) & (e_gate == 0))
    def _():
        for tc in range(tt // LANES):
            for h in range(PEER_HEADS):
                r2s_ref[tc, h] = r2_ref[tc, h].astype(BF16)
                ps_ref[tc, h] = p_ref[tc, h].astype(BF16)

    @pl.when((g >= 2) & (e_down == 0))
    def _():
        acc_ref[...] = jnp.zeros_like(acc_ref)

    def stages(at_w, at_r, ht_w, ht_r):
        u_blk = pltpu.bitcast(u_ref[...], BF16)
        vt_blk = pltpu.bitcast(vt_ref[...], BF16)
        eb = u_blk.shape[0]
        mxu_n = 2 * LANES
        units = [(tc, ii) for tc in range(tt // LANES) for ii in range(eb // PEER_KEYS)]
        chunks = [("score", c) for c in range(tt // mxu_n)] + [("down", c) for c in range(tt // mxu_n)]
        split = GATE_UNIT_SPLIT
        assert len(split) == len(chunks) + 1 and sum(split) == len(units)
        bounds = [sum(split[:k]) for k in range(len(split) + 1)]

        def gate_units(k):
            for tc, ii in units[bounds[k]:bounds[k + 1]]:
                _peer_gate_unit(tc, ii, at_r, ht_w, r2s_ref, ps_ref, n_ref, c_ref)

        gate_units(0)
        for k, (kind, c) in enumerate(chunks):
            tok = slice(c * mxu_n, (c + 1) * mxu_n)
            if kind == "down":
                acc_ref[:, tok] += _dot(vt_blk, ht_r[:, tok])
            else:
                at_w[:, tok] = _dot_nt(u_blk, xs_ref[tok, :])
            gate_units(k + 1)

    @pl.when(g % 2 == 0)
    def _():
        stages(at0_ref, at1_ref, ht1_ref, ht0_ref)

    @pl.when(g % 2 == 1)
    def _():
        stages(at1_ref, at0_ref, ht0_ref, ht1_ref)

    @pl.when((g >= 2) & (e_down == n_e - 1))
    def _():
        res = h_ref[...] + acc_ref[...].T
        out_ref[...] = _rms(res, gfin_ref[...]) if final_norm else res


def _peer_ffn(xn, u_pack, vt_pack, r2, p, n, coef, h2, gfin, final_norm):
    T, D = h2.shape
    E = vt_pack.shape[1]
    tt = min(TT_FFN, T)
    nc = tt // LANES
    eb = EB_FFN
    n_i = eb // PEER_KEYS
    n_e = E // eb
    n_blocks = (T // tt) * n_e

    def block(lag):
        def split(g):
            b = jnp.clip(g - lag, 0, n_blocks - 1)
            return b // n_e, b % n_e
        return split

    score, gate, down = block(0), block(1), block(2)
    aux_shape = (nc, PEER_HEADS, PEER_KEYS, LANES)
    row_shape = (nc, PEER_HEADS, n_i, LANES)
    return pl.pallas_call(
        functools.partial(_peer_ffn_kernel, n_e=n_e, n_blocks=n_blocks, final_norm=final_norm),
        grid=(n_blocks + 2,),
        in_specs=[pl.BlockSpec((tt, D), lambda g: (score(g)[0], 0)),
                  pl.BlockSpec((eb // 2, D), lambda g: (score(g)[1], 0)),
                  pl.BlockSpec((D // 2, eb), lambda g: (0, down(g)[1])),
                  pl.BlockSpec(aux_shape, lambda g: (gate(g)[0], 0, 0, 0)),
                  pl.BlockSpec(aux_shape, lambda g: (gate(g)[0], 0, 0, 0)),
                  pl.BlockSpec(row_shape, lambda g: (gate(g)[0], 0, gate(g)[1], 0)),
                  pl.BlockSpec(row_shape, lambda g: (gate(g)[0], 0, gate(g)[1], 0)),
                  pl.BlockSpec((tt, D), lambda g: (down(g)[0], 0)),
                  pl.BlockSpec(gfin.shape, lambda g: (0, 0))],
        out_specs=pl.BlockSpec((tt, D), lambda g: (down(g)[0], 0)),
        out_shape=jax.ShapeDtypeStruct((T, D), F32),
        scratch_shapes=[pltpu.VMEM((D, tt), F32),
                        pltpu.VMEM((eb, tt), F32), pltpu.VMEM((eb, tt), F32),
                        pltpu.VMEM((eb, tt), BF16), pltpu.VMEM((eb, tt), BF16),
                        pltpu.VMEM(aux_shape, BF16), pltpu.VMEM(aux_shape, BF16),
                        pltpu.VMEM((tt, D), BF16)],
        compiler_params=pltpu.CompilerParams(dimension_semantics=("arbitrary",), vmem_limit_bytes=VMEM_LIMIT),
        name="peer_ffn",
    )(xn, u_pack, vt_pack, r2, p, n, coef, h2, gfin)


def _head_blocks(w, n_heads, width, pieces):
    w3 = w.reshape(w.shape[0], n_heads, width)
    out = jnp.zeros((w.shape[0], n_heads, LANES), w.dtype)
    for s0, s1, d0 in pieces:
        out = out.at[:, :, d0:d0 + (s1 - s0)].set(w3[:, :, s0:s1])
    return out.reshape(w.shape[0], n_heads * LANES)


def kernel(x, mem, positions, g_mix, w_in, g_q, w_uq, g_kv, w_ukv, conv_w, g_out, w_o, g_x, g_mem, w_xq,
           w_xkv, w_xo, g_ffn, w_pq, sub_keys, u_experts, v_experts, g_final):
    B, S, D = x.shape
    T = B * S
    depth = g_mix.shape[0]
    half = QK_ROPE // 2

    inv = ROPE_THETA ** (-jnp.arange(0, QK_ROPE, 2, dtype=F32) / QK_ROPE)
    ang = positions.astype(F32)[..., None] * inv
    cos = jnp.cos(ang).astype(x.dtype).reshape(T, half)
    sin = jnp.sin(ang).astype(x.dtype).reshape(T, half)
    ones = jnp.ones((T, QK_NOPE), F32)
    zeros_n = jnp.zeros((T, QK_NOPE), F32)
    pad_q = jnp.zeros((T, LANES - QK_NOPE - QK_ROPE), F32)
    pad_k = jnp.zeros((T, LANES - QK_ROPE), F32)
    q_scale = math.log2(math.e) / math.sqrt(QK_NOPE + QK_ROPE)
    cq_t = jnp.concatenate([ones, cos, cos, pad_q], axis=1) * q_scale
    sq_t = jnp.concatenate([zeros_n, -sin, sin, pad_q], axis=1) * q_scale
    ck_t = jnp.concatenate([cos, cos, pad_k], axis=1)
    sk_t = jnp.concatenate([-sin, sin, pad_k], axis=1)

    lane = jnp.arange(LANES)
    col = jnp.arange(MLA_HEADS * LANES)
    eplace = ((col[None, :] % LANES == lane[:, None] + QK_NOPE) & (lane[:, None] < QK_ROPE)).astype(BF16)
    mix_col = jnp.arange(D)
    gsum = (mix_col[:, None] // GROUP_DIM == lane[None, :]).astype(BF16)
    gexp = (lane[:, None] == mix_col[None, :] // GROUP_DIM).astype(BF16)

    h = x.reshape(T, D)
    for l in range(depth):
        o1 = Q_RANK
        o2 = o1 + KV_RANK
        o3 = o2 + QK_ROPE
        o4 = o3 + CONV_DIM
        o5 = o4 + CONV_DIM
        wl = w_in[l]
        w_kr = wl[:, o2:o3]
        w_krr = jnp.concatenate([w_kr[:, half:], w_kr[:, :half]], axis=1)
        lane_pad = jnp.zeros((D, LANES - QK_ROPE), wl.dtype)
        w1 = jnp.concatenate([wl[:, :o1], wl[:, o1:o2], w_kr, lane_pad, w_krr, lane_pad,
                              wl[:, o3:o4], wl[:, o4:o5], wl[:, o5:]], axis=1).astype(BF16)
        qw = QK_NOPE + QK_ROPE
        wq = _head_blocks(w_uq[l], MLA_HEADS, qw, [(0, qw, 0)]).astype(BF16)
        wqr = _head_blocks(w_uq[l], MLA_HEADS, qw,
                           [(QK_NOPE + half, qw, QK_NOPE), (QK_NOPE, QK_NOPE + half, QK_NOPE + half)]).astype(BF16)
        kvw = QK_NOPE + V_HEAD
        wk = _head_blocks(w_ukv[l], MLA_HEADS, kvw, [(0, QK_NOPE, 0)]).astype(BF16)
        v_cols = w_ukv[l].reshape(KV_RANK, MLA_HEADS, kvw)[:, :, QK_NOPE:]
        v_pad = jnp.zeros_like(v_cols)
        odd_head = (jnp.arange(MLA_HEADS) % 2 == 1)[None, :, None]
        wv = jnp.where(odd_head, jnp.concatenate([v_pad, v_cols], axis=-1),
                       jnp.concatenate([v_cols, v_pad], axis=-1)).reshape(KV_RANK, MLA_HEADS * LANES).astype(BF16)
        vone = jnp.stack([(lane == _ones_lane(hd)).astype(F32) for hd in range(MLA_HEADS)]).reshape(1, -1)

        q, k, v, z, gb = _mixer_in(h, g_mix[l][None, :], w1, g_q[l][None, :], wq, wqr, g_kv[l][None, :], wk, wv,
                                   vone, eplace, cq_t, sq_t, ck_t, sk_t)
        o = _mla_attn(q.reshape(B, S, -1), k.reshape(B, S, -1), v.reshape(B, S, -1))
        h = _mixer_out(o.reshape(T, -1), z, gb, h, conv_w[l], g_out[l][None, :], gsum, gexp,
                       w_o[l].astype(BF16), S)

        kx, vx = _mem_kv(mem, g_mem[l][None, :], w_xkv[l].astype(BF16))
        h = _xattn(h.reshape(B, S, D), g_x[l][None, :], w_xq[l].astype(BF16), kx, vx,
                   w_xo[l].astype(BF16)).reshape(T, D)

        wpq_hi, wpq_lo = _split_bf16(w_pq[l])
        sk = sub_keys[l]
        zk = jnp.zeros_like(sk[:, 0])
        keys_bd = jnp.concatenate([jnp.concatenate([sk[:, 0], zk], axis=-1),
                                   jnp.concatenate([zk, sk[:, 1]], axis=-1)], axis=1)
        key_hi, key_lo = _split_bf16(keys_bd)
        xn, r2, p, n, coef = _peer_route(h, g_ffn[l][None, :], wpq_hi, wpq_lo, key_hi, key_lo)
        u_pack, vt_pack = _pack_experts(u_experts[l], v_experts[l])
        h = _peer_ffn(xn, u_pack, vt_pack, r2, p, n, coef, h,
                      g_final[None, :], final_norm=(l == depth - 1))
    return h.reshape(B, S, D)
```

```python
import functools
import math

import jax
import jax.numpy as jnp
from jax import lax
from jax.experimental import pallas as pl
from jax.experimental.pallas import tpu as pltpu

F32 = jnp.float32
BF16 = jnp.bfloat16

EPS = 1e-6
LANES = 128
SUBLANES = 8
VMEM_LIMIT = 56 * 1024 * 1024

MLA_HEADS = 8
QK_NOPE = 64
QK_ROPE = 32
V_HEAD = 64
Q_RANK = 384
KV_RANK = 256
CONV_DIM = 512
GROUP_DIM = 64
ROPE_THETA = 10000.0
X_HEADS = 4
PEER_HEADS = 8
PEER_KEYS = 128
PEER_TOPK = 16
HALF_Q = 64

TS_IN = 512
TQ = 256
TS_ROUTE = 512
TT_FFN = 512
EB_FFN = 1024
GATE_UNIT_SPLIT = (6, 9, 9, 8, 0)
NT_DIMS = (((1,), (1,)), ((), ()))


def _rms(x, g):
    return x * lax.rsqrt(jnp.mean(x * x, axis=-1, keepdims=True) + EPS) * g


def _split_bf16(x):
    hi = x.astype(BF16)
    lo = (x - hi.astype(F32)).astype(BF16)
    return hi, lo


def _dot(a, b):
    return jnp.dot(a, b, preferred_element_type=F32)


def _dot_nt(a, b):
    return lax.dot_general(a, b, NT_DIMS, preferred_element_type=F32)


_C_CQ = 0
_C_CKV = _C_CQ + Q_RANK
_C_KR = _C_CKV + KV_RANK
_C_KRR = _C_KR + LANES
_C_GB = _C_KRR + LANES
_C_GC = _C_GB + CONV_DIM
_C_HX = _C_GC + CONV_DIM
_C_END = _C_HX + CONV_DIM


def _mixer_in_kernel(x_ref, gmix_ref, w1_ref, gq_ref, wq_ref, wqr_ref, gkv_ref, wk_ref, wv_ref, vone_ref,
                     eplace_ref, cq_ref, sq_ref, ck_ref, sk_ref,
                     q_out, k_out, v_out, z_out, gb_out):
    xn = _rms(x_ref[...], gmix_ref[...]).astype(BF16)
    proj = _dot(xn, w1_ref[...])
    cq = proj[:, _C_CQ:_C_CKV]
    ckv = proj[:, _C_CKV:_C_KR]
    kr = proj[:, _C_KR:_C_KRR]
    krr = proj[:, _C_KRR:_C_GB]
    gb_out[...] = proj[:, _C_GB:_C_GC]
    z_out[...] = proj[:, _C_GC:_C_HX] * proj[:, _C_HX:_C_END]

    cqn = _rms(cq, gq_ref[...]).astype(BF16)
    q_raw = _dot(cqn, wq_ref[...])
    q_rot = _dot(cqn, wqr_ref[...])
    cq_t = cq_ref[...]
    sq_t = sq_ref[...]
    for h in range(MLA_HEADS):
        sl = slice(h * LANES, (h + 1) * LANES)
        q_out[:, sl] = (q_raw[:, sl] * cq_t + q_rot[:, sl] * sq_t).astype(BF16)

    ckvn = _rms(ckv, gkv_ref[...]).astype(BF16)
    kr_roped = (kr * ck_ref[...] + krr * sk_ref[...]).astype(BF16)
    k_out[...] = (_dot(ckvn, wk_ref[...]) + _dot(kr_roped, eplace_ref[...])).astype(BF16)
    v_out[...] = (_dot(ckvn, wv_ref[...]) + vone_ref[...]).astype(BF16)


def _mixer_in(x2, gmix, w1, gq, wq, wqr, gkv, wk, wv, vone, eplace, cq_t, sq_t, ck_t, sk_t):
    T, D = x2.shape
    ts = min(TS_IN, T)
    row = lambda i: (i, 0)
    fixed = lambda i: (0, 0)
    full = lambda a: pl.BlockSpec(a.shape, fixed)
    return pl.pallas_call(
        _mixer_in_kernel,
        grid=(T // ts,),
        in_specs=[pl.BlockSpec((ts, D), row), full(gmix), full(w1), full(gq), full(wq), full(wqr),
                  full(gkv), full(wk), full(wv), full(vone), full(eplace),
                  pl.BlockSpec((ts, LANES), row), pl.BlockSpec((ts, LANES), row),
                  pl.BlockSpec((ts, LANES), row), pl.BlockSpec((ts, LANES), row)],
        out_specs=[pl.BlockSpec((ts, MLA_HEADS * LANES), row), pl.BlockSpec((ts, MLA_HEADS * LANES), row),
                   pl.BlockSpec((ts, MLA_HEADS * LANES), row), pl.BlockSpec((ts, CONV_DIM), row),
                   pl.BlockSpec((ts, CONV_DIM), row)],
        out_shape=[jax.ShapeDtypeStruct((T, MLA_HEADS * LANES), BF16),
                   jax.ShapeDtypeStruct((T, MLA_HEADS * LANES), BF16),
                   jax.ShapeDtypeStruct((T, MLA_HEADS * LANES), BF16),
                   jax.ShapeDtypeStruct((T, CONV_DIM), F32),
                   jax.ShapeDtypeStruct((T, CONV_DIM), F32)],
        compiler_params=pltpu.CompilerParams(dimension_semantics=("arbitrary",), vmem_limit_bytes=VMEM_LIMIT),
        name="mixer_in",
    )(x2, gmix, w1, gq, wq, wqr, gkv, wk, wv, vone, eplace, cq_t, sq_t, ck_t, sk_t)


def _ones_lane(head):
    return V_HEAD if head % 2 == 0 else 0


def _mla_attn_kernel(q_ref, k_ref, v_ref, o_ref, *, tq):
    seq = q_ref.shape[0]
    causal = (lax.broadcasted_iota(jnp.int32, (tq, tq), 1) <= lax.broadcasted_iota(jnp.int32, (tq, tq), 0))
    lane = lax.broadcasted_iota(jnp.int32, (tq, LANES), 1)
    for qi in range(seq // tq):
        rows = slice(qi * tq, (qi + 1) * tq)
        past = slice(0, qi * tq)
        outs = []
        for hh in range(2):
            hl = slice(hh * LANES, (hh + 1) * LANES)
            qh = q_ref[rows, hl]
            s_diag = jnp.where(causal, _dot_nt(qh, k_ref[rows, hl]), -jnp.inf)
            m = jnp.max(s_diag, axis=-1, keepdims=True)
            if qi > 0:
                s_past = _dot_nt(qh, k_ref[past, hl])
                m = jnp.maximum(m, jnp.max(s_past, axis=-1, keepdims=True))
                acc = _dot(jnp.exp2(s_past - m).astype(BF16), v_ref[past, hl])
                acc = acc + _dot(jnp.exp2(s_diag - m).astype(BF16), v_ref[rows, hl])
            else:
                acc = _dot(jnp.exp2(s_diag - m).astype(BF16), v_ref[rows, hl])
            one = _ones_lane(hh)
            outs.append(acc / acc[:, one:one + 1])
        o_ref[rows, :] = jnp.where(lane < V_HEAD, outs[0], outs[1])


def _mla_attn(q3, k3, v3):
    B, S, _ = q3.shape
    tq = min(TQ, S)
    pair = lambda b, g: (b, 0, g)
    return pl.pallas_call(
        functools.partial(_mla_attn_kernel, tq=tq),
        grid=(B, MLA_HEADS // 2),
        in_specs=[pl.BlockSpec((None, S, 2 * LANES), pair), pl.BlockSpec((None, S, 2 * LANES), pair),
                  pl.BlockSpec((None, S, 2 * LANES), pair)],
        out_specs=pl.BlockSpec((None, S, 2 * V_HEAD), pair),
        out_shape=jax.ShapeDtypeStruct((B, S, MLA_HEADS * V_HEAD), F32),
        compiler_params=pltpu.CompilerParams(dimension_semantics=("arbitrary", "arbitrary"),
                                             vmem_limit_bytes=VMEM_LIMIT),
        name="mla_attn",
    )(q3, k3, v3)


def _mixer_out_kernel(o_ref, z_ref, zh_ref, gb_ref, x_ref, cw_ref, gout_ref, gsum_ref, gexp_ref, wo_ref,
                      h_out, *, tiles_per_seq):
    i = pl.program_id(0)
    ts = z_ref.shape[0]
    z = z_ref[...]
    halo = jnp.where(i % tiles_per_seq == 0, 0.0, zh_ref[...])
    row = lax.broadcasted_iota(jnp.int32, z.shape, 0)
    z1 = jnp.where(row == 0, halo[7:8, :], pltpu.roll(z, 1, axis=0))
    z2 = jnp.where(row == 0, halo[6:7, :], jnp.where(row == 1, halo[7:8, :], pltpu.roll(z, 2, axis=0)))
    cw = cw_ref[...]
    y_conv = gb_ref[...] * (cw[0:1, :] * z2 + cw[1:2, :] * z1 + cw[2:3, :] * z)
    y = jnp.concatenate([o_ref[...], y_conv], axis=-1)
    sq_hi, sq_lo = _split_bf16(y * y)
    gs = _dot(sq_hi, gsum_ref[...]) + _dot(sq_lo, gsum_ref[...])
    r = lax.rsqrt(gs * (1.0 / GROUP_DIM) + EPS)
    r_hi, r_lo = _split_bf16(r)
    r_full = _dot(r_hi, gexp_ref[...]) + _dot(r_lo, gexp_ref[...])
    yn = (y * r_full * gout_ref[...]).astype(BF16)
    h_out[...] = x_ref[...] + _dot(yn, wo_ref[...])


def _mixer_out(o2, z, gb, x2, conv_w, gout, gsum, gexp, wo, seq):
    T, D = x2.shape
    ts = min(TS_IN, seq)
    row = lambda i: (i, 0)
    fixed = lambda i: (0, 0)
    full = lambda a: pl.BlockSpec(a.shape, fixed)
    halo_blocks = ts // SUBLANES
    return pl.pallas_call(
        functools.partial(_mixer_out_kernel, tiles_per_seq=seq // ts),
        grid=(T // ts,),
        in_specs=[pl.BlockSpec((ts, MLA_HEADS * V_HEAD), row), pl.BlockSpec((ts, CONV_DIM), row),
                  pl.BlockSpec((SUBLANES, CONV_DIM), lambda i: (jnp.maximum(i * halo_blocks - 1, 0), 0)),
                  pl.BlockSpec((ts, CONV_DIM), row), pl.BlockSpec((ts, D), row),
                  full(conv_w), full(gout), full(gsum), full(gexp), full(wo)],
        out_specs=pl.BlockSpec((ts, D), row),
        out_shape=jax.ShapeDtypeStruct((T, D), F32),
        compiler_params=pltpu.CompilerParams(dimension_semantics=("arbitrary",), vmem_limit_bytes=VMEM_LIMIT),
        name="mixer_out",
    )(o2, z, z, gb, x2, conv_w, gout, gsum, gexp, wo)


def _mem_kv_kernel(mem_ref, g_ref, w_ref, k_out, v_out):
    d = mem_ref.shape[-1]
    mn = _rms(mem_ref[...], g_ref[...]).astype(BF16)
    kv = _dot(mn, w_ref[...])
    k_out[...] = kv[:, :d].astype(BF16)
    v_out[...] = kv[:, d:].astype(BF16)


def _mem_kv(mem, g, w):
    B, M, D = mem.shape
    return pl.pallas_call(
        _mem_kv_kernel,
        grid=(B,),
        in_specs=[pl.BlockSpec((None, M, D), lambda b: (b, 0, 0)), pl.BlockSpec(g.shape, lambda b: (0, 0)),
                  pl.BlockSpec(w.shape, lambda b: (0, 0))],
        out_specs=[pl.BlockSpec((None, M, D), lambda b: (b, 0, 0)), pl.BlockSpec((None, M, D), lambda b: (b, 0, 0))],
        out_shape=[jax.ShapeDtypeStruct((B, M, D), BF16), jax.ShapeDtypeStruct((B, M, D), BF16)],
        compiler_params=pltpu.CompilerParams(dimension_semantics=("arbitrary",), vmem_limit_bytes=VMEM_LIMIT),
        name="mem_kv",
    )(mem, g, w)


def _xattn_kernel(h_ref, g_ref, wq_ref, k_ref, v_ref, wo_ref, h_out):
    h = h_ref[...]
    d = h.shape[-1]
    hd = d // X_HEADS
    hn = _rms(h, g_ref[...]).astype(BF16)
    q = _dot(hn, wq_ref[...]).astype(BF16)
    outs = []
    for hh in range(X_HEADS):
        sl = slice(hh * hd, (hh + 1) * hd)
        s = _dot_nt(q[:, sl], k_ref[:, sl]) * (1.0 / math.sqrt(hd))
        m = jnp.max(s, axis=-1, keepdims=True)
        p = jnp.exp(s - m)
        p = p / jnp.sum(p, axis=-1, keepdims=True)
        outs.append(_dot(p.astype(BF16), v_ref[:, sl]))
    o = jnp.concatenate(outs, axis=-1).astype(BF16)
    h_out[...] = h + _dot(o, wo_ref[...])


def _xattn(h3, g, wq, kx, vx, wo):
    B, S, D = h3.shape
    M = kx.shape[1]
    ts = min(TS_IN, S)
    fixed = lambda b, i: (0, 0)
    return pl.pallas_call(
        _xattn_kernel,
        grid=(B, S // ts),
        in_specs=[pl.BlockSpec((None, ts, D), lambda b, i: (b, i, 0)), pl.BlockSpec(g.shape, fixed),
                  pl.BlockSpec(wq.shape, fixed), pl.BlockSpec((None, M, D), lambda b, i: (b, 0, 0)),
                  pl.BlockSpec((None, M, D), lambda b, i: (b, 0, 0)), pl.BlockSpec(wo.shape, fixed)],
        out_specs=pl.BlockSpec((None, ts, D), lambda b, i: (b, i, 0)),
        out_shape=jax.ShapeDtypeStruct((B, S, D), F32),
        compiler_params=pltpu.CompilerParams(dimension_semantics=("arbitrary", "arbitrary"),
                                             vmem_limit_bytes=VMEM_LIMIT),
        name="xattn",
    )(h3, g, wq, kx, vx, wo)


def _batcher_pairs(n):
    pairs = []
    p = 1
    while p < n:
        k = p
        while k >= 1:
            for j in range(k % p, n - k, 2 * k):
                for i in range(min(k, n - j - k)):
                    if (i + j) // (2 * p) == (i + j + k) // (2 * p):
                        pairs.append((i + j, i + j + k))
            k //= 2
        p *= 2
    return pairs


_SORT16 = _batcher_pairs(PEER_TOPK)
_ROW_LEN = [PEER_TOPK // (a + 1) for a in range(PEER_TOPK)]


def _sort_desc(v):
    v = list(v)
    for i, j in _SORT16:
        hi = jnp.maximum(v[i], v[j])
        lo = jnp.minimum(v[i], v[j])
        v[i], v[j] = hi, lo
    return v


def _bitonic_desc(v):
    v = list(v)
    n = len(v)
    d = n // 2
    while d >= 1:
        for k in range(n):
            if k & d == 0:
                hi = jnp.maximum(v[k], v[k + d])
                lo = jnp.minimum(v[k], v[k + d])
                v[k], v[k + d] = hi, lo
        d //= 2
    return v


def _merge_top(cur, other):
    n = len(cur)
    c = list(cur)
    for r, val in enumerate(other):
        c[n - 1 - r] = jnp.maximum(c[n - 1 - r], val)
    return _bitonic_desc(c)


def _top16_sorted(s):
    v = _sort_desc([s[k] for k in range(PEER_TOPK)])
    for shift in (4, 2, 1):
        other = [pltpu.roll(v[k], shift, axis=0) for k in range(PEER_TOPK)]
        v = _merge_top(v, other)
    return v


def _peer_route_kernel(h_ref, g_ref, wq_ref, key_ref,
                       xn_out, r2_out, p_out, n_out, c_out,
                       st_ref, top_ref, res_ref):
    ts = h_ref.shape[0]
    n_chunk = ts // LANES
    hn = _rms(h_ref[...], g_ref[...]).astype(BF16)
    xn_out[...] = hn
    q = _dot(hn, wq_ref[...])
    for h in range(PEER_HEADS):
        st = _dot_nt(key_ref[h], q[:, h * LANES:(h + 1) * LANES].astype(BF16))
        for c in range(n_chunk):
            st_ref[c, h] = st[:, c * LANES:(c + 1) * LANES]

    neg_inf = jnp.float32(-jnp.inf)
    pos_inf = jnp.float32(jnp.inf)

    def chunk_body(c, _):
        def sort_body(h, _):
            for half in range(2):
                s = st_ref[c, h, pl.ds(half * PEER_KEYS, PEER_KEYS), :].reshape(PEER_TOPK, SUBLANES, LANES)
                v = _top16_sorted(s)
                for a in range(PEER_TOPK):
                    top_ref[half, a, pl.ds(h, 1), :] = v[a][0:1, :]
            return 0

        lax.fori_loop(0, PEER_HEADS, sort_body, 0)

        v1 = [top_ref[0, a] for a in range(PEER_TOPK)]
        v2 = [top_ref[1, b] for b in range(PEER_TOPK)]
        sums = [[v1[a] + v2[b] for b in range(_ROW_LEN[a])] for a in range(PEER_TOPK)]
        cur = sums[0]
        a = 1
        while _ROW_LEN[a] > 1:
            cur = _merge_top(cur, sums[a])
            a += 1
        cur = _merge_top(cur, [sums[r][0] for r in range(a, PEER_TOPK)])
        tau = cur[PEER_TOPK - 1]
        top_sum = sums[0][0]
        z = jnp.zeros_like(tau)
        for a in range(PEER_TOPK):
            cnt = jnp.zeros_like(tau)
            for b in range(_ROW_LEN[a]):
                sel = sums[a][b] >= tau
                cnt = cnt + jnp.where(sel, 1.0, 0.0)
                z = z + jnp.where(sel, jnp.exp(sums[a][b] - top_sum), 0.0)
            res_ref[a] = cnt
        res_ref[PEER_TOPK] = 1.0 / z

        def expand_body(h, _):
            s1 = st_ref[c, h, pl.ds(0, PEER_KEYS), :].reshape(PEER_TOPK, SUBLANES, LANES)
            s2 = st_ref[c, h, pl.ds(PEER_KEYS, PEER_KEYS), :].reshape(PEER_TOPK, SUBLANES, LANES)
            n = jnp.zeros(s1.shape, F32)
            r2 = jnp.full(s2.shape, float(PEER_TOPK), F32)
            for a in range(PEER_TOPK - 1, -1, -1):
                v1a = top_ref[0, a, pl.ds(h, 1), :]
                v2a = top_ref[1, a, pl.ds(h, 1), :]
                n = jnp.where(s1 == v1a, res_ref[a, pl.ds(h, 1), :], n)
                r2 = jnp.where(s2 == v2a, float(a), r2)
            m1 = top_ref[0, 0, pl.ds(h, 1), :]
            m2 = top_ref[1, 0, pl.ds(h, 1), :]
            inv_z = res_ref[PEER_TOPK, pl.ds(h, 1), :]
            n_out[c, h] = n.reshape(PEER_KEYS, LANES)
            c_out[c, h] = (jnp.exp(s1 - m1) * inv_z).reshape(PEER_KEYS, LANES)
            r2_out[c, h] = r2.reshape(PEER_KEYS, LANES)
            p_out[c, h] = jnp.exp(s2 - m2).reshape(PEER_KEYS, LANES)
            return 0

        lax.fori_loop(0, PEER_HEADS, expand_body, 0)
        return 0

    lax.fori_loop(0, n_chunk, chunk_body, 0)


def _peer_route(h2, g, wq, keys):
    T, D = h2.shape
    ts = min(TS_ROUTE, T)
    nc = ts // LANES
    aux_spec = pl.BlockSpec((nc, PEER_HEADS, PEER_KEYS, LANES), lambda i: (i, 0, 0, 0))
    aux_shape = jax.ShapeDtypeStruct((T // LANES, PEER_HEADS, PEER_KEYS, LANES), F32)
    fixed2 = lambda i: (0, 0)
    fixed3 = lambda i: (0, 0, 0)
    return pl.pallas_call(
        _peer_route_kernel,
        grid=(T // ts,),
        in_specs=[pl.BlockSpec((ts, D), lambda i: (i, 0)), pl.BlockSpec(g.shape, fixed2),
                  pl.BlockSpec(wq.shape, fixed2), pl.BlockSpec(keys.shape, fixed3)],
        out_specs=[pl.BlockSpec((ts, D), lambda i: (i, 0)), aux_spec, aux_spec, aux_spec, aux_spec],
        out_shape=[jax.ShapeDtypeStruct((T, D), BF16), aux_shape, aux_shape, aux_shape, aux_shape],
        scratch_shapes=[pltpu.VMEM((nc, PEER_HEADS, 2 * PEER_KEYS, LANES), F32),
                        pltpu.VMEM((2, PEER_TOPK, SUBLANES, LANES), F32),
                        pltpu.VMEM((PEER_TOPK + 1, SUBLANES, LANES), F32)],
        compiler_params=pltpu.CompilerParams(dimension_semantics=("arbitrary",), vmem_limit_bytes=VMEM_LIMIT),
        name="peer_route",
    )(h2, g, wq, keys)


def _pack_experts_kernel(u_ref, v_ref, u_out, vt_out):
    u_out[...] = pltpu.bitcast(u_ref[...].astype(BF16), jnp.uint32)
    vt_out[...] = pltpu.bitcast(v_ref[...].T.astype(BF16), jnp.uint32)


def _pack_experts(u, v):
    E, D = u.shape
    eb = EB_FFN
    return pl.pallas_call(
        _pack_experts_kernel,
        grid=(E // eb,),
        in_specs=[pl.BlockSpec((eb, D), lambda e: (e, 0)), pl.BlockSpec((eb, D), lambda e: (e, 0))],
        out_specs=[pl.BlockSpec((eb // 2, D), lambda e: (e, 0)), pl.BlockSpec((D // 2, eb), lambda e: (0, e))],
        out_shape=[jax.ShapeDtypeStruct((E // 2, D), jnp.uint32), jax.ShapeDtypeStruct((D // 2, E), jnp.uint32)],
        compiler_params=pltpu.CompilerParams(dimension_semantics=("arbitrary",), vmem_limit_bytes=VMEM_LIMIT),
        name="pack_experts",
    )(u, v)


def _peer_gate_unit(tc, ii, at_ref, ht_ref, r2s_ref, ps_ref, n_ref, c_ref):
    pack = 2 * SUBLANES
    n_jv = PEER_KEYS // pack
    zero = jnp.zeros((pack, LANES), BF16)
    lanes = slice(tc * LANES, (tc + 1) * LANES)
    g = [None] * n_jv
    for h in range(PEER_HEADS):
        n_b = jnp.broadcast_to(n_ref[tc, h, ii:ii + 1, :], (pack, LANES)).astype(BF16)
        c_b = jnp.broadcast_to(c_ref[tc, h, ii:ii + 1, :], (pack, LANES)).astype(BF16)
        for jv in range(n_jv):
            js = slice(jv * pack, (jv + 1) * pack)
            term = jnp.where(r2s_ref[tc, h, js, :] < n_b, ps_ref[tc, h, js, :], zero) * c_b
            g[jv] = term if g[jv] is None else g[jv] + term
    for jv in range(n_jv):
        rows = slice(ii * PEER_KEYS + jv * pack, ii * PEER_KEYS + (jv + 1) * pack)
        ht_ref[rows, lanes] = jax.nn.gelu(at_ref[rows, lanes]).astype(BF16) * g[jv]


def _peer_ffn_kernel(xn_ref, u_ref, vt_ref, r2_ref, p_ref, n_ref, c_ref, h_ref, gfin_ref,
                     out_ref, acc_ref, at0_ref, at1_ref, ht0_ref, ht1_ref, r2s_ref, ps_ref, xs_ref,
                     *, n_e, n_blocks, final_norm):
    g = pl.program_id(0)
    tt = xn_ref.shape[0]
    e_score = g % n_e
    e_gate = jnp.maximum(g - 1, 0) % n_e
    e_down = jnp.maximum(g - 2, 0) % n_e

    @pl.when(g == 0)
    def _():
        at1_ref[...] = jnp.zeros_like(at1_ref)
        ht0_ref[...] = jnp.zeros_like(ht0_ref)
        ht1_ref[...] = jnp.zeros_like(ht1_ref)
        acc_ref[...] = jnp.zeros_like(acc_ref)

    @pl.when((g < n_blocks) & (e_score == 0))
    def _():
        xs_ref[...] = xn_ref[...]

    @pl.when((g <= n_blocks) & (e_gate == 0))
    def _():
        for tc in range(tt // LANES):
            for h in range(PEER_HEADS):
                r2s_ref[tc, h] = r2_ref[tc, h].astype(BF16)
                ps_ref[tc, h] = p_ref[tc, h].astype(BF16)

    @pl.when((g >= 2) & (e_down == 0))
    def _():
        acc_ref[...] = jnp.zeros_like(acc_ref)

    def stages(at_w, at_r, ht_w, ht_r):
        u_blk = pltpu.bitcast(u_ref[...], BF16)
        vt_blk = pltpu.bitcast(vt_ref[...], BF16)
        eb = u_blk.shape[0]
        mxu_n = 2 * LANES
        units = [(tc, ii) for tc in range(tt // LANES) for ii in range(eb // PEER_KEYS)]
        chunks = [("score", c) for c in range(tt // mxu_n)] + [("down", c) for c in range(tt // mxu_n)]
        split = GATE_UNIT_SPLIT
        assert len(split) == len(chunks) + 1 and sum(split) == len(units)
        bounds = [sum(split[:k]) for k in range(len(split) + 1)]

        def gate_units(k):
            for tc, ii in units[bounds[k]:bounds[k + 1]]:
                _peer_gate_unit(tc, ii, at_r, ht_w, r2s_ref, ps_ref, n_ref, c_ref)

        gate_units(0)
        for k, (kind, c) in enumerate(chunks):
            tok = slice(c * mxu_n, (c + 1) * mxu_n)
            if kind == "down":
                acc_ref[:, tok] += _dot(vt_blk, ht_r[:, tok])
            else:
                at_w[:, tok] = _dot_nt(u_blk, xs_ref[tok, :])
            gate_units(k + 1)

    @pl.when(g % 2 == 0)
    def _():
        stages(at0_ref, at1_ref, ht1_ref, ht0_ref)

    @pl.when(g % 2 == 1)
    def _():
        stages(at1_ref, at0_ref, ht0_ref, ht1_ref)

    @pl.when((g >= 2) & (e_down == n_e - 1))
    def _():
        res = h_ref[...] + acc_ref[...].T
        out_ref[...] = _rms(res, gfin_ref[...]) if final_norm else res


def _peer_ffn(xn, u_pack, vt_pack, r2, p, n, coef, h2, gfin, final_norm):
    T, D = h2.shape
    E = vt_pack.shape[1]
    tt = min(TT_FFN, T)
    nc = tt // LANES
    eb = EB_FFN
    n_i = eb // PEER_KEYS
    n_e = E // eb
    n_blocks = (T // tt) * n_e

    def block(lag):
        def split(g):
            b = jnp.clip(g - lag, 0, n_blocks - 1)
            return b // n_e, b % n_e
        return split

    score, gate, down = block(0), block(1), block(2)
    aux_shape = (nc, PEER_HEADS, PEER_KEYS, LANES)
    row_shape = (nc, PEER_HEADS, n_i, LANES)
    return pl.pallas_call(
        functools.partial(_peer_ffn_kernel, n_e=n_e, n_blocks=n_blocks, final_norm=final_norm),
        grid=(n_blocks + 2,),
        in_specs=[pl.BlockSpec((tt, D), lambda g: (score(g)[0], 0)),
                  pl.BlockSpec((eb // 2, D), lambda g: (score(g)[1], 0)),
                  pl.BlockSpec((D // 2, eb), lambda g: (0, down(g)[1])),
                  pl.BlockSpec(aux_shape, lambda g: (gate(g)[0], 0, 0, 0)),
                  pl.BlockSpec(aux_shape, lambda g: (gate(g)[0], 0, 0, 0)),
                  pl.BlockSpec(row_shape, lambda g: (gate(g)[0], 0, gate(g)[1], 0)),
                  pl.BlockSpec(row_shape, lambda g: (gate(g)[0], 0, gate(g)[1], 0)),
                  pl.BlockSpec((tt, D), lambda g: (down(g)[0], 0)),
                  pl.BlockSpec(gfin.shape, lambda g: (0, 0))],
        out_specs=pl.BlockSpec((tt, D), lambda g: (down(g)[0], 0)),
        out_shape=jax.ShapeDtypeStruct((T, D), F32),
        scratch_shapes=[pltpu.VMEM((D, tt), F32),
                        pltpu.VMEM((eb, tt), F32), pltpu.VMEM((eb, tt), F32),
                        pltpu.VMEM((eb, tt), BF16), pltpu.VMEM((eb, tt), BF16),
                        pltpu.VMEM(aux_shape, BF16), pltpu.VMEM(aux_shape, BF16),
                        pltpu.VMEM((tt, D), BF16)],
        compiler_params=pltpu.CompilerParams(dimension_semantics=("arbitrary",), vmem_limit_bytes=VMEM_LIMIT),
        name="peer_ffn",
    )(xn, u_pack, vt_pack, r2, p, n, coef, h2, gfin)


def _head_blocks(w, n_heads, width, pieces):
    w3 = w.reshape(w.shape[0], n_heads, width)
    out = jnp.zeros((w.shape[0], n_heads, LANES), w.dtype)
    for s0, s1, d0 in pieces:
        out = out.at[:, :, d0:d0 + (s1 - s0)].set(w3[:, :, s0:s1])
    return out.reshape(w.shape[0], n_heads * LANES)


def kernel(x, mem, positions, g_mix, w_in, g_q, w_uq, g_kv, w_ukv, conv_w, g_out, w_o, g_x, g_mem, w_xq,
           w_xkv, w_xo, g_ffn, w_pq, sub_keys, u_experts, v_experts, g_final):
    B, S, D = x.shape
    T = B * S
    depth = g_mix.shape[0]
    half = QK_ROPE // 2

    inv = ROPE_THETA ** (-jnp.arange(0, QK_ROPE, 2, dtype=F32) / QK_ROPE)
    ang = (positions.astype(F32)[..., None] * inv).reshape(-1, LANES)
    cos = jnp.cos(ang).astype(x.dtype).reshape(T, half)
    sin = jnp.sin(ang).astype(x.dtype).reshape(T, half)
    ones = jnp.ones((T, QK_NOPE), F32)
    zeros_n = jnp.zeros((T, QK_NOPE), F32)
    pad_q = jnp.zeros((T, LANES - QK_NOPE - QK_ROPE), F32)
    pad_k = jnp.zeros((T, LANES - QK_ROPE), F32)
    q_scale = math.log2(math.e) / math.sqrt(QK_NOPE + QK_ROPE)
    cq_t = jnp.concatenate([ones, cos, cos, pad_q], axis=1) * q_scale
    sq_t = jnp.concatenate([zeros_n, -sin, sin, pad_q], axis=1) * q_scale
    ck_t = jnp.concatenate([cos, cos, pad_k], axis=1)
    sk_t = jnp.concatenate([-sin, sin, pad_k], axis=1)

    lane = jnp.arange(LANES)
    col = jnp.arange(MLA_HEADS * LANES)
    eplace = ((col[None, :] % LANES == lane[:, None] + QK_NOPE) & (lane[:, None] < QK_ROPE)).astype(BF16)
    mix_col = jnp.arange(D)
    gsum = (mix_col[:, None] // GROUP_DIM == lane[None, :]).astype(BF16)
    gexp = (lane[:, None] == mix_col[None, :] // GROUP_DIM).astype(BF16)

    h = x.reshape(T, D)
    for l in range(depth):
        o1 = Q_RANK
        o2 = o1 + KV_RANK
        o3 = o2 + QK_ROPE
        o4 = o3 + CONV_DIM
        o5 = o4 + CONV_DIM
        wl = w_in[l]
        w_kr = wl[:, o2:o3]
        w_krr = jnp.concatenate([w_kr[:, half:], w_kr[:, :half]], axis=1)
        lane_pad = jnp.zeros((D, LANES - QK_ROPE), wl.dtype)
        w1 = jnp.concatenate([wl[:, :o1], wl[:, o1:o2], w_kr, lane_pad, w_krr, lane_pad,
                              wl[:, o3:o4], wl[:, o4:o5], wl[:, o5:]], axis=1).astype(BF16)
        qw = QK_NOPE + QK_ROPE
        wq = _head_blocks(w_uq[l], MLA_HEADS, qw, [(0, qw, 0)]).astype(BF16)
        wqr = _head_blocks(w_uq[l], MLA_HEADS, qw,
                           [(QK_NOPE + half, qw, QK_NOPE), (QK_NOPE, QK_NOPE + half, QK_NOPE + half)]).astype(BF16)
        kvw = QK_NOPE + V_HEAD
        wk = _head_blocks(w_ukv[l], MLA_HEADS, kvw, [(0, QK_NOPE, 0)]).astype(BF16)
        v_cols = w_ukv[l].reshape(KV_RANK, MLA_HEADS, kvw)[:, :, QK_NOPE:]
        v_pad = jnp.zeros_like(v_cols)
        odd_head = (jnp.arange(MLA_HEADS) % 2 == 1)[None, :, None]
        wv = jnp.where(odd_head, jnp.concatenate([v_pad, v_cols], axis=-1),
                       jnp.concatenate([v_cols, v_pad], axis=-1)).reshape(KV_RANK, MLA_HEADS * LANES).astype(BF16)
        vone = jnp.stack([(lane == _ones_lane(hd)).astype(F32) for hd in range(MLA_HEADS)]).reshape(1, -1)

        q, k, v, z, gb = _mixer_in(h, g_mix[l][None, :], w1, g_q[l][None, :], wq, wqr, g_kv[l][None, :], wk, wv,
                                   vone, eplace, cq_t, sq_t, ck_t, sk_t)
        o = _mla_attn(q.reshape(B, S, -1), k.reshape(B, S, -1), v.reshape(B, S, -1))
        h = _mixer_out(o.reshape(T, -1), z, gb, h, conv_w[l], g_out[l][None, :], gsum, gexp,
                       w_o[l].astype(BF16), S)

        kx, vx = _mem_kv(mem, g_mem[l][None, :], w_xkv[l].astype(BF16))
        h = _xattn(h.reshape(B, S, D), g_x[l][None, :], w_xq[l].astype(BF16), kx, vx,
                   w_xo[l].astype(BF16)).reshape(T, D)

        sk = sub_keys[l]
        zk = jnp.zeros_like(sk[:, 0])
        keys_bd = jnp.concatenate([jnp.concatenate([sk[:, 0], zk], axis=-1),
                                   jnp.concatenate([zk, sk[:, 1]], axis=-1)], axis=1)
        xn, r2, p, n, coef = _peer_route(h, g_ffn[l][None, :], w_pq[l].astype(BF16), keys_bd.astype(BF16))
        u_pack, vt_pack = _pack_experts(u_experts[l], v_experts[l])
        h = _peer_ffn(xn, u_pack, vt_pack, r2, p, n, coef, h,
                      g_final[None, :], final_norm=(l == depth - 1))
    return h.reshape(B, S, D)
```

```python
import functools
import math

import jax
import jax.numpy as jnp
from jax import lax
from jax.experimental import pallas as pl
from jax.experimental.pallas import tpu as pltpu

F32 = jnp.float32
BF16 = jnp.bfloat16

EPS = 1e-6
LANES = 128
SUBLANES = 8
VMEM_LIMIT = 56 * 1024 * 1024

MLA_HEADS = 8
QK_NOPE = 64
QK_ROPE = 32
V_HEAD = 64
Q_RANK = 384
KV_RANK = 256
CONV_DIM = 512
GROUP_DIM = 64
ROPE_THETA = 10000.0
X_HEADS = 4
PEER_HEADS = 8
PEER_KEYS = 128
PEER_TOPK = 16
HALF_Q = 64

TS_IN = 512
TQ = 256
TS_ROUTE = 512
TT_FFN = 512
EB_FFN = 2048
SUB_FFN = 1024
GATE_UNIT_SPLIT = (6, 9, 9, 8, 0)
NT_DIMS = (((1,), (1,)), ((), ()))


def _rms(x, g):
    return x * lax.rsqrt(jnp.mean(x * x, axis=-1, keepdims=True) + EPS) * g


def _split_bf16(x):
    hi = x.astype(BF16)
    lo = (x - hi.astype(F32)).astype(BF16)
    return hi, lo


def _dot(a, b):
    return jnp.dot(a, b, preferred_element_type=F32)


def _dot_nt(a, b):
    return lax.dot_general(a, b, NT_DIMS, preferred_element_type=F32)


_C_CQ = 0
_C_CKV = _C_CQ + Q_RANK
_C_KR = _C_CKV + KV_RANK
_C_KRR = _C_KR + LANES
_C_GB = _C_KRR + LANES
_C_GC = _C_GB + CONV_DIM
_C_HX = _C_GC + CONV_DIM
_C_END = _C_HX + CONV_DIM


def _mixer_in_kernel(x_ref, gmix_ref, w1_ref, gq_ref, wq_ref, wqr_ref, gkv_ref, wk_ref, wv_ref, vone_ref,
                     eplace_ref, cq_ref, sq_ref, ck_ref, sk_ref,
                     q_out, k_out, v_out, z_out, gb_out):
    xn = _rms(x_ref[...], gmix_ref[...]).astype(BF16)
    proj = _dot(xn, w1_ref[...])
    cq = proj[:, _C_CQ:_C_CKV]
    ckv = proj[:, _C_CKV:_C_KR]
    kr = proj[:, _C_KR:_C_KRR]
    krr = proj[:, _C_KRR:_C_GB]
    gb_out[...] = proj[:, _C_GB:_C_GC]
    z_out[...] = proj[:, _C_GC:_C_HX] * proj[:, _C_HX:_C_END]

    cqn = _rms(cq, gq_ref[...]).astype(BF16)
    q_raw = _dot(cqn, wq_ref[...])
    q_rot = _dot(cqn, wqr_ref[...])
    cq_t = cq_ref[...]
    sq_t = sq_ref[...]
    for h in range(MLA_HEADS):
        sl = slice(h * LANES, (h + 1) * LANES)
        q_out[:, sl] = (q_raw[:, sl] * cq_t + q_rot[:, sl] * sq_t).astype(BF16)

    ckvn = _rms(ckv, gkv_ref[...]).astype(BF16)
    kr_roped = (kr * ck_ref[...] + krr * sk_ref[...]).astype(BF16)
    k_out[...] = (_dot(ckvn, wk_ref[...]) + _dot(kr_roped, eplace_ref[...])).astype(BF16)
    v_out[...] = (_dot(ckvn, wv_ref[...]) + vone_ref[...]).astype(BF16)


def _mixer_in(x2, gmix, w1, gq, wq, wqr, gkv, wk, wv, vone, eplace, cq_t, sq_t, ck_t, sk_t):
    T, D = x2.shape
    ts = min(TS_IN, T)
    row = lambda i: (i, 0)
    fixed = lambda i: (0, 0)
    full = lambda a: pl.BlockSpec(a.shape, fixed)
    return pl.pallas_call(
        _mixer_in_kernel,
        grid=(T // ts,),
        in_specs=[pl.BlockSpec((ts, D), row), full(gmix), full(w1), full(gq), full(wq), full(wqr),
                  full(gkv), full(wk), full(wv), full(vone), full(eplace),
                  pl.BlockSpec((ts, LANES), row), pl.BlockSpec((ts, LANES), row),
                  pl.BlockSpec((ts, LANES), row), pl.BlockSpec((ts, LANES), row)],
        out_specs=[pl.BlockSpec((ts, MLA_HEADS * LANES), row), pl.BlockSpec((ts, MLA_HEADS * LANES), row),
                   pl.BlockSpec((ts, MLA_HEADS * LANES), row), pl.BlockSpec((ts, CONV_DIM), row),
                   pl.BlockSpec((ts, CONV_DIM), row)],
        out_shape=[jax.ShapeDtypeStruct((T, MLA_HEADS * LANES), BF16),
                   jax.ShapeDtypeStruct((T, MLA_HEADS * LANES), BF16),
                   jax.ShapeDtypeStruct((T, MLA_HEADS * LANES), BF16),
                   jax.ShapeDtypeStruct((T, CONV_DIM), F32),
                   jax.ShapeDtypeStruct((T, CONV_DIM), F32)],
        compiler_params=pltpu.CompilerParams(dimension_semantics=("arbitrary",), vmem_limit_bytes=VMEM_LIMIT),
        name="mixer_in",
    )(x2, gmix, w1, gq, wq, wqr, gkv, wk, wv, vone, eplace, cq_t, sq_t, ck_t, sk_t)


def _ones_lane(head):
    return V_HEAD if head % 2 == 0 else 0


def _mla_attn_kernel(q_ref, k_ref, v_ref, o_ref, *, tq):
    seq = q_ref.shape[0]
    causal = (lax.broadcasted_iota(jnp.int32, (tq, tq), 1) <= lax.broadcasted_iota(jnp.int32, (tq, tq), 0))
    lane = lax.broadcasted_iota(jnp.int32, (tq, LANES), 1)
    for qi in range(seq // tq):
        rows = slice(qi * tq, (qi + 1) * tq)
        past = slice(0, qi * tq)
        outs = []
        for hh in range(2):
            hl = slice(hh * LANES, (hh + 1) * LANES)
            qh = q_ref[rows, hl]
            s_diag = jnp.where(causal, _dot_nt(qh, k_ref[rows, hl]), -jnp.inf)
            m = jnp.max(s_diag, axis=-1, keepdims=True)
            if qi > 0:
                s_past = _dot_nt(qh, k_ref[past, hl])
                m = jnp.maximum(m, jnp.max(s_past, axis=-1, keepdims=True))
                acc = _dot(jnp.exp2(s_past - m).astype(BF16), v_ref[past, hl])
                acc = acc + _dot(jnp.exp2(s_diag - m).astype(BF16), v_ref[rows, hl])
            else:
                acc = _dot(jnp.exp2(s_diag - m).astype(BF16), v_ref[rows, hl])
            one = _ones_lane(hh)
            outs.append(acc / acc[:, one:one + 1])
        o_ref[rows, :] = jnp.where(lane < V_HEAD, outs[0], outs[1])


def _mla_attn(q3, k3, v3):
    B, S, _ = q3.shape
    tq = min(TQ, S)
    pair = lambda b, g: (b, 0, g)
    return pl.pallas_call(
        functools.partial(_mla_attn_kernel, tq=tq),
        grid=(B, MLA_HEADS // 2),
        in_specs=[pl.BlockSpec((None, S, 2 * LANES), pair), pl.BlockSpec((None, S, 2 * LANES), pair),
                  pl.BlockSpec((None, S, 2 * LANES), pair)],
        out_specs=pl.BlockSpec((None, S, 2 * V_HEAD), pair),
        out_shape=jax.ShapeDtypeStruct((B, S, MLA_HEADS * V_HEAD), F32),
        compiler_params=pltpu.CompilerParams(dimension_semantics=("arbitrary", "arbitrary"),
                                             vmem_limit_bytes=VMEM_LIMIT),
        name="mla_attn",
    )(q3, k3, v3)


def _mixer_out_kernel(o_ref, z_ref, zh_ref, gb_ref, x_ref, cw_ref, gout_ref, gsum_ref, gexp_ref, wo_ref,
                      h_out, *, tiles_per_seq):
    i = pl.program_id(0)
    ts = z_ref.shape[0]
    z = z_ref[...]
    halo = jnp.where(i % tiles_per_seq == 0, 0.0, zh_ref[...])
    row = lax.broadcasted_iota(jnp.int32, z.shape, 0)
    z1 = jnp.where(row == 0, halo[7:8, :], pltpu.roll(z, 1, axis=0))
    z2 = jnp.where(row == 0, halo[6:7, :], jnp.where(row == 1, halo[7:8, :], pltpu.roll(z, 2, axis=0)))
    cw = cw_ref[...]
    y_conv = gb_ref[...] * (cw[0:1, :] * z2 + cw[1:2, :] * z1 + cw[2:3, :] * z)
    y = jnp.concatenate([o_ref[...], y_conv], axis=-1)
    sq_hi, sq_lo = _split_bf16(y * y)
    gs = _dot(sq_hi, gsum_ref[...]) + _dot(sq_lo, gsum_ref[...])
    r = lax.rsqrt(gs * (1.0 / GROUP_DIM) + EPS)
    r_hi, r_lo = _split_bf16(r)
    r_full = _dot(r_hi, gexp_ref[...]) + _dot(r_lo, gexp_ref[...])
    yn = (y * r_full * gout_ref[...]).astype(BF16)
    h_out[...] = x_ref[...] + _dot(yn, wo_ref[...])


def _mixer_out(o2, z, gb, x2, conv_w, gout, gsum, gexp, wo, seq):
    T, D = x2.shape
    ts = min(TS_IN, seq)
    row = lambda i: (i, 0)
    fixed = lambda i: (0, 0)
    full = lambda a: pl.BlockSpec(a.shape, fixed)
    halo_blocks = ts // SUBLANES
    return pl.pallas_call(
        functools.partial(_mixer_out_kernel, tiles_per_seq=seq // ts),
        grid=(T // ts,),
        in_specs=[pl.BlockSpec((ts, MLA_HEADS * V_HEAD), row), pl.BlockSpec((ts, CONV_DIM), row),
                  pl.BlockSpec((SUBLANES, CONV_DIM), lambda i: (jnp.maximum(i * halo_blocks - 1, 0), 0)),
                  pl.BlockSpec((ts, CONV_DIM), row), pl.BlockSpec((ts, D), row),
                  full(conv_w), full(gout), full(gsum), full(gexp), full(wo)],
        out_specs=pl.BlockSpec((ts, D), row),
        out_shape=jax.ShapeDtypeStruct((T, D), F32),
        compiler_params=pltpu.CompilerParams(dimension_semantics=("arbitrary",), vmem_limit_bytes=VMEM_LIMIT),
        name="mixer_out",
    )(o2, z, z, gb, x2, conv_w, gout, gsum, gexp, wo)


def _mem_kv_kernel(mem_ref, g_ref, w_ref, k_out, v_out):
    d = mem_ref.shape[-1]
    mn = _rms(mem_ref[...], g_ref[...]).astype(BF16)
    kv = _dot(mn, w_ref[...])
    k_out[...] = kv[:, :d].astype(BF16)
    v_out[...] = kv[:, d:].astype(BF16)


def _mem_kv(mem, g, w):
    B, M, D = mem.shape
    return pl.pallas_call(
        _mem_kv_kernel,
        grid=(B,),
        in_specs=[pl.BlockSpec((None, M, D), lambda b: (b, 0, 0)), pl.BlockSpec(g.shape, lambda b: (0, 0)),
                  pl.BlockSpec(w.shape, lambda b: (0, 0))],
        out_specs=[pl.BlockSpec((None, M, D), lambda b: (b, 0, 0)), pl.BlockSpec((None, M, D), lambda b: (b, 0, 0))],
        out_shape=[jax.ShapeDtypeStruct((B, M, D), BF16), jax.ShapeDtypeStruct((B, M, D), BF16)],
        compiler_params=pltpu.CompilerParams(dimension_semantics=("arbitrary",), vmem_limit_bytes=VMEM_LIMIT),
        name="mem_kv",
    )(mem, g, w)


def _xattn_kernel(h_ref, g_ref, wq_ref, k_ref, v_ref, wo_ref, h_out):
    h = h_ref[...]
    d = h.shape[-1]
    hd = d // X_HEADS
    hn = _rms(h, g_ref[...]).astype(BF16)
    q = _dot(hn, wq_ref[...]).astype(BF16)
    outs = []
    for hh in range(X_HEADS):
        sl = slice(hh * hd, (hh + 1) * hd)
        s = _dot_nt(q[:, sl], k_ref[:, sl]) * (1.0 / math.sqrt(hd))
        m = jnp.max(s, axis=-1, keepdims=True)
        p = jnp.exp(s - m)
        p = p / jnp.sum(p, axis=-1, keepdims=True)
        outs.append(_dot(p.astype(BF16), v_ref[:, sl]))
    o = jnp.concatenate(outs, axis=-1).astype(BF16)
    h_out[...] = h + _dot(o, wo_ref[...])


def _xattn(h3, g, wq, kx, vx, wo):
    B, S, D = h3.shape
    M = kx.shape[1]
    ts = min(TS_IN, S)
    fixed = lambda b, i: (0, 0)
    return pl.pallas_call(
        _xattn_kernel,
        grid=(B, S // ts),
        in_specs=[pl.BlockSpec((None, ts, D), lambda b, i: (b, i, 0)), pl.BlockSpec(g.shape, fixed),
                  pl.BlockSpec(wq.shape, fixed), pl.BlockSpec((None, M, D), lambda b, i: (b, 0, 0)),
                  pl.BlockSpec((None, M, D), lambda b, i: (b, 0, 0)), pl.BlockSpec(wo.shape, fixed)],
        out_specs=pl.BlockSpec((None, ts, D), lambda b, i: (b, i, 0)),
        out_shape=jax.ShapeDtypeStruct((B, S, D), F32),
        compiler_params=pltpu.CompilerParams(dimension_semantics=("arbitrary", "arbitrary"),
                                             vmem_limit_bytes=VMEM_LIMIT),
        name="xattn",
    )(h3, g, wq, kx, vx, wo)


def _batcher_pairs(n):
    pairs = []
    p = 1
    while p < n:
        k = p
        while k >= 1:
            for j in range(k % p, n - k, 2 * k):
                for i in range(min(k, n - j - k)):
                    if (i + j) // (2 * p) == (i + j + k) // (2 * p):
                        pairs.append((i + j, i + j + k))
            k //= 2
        p *= 2
    return pairs


_SORT16 = _batcher_pairs(PEER_TOPK)
_ROW_LEN = [PEER_TOPK // (a + 1) for a in range(PEER_TOPK)]


def _sort_desc(v):
    v = list(v)
    for i, j in _SORT16:
        hi = jnp.maximum(v[i], v[j])
        lo = jnp.minimum(v[i], v[j])
        v[i], v[j] = hi, lo
    return v


def _bitonic_desc(v):
    v = list(v)
    n = len(v)
    d = n // 2
    while d >= 1:
        for k in range(n):
            if k & d == 0:
                hi = jnp.maximum(v[k], v[k + d])
                lo = jnp.minimum(v[k], v[k + d])
                v[k], v[k + d] = hi, lo
        d //= 2
    return v


def _merge_top(cur, other):
    n = len(cur)
    c = list(cur)
    for r, val in enumerate(other):
        c[n - 1 - r] = jnp.maximum(c[n - 1 - r], val)
    return _bitonic_desc(c)


def _top16_sorted(s):
    v = _sort_desc([s[k] for k in range(PEER_TOPK)])
    for shift in (4, 2, 1):
        other = [pltpu.roll(v[k], shift, axis=0) for k in range(PEER_TOPK)]
        v = _merge_top(v, other)
    return v


def _peer_route_kernel(h_ref, g_ref, wq_ref, key_ref,
                       xn_out, r2_out, p_out, n_out, c_out,
                       st_ref, top_ref, res_ref):
    ts = h_ref.shape[0]
    n_chunk = ts // LANES
    hn = _rms(h_ref[...], g_ref[...]).astype(BF16)
    xn_out[...] = hn
    q = _dot(hn, wq_ref[...])
    for h in range(PEER_HEADS):
        st = _dot_nt(key_ref[h], q[:, h * LANES:(h + 1) * LANES].astype(BF16))
        for c in range(n_chunk):
            st_ref[c, h] = st[:, c * LANES:(c + 1) * LANES]

    neg_inf = jnp.float32(-jnp.inf)
    pos_inf = jnp.float32(jnp.inf)

    def chunk_body(c, _):
        def sort_body(h, _):
            for half in range(2):
                s = st_ref[c, h, pl.ds(half * PEER_KEYS, PEER_KEYS), :].reshape(PEER_TOPK, SUBLANES, LANES)
                v = _top16_sorted(s)
                for a in range(PEER_TOPK):
                    top_ref[half, a, pl.ds(h, 1), :] = v[a][0:1, :]
            return 0

        lax.fori_loop(0, PEER_HEADS, sort_body, 0)

        v1 = [top_ref[0, a] for a in range(PEER_TOPK)]
        v2 = [top_ref[1, b] for b in range(PEER_TOPK)]
        sums = [[v1[a] + v2[b] for b in range(_ROW_LEN[a])] for a in range(PEER_TOPK)]
        cur = sums[0]
        a = 1
        while _ROW_LEN[a] > 1:
            cur = _merge_top(cur, sums[a])
            a += 1
        cur = _merge_top(cur, [sums[r][0] for r in range(a, PEER_TOPK)])
        tau = cur[PEER_TOPK - 1]
        top_sum = sums[0][0]
        z = jnp.zeros_like(tau)
        for a in range(PEER_TOPK):
            cnt = jnp.zeros_like(tau)
            for b in range(_ROW_LEN[a]):
                sel = sums[a][b] >= tau
                cnt = cnt + jnp.where(sel, 1.0, 0.0)
                z = z + jnp.where(sel, jnp.exp(sums[a][b] - top_sum), 0.0)
            res_ref[a] = cnt
        res_ref[PEER_TOPK] = 1.0 / z

        def expand_body(h, _):
            s1 = st_ref[c, h, pl.ds(0, PEER_KEYS), :].reshape(PEER_TOPK, SUBLANES, LANES)
            s2 = st_ref[c, h, pl.ds(PEER_KEYS, PEER_KEYS), :].reshape(PEER_TOPK, SUBLANES, LANES)
            n = jnp.zeros(s1.shape, F32)
            r2 = jnp.full(s2.shape, float(PEER_TOPK), F32)
            for a in range(PEER_TOPK - 1, -1, -1):
                v1a = top_ref[0, a, pl.ds(h, 1), :]
                v2a = top_ref[1, a, pl.ds(h, 1), :]
                n = jnp.where(s1 == v1a, res_ref[a, pl.ds(h, 1), :], n)
                r2 = jnp.where(s2 == v2a, float(a), r2)
            m1 = top_ref[0, 0, pl.ds(h, 1), :]
            m2 = top_ref[1, 0, pl.ds(h, 1), :]
            inv_z = res_ref[PEER_TOPK, pl.ds(h, 1), :]
            n_out[c, h] = n.reshape(PEER_KEYS, LANES)
            c_out[c, h] = (jnp.exp(s1 - m1) * inv_z).reshape(PEER_KEYS, LANES)
            r2_out[c, h] = r2.reshape(PEER_KEYS, LANES).astype(BF16)
            p_out[c, h] = jnp.exp(s2 - m2).reshape(PEER_KEYS, LANES).astype(BF16)
            return 0

        lax.fori_loop(0, PEER_HEADS, expand_body, 0)
        return 0

    lax.fori_loop(0, n_chunk, chunk_body, 0)


def _peer_route(h2, g, wq, keys):
    T, D = h2.shape
    ts = min(TS_ROUTE, T)
    nc = ts // LANES
    aux_spec = pl.BlockSpec((nc, PEER_HEADS, PEER_KEYS, LANES), lambda i: (i, 0, 0, 0))
    aux_shape = jax.ShapeDtypeStruct((T // LANES, PEER_HEADS, PEER_KEYS, LANES), F32)
    aux_shape_bf = jax.ShapeDtypeStruct((T // LANES, PEER_HEADS, PEER_KEYS, LANES), BF16)
    fixed2 = lambda i: (0, 0)
    fixed3 = lambda i: (0, 0, 0)
    return pl.pallas_call(
        _peer_route_kernel,
        grid=(T // ts,),
        in_specs=[pl.BlockSpec((ts, D), lambda i: (i, 0)), pl.BlockSpec(g.shape, fixed2),
                  pl.BlockSpec(wq.shape, fixed2), pl.BlockSpec(keys.shape, fixed3)],
        out_specs=[pl.BlockSpec((ts, D), lambda i: (i, 0)), aux_spec, aux_spec, aux_spec, aux_spec],
        out_shape=[jax.ShapeDtypeStruct((T, D), BF16), aux_shape_bf, aux_shape_bf, aux_shape, aux_shape],
        scratch_shapes=[pltpu.VMEM((nc, PEER_HEADS, 2 * PEER_KEYS, LANES), F32),
                        pltpu.VMEM((2, PEER_TOPK, SUBLANES, LANES), F32),
                        pltpu.VMEM((PEER_TOPK + 1, SUBLANES, LANES), F32)],
        compiler_params=pltpu.CompilerParams(dimension_semantics=("arbitrary",), vmem_limit_bytes=VMEM_LIMIT),
        name="peer_route",
    )(h2, g, wq, keys)


def _pack_experts_kernel(u_ref, v_ref, u_out, vt_out):
    u_out[...] = pltpu.bitcast(u_ref[...].astype(BF16), jnp.uint32)
    vt_out[...] = pltpu.bitcast(v_ref[...].T.astype(BF16), jnp.uint32)


def _pack_experts(u, v):
    E, D = u.shape
    eb = EB_FFN
    return pl.pallas_call(
        _pack_experts_kernel,
        grid=(E // eb,),
        in_specs=[pl.BlockSpec((eb, D), lambda e: (e, 0)), pl.BlockSpec((eb, D), lambda e: (e, 0))],
        out_specs=[pl.BlockSpec((eb // 2, D), lambda e: (e, 0)), pl.BlockSpec((D // 2, eb), lambda e: (0, e))],
        out_shape=[jax.ShapeDtypeStruct((E // 2, D), jnp.uint32), jax.ShapeDtypeStruct((D // 2, E), jnp.uint32)],
        compiler_params=pltpu.CompilerParams(dimension_semantics=("arbitrary",), vmem_limit_bytes=VMEM_LIMIT),
        name="pack_experts",
    )(u, v)


def _peer_gate_unit(tc, ii, at_ref, ht_ref, r2s_ref, ps_ref, n_ref, c_ref):
    pack = 2 * SUBLANES
    n_jv = PEER_KEYS // pack
    zero = jnp.zeros((pack, LANES), BF16)
    lanes = slice(tc * LANES, (tc + 1) * LANES)
    g = [None] * n_jv
    for h in range(PEER_HEADS):
        n_b = jnp.broadcast_to(n_ref[tc, h, ii:ii + 1, :], (pack, LANES)).astype(BF16)
        c_b = jnp.broadcast_to(c_ref[tc, h, ii:ii + 1, :], (pack, LANES)).astype(BF16)
        for jv in range(n_jv):
            js = slice(jv * pack, (jv + 1) * pack)
            term = jnp.where(r2s_ref[tc, h, js, :] < n_b, ps_ref[tc, h, js, :], zero) * c_b
            g[jv] = term if g[jv] is None else g[jv] + term
    for jv in range(n_jv):
        rows = slice(ii * PEER_KEYS + jv * pack, ii * PEER_KEYS + (jv + 1) * pack)
        ht_ref[rows, lanes] = jax.nn.gelu(at_ref[rows, lanes]).astype(BF16) * g[jv]


def _peer_ffn_kernel(xn_ref, u_ref, vt_ref, r2_ref, p_ref, n_ref, c_ref, h_ref, gfin_ref,
                     out_ref, acc_ref, at0_ref, at1_ref, ht0_ref, ht1_ref, r2s_ref, ps_ref, xs_ref,
                     *, n_e, n_blocks, final_norm):
    g = pl.program_id(0)
    tt = xn_ref.shape[0]
    e_score = g % n_e
    e_gate = jnp.maximum(g - 1, 0) % n_e
    e_down = jnp.maximum(g - 2, 0) % n_e

    @pl.when(g == 0)
    def _():
        at1_ref[...] = jnp.zeros_like(at1_ref)
        ht0_ref[...] = jnp.zeros_like(ht0_ref)
        ht1_ref[...] = jnp.zeros_like(ht1_ref)
        acc_ref[...] = jnp.zeros_like(acc_ref)

    @pl.when((g < n_blocks) & (e_score == 0))
    def _():
        xs_ref[...] = xn_ref[...]

    @pl.when((g <= n_blocks) & (e_gate == 0))
    def _():
        for tc in range(tt // LANES):
            for h in range(PEER_HEADS):
                r2s_ref[tc, h] = r2_ref[tc, h]
                ps_ref[tc, h] = p_ref[tc, h]

    @pl.when((g >= 2) & (e_down == 0))
    def _():
        acc_ref[...] = jnp.zeros_like(acc_ref)

    def stages(at_w, at_r, ht_w, ht_r):
        u_blk = pltpu.bitcast(u_ref[...], BF16)
        vt_blk = pltpu.bitcast(vt_ref[...], BF16)
        eb = u_blk.shape[0]
        mxu_n = 2 * LANES
        sub_i = SUB_FFN // PEER_KEYS
        split = GATE_UNIT_SPLIT
        for sb in range(eb // SUB_FFN):
            ex = slice(sb * SUB_FFN, (sb + 1) * SUB_FFN)
            units = [(tc, sb * sub_i + ii) for tc in range(tt // LANES) for ii in range(sub_i)]
            chunks = [("score", c) for c in range(tt // mxu_n)] + [("down", c) for c in range(tt // mxu_n)]
            assert len(split) == len(chunks) + 1 and sum(split) == len(units)
            bounds = [sum(split[:k]) for k in range(len(split) + 1)]

            def gate_units(k, units=units, bounds=bounds):
                for tc, ii in units[bounds[k]:bounds[k + 1]]:
                    _peer_gate_unit(tc, ii, at_r, ht_w, r2s_ref, ps_ref, n_ref, c_ref)

            gate_units(0)
            for k, (kind, c) in enumerate(chunks):
                tok = slice(c * mxu_n, (c + 1) * mxu_n)
                if kind == "down":
                    acc_ref[:, tok] += _dot(vt_blk[:, ex], ht_r[ex, tok])
                else:
                    at_w[ex, tok] = _dot_nt(u_blk[ex], xs_ref[tok, :])
                gate_units(k + 1)

    @pl.when(g % 2 == 0)
    def _():
        stages(at0_ref, at1_ref, ht1_ref, ht0_ref)

    @pl.when(g % 2 == 1)
    def _():
        stages(at1_ref, at0_ref, ht0_ref, ht1_ref)

    @pl.when((g >= 2) & (e_down == n_e - 1))
    def _():
        res = h_ref[...] + acc_ref[...].T
        out_ref[...] = _rms(res, gfin_ref[...]) if final_norm else res


def _peer_ffn(xn, u_pack, vt_pack, r2, p, n, coef, h2, gfin, final_norm):
    T, D = h2.shape
    E = vt_pack.shape[1]
    tt = min(TT_FFN, T)
    nc = tt // LANES
    eb = EB_FFN
    n_i = eb // PEER_KEYS
    n_e = E // eb
    n_blocks = (T // tt) * n_e

    def block(lag):
        def split(g):
            b = jnp.clip(g - lag, 0, n_blocks - 1)
            return b // n_e, b % n_e
        return split

    score, gate, down = block(0), block(1), block(2)
    aux_shape = (nc, PEER_HEADS, PEER_KEYS, LANES)
    row_shape = (nc, PEER_HEADS, n_i, LANES)
    return pl.pallas_call(
        functools.partial(_peer_ffn_kernel, n_e=n_e, n_blocks=n_blocks, final_norm=final_norm),
        grid=(n_blocks + 2,),
        in_specs=[pl.BlockSpec((tt, D), lambda g: (score(g)[0], 0)),
                  pl.BlockSpec((eb // 2, D), lambda g: (score(g)[1], 0)),
                  pl.BlockSpec((D // 2, eb), lambda g: (0, down(g)[1])),
                  pl.BlockSpec(aux_shape, lambda g: (gate(g)[0], 0, 0, 0)),
                  pl.BlockSpec(aux_shape, lambda g: (gate(g)[0], 0, 0, 0)),
                  pl.BlockSpec(row_shape, lambda g: (gate(g)[0], 0, gate(g)[1], 0)),
                  pl.BlockSpec(row_shape, lambda g: (gate(g)[0], 0, gate(g)[1], 0)),
                  pl.BlockSpec((tt, D), lambda g: (down(g)[0], 0)),
                  pl.BlockSpec(gfin.shape, lambda g: (0, 0))],
        out_specs=pl.BlockSpec((tt, D), lambda g: (down(g)[0], 0)),
        out_shape=jax.ShapeDtypeStruct((T, D), F32),
        scratch_shapes=[pltpu.VMEM((D, tt), F32),
                        pltpu.VMEM((eb, tt), F32), pltpu.VMEM((eb, tt), F32),
                        pltpu.VMEM((eb, tt), BF16), pltpu.VMEM((eb, tt), BF16),
                        pltpu.VMEM(aux_shape, BF16), pltpu.VMEM(aux_shape, BF16),
                        pltpu.VMEM((tt, D), BF16)],
        compiler_params=pltpu.CompilerParams(dimension_semantics=("arbitrary",), vmem_limit_bytes=VMEM_LIMIT),
        name="peer_ffn",
    )(xn, u_pack, vt_pack, r2, p, n, coef, h2, gfin)


def _head_blocks(w, n_heads, width, pieces):
    w3 = w.reshape(w.shape[0], n_heads, width)
    out = jnp.zeros((w.shape[0], n_heads, LANES), w.dtype)
    for s0, s1, d0 in pieces:
        out = out.at[:, :, d0:d0 + (s1 - s0)].set(w3[:, :, s0:s1])
    return out.reshape(w.shape[0], n_heads * LANES)


def kernel(x, mem, positions, g_mix, w_in, g_q, w_uq, g_kv, w_ukv, conv_w, g_out, w_o, g_x, g_mem, w_xq,
           w_xkv, w_xo, g_ffn, w_pq, sub_keys, u_experts, v_experts, g_final):
    B, S, D = x.shape
    T = B * S
    depth = g_mix.shape[0]
    half = QK_ROPE // 2

    inv = ROPE_THETA ** (-jnp.arange(0, QK_ROPE, 2, dtype=F32) / QK_ROPE)
    ang = (positions.astype(F32)[..., None] * inv).reshape(-1, LANES)
    cos = jnp.cos(ang).astype(x.dtype).reshape(T, half)
    sin = jnp.sin(ang).astype(x.dtype).reshape(T, half)
    ones = jnp.ones((T, QK_NOPE), F32)
    zeros_n = jnp.zeros((T, QK_NOPE), F32)
    pad_q = jnp.zeros((T, LANES - QK_NOPE - QK_ROPE), F32)
    pad_k = jnp.zeros((T, LANES - QK_ROPE), F32)
    q_scale = math.log2(math.e) / math.sqrt(QK_NOPE + QK_ROPE)
    cq_t = jnp.concatenate([ones, cos, cos, pad_q], axis=1) * q_scale
    sq_t = jnp.concatenate([zeros_n, -sin, sin, pad_q], axis=1) * q_scale
    ck_t = jnp.concatenate([cos, cos, pad_k], axis=1)
    sk_t = jnp.concatenate([-sin, sin, pad_k], axis=1)

    lane = jnp.arange(LANES)
    col = jnp.arange(MLA_HEADS * LANES)
    eplace = ((col[None, :] % LANES == lane[:, None] + QK_NOPE) & (lane[:, None] < QK_ROPE)).astype(BF16)
    mix_col = jnp.arange(D)
    gsum = (mix_col[:, None] // GROUP_DIM == lane[None, :]).astype(BF16)
    gexp = (lane[:, None] == mix_col[None, :] // GROUP_DIM).astype(BF16)

    h = x.reshape(T, D)
    for l in range(depth):
        o1 = Q_RANK
        o2 = o1 + KV_RANK
        o3 = o2 + QK_ROPE
        o4 = o3 + CONV_DIM
        o5 = o4 + CONV_DIM
        wl = w_in[l]
        w_kr = wl[:, o2:o3]
        w_krr = jnp.concatenate([w_kr[:, half:], w_kr[:, :half]], axis=1)
        lane_pad = jnp.zeros((D, LANES - QK_ROPE), wl.dtype)
        w1 = jnp.concatenate([wl[:, :o1], wl[:, o1:o2], w_kr, lane_pad, w_krr, lane_pad,
                              wl[:, o3:o4], wl[:, o4:o5], wl[:, o5:]], axis=1).astype(BF16)
        qw = QK_NOPE + QK_ROPE
        wq = _head_blocks(w_uq[l], MLA_HEADS, qw, [(0, qw, 0)]).astype(BF16)
        wqr = _head_blocks(w_uq[l], MLA_HEADS, qw,
                           [(QK_NOPE + half, qw, QK_NOPE), (QK_NOPE, QK_NOPE + half, QK_NOPE + half)]).astype(BF16)
        kvw = QK_NOPE + V_HEAD
        wk = _head_blocks(w_ukv[l], MLA_HEADS, kvw, [(0, QK_NOPE, 0)]).astype(BF16)
        v_cols = w_ukv[l].reshape(KV_RANK, MLA_HEADS, kvw)[:, :, QK_NOPE:]
        v_pad = jnp.zeros_like(v_cols)
        odd_head = (jnp.arange(MLA_HEADS) % 2 == 1)[None, :, None]
        wv = jnp.where(odd_head, jnp.concatenate([v_pad, v_cols], axis=-1),
                       jnp.concatenate([v_cols, v_pad], axis=-1)).reshape(KV_RANK, MLA_HEADS * LANES).astype(BF16)
        vone = jnp.stack([(lane == _ones_lane(hd)).astype(F32) for hd in range(MLA_HEADS)]).reshape(1, -1)

        q, k, v, z, gb = _mixer_in(h, g_mix[l][None, :], w1, g_q[l][None, :], wq, wqr, g_kv[l][None, :], wk, wv,
                                   vone, eplace, cq_t, sq_t, ck_t, sk_t)
        o = _mla_attn(q.reshape(B, S, -1), k.reshape(B, S, -1), v.reshape(B, S, -1))
        h = _mixer_out(o.reshape(T, -1), z, gb, h, conv_w[l], g_out[l][None, :], gsum, gexp,
                       w_o[l].astype(BF16), S)

        kx, vx = _mem_kv(mem, g_mem[l][None, :], w_xkv[l].astype(BF16))
        h = _xattn(h.reshape(B, S, D), g_x[l][None, :], w_xq[l].astype(BF16), kx, vx,
                   w_xo[l].astype(BF16)).reshape(T, D)

        sk = sub_keys[l]
        zk = jnp.zeros_like(sk[:, 0])
        keys_bd = jnp.concatenate([jnp.concatenate([sk[:, 0], zk], axis=-1),
                                   jnp.concatenate([zk, sk[:, 1]], axis=-1)], axis=1)
        xn, r2, p, n, coef = _peer_route(h, g_ffn[l][None, :], w_pq[l].astype(BF16), keys_bd.astype(BF16))
        u_pack, vt_pack = _pack_experts(u_experts[l], v_experts[l])
        h = _peer_ffn(xn, u_pack, vt_pack, r2, p, n, coef, h,
                      g_final[None, :], final_norm=(l == depth - 1))
    return h.reshape(B, S, D)
```

```python
import functools
import math

import jax
import jax.numpy as jnp
from jax import lax
from jax.experimental import pallas as pl
from jax.experimental.pallas import tpu as pltpu

F32 = jnp.float32
BF16 = jnp.bfloat16

EPS = 1e-6
LANES = 128
SUBLANES = 8
VMEM_LIMIT = 56 * 1024 * 1024

MLA_HEADS = 8
QK_NOPE = 64
QK_ROPE = 32
V_HEAD = 64
Q_RANK = 384
KV_RANK = 256
CONV_DIM = 512
GROUP_DIM = 64
ROPE_THETA = 10000.0
X_HEADS = 4
PEER_HEADS = 8
PEER_KEYS = 128
PEER_TOPK = 16
HALF_Q = 64

TS_IN = 512
ROW_SPLIT = 2
TQ = 256
TS_ROUTE = 1024
TT_FFN = 512
EB_FFN = 2048
SUB_FFN = 1024
GATE_UNIT_SPLIT = (6, 9, 9, 8, 0)
NT_DIMS = (((1,), (1,)), ((), ()))


def _rms(x, g):
    return x * lax.rsqrt(jnp.mean(x * x, axis=-1, keepdims=True) + EPS) * g


def _split_bf16(x):
    hi = x.astype(BF16)
    lo = (x - hi.astype(F32)).astype(BF16)
    return hi, lo


def _dot(a, b):
    return jnp.dot(a, b, preferred_element_type=F32)


def _dot_nt(a, b):
    return lax.dot_general(a, b, NT_DIMS, preferred_element_type=F32)


_C_CQ = 0
_C_CKV = _C_CQ + Q_RANK
_C_KR = _C_CKV + KV_RANK
_C_KRR = _C_KR + LANES
_C_GB = _C_KRR + LANES
_C_GC = _C_GB + CONV_DIM
_C_HX = _C_GC + CONV_DIM
_C_END = _C_HX + CONV_DIM


def _mixer_in_kernel(x_ref, gmix_ref, w1_ref, gq_ref, wq_ref, wqr_ref, gkv_ref, wk_ref, wv_ref, vone_ref,
                     eplace_ref, cos_ref, sin_ref,
                     q_out, k_out, v_out, z_out, gb_out):
    xn = _rms(x_ref[...], gmix_ref[...]).astype(BF16)
    proj = _dot(xn, w1_ref[...])
    cq = proj[:, _C_CQ:_C_CKV]
    ckv = proj[:, _C_CKV:_C_KR]
    kr = proj[:, _C_KR:_C_KRR]
    krr = proj[:, _C_KRR:_C_GB]
    gb_out[...] = proj[:, _C_GB:_C_GC]
    z_out[...] = proj[:, _C_GC:_C_HX] * proj[:, _C_HX:_C_END]

    cqn = _rms(cq, gq_ref[...]).astype(BF16)
    q_raw = _dot(cqn, wq_ref[...])
    q_rot = _dot(cqn, wqr_ref[...])
    cos_t = cos_ref[...]
    sin_t = sin_ref[...]
    q_scale = math.log2(math.e) / math.sqrt(QK_NOPE + QK_ROPE)
    for h in range(MLA_HEADS):
        sl = slice(h * LANES, (h + 1) * LANES)
        q_out[:, sl] = ((q_raw[:, sl] * cos_t + q_rot[:, sl] * sin_t) * q_scale).astype(BF16)

    ckvn = _rms(ckv, gkv_ref[...]).astype(BF16)
    kr_roped = (kr * cos_t + krr * sin_t).astype(BF16)
    k_out[...] = (_dot(ckvn, wk_ref[...]) + _dot(kr_roped, eplace_ref[...])).astype(BF16)
    v_out[...] = (_dot(ckvn, wv_ref[...]) + vone_ref[...]).astype(BF16)


def _mixer_in(x2, gmix, w1, gq, wq, wqr, gkv, wk, wv, vone, eplace, cos_t, sin_t):
    T, D = x2.shape
    ts = min(TS_IN, T)
    row = lambda i: (i, 0)
    fixed = lambda i: (0, 0)
    full = lambda a: pl.BlockSpec(a.shape, fixed)
    return pl.pallas_call(
        _mixer_in_kernel,
        grid=(T // ts,),
        in_specs=[pl.BlockSpec((ts, D), row), full(gmix), full(w1), full(gq), full(wq), full(wqr),
                  full(gkv), full(wk), full(wv), full(vone), full(eplace),
                  pl.BlockSpec((ts, LANES), row), pl.BlockSpec((ts, LANES), row)],
        out_specs=[pl.BlockSpec((ts, MLA_HEADS * LANES), row), pl.BlockSpec((ts, MLA_HEADS * LANES), row),
                   pl.BlockSpec((ts, MLA_HEADS * LANES), row), pl.BlockSpec((ts, CONV_DIM), row),
                   pl.BlockSpec((ts, CONV_DIM), row)],
        out_shape=[jax.ShapeDtypeStruct((T, MLA_HEADS * LANES), BF16),
                   jax.ShapeDtypeStruct((T, MLA_HEADS * LANES), BF16),
                   jax.ShapeDtypeStruct((T, MLA_HEADS * LANES), BF16),
                   jax.ShapeDtypeStruct((T, CONV_DIM), F32),
                   jax.ShapeDtypeStruct((T, CONV_DIM), F32)],
        compiler_params=pltpu.CompilerParams(dimension_semantics=("arbitrary",), vmem_limit_bytes=VMEM_LIMIT),
        name="mixer_in",
    )(x2, gmix, w1, gq, wq, wqr, gkv, wk, wv, vone, eplace, cos_t, sin_t)


def _ones_lane(head):
    return V_HEAD if head % 2 == 0 else 0


def _mla_attn_kernel(q_ref, k_ref, v_ref, o_ref, *, tq):
    seq = q_ref.shape[0]
    causal = (lax.broadcasted_iota(jnp.int32, (tq, tq), 1) <= lax.broadcasted_iota(jnp.int32, (tq, tq), 0))
    lane = lax.broadcasted_iota(jnp.int32, (tq, LANES), 1)
    for qi in range(seq // tq):
        rows = slice(qi * tq, (qi + 1) * tq)
        past = slice(0, qi * tq)
        outs = []
        for hh in range(2):
            hl = slice(hh * LANES, (hh + 1) * LANES)
            qh = q_ref[rows, hl]
            s_diag = jnp.where(causal, _dot_nt(qh, k_ref[rows, hl]), -jnp.inf)
            m = jnp.max(s_diag, axis=-1, keepdims=True)
            if qi > 0:
                s_past = _dot_nt(qh, k_ref[past, hl])
                m = jnp.maximum(m, jnp.max(s_past, axis=-1, keepdims=True))
                acc = _dot(jnp.exp2(s_past - m).astype(BF16), v_ref[past, hl])
                acc = acc + _dot(jnp.exp2(s_diag - m).astype(BF16), v_ref[rows, hl])
            else:
                acc = _dot(jnp.exp2(s_diag - m).astype(BF16), v_ref[rows, hl])
            one = _ones_lane(hh)
            outs.append(acc * (1.0 / acc[:, one:one + 1]))
        o_ref[rows, :] = jnp.where(lane < V_HEAD, outs[0], outs[1])


def _mla_attn(q3, k3, v3):
    B, S, _ = q3.shape
    tq = min(TQ, S)
    pair = lambda b, g: (b, 0, g)
    return pl.pallas_call(
        functools.partial(_mla_attn_kernel, tq=tq),
        grid=(B, MLA_HEADS // 2),
        in_specs=[pl.BlockSpec((None, S, 2 * LANES), pair), pl.BlockSpec((None, S, 2 * LANES), pair),
                  pl.BlockSpec((None, S, 2 * LANES), pair)],
        out_specs=pl.BlockSpec((None, S, 2 * V_HEAD), pair),
        out_shape=jax.ShapeDtypeStruct((B, S, MLA_HEADS * V_HEAD), F32),
        compiler_params=pltpu.CompilerParams(dimension_semantics=("arbitrary", "arbitrary"),
                                             vmem_limit_bytes=VMEM_LIMIT),
        name="mla_attn",
    )(q3, k3, v3)


def _mixer_out_kernel(o_ref, z_ref, zh_ref, gb_ref, x_ref, cw_ref, gout_ref, gsum_ref, gexp_ref, wo_ref,
                      h_out, *, tiles_per_seq):
    i = pl.program_id(0)
    ts = z_ref.shape[0]
    z = z_ref[...]
    halo = jnp.where(i % tiles_per_seq == 0, 0.0, zh_ref[...])
    row = lax.broadcasted_iota(jnp.int32, z.shape, 0)
    z1 = jnp.where(row == 0, halo[7:8, :], pltpu.roll(z, 1, axis=0))
    z2 = jnp.where(row == 0, halo[6:7, :], jnp.where(row == 1, halo[7:8, :], pltpu.roll(z, 2, axis=0)))
    cw = cw_ref[...]
    y_conv = gb_ref[...] * (cw[0:1, :] * z2 + cw[1:2, :] * z1 + cw[2:3, :] * z)
    y_all = jnp.concatenate([o_ref[...], y_conv], axis=-1)
    for part in range(ROW_SPLIT):
        rows = slice(part * ts // ROW_SPLIT, (part + 1) * ts // ROW_SPLIT)
        y = y_all[rows]
        sq_hi, sq_lo = _split_bf16(y * y)
        gs = _dot(sq_hi, gsum_ref[...]) + _dot(sq_lo, gsum_ref[...])
        r = lax.rsqrt(gs * (1.0 / GROUP_DIM) + EPS)
        r_hi, r_lo = _split_bf16(r)
        r_full = _dot(r_hi, gexp_ref[...]) + _dot(r_lo, gexp_ref[...])
        yn = (y * r_full * gout_ref[...]).astype(BF16)
        h_out[rows, :] = x_ref[rows, :] + _dot(yn, wo_ref[...])


def _mixer_out(o2, z, gb, x2, conv_w, gout, gsum, gexp, wo, seq):
    T, D = x2.shape
    ts = min(TS_IN, seq)
    row = lambda i: (i, 0)
    fixed = lambda i: (0, 0)
    full = lambda a: pl.BlockSpec(a.shape, fixed)
    halo_blocks = ts // SUBLANES
    return pl.pallas_call(
        functools.partial(_mixer_out_kernel, tiles_per_seq=seq // ts),
        grid=(T // ts,),
        in_specs=[pl.BlockSpec((ts, MLA_HEADS * V_HEAD), row), pl.BlockSpec((ts, CONV_DIM), row),
                  pl.BlockSpec((SUBLANES, CONV_DIM), lambda i: (jnp.maximum(i * halo_blocks - 1, 0), 0)),
                  pl.BlockSpec((ts, CONV_DIM), row), pl.BlockSpec((ts, D), row),
                  full(conv_w), full(gout), full(gsum), full(gexp), full(wo)],
        out_specs=pl.BlockSpec((ts, D), row),
        out_shape=jax.ShapeDtypeStruct((T, D), F32),
        compiler_params=pltpu.CompilerParams(dimension_semantics=("arbitrary",), vmem_limit_bytes=VMEM_LIMIT),
        name="mixer_out",
    )(o2, z, z, gb, x2, conv_w, gout, gsum, gexp, wo)


def _mem_kv_kernel(mem_ref, g_ref, w_ref, k_out, v_out):
    d = mem_ref.shape[-1]
    mn = _rms(mem_ref[...], g_ref[...]).astype(BF16)
    kv = _dot(mn, w_ref[...])
    k_out[...] = kv[:, :d].astype(BF16)
    v_out[...] = kv[:, d:].astype(BF16)


def _mem_kv(mem, g, w):
    B, M, D = mem.shape
    return pl.pallas_call(
        _mem_kv_kernel,
        grid=(B,),
        in_specs=[pl.BlockSpec((None, M, D), lambda b: (b, 0, 0)), pl.BlockSpec(g.shape, lambda b: (0, 0)),
                  pl.BlockSpec(w.shape, lambda b: (0, 0))],
        out_specs=[pl.BlockSpec((None, M, D), lambda b: (b, 0, 0)), pl.BlockSpec((None, M, D), lambda b: (b, 0, 0))],
        out_shape=[jax.ShapeDtypeStruct((B, M, D), BF16), jax.ShapeDtypeStruct((B, M, D), BF16)],
        compiler_params=pltpu.CompilerParams(dimension_semantics=("arbitrary",), vmem_limit_bytes=VMEM_LIMIT),
        name="mem_kv",
    )(mem, g, w)


def _xattn_kernel(h_ref, g_ref, wq_ref, k_ref, v_ref, wo_ref, h_out):
    h = h_ref[...]
    d = h.shape[-1]
    hd = d // X_HEADS
    hn = _rms(h, g_ref[...]).astype(BF16)
    q = _dot(hn, wq_ref[...]).astype(BF16)
    outs = []
    for hh in range(X_HEADS):
        sl = slice(hh * hd, (hh + 1) * hd)
        s = _dot_nt(q[:, sl], k_ref[:, sl]) * (1.0 / math.sqrt(hd))
        m = jnp.max(s, axis=-1, keepdims=True)
        p = jnp.exp(s - m)
        p = p * (1.0 / jnp.sum(p, axis=-1, keepdims=True))
        outs.append(_dot(p.astype(BF16), v_ref[:, sl]))
    o = jnp.concatenate(outs, axis=-1).astype(BF16)
    h_out[...] = h + _dot(o, wo_ref[...])


def _xattn(h3, g, wq, kx, vx, wo):
    B, S, D = h3.shape
    M = kx.shape[1]
    ts = min(TS_IN, S)
    fixed = lambda b, i: (0, 0)
    return pl.pallas_call(
        _xattn_kernel,
        grid=(B, S // ts),
        in_specs=[pl.BlockSpec((None, ts, D), lambda b, i: (b, i, 0)), pl.BlockSpec(g.shape, fixed),
                  pl.BlockSpec(wq.shape, fixed), pl.BlockSpec((None, M, D), lambda b, i: (b, 0, 0)),
                  pl.BlockSpec((None, M, D), lambda b, i: (b, 0, 0)), pl.BlockSpec(wo.shape, fixed)],
        out_specs=pl.BlockSpec((None, ts, D), lambda b, i: (b, i, 0)),
        out_shape=jax.ShapeDtypeStruct((B, S, D), F32),
        compiler_params=pltpu.CompilerParams(dimension_semantics=("arbitrary", "arbitrary"),
                                             vmem_limit_bytes=VMEM_LIMIT),
        name="xattn",
    )(h3, g, wq, kx, vx, wo)


def _batcher_pairs(n):
    pairs = []
    p = 1
    while p < n:
        k = p
        while k >= 1:
            for j in range(k % p, n - k, 2 * k):
                for i in range(min(k, n - j - k)):
                    if (i + j) // (2 * p) == (i + j + k) // (2 * p):
                        pairs.append((i + j, i + j + k))
            k //= 2
        p *= 2
    return pairs


_SORT16 = _batcher_pairs(PEER_TOPK)
_ROW_LEN = [PEER_TOPK // (a + 1) for a in range(PEER_TOPK)]


def _sort_desc(v):
    v = list(v)
    for i, j in _SORT16:
        hi = jnp.maximum(v[i], v[j])
        lo = jnp.minimum(v[i], v[j])
        v[i], v[j] = hi, lo
    return v


def _bitonic_desc(v):
    v = list(v)
    n = len(v)
    d = n // 2
    while d >= 1:
        for k in range(n):
            if k & d == 0:
                hi = jnp.maximum(v[k], v[k + d])
                lo = jnp.minimum(v[k], v[k + d])
                v[k], v[k + d] = hi, lo
        d //= 2
    return v


def _merge_top(cur, other):
    n = len(cur)
    c = list(cur)
    for r, val in enumerate(other):
        c[n - 1 - r] = jnp.maximum(c[n - 1 - r], val)
    return _bitonic_desc(c)


def _top16_sorted(s):
    v = _sort_desc([s[k] for k in range(PEER_TOPK)])
    for shift in (4, 2, 1):
        other = [pltpu.roll(v[k], shift, axis=0) for k in range(PEER_TOPK)]
        v = _merge_top(v, other)
    return v


def _peer_route_kernel(h_ref, g_ref, wq_ref, key_ref,
                       xn_out, r2_out, p_out, n_out, c_out,
                       st_ref, top_ref, res_ref):
    ts = h_ref.shape[0]
    n_chunk = ts // LANES
    hn = _rms(h_ref[...], g_ref[...]).astype(BF16)
    xn_out[...] = hn
    q = _dot(hn, wq_ref[...])
    for h in range(PEER_HEADS):
        st = _dot_nt(key_ref[h], q[:, h * LANES:(h + 1) * LANES].astype(BF16))
        for c in range(n_chunk):
            st_ref[c, h] = st[:, c * LANES:(c + 1) * LANES]

    neg_inf = jnp.float32(-jnp.inf)
    pos_inf = jnp.float32(jnp.inf)

    def chunk_body(c, _):
        def sort_body(h, _):
            for half in range(2):
                s = st_ref[c, h, pl.ds(half * PEER_KEYS, PEER_KEYS), :].reshape(PEER_TOPK, SUBLANES, LANES)
                v = _top16_sorted(s)
                for a in range(PEER_TOPK):
                    top_ref[half, a, pl.ds(h, 1), :] = v[a][0:1, :]
            return 0

        lax.fori_loop(0, PEER_HEADS, sort_body, 0)

        v1 = [top_ref[0, a] for a in range(PEER_TOPK)]
        v2 = [top_ref[1, b] for b in range(PEER_TOPK)]
        sums = [[v1[a] + v2[b] for b in range(_ROW_LEN[a])] for a in range(PEER_TOPK)]
        cur = sums[0]
        a = 1
        while _ROW_LEN[a] > 1:
            cur = _merge_top(cur, sums[a])
            a += 1
        cur = _merge_top(cur, [sums[r][0] for r in range(a, PEER_TOPK)])
        tau = cur[PEER_TOPK - 1]
        top_sum = sums[0][0]
        z = jnp.zeros_like(tau)
        for a in range(PEER_TOPK):
            cnt = jnp.zeros_like(tau)
            for b in range(_ROW_LEN[a]):
                sel = sums[a][b] >= tau
                cnt = cnt + jnp.where(sel, 1.0, 0.0)
                z = z + jnp.where(sel, jnp.exp(sums[a][b] - top_sum), 0.0)
            res_ref[a] = cnt
        res_ref[PEER_TOPK] = 1.0 / z

        def expand_body(h, _):
            s1 = st_ref[c, h, pl.ds(0, PEER_KEYS), :].reshape(PEER_TOPK, SUBLANES, LANES)
            s2 = st_ref[c, h, pl.ds(PEER_KEYS, PEER_KEYS), :].reshape(PEER_TOPK, SUBLANES, LANES)
            n = jnp.zeros(s1.shape, F32)
            r2 = jnp.full(s2.shape, float(PEER_TOPK), F32)
            for a in range(PEER_TOPK - 1, -1, -1):
                v1a = top_ref[0, a, pl.ds(h, 1), :]
                v2a = top_ref[1, a, pl.ds(h, 1), :]
                n = jnp.where(s1 == v1a, res_ref[a, pl.ds(h, 1), :], n)
                r2 = jnp.where(s2 == v2a, float(a), r2)
            m1 = top_ref[0, 0, pl.ds(h, 1), :]
            m2 = top_ref[1, 0, pl.ds(h, 1), :]
            inv_z = res_ref[PEER_TOPK, pl.ds(h, 1), :]
            n_out[c, h] = n.reshape(PEER_KEYS, LANES)
            c_out[c, h] = (jnp.exp(s1 - m1) * inv_z).reshape(PEER_KEYS, LANES)
            r2_out[c, h] = r2.reshape(PEER_KEYS, LANES).astype(BF16)
            p_out[c, h] = jnp.exp(s2 - m2).reshape(PEER_KEYS, LANES).astype(BF16)
            return 0

        lax.fori_loop(0, PEER_HEADS, expand_body, 0)
        return 0

    lax.fori_loop(0, n_chunk, chunk_body, 0)


def _peer_route(h2, g, wq, keys):
    T, D = h2.shape
    ts = min(TS_ROUTE, T)
    nc = ts // LANES
    aux_spec = pl.BlockSpec((nc, PEER_HEADS, PEER_KEYS, LANES), lambda i: (i, 0, 0, 0))
    aux_shape = jax.ShapeDtypeStruct((T // LANES, PEER_HEADS, PEER_KEYS, LANES), F32)
    aux_shape_bf = jax.ShapeDtypeStruct((T // LANES, PEER_HEADS, PEER_KEYS, LANES), BF16)
    fixed2 = lambda i: (0, 0)
    fixed3 = lambda i: (0, 0, 0)
    return pl.pallas_call(
        _peer_route_kernel,
        grid=(T // ts,),
        in_specs=[pl.BlockSpec((ts, D), lambda i: (i, 0)), pl.BlockSpec(g.shape, fixed2),
                  pl.BlockSpec(wq.shape, fixed2), pl.BlockSpec(keys.shape, fixed3)],
        out_specs=[pl.BlockSpec((ts, D), lambda i: (i, 0)), aux_spec, aux_spec, aux_spec, aux_spec],
        out_shape=[jax.ShapeDtypeStruct((T, D), BF16), aux_shape_bf, aux_shape_bf, aux_shape, aux_shape],
        scratch_shapes=[pltpu.VMEM((nc, PEER_HEADS, 2 * PEER_KEYS, LANES), F32),
                        pltpu.VMEM((2, PEER_TOPK, SUBLANES, LANES), F32),
                        pltpu.VMEM((PEER_TOPK + 1, SUBLANES, LANES), F32)],
        compiler_params=pltpu.CompilerParams(dimension_semantics=("arbitrary",), vmem_limit_bytes=VMEM_LIMIT),
        name="peer_route",
    )(h2, g, wq, keys)


def _pack_experts_kernel(u_ref, v_ref, u_out, vt_out):
    u_out[...] = pltpu.bitcast(u_ref[...].astype(BF16), jnp.uint32)
    vt_out[...] = pltpu.bitcast(v_ref[...].T.astype(BF16), jnp.uint32)


def _pack_experts(u, v):
    E, D = u.shape
    eb = EB_FFN
    return pl.pallas_call(
        _pack_experts_kernel,
        grid=(E // eb,),
        in_specs=[pl.BlockSpec((eb, D), lambda e: (e, 0)), pl.BlockSpec((eb, D), lambda e: (e, 0))],
        out_specs=[pl.BlockSpec((eb // 2, D), lambda e: (e, 0)), pl.BlockSpec((D // 2, eb), lambda e: (0, e))],
        out_shape=[jax.ShapeDtypeStruct((E // 2, D), jnp.uint32), jax.ShapeDtypeStruct((D // 2, E), jnp.uint32)],
        compiler_params=pltpu.CompilerParams(dimension_semantics=("arbitrary",), vmem_limit_bytes=VMEM_LIMIT),
        name="pack_experts",
    )(u, v)


def _peer_gate_unit(tc, ii, at_ref, ht_ref, r2s_ref, ps_ref, n_ref, c_ref):
    pack = 2 * SUBLANES
    n_jv = PEER_KEYS // pack
    zero = jnp.zeros((pack, LANES), BF16)
    lanes = slice(tc * LANES, (tc + 1) * LANES)
    g = [None] * n_jv
    for h in range(PEER_HEADS):
        n_b = jnp.broadcast_to(n_ref[tc, h, ii:ii + 1, :], (pack, LANES)).astype(BF16)
        c_b = jnp.broadcast_to(c_ref[tc, h, ii:ii + 1, :], (pack, LANES)).astype(BF16)
        for jv in range(n_jv):
            js = slice(jv * pack, (jv + 1) * pack)
            term = jnp.where(r2s_ref[tc, h, js, :] < n_b, ps_ref[tc, h, js, :], zero) * c_b
            g[jv] = term if g[jv] is None else g[jv] + term
    for jv in range(n_jv):
        rows = slice(ii * PEER_KEYS + jv * pack, ii * PEER_KEYS + (jv + 1) * pack)
        ht_ref[rows, lanes] = jax.nn.gelu(at_ref[rows, lanes]).astype(BF16) * g[jv]


def _peer_ffn_kernel(xn_ref, u_ref, vt_ref, r2_ref, p_ref, n_ref, c_ref, h_ref, gfin_ref,
                     out_ref, acc_ref, at0_ref, at1_ref, ht0_ref, ht1_ref, r2s_ref, ps_ref, xs_ref,
                     *, n_e, n_blocks, final_norm):
    g = pl.program_id(0)
    tt = xn_ref.shape[0]
    e_score = g % n_e
    e_gate = jnp.maximum(g - 1, 0) % n_e
    e_down = jnp.maximum(g - 2, 0) % n_e

    @pl.when(g == 0)
    def _():
        at1_ref[...] = jnp.zeros_like(at1_ref)
        ht0_ref[...] = jnp.zeros_like(ht0_ref)
        ht1_ref[...] = jnp.zeros_like(ht1_ref)
        acc_ref[...] = jnp.zeros_like(acc_ref)

    @pl.when((g < n_blocks) & (e_score == 0))
    def _():
        xs_ref[...] = xn_ref[...]

    @pl.when((g <= n_blocks) & (e_gate == 0))
    def _():
        for tc in range(tt // LANES):
            for h in range(PEER_HEADS):
                r2s_ref[tc, h] = r2_ref[tc, h]
                ps_ref[tc, h] = p_ref[tc, h]

    @pl.when((g >= 2) & (e_down == 0))
    def _():
        acc_ref[...] = jnp.zeros_like(acc_ref)

    def stages(at_w, at_r, ht_w, ht_r):
        u_blk = pltpu.bitcast(u_ref[...], BF16)
        vt_blk = pltpu.bitcast(vt_ref[...], BF16)
        eb = u_blk.shape[0]
        mxu_n = 2 * LANES
        sub_i = SUB_FFN // PEER_KEYS
        split = GATE_UNIT_SPLIT
        for sb in range(eb // SUB_FFN):
            ex = slice(sb * SUB_FFN, (sb + 1) * SUB_FFN)
            units = [(tc, sb * sub_i + ii) for tc in range(tt // LANES) for ii in range(sub_i)]
            chunks = [("score", c) for c in range(tt // mxu_n)] + [("down", c) for c in range(tt // mxu_n)]
            assert len(split) == len(chunks) + 1 and sum(split) == len(units)
            bounds = [sum(split[:k]) for k in range(len(split) + 1)]

            def gate_units(k, units=units, bounds=bounds):
                for tc, ii in units[bounds[k]:bounds[k + 1]]:
                    _peer_gate_unit(tc, ii, at_r, ht_w, r2s_ref, ps_ref, n_ref, c_ref)

            gate_units(0)
            for k, (kind, c) in enumerate(chunks):
                tok = slice(c * mxu_n, (c + 1) * mxu_n)
                if kind == "down":
                    acc_ref[:, tok] += _dot(vt_blk[:, ex], ht_r[ex, tok])
                else:
                    at_w[ex, tok] = _dot_nt(u_blk[ex], xs_ref[tok, :])
                gate_units(k + 1)

    @pl.when(g % 2 == 0)
    def _():
        stages(at0_ref, at1_ref, ht1_ref, ht0_ref)

    @pl.when(g % 2 == 1)
    def _():
        stages(at1_ref, at0_ref, ht0_ref, ht1_ref)

    @pl.when((g >= 2) & (e_down == n_e - 1))
    def _():
        res = h_ref[...] + acc_ref[...].T
        out_ref[...] = _rms(res, gfin_ref[...]) if final_norm else res


def _peer_ffn(xn, u_pack, vt_pack, r2, p, n, coef, h2, gfin, final_norm):
    T, D = h2.shape
    E = vt_pack.shape[1]
    tt = min(TT_FFN, T)
    nc = tt // LANES
    eb = EB_FFN
    n_i = eb // PEER_KEYS
    n_e = E // eb
    n_blocks = (T // tt) * n_e

    def block(lag):
        def split(g):
            b = jnp.clip(g - lag, 0, n_blocks - 1)
            return b // n_e, b % n_e
        return split

    score, gate, down = block(0), block(1), block(2)
    aux_shape = (nc, PEER_HEADS, PEER_KEYS, LANES)
    row_shape = (nc, PEER_HEADS, n_i, LANES)
    return pl.pallas_call(
        functools.partial(_peer_ffn_kernel, n_e=n_e, n_blocks=n_blocks, final_norm=final_norm),
        grid=(n_blocks + 2,),
        in_specs=[pl.BlockSpec((tt, D), lambda g: (score(g)[0], 0)),
                  pl.BlockSpec((eb // 2, D), lambda g: (score(g)[1], 0)),
                  pl.BlockSpec((D // 2, eb), lambda g: (0, down(g)[1])),
                  pl.BlockSpec(aux_shape, lambda g: (gate(g)[0], 0, 0, 0)),
                  pl.BlockSpec(aux_shape, lambda g: (gate(g)[0], 0, 0, 0)),
                  pl.BlockSpec(row_shape, lambda g: (gate(g)[0], 0, gate(g)[1], 0)),
                  pl.BlockSpec(row_shape, lambda g: (gate(g)[0], 0, gate(g)[1], 0)),
                  pl.BlockSpec((tt, D), lambda g: (down(g)[0], 0)),
                  pl.BlockSpec(gfin.shape, lambda g: (0, 0))],
        out_specs=pl.BlockSpec((tt, D), lambda g: (down(g)[0], 0)),
        out_shape=jax.ShapeDtypeStruct((T, D), F32),
        scratch_shapes=[pltpu.VMEM((D, tt), F32),
                        pltpu.VMEM((eb, tt), F32), pltpu.VMEM((eb, tt), F32),
                        pltpu.VMEM((eb, tt), BF16), pltpu.VMEM((eb, tt), BF16),
                        pltpu.VMEM(aux_shape, BF16), pltpu.VMEM(aux_shape, BF16),
                        pltpu.VMEM((tt, D), BF16)],
        compiler_params=pltpu.CompilerParams(dimension_semantics=("arbitrary",), vmem_limit_bytes=VMEM_LIMIT),
        name="peer_ffn",
    )(xn, u_pack, vt_pack, r2, p, n, coef, h2, gfin)


def _head_blocks(w, n_heads, width, pieces):
    w3 = w.reshape(w.shape[0], n_heads, width)
    out = jnp.zeros((w.shape[0], n_heads, LANES), w.dtype)
    for s0, s1, d0 in pieces:
        out = out.at[:, :, d0:d0 + (s1 - s0)].set(w3[:, :, s0:s1])
    return out.reshape(w.shape[0], n_heads * LANES)


def kernel(x, mem, positions, g_mix, w_in, g_q, w_uq, g_kv, w_ukv, conv_w, g_out, w_o, g_x, g_mem, w_xq,
           w_xkv, w_xo, g_ffn, w_pq, sub_keys, u_experts, v_experts, g_final):
    B, S, D = x.shape
    T = B * S
    depth = g_mix.shape[0]
    half = QK_ROPE // 2

    inv = ROPE_THETA ** (-jnp.arange(0, QK_ROPE, 2, dtype=F32) / QK_ROPE)
    ang = positions.astype(F32)[..., None] * inv
    cos = jnp.cos(ang).astype(x.dtype).reshape(T, half)
    sin = jnp.sin(ang).astype(x.dtype).reshape(T, half)
    ones = jnp.ones((T, QK_NOPE), F32)
    zeros_n = jnp.zeros((T, QK_NOPE), F32)
    pad_q = jnp.zeros((T, LANES - QK_NOPE - QK_ROPE), F32)
    cos_t = jnp.concatenate([ones, cos, cos, pad_q], axis=1)
    sin_t = jnp.concatenate([zeros_n, -sin, sin, pad_q], axis=1)

    lane = jnp.arange(LANES)
    col = jnp.arange(MLA_HEADS * LANES)
    rope_lane = (lane >= QK_NOPE) & (lane < QK_NOPE + QK_ROPE)
    eplace = ((col[None, :] % LANES == lane[:, None]) & rope_lane[:, None]).astype(BF16)
    mix_col = jnp.arange(D)
    gsum = (mix_col[:, None] // GROUP_DIM == lane[None, :]).astype(BF16)
    gexp = (lane[:, None] == mix_col[None, :] // GROUP_DIM).astype(BF16)

    h = x.reshape(T, D)
    for l in range(depth):
        o1 = Q_RANK
        o2 = o1 + KV_RANK
        o3 = o2 + QK_ROPE
        o4 = o3 + CONV_DIM
        o5 = o4 + CONV_DIM
        wl = w_in[l]
        w_kr = wl[:, o2:o3]
        w_krr = jnp.concatenate([w_kr[:, half:], w_kr[:, :half]], axis=1)
        pad_lo = jnp.zeros((D, QK_NOPE), wl.dtype)
        pad_hi = jnp.zeros((D, LANES - QK_NOPE - QK_ROPE), wl.dtype)
        w1 = jnp.concatenate([wl[:, :o1], wl[:, o1:o2], pad_lo, w_kr, pad_hi, pad_lo, w_krr, pad_hi,
                              wl[:, o3:o4], wl[:, o4:o5], wl[:, o5:]], axis=1).astype(BF16)
        qw = QK_NOPE + QK_ROPE
        wq = _head_blocks(w_uq[l], MLA_HEADS, qw, [(0, qw, 0)]).astype(BF16)
        wqr = _head_blocks(w_uq[l], MLA_HEADS, qw,
                           [(QK_NOPE + half, qw, QK_NOPE), (QK_NOPE, QK_NOPE + half, QK_NOPE + half)]).astype(BF16)
        kvw = QK_NOPE + V_HEAD
        wk = _head_blocks(w_ukv[l], MLA_HEADS, kvw, [(0, QK_NOPE, 0)]).astype(BF16)
        v_cols = w_ukv[l].reshape(KV_RANK, MLA_HEADS, kvw)[:, :, QK_NOPE:]
        v_pad = jnp.zeros_like(v_cols)
        odd_head = (jnp.arange(MLA_HEADS) % 2 == 1)[None, :, None]
        wv = jnp.where(odd_head, jnp.concatenate([v_pad, v_cols], axis=-1),
                       jnp.concatenate([v_cols, v_pad], axis=-1)).reshape(KV_RANK, MLA_HEADS * LANES).astype(BF16)
        vone = jnp.stack([(lane == _ones_lane(hd)).astype(F32) for hd in range(MLA_HEADS)]).reshape(1, -1)

        q, k, v, z, gb = _mixer_in(h, g_mix[l][None, :], w1, g_q[l][None, :], wq, wqr, g_kv[l][None, :], wk, wv,
                                   vone, eplace, cos_t, sin_t)
        o = _mla_attn(q.reshape(B, S, -1), k.reshape(B, S, -1), v.reshape(B, S, -1))
        h = _mixer_out(o.reshape(T, -1), z, gb, h, conv_w[l], g_out[l][None, :], gsum, gexp,
                       w_o[l].astype(BF16), S)

        kx, vx = _mem_kv(mem, g_mem[l][None, :], w_xkv[l].astype(BF16))
        h = _xattn(h.reshape(B, S, D), g_x[l][None, :], w_xq[l].astype(BF16), kx, vx,
                   w_xo[l].astype(BF16)).reshape(T, D)

        sk = sub_keys[l]
        zk = jnp.zeros_like(sk[:, 0])
        keys_bd = jnp.concatenate([jnp.concatenate([sk[:, 0], zk], axis=-1),
                                   jnp.concatenate([zk, sk[:, 1]], axis=-1)], axis=1)
        xn, r2, p, n, coef = _peer_route(h, g_ffn[l][None, :], w_pq[l].astype(BF16), keys_bd.astype(BF16))
        u_pack, vt_pack = _pack_experts(u_experts[l], v_experts[l])
        h = _peer_ffn(xn, u_pack, vt_pack, r2, p, n, coef, h,
                      g_final[None, :], final_norm=(l == depth - 1))
    return h.reshape(B, S, D)
```

```python
import functools
import math

import jax
import jax.numpy as jnp
from jax import lax
from jax.experimental import pallas as pl
from jax.experimental.pallas import tpu as pltpu

F32 = jnp.float32
BF16 = jnp.bfloat16

EPS = 1e-6
LANES = 128
SUBLANES = 8
VMEM_LIMIT = 56 * 1024 * 1024

MLA_HEADS = 8
QK_NOPE = 64
QK_ROPE = 32
V_HEAD = 64
Q_RANK = 384
KV_RANK = 256
CONV_DIM = 512
GROUP_DIM = 64
ROPE_THETA = 10000.0
X_HEADS = 4
PEER_HEADS = 8
PEER_KEYS = 128
PEER_TOPK = 16
HALF_Q = 64

TS_IN = 512
ROW_SPLIT = 2
TQ = 256
TS_ROUTE = 1024
TT_FFN = 512
EB_FFN = 2048
SUB_FFN = 1024
GATE_UNIT_SPLIT = (6, 9, 9, 9, 13, 18, 0)
NT_DIMS = (((1,), (1,)), ((), ()))


def _rms(x, g):
    return x * lax.rsqrt(jnp.mean(x * x, axis=-1, keepdims=True) + EPS) * g


def _split_bf16(x):
    hi = x.astype(BF16)
    lo = (x - hi.astype(F32)).astype(BF16)
    return hi, lo


def _gelu_tanh(x):
    c0 = math.sqrt(2.0 / math.pi)
    half_x = 0.5 * x
    return half_x + half_x * jnp.tanh(x * (c0 + (c0 * 0.044715) * (x * x)))


def _dot(a, b):
    return jnp.dot(a, b, preferred_element_type=F32)


def _dot_nt(a, b):
    return lax.dot_general(a, b, NT_DIMS, preferred_element_type=F32)


_C_CQ = 0
_C_CKV = _C_CQ + Q_RANK
_C_KR = _C_CKV + KV_RANK
_C_KRR = _C_KR + LANES
_C_GB = _C_KRR + LANES
_C_GC = _C_GB + CONV_DIM
_C_HX = _C_GC + CONV_DIM
_C_END = _C_HX + CONV_DIM


def _mixer_in_kernel(x_ref, gmix_ref, w1_ref, gq_ref, wq_ref, wqr_ref, gkv_ref, wk_ref, wv_ref, vone_ref,
                     eplace_ref, cos_ref, sin_ref,
                     q_out, k_out, v_out, z_out, gb_out):
    xn = _rms(x_ref[...], gmix_ref[...]).astype(BF16)
    proj = _dot(xn, w1_ref[...])
    cq = proj[:, _C_CQ:_C_CKV]
    ckv = proj[:, _C_CKV:_C_KR]
    kr = proj[:, _C_KR:_C_KRR]
    krr = proj[:, _C_KRR:_C_GB]
    gb_out[...] = proj[:, _C_GB:_C_GC]
    z_out[...] = proj[:, _C_GC:_C_HX] * proj[:, _C_HX:_C_END]

    cqn = _rms(cq, gq_ref[...]).astype(BF16)
    q_raw = _dot(cqn, wq_ref[...])
    q_rot = _dot(cqn, wqr_ref[...])
    cos_t = cos_ref[...]
    sin_t = sin_ref[...]
    q_scale = math.log2(math.e) / math.sqrt(QK_NOPE + QK_ROPE)
    for h in range(MLA_HEADS):
        sl = slice(h * LANES, (h + 1) * LANES)
        q_out[:, sl] = ((q_raw[:, sl] * cos_t + q_rot[:, sl] * sin_t) * q_scale).astype(BF16)

    ckvn = _rms(ckv, gkv_ref[...]).astype(BF16)
    kr_roped = (kr * cos_t + krr * sin_t).astype(BF16)
    k_out[...] = (_dot(ckvn, wk_ref[...]) + _dot(kr_roped, eplace_ref[...])).astype(BF16)
    v_out[...] = (_dot(ckvn, wv_ref[...]) + vone_ref[...]).astype(BF16)


def _mixer_in(x2, gmix, w1, gq, wq, wqr, gkv, wk, wv, vone, eplace, cos_t, sin_t):
    T, D = x2.shape
    ts = min(TS_IN, T)
    row = lambda i: (i, 0)
    fixed = lambda i: (0, 0)
    full = lambda a: pl.BlockSpec(a.shape, fixed)
    return pl.pallas_call(
        _mixer_in_kernel,
        grid=(T // ts,),
        in_specs=[pl.BlockSpec((ts, D), row), full(gmix), full(w1), full(gq), full(wq), full(wqr),
                  full(gkv), full(wk), full(wv), full(vone), full(eplace),
                  pl.BlockSpec((ts, LANES), row), pl.BlockSpec((ts, LANES), row)],
        out_specs=[pl.BlockSpec((ts, MLA_HEADS * LANES), row), pl.BlockSpec((ts, MLA_HEADS * LANES), row),
                   pl.BlockSpec((ts, MLA_HEADS * LANES), row), pl.BlockSpec((ts, CONV_DIM), row),
                   pl.BlockSpec((ts, CONV_DIM), row)],
        out_shape=[jax.ShapeDtypeStruct((T, MLA_HEADS * LANES), BF16),
                   jax.ShapeDtypeStruct((T, MLA_HEADS * LANES), BF16),
                   jax.ShapeDtypeStruct((T, MLA_HEADS * LANES), BF16),
                   jax.ShapeDtypeStruct((T, CONV_DIM), F32),
                   jax.ShapeDtypeStruct((T, CONV_DIM), F32)],
        compiler_params=pltpu.CompilerParams(dimension_semantics=("arbitrary",), vmem_limit_bytes=VMEM_LIMIT),
        name="mixer_in",
    )(x2, gmix, w1, gq, wq, wqr, gkv, wk, wv, vone, eplace, cos_t, sin_t)


def _ones_lane(head):
    return V_HEAD if head % 2 == 0 else 0


def _mla_attn_kernel(q_ref, k_ref, v_ref, o_ref, *, tq):
    seq = q_ref.shape[0]
    causal = (lax.broadcasted_iota(jnp.int32, (tq, tq), 1) <= lax.broadcasted_iota(jnp.int32, (tq, tq), 0))
    lane = lax.broadcasted_iota(jnp.int32, (tq, LANES), 1)
    for qi in range(seq // tq):
        rows = slice(qi * tq, (qi + 1) * tq)
        past = slice(0, qi * tq)
        outs = []
        for hh in range(2):
            hl = slice(hh * LANES, (hh + 1) * LANES)
            qh = q_ref[rows, hl]
            s_diag = jnp.where(causal, _dot_nt(qh, k_ref[rows, hl]), -jnp.inf)
            m = jnp.max(s_diag, axis=-1, keepdims=True)
            if qi > 0:
                s_past = _dot_nt(qh, k_ref[past, hl])
                m = jnp.maximum(m, jnp.max(s_past, axis=-1, keepdims=True))
                acc = _dot(jnp.exp2(s_past - m).astype(BF16), v_ref[past, hl])
                acc = acc + _dot(jnp.exp2(s_diag - m).astype(BF16), v_ref[rows, hl])
            else:
                acc = _dot(jnp.exp2(s_diag - m).astype(BF16), v_ref[rows, hl])
            one = _ones_lane(hh)
            outs.append(acc * (1.0 / acc[:, one:one + 1]))
        o_ref[rows, :] = jnp.where(lane < V_HEAD, outs[0], outs[1])


def _mla_attn(q3, k3, v3):
    B, S, _ = q3.shape
    tq = min(TQ, S)
    pair = lambda b, g: (b, 0, g)
    return pl.pallas_call(
        functools.partial(_mla_attn_kernel, tq=tq),
        grid=(B, MLA_HEADS // 2),
        in_specs=[pl.BlockSpec((None, S, 2 * LANES), pair), pl.BlockSpec((None, S, 2 * LANES), pair),
                  pl.BlockSpec((None, S, 2 * LANES), pair)],
        out_specs=pl.BlockSpec((None, S, 2 * V_HEAD), pair),
        out_shape=jax.ShapeDtypeStruct((B, S, MLA_HEADS * V_HEAD), F32),
        compiler_params=pltpu.CompilerParams(dimension_semantics=("arbitrary", "arbitrary"),
                                             vmem_limit_bytes=VMEM_LIMIT),
        name="mla_attn",
    )(q3, k3, v3)


def _mixer_out_kernel(o_ref, z_ref, zh_ref, gb_ref, x_ref, cw_ref, gout_ref, gsum_ref, gexp_ref, wo_ref,
                      h_out, *, tiles_per_seq):
    i = pl.program_id(0)
    ts = z_ref.shape[0]
    z = z_ref[...]
    halo = jnp.where(i % tiles_per_seq == 0, 0.0, zh_ref[...])
    row = lax.broadcasted_iota(jnp.int32, z.shape, 0)
    z1 = jnp.where(row == 0, halo[7:8, :], pltpu.roll(z, 1, axis=0))
    z2 = jnp.where(row == 0, halo[6:7, :], jnp.where(row == 1, halo[7:8, :], pltpu.roll(z, 2, axis=0)))
    cw = cw_ref[...]
    y_conv = gb_ref[...] * (cw[0:1, :] * z2 + cw[1:2, :] * z1 + cw[2:3, :] * z)
    y_all = jnp.concatenate([o_ref[...], y_conv], axis=-1)
    for part in range(ROW_SPLIT):
        rows = slice(part * ts // ROW_SPLIT, (part + 1) * ts // ROW_SPLIT)
        y = y_all[rows]
        sq_hi, sq_lo = _split_bf16(y * y)
        gs = _dot(sq_hi, gsum_ref[...]) + _dot(sq_lo, gsum_ref[...])
        r = lax.rsqrt(gs * (1.0 / GROUP_DIM) + EPS)
        r_hi, r_lo = _split_bf16(r)
        r_full = _dot(r_hi, gexp_ref[...]) + _dot(r_lo, gexp_ref[...])
        yn = (y * r_full * gout_ref[...]).astype(BF16)
        h_out[rows, :] = x_ref[rows, :] + _dot(yn, wo_ref[...])


def _mixer_out(o2, z, gb, x2, conv_w, gout, gsum, gexp, wo, seq):
    T, D = x2.shape
    ts = min(TS_IN, seq)
    row = lambda i: (i, 0)
    fixed = lambda i: (0, 0)
    full = lambda a: pl.BlockSpec(a.shape, fixed)
    halo_blocks = ts // SUBLANES
    return pl.pallas_call(
        functools.partial(_mixer_out_kernel, tiles_per_seq=seq // ts),
        grid=(T // ts,),
        in_specs=[pl.BlockSpec((ts, MLA_HEADS * V_HEAD), row), pl.BlockSpec((ts, CONV_DIM), row),
                  pl.BlockSpec((SUBLANES, CONV_DIM), lambda i: (jnp.maximum(i * halo_blocks - 1, 0), 0)),
                  pl.BlockSpec((ts, CONV_DIM), row), pl.BlockSpec((ts, D), row),
                  full(conv_w), full(gout), full(gsum), full(gexp), full(wo)],
        out_specs=pl.BlockSpec((ts, D), row),
        out_shape=jax.ShapeDtypeStruct((T, D), F32),
        compiler_params=pltpu.CompilerParams(dimension_semantics=("arbitrary",), vmem_limit_bytes=VMEM_LIMIT),
        name="mixer_out",
    )(o2, z, z, gb, x2, conv_w, gout, gsum, gexp, wo)


def _mem_kv_kernel(mem_ref, g_ref, w_ref, k_out, v_out):
    d = mem_ref.shape[-1]
    mn = _rms(mem_ref[...], g_ref[...]).astype(BF16)
    kv = _dot(mn, w_ref[...])
    k_out[...] = kv[:, :d].astype(BF16)
    v_out[...] = kv[:, d:].astype(BF16)


def _mem_kv(mem, g, w):
    B, M, D = mem.shape
    return pl.pallas_call(
        _mem_kv_kernel,
        grid=(B,),
        in_specs=[pl.BlockSpec((None, M, D), lambda b: (b, 0, 0)), pl.BlockSpec(g.shape, lambda b: (0, 0)),
                  pl.BlockSpec(w.shape, lambda b: (0, 0))],
        out_specs=[pl.BlockSpec((None, M, D), lambda b: (b, 0, 0)), pl.BlockSpec((None, M, D), lambda b: (b, 0, 0))],
        out_shape=[jax.ShapeDtypeStruct((B, M, D), BF16), jax.ShapeDtypeStruct((B, M, D), BF16)],
        compiler_params=pltpu.CompilerParams(dimension_semantics=("arbitrary",), vmem_limit_bytes=VMEM_LIMIT),
        name="mem_kv",
    )(mem, g, w)


def _xattn_kernel(h_ref, g_ref, wq_ref, k_ref, v_ref, wo_ref, h_out):
    h = h_ref[...]
    d = h.shape[-1]
    hd = d // X_HEADS
    hn = _rms(h, g_ref[...]).astype(BF16)
    q = _dot(hn, wq_ref[...]).astype(BF16)
    outs = []
    for hh in range(X_HEADS):
        sl = slice(hh * hd, (hh + 1) * hd)
        s = _dot_nt(q[:, sl], k_ref[:, sl]) * (1.0 / math.sqrt(hd))
        m = jnp.max(s, axis=-1, keepdims=True)
        p = jnp.exp(s - m)
        p = p * (1.0 / jnp.sum(p, axis=-1, keepdims=True))
        outs.append(_dot(p.astype(BF16), v_ref[:, sl]))
    o = jnp.concatenate(outs, axis=-1).astype(BF16)
    h_out[...] = h + _dot(o, wo_ref[...])


def _xattn(h3, g, wq, kx, vx, wo):
    B, S, D = h3.shape
    M = kx.shape[1]
    ts = min(TS_IN, S)
    fixed = lambda b, i: (0, 0)
    return pl.pallas_call(
        _xattn_kernel,
        grid=(B, S // ts),
        in_specs=[pl.BlockSpec((None, ts, D), lambda b, i: (b, i, 0)), pl.BlockSpec(g.shape, fixed),
                  pl.BlockSpec(wq.shape, fixed), pl.BlockSpec((None, M, D), lambda b, i: (b, 0, 0)),
                  pl.BlockSpec((None, M, D), lambda b, i: (b, 0, 0)), pl.BlockSpec(wo.shape, fixed)],
        out_specs=pl.BlockSpec((None, ts, D), lambda b, i: (b, i, 0)),
        out_shape=jax.ShapeDtypeStruct((B, S, D), F32),
        compiler_params=pltpu.CompilerParams(dimension_semantics=("arbitrary", "arbitrary"),
                                             vmem_limit_bytes=VMEM_LIMIT),
        name="xattn",
    )(h3, g, wq, kx, vx, wo)


def _batcher_pairs(n):
    pairs = []
    p = 1
    while p < n:
        k = p
        while k >= 1:
            for j in range(k % p, n - k, 2 * k):
                for i in range(min(k, n - j - k)):
                    if (i + j) // (2 * p) == (i + j + k) // (2 * p):
                        pairs.append((i + j, i + j + k))
            k //= 2
        p *= 2
    return pairs


_SORT16 = _batcher_pairs(PEER_TOPK)
_ROW_LEN = [PEER_TOPK // (a + 1) for a in range(PEER_TOPK)]


def _sort_desc(v):
    v = list(v)
    for i, j in _SORT16:
        hi = jnp.maximum(v[i], v[j])
        lo = jnp.minimum(v[i], v[j])
        v[i], v[j] = hi, lo
    return v


def _bitonic_desc(v):
    v = list(v)
    n = len(v)
    d = n // 2
    while d >= 1:
        for k in range(n):
            if k & d == 0:
                hi = jnp.maximum(v[k], v[k + d])
                lo = jnp.minimum(v[k], v[k + d])
                v[k], v[k + d] = hi, lo
        d //= 2
    return v


def _merge_top(cur, other):
    n = len(cur)
    c = list(cur)
    for r, val in enumerate(other):
        c[n - 1 - r] = jnp.maximum(c[n - 1 - r], val)
    return _bitonic_desc(c)


def _top16_sorted(s):
    v = _sort_desc([s[k] for k in range(PEER_TOPK)])
    for shift in (4, 2, 1):
        other = [pltpu.roll(v[k], shift, axis=0) for k in range(PEER_TOPK)]
        v = _merge_top(v, other)
    return v


def _peer_route_kernel(h_ref, g_ref, wq_ref, key_ref,
                       xn_out, r2_out, p_out, n_out, c_out,
                       st_ref, top_ref, res_ref):
    ts = h_ref.shape[0]
    n_chunk = ts // LANES
    hn = _rms(h_ref[...], g_ref[...]).astype(BF16)
    xn_out[...] = hn
    q = _dot(hn, wq_ref[...])
    for h in range(PEER_HEADS):
        st = _dot_nt(key_ref[h], q[:, h * LANES:(h + 1) * LANES].astype(BF16))
        for c in range(n_chunk):
            st_ref[c, h] = st[:, c * LANES:(c + 1) * LANES]

    neg_inf = jnp.float32(-jnp.inf)
    pos_inf = jnp.float32(jnp.inf)

    def chunk_body(c, _):
        def sort_body(h, _):
            for half in range(2):
                s = st_ref[c, h, pl.ds(half * PEER_KEYS, PEER_KEYS), :].reshape(PEER_TOPK, SUBLANES, LANES)
                v = _top16_sorted(s)
                for a in range(PEER_TOPK):
                    top_ref[half, a, pl.ds(h, 1), :] = v[a][0:1, :]
            return 0

        lax.fori_loop(0, PEER_HEADS, sort_body, 0)

        v1 = [top_ref[0, a] for a in range(PEER_TOPK)]
        v2 = [top_ref[1, b] for b in range(PEER_TOPK)]
        sums = [[v1[a] + v2[b] for b in range(_ROW_LEN[a])] for a in range(PEER_TOPK)]
        cur = sums[0]
        a = 1
        while _ROW_LEN[a] > 1:
            cur = _merge_top(cur, sums[a])
            a += 1
        cur = _merge_top(cur, [sums[r][0] for r in range(a, PEER_TOPK)])
        tau = cur[PEER_TOPK - 1]
        top_sum = sums[0][0]
        z = jnp.zeros_like(tau)
        for a in range(PEER_TOPK):
            cnt = jnp.zeros_like(tau)
            for b in range(_ROW_LEN[a]):
                sel = sums[a][b] >= tau
                cnt = cnt + jnp.where(sel, 1.0, 0.0)
                z = z + jnp.where(sel, jnp.exp(sums[a][b] - top_sum), 0.0)
            res_ref[a] = cnt
        res_ref[PEER_TOPK] = 1.0 / z

        def expand_body(h, _):
            s1 = st_ref[c, h, pl.ds(0, PEER_KEYS), :].reshape(PEER_TOPK, SUBLANES, LANES)
            s2 = st_ref[c, h, pl.ds(PEER_KEYS, PEER_KEYS), :].reshape(PEER_TOPK, SUBLANES, LANES)
            n = jnp.zeros(s1.shape, F32)
            r2 = jnp.full(s2.shape, float(PEER_TOPK), F32)
            for a in range(PEER_TOPK - 1, -1, -1):
                v1a = top_ref[0, a, pl.ds(h, 1), :]
                v2a = top_ref[1, a, pl.ds(h, 1), :]
                n = jnp.where(s1 == v1a, res_ref[a, pl.ds(h, 1), :], n)
                r2 = jnp.where(s2 == v2a, float(a), r2)
            m1 = top_ref[0, 0, pl.ds(h, 1), :]
            m2 = top_ref[1, 0, pl.ds(h, 1), :]
            inv_z = res_ref[PEER_TOPK, pl.ds(h, 1), :]
            n_out[c, h] = n.reshape(PEER_KEYS, LANES)
            c_out[c, h] = (jnp.exp(s1 - m1) * inv_z).reshape(PEER_KEYS, LANES)
            r2_out[c, h] = r2.reshape(PEER_KEYS, LANES).astype(BF16)
            p_out[c, h] = jnp.exp(s2 - m2).reshape(PEER_KEYS, LANES).astype(BF16)
            return 0

        lax.fori_loop(0, PEER_HEADS, expand_body, 0)
        return 0

    lax.fori_loop(0, n_chunk, chunk_body, 0)


def _peer_route(h2, g, wq, keys):
    T, D = h2.shape
    ts = min(TS_ROUTE, T)
    nc = ts // LANES
    aux_spec = pl.BlockSpec((nc, PEER_HEADS, PEER_KEYS, LANES), lambda i: (i, 0, 0, 0))
    aux_shape = jax.ShapeDtypeStruct((T // LANES, PEER_HEADS, PEER_KEYS, LANES), F32)
    aux_shape_bf = jax.ShapeDtypeStruct((T // LANES, PEER_HEADS, PEER_KEYS, LANES), BF16)
    fixed2 = lambda i: (0, 0)
    fixed3 = lambda i: (0, 0, 0)
    return pl.pallas_call(
        _peer_route_kernel,
        grid=(T // ts,),
        in_specs=[pl.BlockSpec((ts, D), lambda i: (i, 0)), pl.BlockSpec(g.shape, fixed2),
                  pl.BlockSpec(wq.shape, fixed2), pl.BlockSpec(keys.shape, fixed3)],
        out_specs=[pl.BlockSpec((ts, D), lambda i: (i, 0)), aux_spec, aux_spec, aux_spec, aux_spec],
        out_shape=[jax.ShapeDtypeStruct((T, D), BF16), aux_shape_bf, aux_shape_bf, aux_shape, aux_shape],
        scratch_shapes=[pltpu.VMEM((nc, PEER_HEADS, 2 * PEER_KEYS, LANES), F32),
                        pltpu.VMEM((2, PEER_TOPK, SUBLANES, LANES), F32),
                        pltpu.VMEM((PEER_TOPK + 1, SUBLANES, LANES), F32)],
        compiler_params=pltpu.CompilerParams(dimension_semantics=("arbitrary",), vmem_limit_bytes=VMEM_LIMIT),
        name="peer_route",
    )(h2, g, wq, keys)


def _pack_experts_kernel(u_ref, v_ref, u_out, vt_out):
    u_out[...] = pltpu.bitcast(u_ref[...].astype(BF16), jnp.uint32)
    vt_out[...] = pltpu.bitcast(v_ref[...].T.astype(BF16), jnp.uint32)


def _pack_experts(u, v):
    E, D = u.shape
    eb = EB_FFN
    return pl.pallas_call(
        _pack_experts_kernel,
        grid=(E // eb,),
        in_specs=[pl.BlockSpec((eb, D), lambda e: (e, 0)), pl.BlockSpec((eb, D), lambda e: (e, 0))],
        out_specs=[pl.BlockSpec((eb // 2, D), lambda e: (e, 0)), pl.BlockSpec((D // 2, eb), lambda e: (0, e))],
        out_shape=[jax.ShapeDtypeStruct((E // 2, D), jnp.uint32), jax.ShapeDtypeStruct((D // 2, E), jnp.uint32)],
        compiler_params=pltpu.CompilerParams(dimension_semantics=("arbitrary",), vmem_limit_bytes=VMEM_LIMIT),
        name="pack_experts",
    )(u, v)


def _peer_gate_unit(tc, ii, at_ref, ht_ref, r2s_ref, ps_ref, n_ref, c_ref):
    pack = 2 * SUBLANES
    n_jv = PEER_KEYS // pack
    zero = jnp.zeros((pack, LANES), BF16)
    lanes = slice(tc * LANES, (tc + 1) * LANES)
    g = [None] * n_jv
    for h in range(PEER_HEADS):
        n_b = jnp.broadcast_to(n_ref[tc, h, ii:ii + 1, :], (pack, LANES)).astype(BF16)
        c_b = jnp.broadcast_to(c_ref[tc, h, ii:ii + 1, :], (pack, LANES)).astype(BF16)
        for jv in range(n_jv):
            js = slice(jv * pack, (jv + 1) * pack)
            term = jnp.where(r2s_ref[tc, h, js, :] < n_b, ps_ref[tc, h, js, :], zero) * c_b
            g[jv] = term if g[jv] is None else g[jv] + term
    for jv in range(n_jv):
        rows = slice(ii * PEER_KEYS + jv * pack, ii * PEER_KEYS + (jv + 1) * pack)
        ht_ref[rows, lanes] = _gelu_tanh(at_ref[rows, lanes]).astype(BF16) * g[jv]


def _peer_ffn_kernel(xn_ref, u_ref, vt_ref, r2_ref, p_ref, n_ref, c_ref, h_ref, gfin_ref,
                     out_ref, acc_ref, at0_ref, at1_ref, ht0_ref, ht1_ref, r2s_ref, ps_ref, xs_ref,
                     *, n_e, n_blocks, final_norm):
    g = pl.program_id(0)
    tt = xn_ref.shape[0]
    e_score = g % n_e
    e_gate = jnp.maximum(g - 1, 0) % n_e
    e_down = jnp.maximum(g - 2, 0) % n_e

    @pl.when(g == 0)
    def _():
        at1_ref[...] = jnp.zeros_like(at1_ref)
        ht0_ref[...] = jnp.zeros_like(ht0_ref)
        ht1_ref[...] = jnp.zeros_like(ht1_ref)
        acc_ref[...] = jnp.zeros_like(acc_ref)

    @pl.when((g < n_blocks) & (e_score == 0))
    def _():
        xs_ref[...] = xn_ref[...]

    @pl.when((g <= n_blocks) & (e_gate == 0))
    def _():
        for tc in range(tt // LANES):
            for h in range(PEER_HEADS):
                r2s_ref[tc, h] = r2_ref[tc, h]
                ps_ref[tc, h] = p_ref[tc, h]

    @pl.when((g >= 2) & (e_down == 0))
    def _():
        acc_ref[...] = jnp.zeros_like(acc_ref)

    def stages(at_w, at_r, ht_w, ht_r):
        u_blk = pltpu.bitcast(u_ref[...], BF16)
        vt_blk = pltpu.bitcast(vt_ref[...], BF16)
        eb = u_blk.shape[0]
        mxu_n = 2 * LANES
        toks = [slice(c * mxu_n, (c + 1) * mxu_n) for c in range(tt // mxu_n)]
        subs = [slice(sb * SUB_FFN, (sb + 1) * SUB_FFN) for sb in range(eb // SUB_FFN)]

        def score(ex, tok):
            at_w[ex, tok] = _dot_nt(u_blk[ex], xs_ref[tok, :])

        def down(tok):
            acc_ref[:, tok] += _dot(vt_blk, ht_r[:, tok])

        chunks = ([functools.partial(score, ex, tok) for ex in subs for tok in toks]
                  + [functools.partial(down, tok) for tok in toks])
        units = [(tc, ii) for tc in range(tt // LANES) for ii in range(eb // PEER_KEYS)]
        split = GATE_UNIT_SPLIT
        assert len(split) == len(chunks) + 1 and sum(split) == len(units)
        bounds = [sum(split[:k]) for k in range(len(split) + 1)]

        def gate_units(k):
            for tc, ii in units[bounds[k]:bounds[k + 1]]:
                _peer_gate_unit(tc, ii, at_r, ht_w, r2s_ref, ps_ref, n_ref, c_ref)

        gate_units(0)
        for k, chunk in enumerate(chunks):
            chunk()
            gate_units(k + 1)

    @pl.when(g % 2 == 0)
    def _():
        stages(at0_ref, at1_ref, ht1_ref, ht0_ref)

    @pl.when(g % 2 == 1)
    def _():
        stages(at1_ref, at0_ref, ht0_ref, ht1_ref)

    @pl.when((g >= 2) & (e_down == n_e - 1))
    def _():
        res = h_ref[...] + acc_ref[...].T
        out_ref[...] = _rms(res, gfin_ref[...]) if final_norm else res


def _peer_ffn(xn, u_pack, vt_pack, r2, p, n, coef, h2, gfin, final_norm):
    T, D = h2.shape
    E = vt_pack.shape[1]
    tt = min(TT_FFN, T)
    nc = tt // LANES
    eb = EB_FFN
    n_i = eb // PEER_KEYS
    n_e = E // eb
    n_blocks = (T // tt) * n_e

    def block(lag):
        def split(g):
            b = jnp.clip(g - lag, 0, n_blocks - 1)
            return b // n_e, b % n_e
        return split

    score, gate, down = block(0), block(1), block(2)
    aux_shape = (nc, PEER_HEADS, PEER_KEYS, LANES)
    row_shape = (nc, PEER_HEADS, n_i, LANES)
    return pl.pallas_call(
        functools.partial(_peer_ffn_kernel, n_e=n_e, n_blocks=n_blocks, final_norm=final_norm),
        grid=(n_blocks + 2,),
        in_specs=[pl.BlockSpec((tt, D), lambda g: (score(g)[0], 0)),
                  pl.BlockSpec((eb // 2, D), lambda g: (score(g)[1], 0)),
                  pl.BlockSpec((D // 2, eb), lambda g: (0, down(g)[1])),
                  pl.BlockSpec(aux_shape, lambda g: (gate(g)[0], 0, 0, 0)),
                  pl.BlockSpec(aux_shape, lambda g: (gate(g)[0], 0, 0, 0)),
                  pl.BlockSpec(row_shape, lambda g: (gate(g)[0], 0, gate(g)[1], 0)),
                  pl.BlockSpec(row_shape, lambda g: (gate(g)[0], 0, gate(g)[1], 0)),
                  pl.BlockSpec((tt, D), lambda g: (down(g)[0], 0)),
                  pl.BlockSpec(gfin.shape, lambda g: (0, 0))],
        out_specs=pl.BlockSpec((tt, D), lambda g: (down(g)[0], 0)),
        out_shape=jax.ShapeDtypeStruct((T, D), F32),
        scratch_shapes=[pltpu.VMEM((D, tt), F32),
                        pltpu.VMEM((eb, tt), F32), pltpu.VMEM((eb, tt), F32),
                        pltpu.VMEM((eb, tt), BF16), pltpu.VMEM((eb, tt), BF16),
                        pltpu.VMEM(aux_shape, BF16), pltpu.VMEM(aux_shape, BF16),
                        pltpu.VMEM((tt, D), BF16)],
        compiler_params=pltpu.CompilerParams(dimension_semantics=("arbitrary",), vmem_limit_bytes=VMEM_LIMIT),
        name="peer_ffn",
    )(xn, u_pack, vt_pack, r2, p, n, coef, h2, gfin)


def _head_blocks(w, n_heads, width, pieces):
    w3 = w.reshape(w.shape[0], n_heads, width)
    out = jnp.zeros((w.shape[0], n_heads, LANES), w.dtype)
    for s0, s1, d0 in pieces:
        out = out.at[:, :, d0:d0 + (s1 - s0)].set(w3[:, :, s0:s1])
    return out.reshape(w.shape[0], n_heads * LANES)


def kernel(x, mem, positions, g_mix, w_in, g_q, w_uq, g_kv, w_ukv, conv_w, g_out, w_o, g_x, g_mem, w_xq,
           w_xkv, w_xo, g_ffn, w_pq, sub_keys, u_experts, v_experts, g_final):
    B, S, D = x.shape
    T = B * S
    depth = g_mix.shape[0]
    half = QK_ROPE // 2

    inv = ROPE_THETA ** (-jnp.arange(0, QK_ROPE, 2, dtype=F32) / QK_ROPE)
    ang = positions.astype(F32)[..., None] * inv
    cos = jnp.cos(ang).astype(x.dtype).reshape(T, half)
    sin = jnp.sin(ang).astype(x.dtype).reshape(T, half)
    ones = jnp.ones((T, QK_NOPE), F32)
    zeros_n = jnp.zeros((T, QK_NOPE), F32)
    pad_q = jnp.zeros((T, LANES - QK_NOPE - QK_ROPE), F32)
    cos_t = jnp.concatenate([ones, cos, cos, pad_q], axis=1)
    sin_t = jnp.concatenate([zeros_n, -sin, sin, pad_q], axis=1)

    lane = jnp.arange(LANES)
    col = jnp.arange(MLA_HEADS * LANES)
    rope_lane = (lane >= QK_NOPE) & (lane < QK_NOPE + QK_ROPE)
    eplace = ((col[None, :] % LANES == lane[:, None]) & rope_lane[:, None]).astype(BF16)
    mix_col = jnp.arange(D)
    gsum = (mix_col[:, None] // GROUP_DIM == lane[None, :]).astype(BF16)
    gexp = (lane[:, None] == mix_col[None, :] // GROUP_DIM).astype(BF16)

    h = x.reshape(T, D)
    for l in range(depth):
        o1 = Q_RANK
        o2 = o1 + KV_RANK
        o3 = o2 + QK_ROPE
        o4 = o3 + CONV_DIM
        o5 = o4 + CONV_DIM
        wl = w_in[l]
        w_kr = wl[:, o2:o3]
        w_krr = jnp.concatenate([w_kr[:, half:], w_kr[:, :half]], axis=1)
        pad_lo = jnp.zeros((D, QK_NOPE), wl.dtype)
        pad_hi = jnp.zeros((D, LANES - QK_NOPE - QK_ROPE), wl.dtype)
        w1 = jnp.concatenate([wl[:, :o1], wl[:, o1:o2], pad_lo, w_kr, pad_hi, pad_lo, w_krr, pad_hi,
                              wl[:, o3:o4], wl[:, o4:o5], wl[:, o5:]], axis=1).astype(BF16)
        qw = QK_NOPE + QK_ROPE
        wq = _head_blocks(w_uq[l], MLA_HEADS, qw, [(0, qw, 0)]).astype(BF16)
        wqr = _head_blocks(w_uq[l], MLA_HEADS, qw,
                           [(QK_NOPE + half, qw, QK_NOPE), (QK_NOPE, QK_NOPE + half, QK_NOPE + half)]).astype(BF16)
        kvw = QK_NOPE + V_HEAD
        wk = _head_blocks(w_ukv[l], MLA_HEADS, kvw, [(0, QK_NOPE, 0)]).astype(BF16)
        v_cols = w_ukv[l].reshape(KV_RANK, MLA_HEADS, kvw)[:, :, QK_NOPE:]
        v_pad = jnp.zeros_like(v_cols)
        odd_head = (jnp.arange(MLA_HEADS) % 2 == 1)[None, :, None]
        wv = jnp.where(odd_head, jnp.concatenate([v_pad, v_cols], axis=-1),
                       jnp.concatenate([v_cols, v_pad], axis=-1)).reshape(KV_RANK, MLA_HEADS * LANES).astype(BF16)
        vone = jnp.stack([(lane == _ones_lane(hd)).astype(F32) for hd in range(MLA_HEADS)]).reshape(1, -1)

        q, k, v, z, gb = _mixer_in(h, g_mix[l][None, :], w1, g_q[l][None, :], wq, wqr, g_kv[l][None, :], wk, wv,
                                   vone, eplace, cos_t, sin_t)
        o = _mla_attn(q.reshape(B, S, -1), k.reshape(B, S, -1), v.reshape(B, S, -1))
        h = _mixer_out(o.reshape(T, -1), z, gb, h, conv_w[l], g_out[l][None, :], gsum, gexp,
                       w_o[l].astype(BF16), S)

        kx, vx = _mem_kv(mem, g_mem[l][None, :], w_xkv[l].astype(BF16))
        h = _xattn(h.reshape(B, S, D), g_x[l][None, :], w_xq[l].astype(BF16), kx, vx,
                   w_xo[l].astype(BF16)).reshape(T, D)

        sk = sub_keys[l]
        zk = jnp.zeros_like(sk[:, 0])
        keys_bd = jnp.concatenate([jnp.concatenate([sk[:, 0], zk], axis=-1),
                                   jnp.concatenate([zk, sk[:, 1]], axis=-1)], axis=1)
        xn, r2, p, n, coef = _peer_route(h, g_ffn[l][None, :], w_pq[l].astype(BF16), keys_bd.astype(BF16))
        u_pack, vt_pack = _pack_experts(u_experts[l], v_experts[l])
        h = _peer_ffn(xn, u_pack, vt_pack, r2, p, n, coef, h,
                      g_final[None, :], final_norm=(l == depth - 1))
    return h.reshape(B, S, D)
```

```python
import functools
import math

import jax
import jax.numpy as jnp
from jax import lax
from jax.experimental import pallas as pl
from jax.experimental.pallas import tpu as pltpu

F32 = jnp.float32
BF16 = jnp.bfloat16

EPS = 1e-6
LANES = 128
SUBLANES = 8
VMEM_LIMIT = 56 * 1024 * 1024

MLA_HEADS = 8
QK_NOPE = 64
QK_ROPE = 32
V_HEAD = 64
Q_RANK = 384
KV_RANK = 256
CONV_DIM = 512
GROUP_DIM = 64
ROPE_THETA = 10000.0
X_HEADS = 4
PEER_HEADS = 8
PEER_KEYS = 128
PEER_TOPK = 16
HALF_Q = 64

TS_IN = 512
ROW_SPLIT = 2
TQ = 256
TS_ROUTE = 1024
TT_FFN = 512
EB_FFN = 2048
SUB_FFN = 1024
DOT_TOKENS_FFN = 512
GATE_UNIT_SPLIT = (8, 16, 40, 0)
NT_DIMS = (((1,), (1,)), ((), ()))


def _rms(x, g):
    return x * lax.rsqrt(jnp.mean(x * x, axis=-1, keepdims=True) + EPS) * g


def _split_bf16(x):
    hi = x.astype(BF16)
    lo = (x - hi.astype(F32)).astype(BF16)
    return hi, lo


def _gelu_tanh(x):
    c0 = math.sqrt(2.0 / math.pi)
    half_x = 0.5 * x
    return half_x + half_x * jnp.tanh(x * (c0 + (c0 * 0.044715) * (x * x)))


def _dot(a, b):
    return jnp.dot(a, b, preferred_element_type=F32)


def _dot_nt(a, b):
    return lax.dot_general(a, b, NT_DIMS, preferred_element_type=F32)


_C_CQ = 0
_C_CKV = _C_CQ + Q_RANK
_C_KR = _C_CKV + KV_RANK
_C_KRR = _C_KR + LANES
_C_GB = _C_KRR + LANES
_C_GC = _C_GB + CONV_DIM
_C_HX = _C_GC + CONV_DIM
_C_END = _C_HX + CONV_DIM


def _mixer_in_kernel(x_ref, gmix_ref, w1_ref, gq_ref, wq_ref, wqr_ref, gkv_ref, wk_ref, wv_ref, vone_ref,
                     eplace_ref, cos_ref, sin_ref,
                     q_out, k_out, v_out, z_out, gb_out):
    xn = _rms(x_ref[...], gmix_ref[...]).astype(BF16)
    proj = _dot(xn, w1_ref[...])
    cq = proj[:, _C_CQ:_C_CKV]
    ckv = proj[:, _C_CKV:_C_KR]
    kr = proj[:, _C_KR:_C_KRR]
    krr = proj[:, _C_KRR:_C_GB]
    gb_out[...] = proj[:, _C_GB:_C_GC]
    z_out[...] = proj[:, _C_GC:_C_HX] * proj[:, _C_HX:_C_END]

    cqn = _rms(cq, gq_ref[...]).astype(BF16)
    q_raw = _dot(cqn, wq_ref[...])
    q_rot = _dot(cqn, wqr_ref[...])
    cos_t = cos_ref[...]
    sin_t = sin_ref[...]
    q_scale = math.log2(math.e) / math.sqrt(QK_NOPE + QK_ROPE)
    for h in range(MLA_HEADS):
        sl = slice(h * LANES, (h + 1) * LANES)
        q_out[:, sl] = ((q_raw[:, sl] * cos_t + q_rot[:, sl] * sin_t) * q_scale).astype(BF16)

    ckvn = _rms(ckv, gkv_ref[...]).astype(BF16)
    kr_roped = (kr * cos_t + krr * sin_t).astype(BF16)
    k_out[...] = (_dot(ckvn, wk_ref[...]) + _dot(kr_roped, eplace_ref[...])).astype(BF16)
    v_out[...] = (_dot(ckvn, wv_ref[...]) + vone_ref[...]).astype(BF16)


def _mixer_in(x2, gmix, w1, gq, wq, wqr, gkv, wk, wv, vone, eplace, cos_t, sin_t):
    T, D = x2.shape
    ts = min(TS_IN, T)
    row = lambda i: (i, 0)
    fixed = lambda i: (0, 0)
    full = lambda a: pl.BlockSpec(a.shape, fixed)
    return pl.pallas_call(
        _mixer_in_kernel,
        grid=(T // ts,),
        in_specs=[pl.BlockSpec((ts, D), row), full(gmix), full(w1), full(gq), full(wq), full(wqr),
                  full(gkv), full(wk), full(wv), full(vone), full(eplace),
                  pl.BlockSpec((ts, LANES), row), pl.BlockSpec((ts, LANES), row)],
        out_specs=[pl.BlockSpec((ts, MLA_HEADS * LANES), row), pl.BlockSpec((ts, MLA_HEADS * LANES), row),
                   pl.BlockSpec((ts, MLA_HEADS * LANES), row), pl.BlockSpec((ts, CONV_DIM), row),
                   pl.BlockSpec((ts, CONV_DIM), row)],
        out_shape=[jax.ShapeDtypeStruct((T, MLA_HEADS * LANES), BF16),
                   jax.ShapeDtypeStruct((T, MLA_HEADS * LANES), BF16),
                   jax.ShapeDtypeStruct((T, MLA_HEADS * LANES), BF16),
                   jax.ShapeDtypeStruct((T, CONV_DIM), F32),
                   jax.ShapeDtypeStruct((T, CONV_DIM), F32)],
        compiler_params=pltpu.CompilerParams(dimension_semantics=("arbitrary",), vmem_limit_bytes=VMEM_LIMIT),
        name="mixer_in",
    )(x2, gmix, w1, gq, wq, wqr, gkv, wk, wv, vone, eplace, cos_t, sin_t)


def _ones_lane(head):
    return V_HEAD if head % 2 == 0 else 0


def _mla_attn_kernel(q_ref, k_ref, v_ref, o_ref, *, tq):
    seq = q_ref.shape[0]
    causal = (lax.broadcasted_iota(jnp.int32, (tq, tq), 1) <= lax.broadcasted_iota(jnp.int32, (tq, tq), 0))
    lane = lax.broadcasted_iota(jnp.int32, (tq, LANES), 1)
    for qi in range(seq // tq):
        rows = slice(qi * tq, (qi + 1) * tq)
        past = slice(0, qi * tq)
        outs = []
        for hh in range(2):
            hl = slice(hh * LANES, (hh + 1) * LANES)
            qh = q_ref[rows, hl]
            s_diag = jnp.where(causal, _dot_nt(qh, k_ref[rows, hl]), -jnp.inf)
            m = jnp.max(s_diag, axis=-1, keepdims=True)
            if qi > 0:
                s_past = _dot_nt(qh, k_ref[past, hl])
                m = jnp.maximum(m, jnp.max(s_past, axis=-1, keepdims=True))
                acc = _dot(jnp.exp2(s_past - m).astype(BF16), v_ref[past, hl])
                acc = acc + _dot(jnp.exp2(s_diag - m).astype(BF16), v_ref[rows, hl])
            else:
                acc = _dot(jnp.exp2(s_diag - m).astype(BF16), v_ref[rows, hl])
            one = _ones_lane(hh)
            outs.append(acc * (1.0 / acc[:, one:one + 1]))
        o_ref[rows, :] = jnp.where(lane < V_HEAD, outs[0], outs[1])


def _mla_attn(q3, k3, v3):
    B, S, _ = q3.shape
    tq = min(TQ, S)
    pair = lambda b, g: (b, 0, g)
    return pl.pallas_call(
        functools.partial(_mla_attn_kernel, tq=tq),
        grid=(B, MLA_HEADS // 2),
        in_specs=[pl.BlockSpec((None, S, 2 * LANES), pair), pl.BlockSpec((None, S, 2 * LANES), pair),
                  pl.BlockSpec((None, S, 2 * LANES), pair)],
        out_specs=pl.BlockSpec((None, S, 2 * V_HEAD), pair),
        out_shape=jax.ShapeDtypeStruct((B, S, MLA_HEADS * V_HEAD), F32),
        compiler_params=pltpu.CompilerParams(dimension_semantics=("arbitrary", "arbitrary"),
                                             vmem_limit_bytes=VMEM_LIMIT),
        name="mla_attn",
    )(q3, k3, v3)


def _mixer_out_kernel(o_ref, z_ref, zh_ref, gb_ref, x_ref, cw_ref, gout_ref, gsum_ref, gexp_ref, wo_ref,
                      h_out, *, tiles_per_seq):
    i = pl.program_id(0)
    ts = z_ref.shape[0]
    z = z_ref[...]
    halo = jnp.where(i % tiles_per_seq == 0, 0.0, zh_ref[...])
    row = lax.broadcasted_iota(jnp.int32, z.shape, 0)
    z1 = jnp.where(row == 0, halo[7:8, :], pltpu.roll(z, 1, axis=0))
    z2 = jnp.where(row == 0, halo[6:7, :], jnp.where(row == 1, halo[7:8, :], pltpu.roll(z, 2, axis=0)))
    cw = cw_ref[...]
    y_conv = gb_ref[...] * (cw[0:1, :] * z2 + cw[1:2, :] * z1 + cw[2:3, :] * z)
    y_all = jnp.concatenate([o_ref[...], y_conv], axis=-1)
    for part in range(ROW_SPLIT):
        rows = slice(part * ts // ROW_SPLIT, (part + 1) * ts // ROW_SPLIT)
        y = y_all[rows]
        sq_hi, sq_lo = _split_bf16(y * y)
        gs = _dot(sq_hi, gsum_ref[...]) + _dot(sq_lo, gsum_ref[...])
        r = lax.rsqrt(gs * (1.0 / GROUP_DIM) + EPS)
        r_hi, r_lo = _split_bf16(r)
        r_full = _dot(r_hi, gexp_ref[...]) + _dot(r_lo, gexp_ref[...])
        yn = (y * r_full * gout_ref[...]).astype(BF16)
        h_out[rows, :] = x_ref[rows, :] + _dot(yn, wo_ref[...])


def _mixer_out(o2, z, gb, x2, conv_w, gout, gsum, gexp, wo, seq):
    T, D = x2.shape
    ts = min(TS_IN, seq)
    row = lambda i: (i, 0)
    fixed = lambda i: (0, 0)
    full = lambda a: pl.BlockSpec(a.shape, fixed)
    halo_blocks = ts // SUBLANES
    return pl.pallas_call(
        functools.partial(_mixer_out_kernel, tiles_per_seq=seq // ts),
        grid=(T // ts,),
        in_specs=[pl.BlockSpec((ts, MLA_HEADS * V_HEAD), row), pl.BlockSpec((ts, CONV_DIM), row),
                  pl.BlockSpec((SUBLANES, CONV_DIM), lambda i: (jnp.maximum(i * halo_blocks - 1, 0), 0)),
                  pl.BlockSpec((ts, CONV_DIM), row), pl.BlockSpec((ts, D), row),
                  full(conv_w), full(gout), full(gsum), full(gexp), full(wo)],
        out_specs=pl.BlockSpec((ts, D), row),
        out_shape=jax.ShapeDtypeStruct((T, D), F32),
        compiler_params=pltpu.CompilerParams(dimension_semantics=("arbitrary",), vmem_limit_bytes=VMEM_LIMIT),
        name="mixer_out",
    )(o2, z, z, gb, x2, conv_w, gout, gsum, gexp, wo)


def _mem_kv_kernel(mem_ref, g_ref, w_ref, k_out, v_out):
    d = mem_ref.shape[-1]
    mn = _rms(mem_ref[...], g_ref[...]).astype(BF16)
    kv = _dot(mn, w_ref[...])
    k_out[...] = kv[:, :d].astype(BF16)
    v_out[...] = kv[:, d:].astype(BF16)


def _mem_kv(mem, g, w):
    B, M, D = mem.shape
    return pl.pallas_call(
        _mem_kv_kernel,
        grid=(B,),
        in_specs=[pl.BlockSpec((None, M, D), lambda b: (b, 0, 0)), pl.BlockSpec(g.shape, lambda b: (0, 0)),
                  pl.BlockSpec(w.shape, lambda b: (0, 0))],
        out_specs=[pl.BlockSpec((None, M, D), lambda b: (b, 0, 0)), pl.BlockSpec((None, M, D), lambda b: (b, 0, 0))],
        out_shape=[jax.ShapeDtypeStruct((B, M, D), BF16), jax.ShapeDtypeStruct((B, M, D), BF16)],
        compiler_params=pltpu.CompilerParams(dimension_semantics=("arbitrary",), vmem_limit_bytes=VMEM_LIMIT),
        name="mem_kv",
    )(mem, g, w)


def _xattn_kernel(h_ref, g_ref, wq_ref, k_ref, v_ref, wo_ref, h_out):
    h = h_ref[...]
    d = h.shape[-1]
    hd = d // X_HEADS
    hn = _rms(h, g_ref[...]).astype(BF16)
    q = _dot(hn, wq_ref[...]).astype(BF16)
    outs = []
    for hh in range(X_HEADS):
        sl = slice(hh * hd, (hh + 1) * hd)
        s = _dot_nt(q[:, sl], k_ref[:, sl]) * (1.0 / math.sqrt(hd))
        m = jnp.max(s, axis=-1, keepdims=True)
        p = jnp.exp(s - m)
        p = p * (1.0 / jnp.sum(p, axis=-1, keepdims=True))
        outs.append(_dot(p.astype(BF16), v_ref[:, sl]))
    o = jnp.concatenate(outs, axis=-1).astype(BF16)
    h_out[...] = h + _dot(o, wo_ref[...])


def _xattn(h3, g, wq, kx, vx, wo):
    B, S, D = h3.shape
    M = kx.shape[1]
    ts = min(TS_IN, S)
    fixed = lambda b, i: (0, 0)
    return pl.pallas_call(
        _xattn_kernel,
        grid=(B, S // ts),
        in_specs=[pl.BlockSpec((None, ts, D), lambda b, i: (b, i, 0)), pl.BlockSpec(g.shape, fixed),
                  pl.BlockSpec(wq.shape, fixed), pl.BlockSpec((None, M, D), lambda b, i: (b, 0, 0)),
                  pl.BlockSpec((None, M, D), lambda b, i: (b, 0, 0)), pl.BlockSpec(wo.shape, fixed)],
        out_specs=pl.BlockSpec((None, ts, D), lambda b, i: (b, i, 0)),
        out_shape=jax.ShapeDtypeStruct((B, S, D), F32),
        compiler_params=pltpu.CompilerParams(dimension_semantics=("arbitrary", "arbitrary"),
                                             vmem_limit_bytes=VMEM_LIMIT),
        name="xattn",
    )(h3, g, wq, kx, vx, wo)


def _batcher_pairs(n):
    pairs = []
    p = 1
    while p < n:
        k = p
        while k >= 1:
            for j in range(k % p, n - k, 2 * k):
                for i in range(min(k, n - j - k)):
                    if (i + j) // (2 * p) == (i + j + k) // (2 * p):
                        pairs.append((i + j, i + j + k))
            k //= 2
        p *= 2
    return pairs


_SORT16 = _batcher_pairs(PEER_TOPK)
_ROW_LEN = [PEER_TOPK // (a + 1) for a in range(PEER_TOPK)]


def _sort_desc(v):
    v = list(v)
    for i, j in _SORT16:
        hi = jnp.maximum(v[i], v[j])
        lo = jnp.minimum(v[i], v[j])
        v[i], v[j] = hi, lo
    return v


def _bitonic_desc(v):
    v = list(v)
    n = len(v)
    d = n // 2
    while d >= 1:
        for k in range(n):
            if k & d == 0:
                hi = jnp.maximum(v[k], v[k + d])
                lo = jnp.minimum(v[k], v[k + d])
                v[k], v[k + d] = hi, lo
        d //= 2
    return v


def _merge_top(cur, other):
    n = len(cur)
    c = list(cur)
    for r, val in enumerate(other):
        c[n - 1 - r] = jnp.maximum(c[n - 1 - r], val)
    return _bitonic_desc(c)


def _top16_sorted(s):
    v = _sort_desc([s[k] for k in range(PEER_TOPK)])
    for shift in (4, 2, 1):
        other = [pltpu.roll(v[k], shift, axis=0) for k in range(PEER_TOPK)]
        v = _merge_top(v, other)
    return v


def _peer_route_kernel(h_ref, g_ref, wq_ref, key_ref,
                       xn_out, r2_out, p_out, n_out, c_out,
                       st_ref, top_ref, res_ref):
    ts = h_ref.shape[0]
    n_chunk = ts // LANES
    hn = _rms(h_ref[...], g_ref[...]).astype(BF16)
    xn_out[...] = hn
    q = _dot(hn, wq_ref[...])
    for h in range(PEER_HEADS):
        st = _dot_nt(key_ref[h], q[:, h * LANES:(h + 1) * LANES].astype(BF16))
        for c in range(n_chunk):
            st_ref[c, h] = st[:, c * LANES:(c + 1) * LANES]

    neg_inf = jnp.float32(-jnp.inf)
    pos_inf = jnp.float32(jnp.inf)

    def chunk_body(c, _):
        def sort_body(h, _):
            for half in range(2):
                s = st_ref[c, h, pl.ds(half * PEER_KEYS, PEER_KEYS), :].reshape(PEER_TOPK, SUBLANES, LANES)
                v = _top16_sorted(s)
                for a in range(PEER_TOPK):
                    top_ref[half, a, pl.ds(h, 1), :] = v[a][0:1, :]
            return 0

        lax.fori_loop(0, PEER_HEADS, sort_body, 0)

        v1 = [top_ref[0, a] for a in range(PEER_TOPK)]
        v2 = [top_ref[1, b] for b in range(PEER_TOPK)]
        sums = [[v1[a] + v2[b] for b in range(_ROW_LEN[a])] for a in range(PEER_TOPK)]
        cur = sums[0]
        a = 1
        while _ROW_LEN[a] > 1:
            cur = _merge_top(cur, sums[a])
            a += 1
        cur = _merge_top(cur, [sums[r][0] for r in range(a, PEER_TOPK)])
        tau = cur[PEER_TOPK - 1]
        top_sum = sums[0][0]
        z = jnp.zeros_like(tau)
        for a in range(PEER_TOPK):
            cnt = jnp.zeros_like(tau)
            for b in range(_ROW_LEN[a]):
                sel = sums[a][b] >= tau
                cnt = cnt + jnp.where(sel, 1.0, 0.0)
                z = z + jnp.where(sel, jnp.exp(sums[a][b] - top_sum), 0.0)
            res_ref[a] = cnt
        res_ref[PEER_TOPK] = 1.0 / z

        def expand_body(h, _):
            s1 = st_ref[c, h, pl.ds(0, PEER_KEYS), :].reshape(PEER_TOPK, SUBLANES, LANES)
            s2 = st_ref[c, h, pl.ds(PEER_KEYS, PEER_KEYS), :].reshape(PEER_TOPK, SUBLANES, LANES)
            n = jnp.zeros(s1.shape, F32)
            r2 = jnp.full(s2.shape, float(PEER_TOPK), F32)
            for a in range(PEER_TOPK - 1, -1, -1):
                v1a = top_ref[0, a, pl.ds(h, 1), :]
                v2a = top_ref[1, a, pl.ds(h, 1), :]
                n = jnp.where(s1 == v1a, res_ref[a, pl.ds(h, 1), :], n)
                r2 = jnp.where(s2 == v2a, float(a), r2)
            m1 = top_ref[0, 0, pl.ds(h, 1), :]
            m2 = top_ref[1, 0, pl.ds(h, 1), :]
            inv_z = res_ref[PEER_TOPK, pl.ds(h, 1), :]
            n_out[c, h] = n.reshape(PEER_KEYS, LANES)
            c_out[c, h] = (jnp.exp(s1 - m1) * inv_z).reshape(PEER_KEYS, LANES)
            r2_out[c, h] = r2.reshape(PEER_KEYS, LANES).astype(BF16)
            p_out[c, h] = jnp.exp(s2 - m2).reshape(PEER_KEYS, LANES).astype(BF16)
            return 0

        lax.fori_loop(0, PEER_HEADS, expand_body, 0)
        return 0

    lax.fori_loop(0, n_chunk, chunk_body, 0)


def _peer_route(h2, g, wq, keys):
    T, D = h2.shape
    ts = min(TS_ROUTE, T)
    nc = ts // LANES
    aux_spec = pl.BlockSpec((nc, PEER_HEADS, PEER_KEYS, LANES), lambda i: (i, 0, 0, 0))
    aux_shape = jax.ShapeDtypeStruct((T // LANES, PEER_HEADS, PEER_KEYS, LANES), F32)
    aux_shape_bf = jax.ShapeDtypeStruct((T // LANES, PEER_HEADS, PEER_KEYS, LANES), BF16)
    fixed2 = lambda i: (0, 0)
    fixed3 = lambda i: (0, 0, 0)
    return pl.pallas_call(
        _peer_route_kernel,
        grid=(T // ts,),
        in_specs=[pl.BlockSpec((ts, D), lambda i: (i, 0)), pl.BlockSpec(g.shape, fixed2),
                  pl.BlockSpec(wq.shape, fixed2), pl.BlockSpec(keys.shape, fixed3)],
        out_specs=[pl.BlockSpec((ts, D), lambda i: (i, 0)), aux_spec, aux_spec, aux_spec, aux_spec],
        out_shape=[jax.ShapeDtypeStruct((T, D), BF16), aux_shape_bf, aux_shape_bf, aux_shape, aux_shape],
        scratch_shapes=[pltpu.VMEM((nc, PEER_HEADS, 2 * PEER_KEYS, LANES), F32),
                        pltpu.VMEM((2, PEER_TOPK, SUBLANES, LANES), F32),
                        pltpu.VMEM((PEER_TOPK + 1, SUBLANES, LANES), F32)],
        compiler_params=pltpu.CompilerParams(dimension_semantics=("arbitrary",), vmem_limit_bytes=VMEM_LIMIT),
        name="peer_route",
    )(h2, g, wq, keys)


def _pack_experts_kernel(u_ref, v_ref, u_out, vt_out):
    u_out[...] = pltpu.bitcast(u_ref[...].astype(BF16), jnp.uint32)
    vt_out[...] = pltpu.bitcast(v_ref[...].T.astype(BF16), jnp.uint32)


def _pack_experts(u, v):
    E, D = u.shape
    eb = EB_FFN
    return pl.pallas_call(
        _pack_experts_kernel,
        grid=(E // eb,),
        in_specs=[pl.BlockSpec((eb, D), lambda e: (e, 0)), pl.BlockSpec((eb, D), lambda e: (e, 0))],
        out_specs=[pl.BlockSpec((eb // 2, D), lambda e: (e, 0)), pl.BlockSpec((D // 2, eb), lambda e: (0, e))],
        out_shape=[jax.ShapeDtypeStruct((E // 2, D), jnp.uint32), jax.ShapeDtypeStruct((D // 2, E), jnp.uint32)],
        compiler_params=pltpu.CompilerParams(dimension_semantics=("arbitrary",), vmem_limit_bytes=VMEM_LIMIT),
        name="pack_experts",
    )(u, v)


def _peer_gate_unit(tc, ii, at_ref, ht_ref, r2s_ref, ps_ref, n_ref, c_ref):
    pack = 2 * SUBLANES
    n_jv = PEER_KEYS // pack
    zero = jnp.zeros((pack, LANES), BF16)
    lanes = slice(tc * LANES, (tc + 1) * LANES)
    g = [None] * n_jv
    for h in range(PEER_HEADS):
        n_b = jnp.broadcast_to(n_ref[tc, h, ii:ii + 1, :], (pack, LANES)).astype(BF16)
        c_b = jnp.broadcast_to(c_ref[tc, h, ii:ii + 1, :], (pack, LANES)).astype(BF16)
        for jv in range(n_jv):
            js = slice(jv * pack, (jv + 1) * pack)
            term = jnp.where(r2s_ref[tc, h, js, :] < n_b, ps_ref[tc, h, js, :], zero) * c_b
            g[jv] = term if g[jv] is None else g[jv] + term
    for jv in range(n_jv):
        rows = slice(ii * PEER_KEYS + jv * pack, ii * PEER_KEYS + (jv + 1) * pack)
        ht_ref[rows, lanes] = _gelu_tanh(at_ref[rows, lanes]).astype(BF16) * g[jv]


def _peer_ffn_kernel(xn_ref, u_ref, vt_ref, r2_ref, p_ref, n_ref, c_ref, h_ref, gfin_ref,
                     out_ref, acc_ref, at0_ref, at1_ref, ht0_ref, ht1_ref, r2s_ref, ps_ref, xs_ref,
                     *, n_e, n_blocks, final_norm):
    g = pl.program_id(0)
    tt = xn_ref.shape[0]
    e_score = g % n_e
    e_gate = jnp.maximum(g - 1, 0) % n_e
    e_down = jnp.maximum(g - 2, 0) % n_e

    @pl.when(g == 0)
    def _():
        at1_ref[...] = jnp.zeros_like(at1_ref)
        ht0_ref[...] = jnp.zeros_like(ht0_ref)
        ht1_ref[...] = jnp.zeros_like(ht1_ref)
        acc_ref[...] = jnp.zeros_like(acc_ref)

    @pl.when((g < n_blocks) & (e_score == 0))
    def _():
        xs_ref[...] = xn_ref[...]

    @pl.when((g <= n_blocks) & (e_gate == 0))
    def _():
        for tc in range(tt // LANES):
            for h in range(PEER_HEADS):
                r2s_ref[tc, h] = r2_ref[tc, h]
                ps_ref[tc, h] = p_ref[tc, h]

    @pl.when((g >= 2) & (e_down == 0))
    def _():
        acc_ref[...] = jnp.zeros_like(acc_ref)

    def stages(at_w, at_r, ht_w, ht_r):
        u_blk = pltpu.bitcast(u_ref[...], BF16)
        vt_blk = pltpu.bitcast(vt_ref[...], BF16)
        eb = u_blk.shape[0]
        mxu_n = DOT_TOKENS_FFN
        toks = [slice(c * mxu_n, (c + 1) * mxu_n) for c in range(tt // mxu_n)]
        subs = [slice(sb * SUB_FFN, (sb + 1) * SUB_FFN) for sb in range(eb // SUB_FFN)]

        def score(ex, tok):
            at_w[ex, tok] = _dot_nt(u_blk[ex], xs_ref[tok, :])

        def down(tok):
            acc_ref[:, tok] += _dot(vt_blk, ht_r[:, tok])

        chunks = ([functools.partial(score, ex, tok) for ex in subs for tok in toks]
                  + [functools.partial(down, tok) for tok in toks])
        units = [(tc, ii) for tc in range(tt // LANES) for ii in range(eb // PEER_KEYS)]
        split = GATE_UNIT_SPLIT
        assert len(split) == len(chunks) + 1 and sum(split) == len(units)
        bounds = [sum(split[:k]) for k in range(len(split) + 1)]

        def gate_units(k):
            for tc, ii in units[bounds[k]:bounds[k + 1]]:
                _peer_gate_unit(tc, ii, at_r, ht_w, r2s_ref, ps_ref, n_ref, c_ref)

        gate_units(0)
        for k, chunk in enumerate(chunks):
            chunk()
            gate_units(k + 1)

    @pl.when(g % 2 == 0)
    def _():
        stages(at0_ref, at1_ref, ht1_ref, ht0_ref)

    @pl.when(g % 2 == 1)
    def _():
        stages(at1_ref, at0_ref, ht0_ref, ht1_ref)

    @pl.when((g >= 2) & (e_down == n_e - 1))
    def _():
        res = h_ref[...] + acc_ref[...].T
        out_ref[...] = _rms(res, gfin_ref[...]) if final_norm else res


def _peer_ffn(xn, u_pack, vt_pack, r2, p, n, coef, h2, gfin, final_norm):
    T, D = h2.shape
    E = vt_pack.shape[1]
    tt = min(TT_FFN, T)
    nc = tt // LANES
    eb = EB_FFN
    n_i = eb // PEER_KEYS
    n_e = E // eb
    n_blocks = (T // tt) * n_e

    def block(lag):
        def split(g):
            b = jnp.clip(g - lag, 0, n_blocks - 1)
            return b // n_e, b % n_e
        return split

    score, gate, down = block(0), block(1), block(2)
    aux_shape = (nc, PEER_HEADS, PEER_KEYS, LANES)
    row_shape = (nc, PEER_HEADS, n_i, LANES)
    return pl.pallas_call(
        functools.partial(_peer_ffn_kernel, n_e=n_e, n_blocks=n_blocks, final_norm=final_norm),
        grid=(n_blocks + 2,),
        in_specs=[pl.BlockSpec((tt, D), lambda g: (score(g)[0], 0)),
                  pl.BlockSpec((eb // 2, D), lambda g: (score(g)[1], 0)),
                  pl.BlockSpec((D // 2, eb), lambda g: (0, down(g)[1])),
                  pl.BlockSpec(aux_shape, lambda g: (gate(g)[0], 0, 0, 0)),
                  pl.BlockSpec(aux_shape, lambda g: (gate(g)[0], 0, 0, 0)),
                  pl.BlockSpec(row_shape, lambda g: (gate(g)[0], 0, gate(g)[1], 0)),
                  pl.BlockSpec(row_shape, lambda g: (gate(g)[0], 0, gate(g)[1], 0)),
                  pl.BlockSpec((tt, D), lambda g: (down(g)[0], 0)),
                  pl.BlockSpec(gfin.shape, lambda g: (0, 0))],
        out_specs=pl.BlockSpec((tt, D), lambda g: (down(g)[0], 0)),
        out_shape=jax.ShapeDtypeStruct((T, D), F32),
        scratch_shapes=[pltpu.VMEM((D, tt), F32),
                        pltpu.VMEM((eb, tt), F32), pltpu.VMEM((eb, tt), F32),
                        pltpu.VMEM((eb, tt), BF16), pltpu.VMEM((eb, tt), BF16),
                        pltpu.VMEM(aux_shape, BF16), pltpu.VMEM(aux_shape, BF16),
                        pltpu.VMEM((tt, D), BF16)],
        compiler_params=pltpu.CompilerParams(dimension_semantics=("arbitrary",), vmem_limit_bytes=VMEM_LIMIT),
        name="peer_ffn",
    )(xn, u_pack, vt_pack, r2, p, n, coef, h2, gfin)


def _head_blocks(w, n_heads, width, pieces):
    w3 = w.reshape(w.shape[0], n_heads, width)
    out = jnp.zeros((w.shape[0], n_heads, LANES), w.dtype)
    for s0, s1, d0 in pieces:
        out = out.at[:, :, d0:d0 + (s1 - s0)].set(w3[:, :, s0:s1])
    return out.reshape(w.shape[0], n_heads * LANES)


def kernel(x, mem, positions, g_mix, w_in, g_q, w_uq, g_kv, w_ukv, conv_w, g_out, w_o, g_x, g_mem, w_xq,
           w_xkv, w_xo, g_ffn, w_pq, sub_keys, u_experts, v_experts, g_final):
    B, S, D = x.shape
    T = B * S
    depth = g_mix.shape[0]
    half = QK_ROPE // 2

    inv = ROPE_THETA ** (-jnp.arange(0, QK_ROPE, 2, dtype=F32) / QK_ROPE)
    ang = positions.astype(F32)[..., None] * inv
    cos = jnp.cos(ang).astype(x.dtype).reshape(T, half)
    sin = jnp.sin(ang).astype(x.dtype).reshape(T, half)
    ones = jnp.ones((T, QK_NOPE), F32)
    zeros_n = jnp.zeros((T, QK_NOPE), F32)
    pad_q = jnp.zeros((T, LANES - QK_NOPE - QK_ROPE), F32)
    cos_t = jnp.concatenate([ones, cos, cos, pad_q], axis=1)
    sin_t = jnp.concatenate([zeros_n, -sin, sin, pad_q], axis=1)

    lane = jnp.arange(LANES)
    col = jnp.arange(MLA_HEADS * LANES)
    rope_lane = (lane >= QK_NOPE) & (lane < QK_NOPE + QK_ROPE)
    eplace = ((col[None, :] % LANES == lane[:, None]) & rope_lane[:, None]).astype(BF16)
    mix_col = jnp.arange(D)
    gsum = (mix_col[:, None] // GROUP_DIM == lane[None, :]).astype(BF16)
    gexp = (lane[:, None] == mix_col[None, :] // GROUP_DIM).astype(BF16)

    h = x.reshape(T, D)
    for l in range(depth):
        o1 = Q_RANK
        o2 = o1 + KV_RANK
        o3 = o2 + QK_ROPE
        o4 = o3 + CONV_DIM
        o5 = o4 + CONV_DIM
        wl = w_in[l]
        w_kr = wl[:, o2:o3]
        w_krr = jnp.concatenate([w_kr[:, half:], w_kr[:, :half]], axis=1)
        pad_lo = jnp.zeros((D, QK_NOPE), wl.dtype)
        pad_hi = jnp.zeros((D, LANES - QK_NOPE - QK_ROPE), wl.dtype)
        w1 = jnp.concatenate([wl[:, :o1], wl[:, o1:o2], pad_lo, w_kr, pad_hi, pad_lo, w_krr, pad_hi,
                              wl[:, o3:o4], wl[:, o4:o5], wl[:, o5:]], axis=1).astype(BF16)
        qw = QK_NOPE + QK_ROPE
        wq = _head_blocks(w_uq[l], MLA_HEADS, qw, [(0, qw, 0)]).astype(BF16)
        wqr = _head_blocks(w_uq[l], MLA_HEADS, qw,
                           [(QK_NOPE + half, qw, QK_NOPE), (QK_NOPE, QK_NOPE + half, QK_NOPE + half)]).astype(BF16)
        kvw = QK_NOPE + V_HEAD
        wk = _head_blocks(w_ukv[l], MLA_HEADS, kvw, [(0, QK_NOPE, 0)]).astype(BF16)
        v_cols = w_ukv[l].reshape(KV_RANK, MLA_HEADS, kvw)[:, :, QK_NOPE:]
        v_pad = jnp.zeros_like(v_cols)
        odd_head = (jnp.arange(MLA_HEADS) % 2 == 1)[None, :, None]
        wv = jnp.where(odd_head, jnp.concatenate([v_pad, v_cols], axis=-1),
                       jnp.concatenate([v_cols, v_pad], axis=-1)).reshape(KV_RANK, MLA_HEADS * LANES).astype(BF16)
        vone = jnp.stack([(lane == _ones_lane(hd)).astype(F32) for hd in range(MLA_HEADS)]).reshape(1, -1)

        q, k, v, z, gb = _mixer_in(h, g_mix[l][None, :], w1, g_q[l][None, :], wq, wqr, g_kv[l][None, :], wk, wv,
                                   vone, eplace, cos_t, sin_t)
        o = _mla_attn(q.reshape(B, S, -1), k.reshape(B, S, -1), v.reshape(B, S, -1))
        h = _mixer_out(o.reshape(T, -1), z, gb, h, conv_w[l], g_out[l][None, :], gsum, gexp,
                       w_o[l].astype(BF16), S)

        kx, vx = _mem_kv(mem, g_mem[l][None, :], w_xkv[l].astype(BF16))
        h = _xattn(h.reshape(B, S, D), g_x[l][None, :], w_xq[l].astype(BF16), kx, vx,
                   w_xo[l].astype(BF16)).reshape(T, D)

        sk = sub_keys[l]
        zk = jnp.zeros_like(sk[:, 0])
        keys_bd = jnp.concatenate([jnp.concatenate([sk[:, 0], zk], axis=-1),
                                   jnp.concatenate([zk, sk[:, 1]], axis=-1)], axis=1)
        xn, r2, p, n, coef = _peer_route(h, g_ffn[l][None, :], w_pq[l].astype(BF16), keys_bd.astype(BF16))
        u_pack, vt_pack = _pack_experts(u_experts[l], v_experts[l])
        h = _peer_ffn(xn, u_pack, vt_pack, r2, p, n, coef, h,
                      g_final[None, :], final_norm=(l == depth - 1))
    return h.reshape(B, S, D)
```

```python
import functools
import math

import jax
import jax.numpy as jnp
from jax import lax
from jax.experimental import pallas as pl
from jax.experimental.pallas import tpu as pltpu

F32 = jnp.float32
BF16 = jnp.bfloat16

EPS = 1e-6
LANES = 128
SUBLANES = 8
VMEM_LIMIT = 56 * 1024 * 1024

MLA_HEADS = 8
QK_NOPE = 64
QK_ROPE = 32
V_HEAD = 64
Q_RANK = 384
KV_RANK = 256
CONV_DIM = 512
GROUP_DIM = 64
ROPE_THETA = 10000.0
X_HEADS = 4
PEER_HEADS = 8
PEER_KEYS = 128
PEER_TOPK = 16

TS_IN = 512
ROW_SPLIT = 2
TQ = 256
TS_ROUTE = 1024
TT_FFN = 512
EB_FFN = 2048
SUB_FFN = 1024
GATE_UNIT_SPLIT = (8, 16, 40, 0)
NT_DIMS = (((1,), (1,)), ((), ()))


def _rms(x, g):
    return x * lax.rsqrt(jnp.mean(x * x, axis=-1, keepdims=True) + EPS) * g


def _split_bf16(x):
    hi = x.astype(BF16)
    lo = (x - hi.astype(F32)).astype(BF16)
    return hi, lo


def _gelu_tanh(x):
    c0 = math.sqrt(2.0 / math.pi)
    half_x = 0.5 * x
    return half_x + half_x * jnp.tanh(x * (c0 + (c0 * 0.044715) * (x * x)))


def _dot(a, b):
    return jnp.dot(a, b, preferred_element_type=F32)


def _dot_nt(a, b):
    return lax.dot_general(a, b, NT_DIMS, preferred_element_type=F32)


_C_CQ = 0
_C_CKV = _C_CQ + Q_RANK
_C_KR = _C_CKV + KV_RANK
_C_KRR = _C_KR + LANES
_C_GB = _C_KRR + LANES
_C_GC = _C_GB + CONV_DIM
_C_HX = _C_GC + CONV_DIM
_C_END = _C_HX + CONV_DIM


def _mixer_in_kernel(x_ref, gmix_ref, w1_ref, gq_ref, wq_ref, wqr_ref, gkv_ref, wk_ref, wv_ref, vone_ref,
                     eplace_ref, cos_ref, sin_ref,
                     q_out, k_out, v_out, z_out, gb_out):
    xn = _rms(x_ref[...], gmix_ref[...]).astype(BF16)
    proj = _dot(xn, w1_ref[...])
    cq = proj[:, _C_CQ:_C_CKV]
    ckv = proj[:, _C_CKV:_C_KR]
    kr = proj[:, _C_KR:_C_KRR]
    krr = proj[:, _C_KRR:_C_GB]
    gb_out[...] = proj[:, _C_GB:_C_GC]
    z_out[...] = proj[:, _C_GC:_C_HX] * proj[:, _C_HX:_C_END]

    cqn = _rms(cq, gq_ref[...]).astype(BF16)
    q_raw = _dot(cqn, wq_ref[...])
    q_rot = _dot(cqn, wqr_ref[...])
    cos_t = cos_ref[...]
    sin_t = sin_ref[...]
    q_scale = math.log2(math.e) / math.sqrt(QK_NOPE + QK_ROPE)
    for h in range(MLA_HEADS):
        sl = slice(h * LANES, (h + 1) * LANES)
        q_out[:, sl] = ((q_raw[:, sl] * cos_t + q_rot[:, sl] * sin_t) * q_scale).astype(BF16)

    ckvn = _rms(ckv, gkv_ref[...]).astype(BF16)
    kr_roped = (kr * cos_t + krr * sin_t).astype(BF16)
    k_out[...] = (_dot(ckvn, wk_ref[...]) + _dot(kr_roped, eplace_ref[...])).astype(BF16)
    v_out[...] = (_dot(ckvn, wv_ref[...]) + vone_ref[...]).astype(BF16)


def _mixer_in(x2, gmix, w1, gq, wq, wqr, gkv, wk, wv, vone, eplace, cos_t, sin_t):
    T, D = x2.shape
    ts = min(TS_IN, T)
    row = lambda i: (i, 0)
    fixed = lambda i: (0, 0)
    full = lambda a: pl.BlockSpec(a.shape, fixed)
    return pl.pallas_call(
        _mixer_in_kernel,
        grid=(T // ts,),
        in_specs=[pl.BlockSpec((ts, D), row), full(gmix), full(w1), full(gq), full(wq), full(wqr),
                  full(gkv), full(wk), full(wv), full(vone), full(eplace),
                  pl.BlockSpec((ts, LANES), row), pl.BlockSpec((ts, LANES), row)],
        out_specs=[pl.BlockSpec((ts, MLA_HEADS * LANES), row), pl.BlockSpec((ts, MLA_HEADS * LANES), row),
                   pl.BlockSpec((ts, MLA_HEADS * LANES), row), pl.BlockSpec((ts, CONV_DIM), row),
                   pl.BlockSpec((ts, CONV_DIM), row)],
        out_shape=[jax.ShapeDtypeStruct((T, MLA_HEADS * LANES), BF16),
                   jax.ShapeDtypeStruct((T, MLA_HEADS * LANES), BF16),
                   jax.ShapeDtypeStruct((T, MLA_HEADS * LANES), BF16),
                   jax.ShapeDtypeStruct((T, CONV_DIM), F32),
                   jax.ShapeDtypeStruct((T, CONV_DIM), F32)],
        compiler_params=pltpu.CompilerParams(dimension_semantics=("arbitrary",), vmem_limit_bytes=VMEM_LIMIT),
        name="mixer_in",
    )(x2, gmix, w1, gq, wq, wqr, gkv, wk, wv, vone, eplace, cos_t, sin_t)


def _ones_lane(head):
    return V_HEAD if head % 2 == 0 else 0


def _mla_attn_kernel(q_ref, k_ref, v_ref, o_ref, *, tq):
    seq = q_ref.shape[0]
    causal = (lax.broadcasted_iota(jnp.int32, (tq, tq), 1) <= lax.broadcasted_iota(jnp.int32, (tq, tq), 0))
    lane = lax.broadcasted_iota(jnp.int32, (tq, LANES), 1)
    for qi in range(seq // tq):
        rows = slice(qi * tq, (qi + 1) * tq)
        keys = slice(0, (qi + 1) * tq)
        outs = []
        for hh in range(2):
            hl = slice(hh * LANES, (hh + 1) * LANES)
            s = _dot_nt(q_ref[rows, hl], k_ref[keys, hl])
            s_diag = jnp.where(causal, s[:, qi * tq:], -jnp.inf)
            s = s_diag if qi == 0 else jnp.concatenate([s[:, :qi * tq], s_diag], axis=1)
            m = jnp.max(s, axis=-1, keepdims=True)
            acc = _dot(jnp.exp2(s - m).astype(BF16), v_ref[keys, hl])
            one = _ones_lane(hh)
            outs.append(acc * (1.0 / acc[:, one:one + 1]))
        o_ref[rows, :] = jnp.where(lane < V_HEAD, outs[0], outs[1])


def _mla_attn(q3, k3, v3):
    B, S, _ = q3.shape
    tq = min(TQ, S)
    pair = lambda b, g: (b, 0, g)
    return pl.pallas_call(
        functools.partial(_mla_attn_kernel, tq=tq),
        grid=(B, MLA_HEADS // 2),
        in_specs=[pl.BlockSpec((None, S, 2 * LANES), pair), pl.BlockSpec((None, S, 2 * LANES), pair),
                  pl.BlockSpec((None, S, 2 * LANES), pair)],
        out_specs=pl.BlockSpec((None, S, 2 * V_HEAD), pair),
        out_shape=jax.ShapeDtypeStruct((B, S, MLA_HEADS * V_HEAD), F32),
        compiler_params=pltpu.CompilerParams(dimension_semantics=("arbitrary", "arbitrary"),
                                             vmem_limit_bytes=VMEM_LIMIT),
        name="mla_attn",
    )(q3, k3, v3)


def _mixer_out_kernel(o_ref, z_ref, zh_ref, gb_ref, x_ref, cw_ref, gout_ref, gsum_ref, gexp_ref, wo_ref,
                      h_out, *, tiles_per_seq):
    i = pl.program_id(0)
    ts = z_ref.shape[0]
    z = z_ref[...]
    halo = jnp.where(i % tiles_per_seq == 0, 0.0, zh_ref[...])
    row = lax.broadcasted_iota(jnp.int32, z.shape, 0)
    z1 = jnp.where(row == 0, halo[7:8, :], pltpu.roll(z, 1, axis=0))
    z2 = jnp.where(row == 0, halo[6:7, :], jnp.where(row == 1, halo[7:8, :], pltpu.roll(z, 2, axis=0)))
    cw = cw_ref[...]
    y_conv = gb_ref[...] * (cw[0:1, :] * z2 + cw[1:2, :] * z1 + cw[2:3, :] * z)
    y_all = jnp.concatenate([o_ref[...], y_conv], axis=-1)
    for part in range(ROW_SPLIT):
        rows = slice(part * ts // ROW_SPLIT, (part + 1) * ts // ROW_SPLIT)
        y = y_all[rows]
        sq_hi, sq_lo = _split_bf16(y * y)
        gs = _dot(sq_hi, gsum_ref[...]) + _dot(sq_lo, gsum_ref[...])
        r = lax.rsqrt(gs * (1.0 / GROUP_DIM) + EPS)
        r_hi, r_lo = _split_bf16(r)
        r_full = _dot(r_hi, gexp_ref[...]) + _dot(r_lo, gexp_ref[...])
        yn = (y * r_full * gout_ref[...]).astype(BF16)
        h_out[rows, :] = x_ref[rows, :] + _dot(yn, wo_ref[...])


def _mixer_out(o2, z, gb, x2, conv_w, gout, gsum, gexp, wo, seq):
    T, D = x2.shape
    ts = min(TS_IN, seq)
    row = lambda i: (i, 0)
    fixed = lambda i: (0, 0)
    full = lambda a: pl.BlockSpec(a.shape, fixed)
    halo_blocks = ts // SUBLANES
    return pl.pallas_call(
        functools.partial(_mixer_out_kernel, tiles_per_seq=seq // ts),
        grid=(T // ts,),
        in_specs=[pl.BlockSpec((ts, MLA_HEADS * V_HEAD), row), pl.BlockSpec((ts, CONV_DIM), row),
                  pl.BlockSpec((SUBLANES, CONV_DIM), lambda i: (jnp.maximum(i * halo_blocks - 1, 0), 0)),
                  pl.BlockSpec((ts, CONV_DIM), row), pl.BlockSpec((ts, D), row),
                  full(conv_w), full(gout), full(gsum), full(gexp), full(wo)],
        out_specs=pl.BlockSpec((ts, D), row),
        out_shape=jax.ShapeDtypeStruct((T, D), F32),
        compiler_params=pltpu.CompilerParams(dimension_semantics=("arbitrary",), vmem_limit_bytes=VMEM_LIMIT),
        name="mixer_out",
    )(o2, z, z, gb, x2, conv_w, gout, gsum, gexp, wo)


def _mem_kv_kernel(mem_ref, g_ref, w_ref, k_out, v_out):
    d = mem_ref.shape[-1]
    mn = _rms(mem_ref[...], g_ref[...]).astype(BF16)
    kv = _dot(mn, w_ref[...])
    k_out[...] = kv[:, :d].astype(BF16)
    v_out[...] = kv[:, d:].astype(BF16)


def _mem_kv(mem, g, w):
    B, M, D = mem.shape
    return pl.pallas_call(
        _mem_kv_kernel,
        grid=(B,),
        in_specs=[pl.BlockSpec((None, M, D), lambda b: (b, 0, 0)), pl.BlockSpec(g.shape, lambda b: (0, 0)),
                  pl.BlockSpec(w.shape, lambda b: (0, 0))],
        out_specs=[pl.BlockSpec((None, M, D), lambda b: (b, 0, 0)), pl.BlockSpec((None, M, D), lambda b: (b, 0, 0))],
        out_shape=[jax.ShapeDtypeStruct((B, M, D), BF16), jax.ShapeDtypeStruct((B, M, D), BF16)],
        compiler_params=pltpu.CompilerParams(dimension_semantics=("arbitrary",), vmem_limit_bytes=VMEM_LIMIT),
        name="mem_kv",
    )(mem, g, w)


def _xattn_kernel(h_ref, g_ref, wq_ref, k_ref, v_ref, wo_ref, h_out):
    h = h_ref[...]
    d = h.shape[-1]
    hd = d // X_HEADS
    hn = _rms(h, g_ref[...]).astype(BF16)
    q = _dot(hn, wq_ref[...]).astype(BF16)
    outs = []
    for hh in range(X_HEADS):
        sl = slice(hh * hd, (hh + 1) * hd)
        s = _dot_nt(q[:, sl], k_ref[:, sl]) * (1.0 / math.sqrt(hd))
        m = jnp.max(s, axis=-1, keepdims=True)
        p = jnp.exp(s - m)
        p = p * (1.0 / jnp.sum(p, axis=-1, keepdims=True))
        outs.append(_dot(p.astype(BF16), v_ref[:, sl]))
    o = jnp.concatenate(outs, axis=-1).astype(BF16)
    h_out[...] = h + _dot(o, wo_ref[...])


def _xattn(h3, g, wq, kx, vx, wo):
    B, S, D = h3.shape
    M = kx.shape[1]
    ts = min(TS_IN, S)
    fixed = lambda b, i: (0, 0)
    return pl.pallas_call(
        _xattn_kernel,
        grid=(B, S // ts),
        in_specs=[pl.BlockSpec((None, ts, D), lambda b, i: (b, i, 0)), pl.BlockSpec(g.shape, fixed),
                  pl.BlockSpec(wq.shape, fixed), pl.BlockSpec((None, M, D), lambda b, i: (b, 0, 0)),
                  pl.BlockSpec((None, M, D), lambda b, i: (b, 0, 0)), pl.BlockSpec(wo.shape, fixed)],
        out_specs=pl.BlockSpec((None, ts, D), lambda b, i: (b, i, 0)),
        out_shape=jax.ShapeDtypeStruct((B, S, D), F32),
        compiler_params=pltpu.CompilerParams(dimension_semantics=("arbitrary", "arbitrary"),
                                             vmem_limit_bytes=VMEM_LIMIT),
        name="xattn",
    )(h3, g, wq, kx, vx, wo)


def _batcher_pairs(n):
    pairs = []
    p = 1
    while p < n:
        k = p
        while k >= 1:
            for j in range(k % p, n - k, 2 * k):
                for i in range(min(k, n - j - k)):
                    if (i + j) // (2 * p) == (i + j + k) // (2 * p):
                        pairs.append((i + j, i + j + k))
            k //= 2
        p *= 2
    return pairs


_SORT16 = _batcher_pairs(PEER_TOPK)
_ROW_LEN = [PEER_TOPK // (a + 1) for a in range(PEER_TOPK)]


def _sort_desc(v):
    v = list(v)
    for i, j in _SORT16:
        hi = jnp.maximum(v[i], v[j])
        lo = jnp.minimum(v[i], v[j])
        v[i], v[j] = hi, lo
    return v


def _bitonic_desc(v):
    v = list(v)
    n = len(v)
    d = n // 2
    while d >= 1:
        for k in range(n):
            if k & d == 0:
                hi = jnp.maximum(v[k], v[k + d])
                lo = jnp.minimum(v[k], v[k + d])
                v[k], v[k + d] = hi, lo
        d //= 2
    return v


def _merge_top(cur, other):
    n = len(cur)
    c = list(cur)
    for r, val in enumerate(other):
        c[n - 1 - r] = jnp.maximum(c[n - 1 - r], val)
    return _bitonic_desc(c)


def _top16_sorted(s):
    v = _sort_desc([s[k] for k in range(PEER_TOPK)])
    for shift in (4, 2, 1):
        other = [pltpu.roll(v[k], shift, axis=0) for k in range(PEER_TOPK)]
        v = _merge_top(v, other)
    return v


def _peer_route_kernel(h_ref, g_ref, wq_ref, key_ref,
                       xn_out, r2_out, p_out, n_out, c_out,
                       st_ref, top_ref, res_ref):
    ts = h_ref.shape[0]
    n_chunk = ts // LANES
    hn = _rms(h_ref[...], g_ref[...]).astype(BF16)
    xn_out[...] = hn
    q = _dot(hn, wq_ref[...])
    for h in range(PEER_HEADS):
        st = _dot_nt(key_ref[h], q[:, h * LANES:(h + 1) * LANES].astype(BF16))
        for c in range(n_chunk):
            st_ref[c, h] = st[:, c * LANES:(c + 1) * LANES]


    def chunk_body(c, _):
        def sort_body(h, _):
            for half in range(2):
                s = st_ref[c, h, pl.ds(half * PEER_KEYS, PEER_KEYS), :].reshape(PEER_TOPK, SUBLANES, LANES)
                v = _top16_sorted(s)
                for a in range(PEER_TOPK):
                    top_ref[half, a, pl.ds(h, 1), :] = v[a][0:1, :]
            return 0

        lax.fori_loop(0, PEER_HEADS, sort_body, 0)

        v1 = [top_ref[0, a] for a in range(PEER_TOPK)]
        v2 = [top_ref[1, b] for b in range(PEER_TOPK)]
        sums = [[v1[a] + v2[b] for b in range(_ROW_LEN[a])] for a in range(PEER_TOPK)]
        cur = sums[0]
        a = 1
        while _ROW_LEN[a] > 1:
            cur = _merge_top(cur, sums[a])
            a += 1
        cur = _merge_top(cur, [sums[r][0] for r in range(a, PEER_TOPK)])
        tau = cur[PEER_TOPK - 1]
        top_sum = sums[0][0]
        z = jnp.zeros_like(tau)
        for a in range(PEER_TOPK):
            cnt = jnp.zeros_like(tau)
            for b in range(_ROW_LEN[a]):
                sel = sums[a][b] >= tau
                cnt = cnt + jnp.where(sel, 1.0, 0.0)
                z = z + jnp.where(sel, jnp.exp(sums[a][b] - top_sum), 0.0)
            res_ref[a] = cnt
        res_ref[PEER_TOPK] = 1.0 / z

        def expand_body(h, _):
            s1 = st_ref[c, h, pl.ds(0, PEER_KEYS), :].reshape(PEER_TOPK, SUBLANES, LANES)
            s2 = st_ref[c, h, pl.ds(PEER_KEYS, PEER_KEYS), :].reshape(PEER_TOPK, SUBLANES, LANES)
            n = jnp.zeros(s1.shape, F32)
            r2 = jnp.full(s2.shape, float(PEER_TOPK), F32)
            for a in range(PEER_TOPK - 1, -1, -1):
                v1a = top_ref[0, a, pl.ds(h, 1), :]
                v2a = top_ref[1, a, pl.ds(h, 1), :]
                n = jnp.where(s1 == v1a, res_ref[a, pl.ds(h, 1), :], n)
                r2 = jnp.where(s2 == v2a, float(a), r2)
            m1 = top_ref[0, 0, pl.ds(h, 1), :]
            m2 = top_ref[1, 0, pl.ds(h, 1), :]
            inv_z = res_ref[PEER_TOPK, pl.ds(h, 1), :]
            n_out[c, h] = n.reshape(PEER_KEYS, LANES)
            c_out[c, h] = (jnp.exp(s1 - m1) * inv_z).reshape(PEER_KEYS, LANES)
            r2_out[c, h] = r2.reshape(PEER_KEYS, LANES).astype(BF16)
            p_out[c, h] = jnp.exp(s2 - m2).reshape(PEER_KEYS, LANES).astype(BF16)
            return 0

        lax.fori_loop(0, PEER_HEADS, expand_body, 0)
        return 0

    lax.fori_loop(0, n_chunk, chunk_body, 0)


def _peer_route(h2, g, wq, keys):
    T, D = h2.shape
    ts = min(TS_ROUTE, T)
    nc = ts // LANES
    aux_spec = pl.BlockSpec((nc, PEER_HEADS, PEER_KEYS, LANES), lambda i: (i, 0, 0, 0))
    aux_shape = jax.ShapeDtypeStruct((T // LANES, PEER_HEADS, PEER_KEYS, LANES), F32)
    aux_shape_bf = jax.ShapeDtypeStruct((T // LANES, PEER_HEADS, PEER_KEYS, LANES), BF16)
    fixed2 = lambda i: (0, 0)
    fixed3 = lambda i: (0, 0, 0)
    return pl.pallas_call(
        _peer_route_kernel,
        grid=(T // ts,),
        in_specs=[pl.BlockSpec((ts, D), lambda i: (i, 0)), pl.BlockSpec(g.shape, fixed2),
                  pl.BlockSpec(wq.shape, fixed2), pl.BlockSpec(keys.shape, fixed3)],
        out_specs=[pl.BlockSpec((ts, D), lambda i: (i, 0)), aux_spec, aux_spec, aux_spec, aux_spec],
        out_shape=[jax.ShapeDtypeStruct((T, D), BF16), aux_shape_bf, aux_shape_bf, aux_shape, aux_shape],
        scratch_shapes=[pltpu.VMEM((nc, PEER_HEADS, 2 * PEER_KEYS, LANES), F32),
                        pltpu.VMEM((2, PEER_TOPK, SUBLANES, LANES), F32),
                        pltpu.VMEM((PEER_TOPK + 1, SUBLANES, LANES), F32)],
        compiler_params=pltpu.CompilerParams(dimension_semantics=("arbitrary",), vmem_limit_bytes=VMEM_LIMIT),
        name="peer_route",
    )(h2, g, wq, keys)


def _pack_experts_kernel(u_ref, v_ref, u_out, vt_out):
    u_out[...] = pltpu.bitcast(u_ref[...].astype(BF16), jnp.uint32)
    vt_out[...] = pltpu.bitcast(v_ref[...].T.astype(BF16), jnp.uint32)


def _pack_experts(u, v):
    E, D = u.shape
    eb = EB_FFN
    return pl.pallas_call(
        _pack_experts_kernel,
        grid=(E // eb,),
        in_specs=[pl.BlockSpec((eb, D), lambda e: (e, 0)), pl.BlockSpec((eb, D), lambda e: (e, 0))],
        out_specs=[pl.BlockSpec((eb // 2, D), lambda e: (e, 0)), pl.BlockSpec((D // 2, eb), lambda e: (0, e))],
        out_shape=[jax.ShapeDtypeStruct((E // 2, D), jnp.uint32), jax.ShapeDtypeStruct((D // 2, E), jnp.uint32)],
        compiler_params=pltpu.CompilerParams(dimension_semantics=("arbitrary",), vmem_limit_bytes=VMEM_LIMIT),
        name="pack_experts",
    )(u, v)


def _peer_gate_unit(tc, ii, at_ref, ht_ref, r2s_ref, ps_ref, n_ref, c_ref):
    pack = 2 * SUBLANES
    n_jv = PEER_KEYS // pack
    zero = jnp.zeros((pack, LANES), BF16)
    lanes = slice(tc * LANES, (tc + 1) * LANES)
    g = [None] * n_jv
    for h in range(PEER_HEADS):
        n_b = jnp.broadcast_to(n_ref[tc, h, ii:ii + 1, :], (pack, LANES)).astype(BF16)
        c_b = jnp.broadcast_to(c_ref[tc, h, ii:ii + 1, :], (pack, LANES)).astype(BF16)
        for jv in range(n_jv):
            js = slice(jv * pack, (jv + 1) * pack)
            term = jnp.where(r2s_ref[tc, h, js, :] < n_b, ps_ref[tc, h, js, :], zero) * c_b
            g[jv] = term if g[jv] is None else g[jv] + term
    for jv in range(n_jv):
        rows = slice(ii * PEER_KEYS + jv * pack, ii * PEER_KEYS + (jv + 1) * pack)
        ht_ref[rows, lanes] = _gelu_tanh(at_ref[rows, lanes]).astype(BF16) * g[jv]


def _peer_ffn_kernel(xn_ref, u_ref, vt_ref, r2_ref, p_ref, n_ref, c_ref, h_ref, gfin_ref,
                     out_ref, acc_ref, at0_ref, at1_ref, ht0_ref, ht1_ref, r2s_ref, ps_ref, xs_ref,
                     *, n_e, n_blocks, final_norm):
    g = pl.program_id(0)
    tt = xn_ref.shape[0]
    e_score = g % n_e
    e_gate = jnp.maximum(g - 1, 0) % n_e
    e_down = jnp.maximum(g - 2, 0) % n_e

    @pl.when(g == 0)
    def _():
        at1_ref[...] = jnp.zeros_like(at1_ref)
        ht0_ref[...] = jnp.zeros_like(ht0_ref)
        ht1_ref[...] = jnp.zeros_like(ht1_ref)
        acc_ref[...] = jnp.zeros_like(acc_ref)

    @pl.when((g < n_blocks) & (e_score == 0))
    def _():
        xs_ref[...] = xn_ref[...]

    @pl.when((g <= n_blocks) & (e_gate == 0))
    def _():
        for tc in range(tt // LANES):
            for h in range(PEER_HEADS):
                r2s_ref[tc, h] = r2_ref[tc, h]
                ps_ref[tc, h] = p_ref[tc, h]

    @pl.when((g >= 2) & (e_down == 0))
    def _():
        acc_ref[...] = jnp.zeros_like(acc_ref)

    def stages(at_w, at_r, ht_w, ht_r):
        u_blk = pltpu.bitcast(u_ref[...], BF16)
        vt_blk = pltpu.bitcast(vt_ref[...], BF16)
        eb = u_blk.shape[0]
        subs = [slice(sb * SUB_FFN, (sb + 1) * SUB_FFN) for sb in range(eb // SUB_FFN)]

        def score(ex):
            at_w[ex, :] = _dot_nt(u_blk[ex], xs_ref[...])

        def down():
            acc_ref[...] += _dot(vt_blk, ht_r[...])

        chunks = [functools.partial(score, ex) for ex in subs] + [down]
        units = [(tc, ii) for tc in range(tt // LANES) for ii in range(eb // PEER_KEYS)]
        split = GATE_UNIT_SPLIT
        assert len(split) == len(chunks) + 1 and sum(split) == len(units)
        bounds = [sum(split[:k]) for k in range(len(split) + 1)]

        def gate_units(k):
            for tc, ii in units[bounds[k]:bounds[k + 1]]:
                _peer_gate_unit(tc, ii, at_r, ht_w, r2s_ref, ps_ref, n_ref, c_ref)

        gate_units(0)
        for k, chunk in enumerate(chunks):
            chunk()
            gate_units(k + 1)

    @pl.when(g % 2 == 0)
    def _():
        stages(at0_ref, at1_ref, ht1_ref, ht0_ref)

    @pl.when(g % 2 == 1)
    def _():
        stages(at1_ref, at0_ref, ht0_ref, ht1_ref)

    @pl.when((g >= 2) & (e_down == n_e - 1))
    def _():
        res = h_ref[...] + acc_ref[...].T
        out_ref[...] = _rms(res, gfin_ref[...]) if final_norm else res


def _peer_ffn(xn, u_pack, vt_pack, r2, p, n, coef, h2, gfin, final_norm):
    T, D = h2.shape
    E = vt_pack.shape[1]
    tt = min(TT_FFN, T)
    nc = tt // LANES
    eb = EB_FFN
    n_i = eb // PEER_KEYS
    n_e = E // eb
    n_blocks = (T // tt) * n_e

    def block(lag):
        def split(g):
            b = jnp.clip(g - lag, 0, n_blocks - 1)
            return b // n_e, b % n_e
        return split

    score, gate, down = block(0), block(1), block(2)
    aux_shape = (nc, PEER_HEADS, PEER_KEYS, LANES)
    row_shape = (nc, PEER_HEADS, n_i, LANES)
    return pl.pallas_call(
        functools.partial(_peer_ffn_kernel, n_e=n_e, n_blocks=n_blocks, final_norm=final_norm),
        grid=(n_blocks + 2,),
        in_specs=[pl.BlockSpec((tt, D), lambda g: (score(g)[0], 0)),
                  pl.BlockSpec((eb // 2, D), lambda g: (score(g)[1], 0)),
                  pl.BlockSpec((D // 2, eb), lambda g: (0, down(g)[1])),
                  pl.BlockSpec(aux_shape, lambda g: (gate(g)[0], 0, 0, 0)),
                  pl.BlockSpec(aux_shape, lambda g: (gate(g)[0], 0, 0, 0)),
                  pl.BlockSpec(row_shape, lambda g: (gate(g)[0], 0, gate(g)[1], 0)),
                  pl.BlockSpec(row_shape, lambda g: (gate(g)[0], 0, gate(g)[1], 0)),
                  pl.BlockSpec((tt, D), lambda g: (down(g)[0], 0)),
                  pl.BlockSpec(gfin.shape, lambda g: (0, 0))],
        out_specs=pl.BlockSpec((tt, D), lambda g: (down(g)[0], 0)),
        out_shape=jax.ShapeDtypeStruct((T, D), F32),
        scratch_shapes=[pltpu.VMEM((D, tt), F32),
                        pltpu.VMEM((eb, tt), F32), pltpu.VMEM((eb, tt), F32),
                        pltpu.VMEM((eb, tt), BF16), pltpu.VMEM((eb, tt), BF16),
                        pltpu.VMEM(aux_shape, BF16), pltpu.VMEM(aux_shape, BF16),
                        pltpu.VMEM((tt, D), BF16)],
        compiler_params=pltpu.CompilerParams(dimension_semantics=("arbitrary",), vmem_limit_bytes=VMEM_LIMIT),
        name="peer_ffn",
    )(xn, u_pack, vt_pack, r2, p, n, coef, h2, gfin)


def _head_blocks(w, n_heads, width, pieces):
    w3 = w.reshape(w.shape[0], n_heads, width)
    out = jnp.zeros((w.shape[0], n_heads, LANES), w.dtype)
    for s0, s1, d0 in pieces:
        out = out.at[:, :, d0:d0 + (s1 - s0)].set(w3[:, :, s0:s1])
    return out.reshape(w.shape[0], n_heads * LANES)


def kernel(x, mem, positions, g_mix, w_in, g_q, w_uq, g_kv, w_ukv, conv_w, g_out, w_o, g_x, g_mem, w_xq,
           w_xkv, w_xo, g_ffn, w_pq, sub_keys, u_experts, v_experts, g_final):
    B, S, D = x.shape
    T = B * S
    depth = g_mix.shape[0]
    half = QK_ROPE // 2
    assert S % min(TS_IN, S) == 0 and S % min(TQ, S) == 0 and D == MLA_HEADS * LANES
    assert T % min(TS_ROUTE, T) == 0 and T % min(TT_FFN, T) == 0 and u_experts.shape[1] % EB_FFN == 0

    inv = ROPE_THETA ** (-jnp.arange(0, QK_ROPE, 2, dtype=F32) / QK_ROPE)
    ang = positions.astype(F32)[..., None] * inv
    cos = jnp.cos(ang).astype(x.dtype).reshape(T, half)
    sin = jnp.sin(ang).astype(x.dtype).reshape(T, half)
    ones = jnp.ones((T, QK_NOPE), F32)
    zeros_n = jnp.zeros((T, QK_NOPE), F32)
    pad_q = jnp.zeros((T, LANES - QK_NOPE - QK_ROPE), F32)
    cos_t = jnp.concatenate([ones, cos, cos, pad_q], axis=1)
    sin_t = jnp.concatenate([zeros_n, -sin, sin, pad_q], axis=1)

    lane = jnp.arange(LANES)
    col = jnp.arange(MLA_HEADS * LANES)
    rope_lane = (lane >= QK_NOPE) & (lane < QK_NOPE + QK_ROPE)
    eplace = ((col[None, :] % LANES == lane[:, None]) & rope_lane[:, None]).astype(BF16)
    mix_col = jnp.arange(D)
    gsum = (mix_col[:, None] // GROUP_DIM == lane[None, :]).astype(BF16)
    gexp = (lane[:, None] == mix_col[None, :] // GROUP_DIM).astype(BF16)

    h = x.reshape(T, D)
    for l in range(depth):
        o1 = Q_RANK
        o2 = o1 + KV_RANK
        o3 = o2 + QK_ROPE
        o4 = o3 + CONV_DIM
        o5 = o4 + CONV_DIM
        wl = w_in[l]
        w_kr = wl[:, o2:o3]
        w_krr = jnp.concatenate([w_kr[:, half:], w_kr[:, :half]], axis=1)
        pad_lo = jnp.zeros((D, QK_NOPE), wl.dtype)
        pad_hi = jnp.zeros((D, LANES - QK_NOPE - QK_ROPE), wl.dtype)
        w1 = jnp.concatenate([wl[:, :o1], wl[:, o1:o2], pad_lo, w_kr, pad_hi, pad_lo, w_krr, pad_hi,
                              wl[:, o3:o4], wl[:, o4:o5], wl[:, o5:]], axis=1).astype(BF16)
        qw = QK_NOPE + QK_ROPE
        wq = _head_blocks(w_uq[l], MLA_HEADS, qw, [(0, qw, 0)]).astype(BF16)
        wqr = _head_blocks(w_uq[l], MLA_HEADS, qw,
                           [(QK_NOPE + half, qw, QK_NOPE), (QK_NOPE, QK_NOPE + half, QK_NOPE + half)]).astype(BF16)
        kvw = QK_NOPE + V_HEAD
        wk = _head_blocks(w_ukv[l], MLA_HEADS, kvw, [(0, QK_NOPE, 0)]).astype(BF16)
        v_cols = w_ukv[l].reshape(KV_RANK, MLA_HEADS, kvw)[:, :, QK_NOPE:]
        v_pad = jnp.zeros_like(v_cols)
        odd_head = (jnp.arange(MLA_HEADS) % 2 == 1)[None, :, None]
        wv = jnp.where(odd_head, jnp.concatenate([v_pad, v_cols], axis=-1),
                       jnp.concatenate([v_cols, v_pad], axis=-1)).reshape(KV_RANK, MLA_HEADS * LANES).astype(BF16)
        vone = jnp.stack([(lane == _ones_lane(hd)).astype(F32) for hd in range(MLA_HEADS)]).reshape(1, -1)

        q, k, v, z, gb = _mixer_in(h, g_mix[l][None, :], w1, g_q[l][None, :], wq, wqr, g_kv[l][None, :], wk, wv,
                                   vone, eplace, cos_t, sin_t)
        o = _mla_attn(q.reshape(B, S, -1), k.reshape(B, S, -1), v.reshape(B, S, -1))
        h = _mixer_out(o.reshape(T, -1), z, gb, h, conv_w[l], g_out[l][None, :], gsum, gexp,
                       w_o[l].astype(BF16), S)

        kx, vx = _mem_kv(mem, g_mem[l][None, :], w_xkv[l].astype(BF16))
        h = _xattn(h.reshape(B, S, D), g_x[l][None, :], w_xq[l].astype(BF16), kx, vx,
                   w_xo[l].astype(BF16)).reshape(T, D)

        sk = sub_keys[l]
        zk = jnp.zeros_like(sk[:, 0])
        keys_bd = jnp.concatenate([jnp.concatenate([sk[:, 0], zk], axis=-1),
                                   jnp.concatenate([zk, sk[:, 1]], axis=-1)], axis=1)
        xn, r2, p, n, coef = _peer_route(h, g_ffn[l][None, :], w_pq[l].astype(BF16), keys_bd.astype(BF16))
        u_pack, vt_pack = _pack_experts(u_experts[l], v_experts[l])
        h = _peer_ffn(xn, u_pack, vt_pack, r2, p, n, coef, h,
                      g_final[None, :], final_norm=(l == depth - 1))
    return h.reshape(B, S, D)
```

```python
import functools
import math

import jax
import jax.numpy as jnp
from jax import lax
from jax.experimental import pallas as pl
from jax.experimental.pallas import tpu as pltpu

F32 = jnp.float32
BF16 = jnp.bfloat16

EPS = 1e-6
LANES = 128
SUBLANES = 8
VMEM_LIMIT = 56 * 1024 * 1024

MLA_HEADS = 8
QK_NOPE = 64
QK_ROPE = 32
V_HEAD = 64
Q_RANK = 384
KV_RANK = 256
CONV_DIM = 512
GROUP_DIM = 64
ROPE_THETA = 10000.0
X_HEADS = 4
PEER_HEADS = 8
PEER_KEYS = 128
PEER_TOPK = 16

TS_IN = 512
ROW_SPLIT = 2
TQ = 512
TS_ROUTE = 1024
TT_FFN = 512
EB_FFN = 2048
SUB_FFN = 1024
GATE_UNIT_SPLIT = (8, 16, 40, 0)
NT_DIMS = (((1,), (1,)), ((), ()))


def _rms(x, g):
    return x * lax.rsqrt(jnp.mean(x * x, axis=-1, keepdims=True) + EPS) * g


def _split_bf16(x):
    hi = x.astype(BF16)
    lo = (x - hi.astype(F32)).astype(BF16)
    return hi, lo


def _gelu_tanh(x):
    c0 = math.sqrt(2.0 / math.pi)
    half_x = 0.5 * x
    return half_x + half_x * jnp.tanh(x * (c0 + (c0 * 0.044715) * (x * x)))


def _dot(a, b):
    return jnp.dot(a, b, preferred_element_type=F32)


def _dot_nt(a, b):
    return lax.dot_general(a, b, NT_DIMS, preferred_element_type=F32)


_C_CQ = 0
_C_CKV = _C_CQ + Q_RANK
_C_KR = _C_CKV + KV_RANK
_C_KRR = _C_KR + LANES
_C_GB = _C_KRR + LANES
_C_GC = _C_GB + CONV_DIM
_C_HX = _C_GC + CONV_DIM
_C_END = _C_HX + CONV_DIM


def _mixer_in_kernel(x_ref, gmix_ref, w1_ref, gq_ref, wq_ref, wqr_ref, gkv_ref, wk_ref, wv_ref, vone_ref,
                     eplace_ref, cos_ref, sin_ref,
                     q_out, k_out, v_out, z_out, gb_out):
    xn = _rms(x_ref[...], gmix_ref[...]).astype(BF16)
    proj = _dot(xn, w1_ref[...])
    cq = proj[:, _C_CQ:_C_CKV]
    ckv = proj[:, _C_CKV:_C_KR]
    kr = proj[:, _C_KR:_C_KRR]
    krr = proj[:, _C_KRR:_C_GB]
    gb_out[...] = proj[:, _C_GB:_C_GC]
    z_out[...] = proj[:, _C_GC:_C_HX] * proj[:, _C_HX:_C_END]

    cqn = _rms(cq, gq_ref[...]).astype(BF16)
    q_raw = _dot(cqn, wq_ref[...])
    q_rot = _dot(cqn, wqr_ref[...])
    cos_t = cos_ref[...]
    sin_t = sin_ref[...]
    q_scale = math.log2(math.e) / math.sqrt(QK_NOPE + QK_ROPE)
    for h in range(MLA_HEADS):
        sl = slice(h * LANES, (h + 1) * LANES)
        q_out[:, sl] = ((q_raw[:, sl] * cos_t + q_rot[:, sl] * sin_t) * q_scale).astype(BF16)

    ckvn = _rms(ckv, gkv_ref[...]).astype(BF16)
    kr_roped = (kr * cos_t + krr * sin_t).astype(BF16)
    k_out[...] = (_dot(ckvn, wk_ref[...]) + _dot(kr_roped, eplace_ref[...])).astype(BF16)
    v_out[...] = (_dot(ckvn, wv_ref[...]) + vone_ref[...]).astype(BF16)


def _mixer_in(x2, gmix, w1, gq, wq, wqr, gkv, wk, wv, vone, eplace, cos_t, sin_t):
    T, D = x2.shape
    ts = min(TS_IN, T)
    row = lambda i: (i, 0)
    fixed = lambda i: (0, 0)
    full = lambda a: pl.BlockSpec(a.shape, fixed)
    return pl.pallas_call(
        _mixer_in_kernel,
        grid=(T // ts,),
        in_specs=[pl.BlockSpec((ts, D), row), full(gmix), full(w1), full(gq), full(wq), full(wqr),
                  full(gkv), full(wk), full(wv), full(vone), full(eplace),
                  pl.BlockSpec((ts, LANES), row), pl.BlockSpec((ts, LANES), row)],
        out_specs=[pl.BlockSpec((ts, MLA_HEADS * LANES), row), pl.BlockSpec((ts, MLA_HEADS * LANES), row),
                   pl.BlockSpec((ts, MLA_HEADS * LANES), row), pl.BlockSpec((ts, CONV_DIM), row),
                   pl.BlockSpec((ts, CONV_DIM), row)],
        out_shape=[jax.ShapeDtypeStruct((T, MLA_HEADS * LANES), BF16),
                   jax.ShapeDtypeStruct((T, MLA_HEADS * LANES), BF16),
                   jax.ShapeDtypeStruct((T, MLA_HEADS * LANES), BF16),
                   jax.ShapeDtypeStruct((T, CONV_DIM), F32),
                   jax.ShapeDtypeStruct((T, CONV_DIM), F32)],
        compiler_params=pltpu.CompilerParams(dimension_semantics=("arbitrary",), vmem_limit_bytes=VMEM_LIMIT),
        name="mixer_in",
    )(x2, gmix, w1, gq, wq, wqr, gkv, wk, wv, vone, eplace, cos_t, sin_t)


def _ones_lane(head):
    return V_HEAD if head % 2 == 0 else 0


def _mla_attn_kernel(q_ref, k_ref, v_ref, o_ref, *, tq):
    seq = q_ref.shape[0]
    causal = (lax.broadcasted_iota(jnp.int32, (tq, tq), 1) <= lax.broadcasted_iota(jnp.int32, (tq, tq), 0))
    lane = lax.broadcasted_iota(jnp.int32, (tq, LANES), 1)
    for qi in range(seq // tq):
        rows = slice(qi * tq, (qi + 1) * tq)
        keys = slice(0, (qi + 1) * tq)
        outs = []
        for hh in range(2):
            hl = slice(hh * LANES, (hh + 1) * LANES)
            s = _dot_nt(q_ref[rows, hl], k_ref[keys, hl])
            s_diag = jnp.where(causal, s[:, qi * tq:], -jnp.inf)
            s = s_diag if qi == 0 else jnp.concatenate([s[:, :qi * tq], s_diag], axis=1)
            m = jnp.max(s, axis=-1, keepdims=True)
            acc = _dot(jnp.exp2(s - m).astype(BF16), v_ref[keys, hl])
            one = _ones_lane(hh)
            outs.append(acc * (1.0 / acc[:, one:one + 1]))
        o_ref[rows, :] = jnp.where(lane < V_HEAD, outs[0], outs[1])


def _mla_attn(q3, k3, v3):
    B, S, _ = q3.shape
    tq = min(TQ, S)
    pair = lambda b, g: (b, 0, g)
    return pl.pallas_call(
        functools.partial(_mla_attn_kernel, tq=tq),
        grid=(B, MLA_HEADS // 2),
        in_specs=[pl.BlockSpec((None, S, 2 * LANES), pair), pl.BlockSpec((None, S, 2 * LANES), pair),
                  pl.BlockSpec((None, S, 2 * LANES), pair)],
        out_specs=pl.BlockSpec((None, S, 2 * V_HEAD), pair),
        out_shape=jax.ShapeDtypeStruct((B, S, MLA_HEADS * V_HEAD), F32),
        compiler_params=pltpu.CompilerParams(dimension_semantics=("arbitrary", "arbitrary"),
                                             vmem_limit_bytes=VMEM_LIMIT),
        name="mla_attn",
    )(q3, k3, v3)


def _mixer_out_kernel(o_ref, z_ref, zh_ref, gb_ref, x_ref, cw_ref, gout_ref, gsum_ref, gexp_ref, wo_ref,
                      h_out, *, tiles_per_seq):
    i = pl.program_id(0)
    ts = z_ref.shape[0]
    z = z_ref[...]
    halo = jnp.where(i % tiles_per_seq == 0, 0.0, zh_ref[...])
    row = lax.broadcasted_iota(jnp.int32, z.shape, 0)
    z1 = jnp.where(row == 0, halo[7:8, :], pltpu.roll(z, 1, axis=0))
    z2 = jnp.where(row == 0, halo[6:7, :], jnp.where(row == 1, halo[7:8, :], pltpu.roll(z, 2, axis=0)))
    cw = cw_ref[...]
    y_conv = gb_ref[...] * (cw[0:1, :] * z2 + cw[1:2, :] * z1 + cw[2:3, :] * z)
    y_all = jnp.concatenate([o_ref[...], y_conv], axis=-1)
    for part in range(ROW_SPLIT):
        rows = slice(part * ts // ROW_SPLIT, (part + 1) * ts // ROW_SPLIT)
        y = y_all[rows]
        sq_hi, sq_lo = _split_bf16(y * y)
        gs = _dot(sq_hi, gsum_ref[...]) + _dot(sq_lo, gsum_ref[...])
        r = lax.rsqrt(gs * (1.0 / GROUP_DIM) + EPS)
        r_hi, r_lo = _split_bf16(r)
        r_full = _dot(r_hi, gexp_ref[...]) + _dot(r_lo, gexp_ref[...])
        yn = (y * r_full * gout_ref[...]).astype(BF16)
        h_out[rows, :] = x_ref[rows, :] + _dot(yn, wo_ref[...])


def _mixer_out(o2, z, gb, x2, conv_w, gout, gsum, gexp, wo, seq):
    T, D = x2.shape
    ts = min(TS_IN, seq)
    row = lambda i: (i, 0)
    fixed = lambda i: (0, 0)
    full = lambda a: pl.BlockSpec(a.shape, fixed)
    halo_blocks = ts // SUBLANES
    return pl.pallas_call(
        functools.partial(_mixer_out_kernel, tiles_per_seq=seq // ts),
        grid=(T // ts,),
        in_specs=[pl.BlockSpec((ts, MLA_HEADS * V_HEAD), row), pl.BlockSpec((ts, CONV_DIM), row),
                  pl.BlockSpec((SUBLANES, CONV_DIM), lambda i: (jnp.maximum(i * halo_blocks - 1, 0), 0)),
                  pl.BlockSpec((ts, CONV_DIM), row), pl.BlockSpec((ts, D), row),
                  full(conv_w), full(gout), full(gsum), full(gexp), full(wo)],
        out_specs=pl.BlockSpec((ts, D), row),
        out_shape=jax.ShapeDtypeStruct((T, D), F32),
        compiler_params=pltpu.CompilerParams(dimension_semantics=("arbitrary",), vmem_limit_bytes=VMEM_LIMIT),
        name="mixer_out",
    )(o2, z, z, gb, x2, conv_w, gout, gsum, gexp, wo)


def _mem_kv_kernel(mem_ref, g_ref, w_ref, k_out, v_out):
    d = mem_ref.shape[-1]
    mn = _rms(mem_ref[...], g_ref[...]).astype(BF16)
    kv = _dot(mn, w_ref[...])
    k_out[...] = kv[:, :d].astype(BF16)
    v_out[...] = kv[:, d:].astype(BF16)


def _mem_kv(mem, g, w):
    B, M, D = mem.shape
    return pl.pallas_call(
        _mem_kv_kernel,
        grid=(B,),
        in_specs=[pl.BlockSpec((None, M, D), lambda b: (b, 0, 0)), pl.BlockSpec(g.shape, lambda b: (0, 0)),
                  pl.BlockSpec(w.shape, lambda b: (0, 0))],
        out_specs=[pl.BlockSpec((None, M, D), lambda b: (b, 0, 0)), pl.BlockSpec((None, M, D), lambda b: (b, 0, 0))],
        out_shape=[jax.ShapeDtypeStruct((B, M, D), BF16), jax.ShapeDtypeStruct((B, M, D), BF16)],
        compiler_params=pltpu.CompilerParams(dimension_semantics=("arbitrary",), vmem_limit_bytes=VMEM_LIMIT),
        name="mem_kv",
    )(mem, g, w)


def _xattn_kernel(h_ref, g_ref, wq_ref, k_ref, v_ref, wo_ref, h_out):
    h = h_ref[...]
    d = h.shape[-1]
    hd = d // X_HEADS
    hn = _rms(h, g_ref[...]).astype(BF16)
    q = _dot(hn, wq_ref[...]).astype(BF16)
    outs = []
    for hh in range(X_HEADS):
        sl = slice(hh * hd, (hh + 1) * hd)
        s = _dot_nt(q[:, sl], k_ref[:, sl]) * (1.0 / math.sqrt(hd))
        m = jnp.max(s, axis=-1, keepdims=True)
        p = jnp.exp(s - m)
        p = p * (1.0 / jnp.sum(p, axis=-1, keepdims=True))
        outs.append(_dot(p.astype(BF16), v_ref[:, sl]))
    o = jnp.concatenate(outs, axis=-1).astype(BF16)
    h_out[...] = h + _dot(o, wo_ref[...])


def _xattn(h3, g, wq, kx, vx, wo):
    B, S, D = h3.shape
    M = kx.shape[1]
    ts = min(TS_IN, S)
    fixed = lambda b, i: (0, 0)
    return pl.pallas_call(
        _xattn_kernel,
        grid=(B, S // ts),
        in_specs=[pl.BlockSpec((None, ts, D), lambda b, i: (b, i, 0)), pl.BlockSpec(g.shape, fixed),
                  pl.BlockSpec(wq.shape, fixed), pl.BlockSpec((None, M, D), lambda b, i: (b, 0, 0)),
                  pl.BlockSpec((None, M, D), lambda b, i: (b, 0, 0)), pl.BlockSpec(wo.shape, fixed)],
        out_specs=pl.BlockSpec((None, ts, D), lambda b, i: (b, i, 0)),
        out_shape=jax.ShapeDtypeStruct((B, S, D), F32),
        compiler_params=pltpu.CompilerParams(dimension_semantics=("arbitrary", "arbitrary"),
                                             vmem_limit_bytes=VMEM_LIMIT),
        name="xattn",
    )(h3, g, wq, kx, vx, wo)


def _batcher_pairs(n):
    pairs = []
    p = 1
    while p < n:
        k = p
        while k >= 1:
            for j in range(k % p, n - k, 2 * k):
                for i in range(min(k, n - j - k)):
                    if (i + j) // (2 * p) == (i + j + k) // (2 * p):
                        pairs.append((i + j, i + j + k))
            k //= 2
        p *= 2
    return pairs


_SORT16 = _batcher_pairs(PEER_TOPK)
_ROW_LEN = [PEER_TOPK // (a + 1) for a in range(PEER_TOPK)]


def _sort_desc(v):
    v = list(v)
    for i, j in _SORT16:
        hi = jnp.maximum(v[i], v[j])
        lo = jnp.minimum(v[i], v[j])
        v[i], v[j] = hi, lo
    return v


def _bitonic_desc(v):
    v = list(v)
    n = len(v)
    d = n // 2
    while d >= 1:
        for k in range(n):
            if k & d == 0:
                hi = jnp.maximum(v[k], v[k + d])
                lo = jnp.minimum(v[k], v[k + d])
                v[k], v[k + d] = hi, lo
        d //= 2
    return v


def _merge_top(cur, other):
    n = len(cur)
    c = list(cur)
    for r, val in enumerate(other):
        c[n - 1 - r] = jnp.maximum(c[n - 1 - r], val)
    return _bitonic_desc(c)


def _top16_sorted(s):
    v = _sort_desc([s[k] for k in range(PEER_TOPK)])
    for shift in (4, 2, 1):
        other = [pltpu.roll(v[k], shift, axis=0) for k in range(PEER_TOPK)]
        v = _merge_top(v, other)
    return v


def _peer_route_kernel(h_ref, g_ref, wq_ref, key_ref,
                       xn_out, r2_out, p_out, n_out, c_out,
                       st_ref, top_ref, res_ref):
    ts = h_ref.shape[0]
    n_chunk = ts // LANES
    hn = _rms(h_ref[...], g_ref[...]).astype(BF16)
    xn_out[...] = hn
    q = _dot(hn, wq_ref[...])
    for h in range(PEER_HEADS):
        st = _dot_nt(key_ref[h], q[:, h * LANES:(h + 1) * LANES].astype(BF16))
        for c in range(n_chunk):
            st_ref[c, h] = st[:, c * LANES:(c + 1) * LANES]


    def chunk_body(c, _):
        def sort_body(h, _):
            for half in range(2):
                s = st_ref[c, h, pl.ds(half * PEER_KEYS, PEER_KEYS), :].reshape(PEER_TOPK, SUBLANES, LANES)
                v = _top16_sorted(s)
                for a in range(PEER_TOPK):
                    top_ref[half, a, pl.ds(h, 1), :] = v[a][0:1, :]
            return 0

        lax.fori_loop(0, PEER_HEADS, sort_body, 0)

        v1 = [top_ref[0, a] for a in range(PEER_TOPK)]
        v2 = [top_ref[1, b] for b in range(PEER_TOPK)]
        sums = [[v1[a] + v2[b] for b in range(_ROW_LEN[a])] for a in range(PEER_TOPK)]
        cur = sums[0]
        a = 1
        while _ROW_LEN[a] > 1:
            cur = _merge_top(cur, sums[a])
            a += 1
        cur = _merge_top(cur, [sums[r][0] for r in range(a, PEER_TOPK)])
        tau = cur[PEER_TOPK - 1]
        top_sum = sums[0][0]
        z = jnp.zeros_like(tau)
        for a in range(PEER_TOPK):
            cnt = jnp.zeros_like(tau)
            for b in range(_ROW_LEN[a]):
                sel = sums[a][b] >= tau
                cnt = cnt + jnp.where(sel, 1.0, 0.0)
                z = z + jnp.where(sel, jnp.exp(sums[a][b] - top_sum), 0.0)
            res_ref[a] = cnt
        res_ref[PEER_TOPK] = 1.0 / z

        def expand_body(h, _):
            s1 = st_ref[c, h, pl.ds(0, PEER_KEYS), :].reshape(PEER_TOPK, SUBLANES, LANES)
            s2 = st_ref[c, h, pl.ds(PEER_KEYS, PEER_KEYS), :].reshape(PEER_TOPK, SUBLANES, LANES)
            n = jnp.zeros(s1.shape, F32)
            r2 = jnp.full(s2.shape, float(PEER_TOPK), F32)
            for a in range(PEER_TOPK - 1, -1, -1):
                v1a = top_ref[0, a, pl.ds(h, 1), :]
                v2a = top_ref[1, a, pl.ds(h, 1), :]
                n = jnp.where(s1 == v1a, res_ref[a, pl.ds(h, 1), :], n)
                r2 = jnp.where(s2 == v2a, float(a), r2)
            m1 = top_ref[0, 0, pl.ds(h, 1), :]
            m2 = top_ref[1, 0, pl.ds(h, 1), :]
            inv_z = res_ref[PEER_TOPK, pl.ds(h, 1), :]
            n_out[c, h] = n.reshape(PEER_KEYS, LANES)
            c_out[c, h] = (jnp.exp(s1 - m1) * inv_z).reshape(PEER_KEYS, LANES)
            r2_out[c, h] = r2.reshape(PEER_KEYS, LANES).astype(BF16)
            p_out[c, h] = jnp.exp(s2 - m2).reshape(PEER_KEYS, LANES).astype(BF16)
            return 0

        lax.fori_loop(0, PEER_HEADS, expand_body, 0)
        return 0

    lax.fori_loop(0, n_chunk, chunk_body, 0)


def _peer_route(h2, g, wq, keys):
    T, D = h2.shape
    ts = min(TS_ROUTE, T)
    nc = ts // LANES
    aux_spec = pl.BlockSpec((nc, PEER_HEADS, PEER_KEYS, LANES), lambda i: (i, 0, 0, 0))
    aux_shape = jax.ShapeDtypeStruct((T // LANES, PEER_HEADS, PEER_KEYS, LANES), F32)
    aux_shape_bf = jax.ShapeDtypeStruct((T // LANES, PEER_HEADS, PEER_KEYS, LANES), BF16)
    fixed2 = lambda i: (0, 0)
    fixed3 = lambda i: (0, 0, 0)
    return pl.pallas_call(
        _peer_route_kernel,
        grid=(T // ts,),
        in_specs=[pl.BlockSpec((ts, D), lambda i: (i, 0)), pl.BlockSpec(g.shape, fixed2),
                  pl.BlockSpec(wq.shape, fixed2), pl.BlockSpec(keys.shape, fixed3)],
        out_specs=[pl.BlockSpec((ts, D), lambda i: (i, 0)), aux_spec, aux_spec, aux_spec, aux_spec],
        out_shape=[jax.ShapeDtypeStruct((T, D), BF16), aux_shape_bf, aux_shape_bf, aux_shape, aux_shape],
        scratch_shapes=[pltpu.VMEM((nc, PEER_HEADS, 2 * PEER_KEYS, LANES), F32),
                        pltpu.VMEM((2, PEER_TOPK, SUBLANES, LANES), F32),
                        pltpu.VMEM((PEER_TOPK + 1, SUBLANES, LANES), F32)],
        compiler_params=pltpu.CompilerParams(dimension_semantics=("arbitrary",), vmem_limit_bytes=VMEM_LIMIT),
        name="peer_route",
    )(h2, g, wq, keys)


def _pack_experts_kernel(u_ref, v_ref, u_out, vt_out):
    u_out[...] = pltpu.bitcast(u_ref[...].astype(BF16), jnp.uint32)
    vt_out[...] = pltpu.bitcast(v_ref[...].T.astype(BF16), jnp.uint32)


def _pack_experts(u, v):
    E, D = u.shape
    eb = EB_FFN
    return pl.pallas_call(
        _pack_experts_kernel,
        grid=(E // eb,),
        in_specs=[pl.BlockSpec((eb, D), lambda e: (e, 0)), pl.BlockSpec((eb, D), lambda e: (e, 0))],
        out_specs=[pl.BlockSpec((eb // 2, D), lambda e: (e, 0)), pl.BlockSpec((D // 2, eb), lambda e: (0, e))],
        out_shape=[jax.ShapeDtypeStruct((E // 2, D), jnp.uint32), jax.ShapeDtypeStruct((D // 2, E), jnp.uint32)],
        compiler_params=pltpu.CompilerParams(dimension_semantics=("arbitrary",), vmem_limit_bytes=VMEM_LIMIT),
        name="pack_experts",
    )(u, v)


def _peer_gate_unit(tc, ii, at_ref, ht_ref, r2s_ref, ps_ref, n_ref, c_ref):
    pack = 2 * SUBLANES
    n_jv = PEER_KEYS // pack
    zero = jnp.zeros((pack, LANES), BF16)
    lanes = slice(tc * LANES, (tc + 1) * LANES)
    g = [None] * n_jv
    for h in range(PEER_HEADS):
        n_b = jnp.broadcast_to(n_ref[tc, h, ii:ii + 1, :], (pack, LANES)).astype(BF16)
        c_b = jnp.broadcast_to(c_ref[tc, h, ii:ii + 1, :], (pack, LANES)).astype(BF16)
        for jv in range(n_jv):
            js = slice(jv * pack, (jv + 1) * pack)
            term = jnp.where(r2s_ref[tc, h, js, :] < n_b, ps_ref[tc, h, js, :], zero) * c_b
            g[jv] = term if g[jv] is None else g[jv] + term
    for jv in range(n_jv):
        rows = slice(ii * PEER_KEYS + jv * pack, ii * PEER_KEYS + (jv + 1) * pack)
        ht_ref[rows, lanes] = _gelu_tanh(at_ref[rows, lanes]).astype(BF16) * g[jv]


def _peer_ffn_kernel(xn_ref, u_ref, vt_ref, r2_ref, p_ref, n_ref, c_ref, h_ref, gfin_ref,
                     out_ref, acc_ref, at0_ref, at1_ref, ht0_ref, ht1_ref, r2s_ref, ps_ref, xs_ref,
                     *, n_e, n_blocks, final_norm):
    g = pl.program_id(0)
    tt = xn_ref.shape[0]
    e_score = g % n_e
    e_gate = jnp.maximum(g - 1, 0) % n_e
    e_down = jnp.maximum(g - 2, 0) % n_e

    @pl.when(g == 0)
    def _():
        at1_ref[...] = jnp.zeros_like(at1_ref)
        ht0_ref[...] = jnp.zeros_like(ht0_ref)
        ht1_ref[...] = jnp.zeros_like(ht1_ref)
        acc_ref[...] = jnp.zeros_like(acc_ref)

    @pl.when((g < n_blocks) & (e_score == 0))
    def _():
        xs_ref[...] = xn_ref[...]

    @pl.when((g <= n_blocks) & (e_gate == 0))
    def _():
        for tc in range(tt // LANES):
            for h in range(PEER_HEADS):
                r2s_ref[tc, h] = r2_ref[tc, h]
                ps_ref[tc, h] = p_ref[tc, h]

    @pl.when((g >= 2) & (e_down == 0))
    def _():
        acc_ref[...] = jnp.zeros_like(acc_ref)

    def stages(at_w, at_r, ht_w, ht_r):
        u_blk = pltpu.bitcast(u_ref[...], BF16)
        vt_blk = pltpu.bitcast(vt_ref[...], BF16)
        eb = u_blk.shape[0]
        subs = [slice(sb * SUB_FFN, (sb + 1) * SUB_FFN) for sb in range(eb // SUB_FFN)]

        def score(ex):
            at_w[ex, :] = _dot_nt(u_blk[ex], xs_ref[...])

        def down():
            acc_ref[...] += _dot(vt_blk, ht_r[...])

        chunks = [functools.partial(score, ex) for ex in subs] + [down]
        units = [(tc, ii) for tc in range(tt // LANES) for ii in range(eb // PEER_KEYS)]
        split = GATE_UNIT_SPLIT
        assert len(split) == len(chunks) + 1 and sum(split) == len(units)
        bounds = [sum(split[:k]) for k in range(len(split) + 1)]

        def gate_units(k):
            for tc, ii in units[bounds[k]:bounds[k + 1]]:
                _peer_gate_unit(tc, ii, at_r, ht_w, r2s_ref, ps_ref, n_ref, c_ref)

        gate_units(0)
        for k, chunk in enumerate(chunks):
            chunk()
            gate_units(k + 1)

    @pl.when(g % 2 == 0)
    def _():
        stages(at0_ref, at1_ref, ht1_ref, ht0_ref)

    @pl.when(g % 2 == 1)
    def _():
        stages(at1_ref, at0_ref, ht0_ref, ht1_ref)

    @pl.when((g >= 2) & (e_down == n_e - 1))
    def _():
        res = h_ref[...] + acc_ref[...].T
        out_ref[...] = _rms(res, gfin_ref[...]) if final_norm else res


def _peer_ffn(xn, u_pack, vt_pack, r2, p, n, coef, h2, gfin, final_norm):
    T, D = h2.shape
    E = vt_pack.shape[1]
    tt = min(TT_FFN, T)
    nc = tt // LANES
    eb = EB_FFN
    n_i = eb // PEER_KEYS
    n_e = E // eb
    n_blocks = (T // tt) * n_e

    def block(lag):
        def split(g):
            b = jnp.clip(g - lag, 0, n_blocks - 1)
            return b // n_e, b % n_e
        return split

    score, gate, down = block(0), block(1), block(2)
    aux_shape = (nc, PEER_HEADS, PEER_KEYS, LANES)
    row_shape = (nc, PEER_HEADS, n_i, LANES)
    return pl.pallas_call(
        functools.partial(_peer_ffn_kernel, n_e=n_e, n_blocks=n_blocks, final_norm=final_norm),
        grid=(n_blocks + 2,),
        in_specs=[pl.BlockSpec((tt, D), lambda g: (score(g)[0], 0)),
                  pl.BlockSpec((eb // 2, D), lambda g: (score(g)[1], 0)),
                  pl.BlockSpec((D // 2, eb), lambda g: (0, down(g)[1])),
                  pl.BlockSpec(aux_shape, lambda g: (gate(g)[0], 0, 0, 0)),
                  pl.BlockSpec(aux_shape, lambda g: (gate(g)[0], 0, 0, 0)),
                  pl.BlockSpec(row_shape, lambda g: (gate(g)[0], 0, gate(g)[1], 0)),
                  pl.BlockSpec(row_shape, lambda g: (gate(g)[0], 0, gate(g)[1], 0)),
                  pl.BlockSpec((tt, D), lambda g: (down(g)[0], 0)),
                  pl.BlockSpec(gfin.shape, lambda g: (0, 0))],
        out_specs=pl.BlockSpec((tt, D), lambda g: (down(g)[0], 0)),
        out_shape=jax.ShapeDtypeStruct((T, D), F32),
        scratch_shapes=[pltpu.VMEM((D, tt), F32),
                        pltpu.VMEM((eb, tt), F32), pltpu.VMEM((eb, tt), F32),
                        pltpu.VMEM((eb, tt), BF16), pltpu.VMEM((eb, tt), BF16),
                        pltpu.VMEM(aux_shape, BF16), pltpu.VMEM(aux_shape, BF16),
                        pltpu.VMEM((tt, D), BF16)],
        compiler_params=pltpu.CompilerParams(dimension_semantics=("arbitrary",), vmem_limit_bytes=VMEM_LIMIT),
        name="peer_ffn",
    )(xn, u_pack, vt_pack, r2, p, n, coef, h2, gfin)


def _head_blocks(w, n_heads, width, pieces):
    w3 = w.reshape(w.shape[0], n_heads, width)
    out = jnp.zeros((w.shape[0], n_heads, LANES), w.dtype)
    for s0, s1, d0 in pieces:
        out = out.at[:, :, d0:d0 + (s1 - s0)].set(w3[:, :, s0:s1])
    return out.reshape(w.shape[0], n_heads * LANES)


def kernel(x, mem, positions, g_mix, w_in, g_q, w_uq, g_kv, w_ukv, conv_w, g_out, w_o, g_x, g_mem, w_xq,
           w_xkv, w_xo, g_ffn, w_pq, sub_keys, u_experts, v_experts, g_final):
    B, S, D = x.shape
    T = B * S
    depth = g_mix.shape[0]
    half = QK_ROPE // 2
    assert S % min(TS_IN, S) == 0 and S % min(TQ, S) == 0 and D == MLA_HEADS * LANES
    assert T % min(TS_ROUTE, T) == 0 and T % min(TT_FFN, T) == 0 and u_experts.shape[1] % EB_FFN == 0

    inv = ROPE_THETA ** (-jnp.arange(0, QK_ROPE, 2, dtype=F32) / QK_ROPE)
    ang = positions.astype(F32)[..., None] * inv
    cos = jnp.cos(ang).astype(x.dtype).reshape(T, half)
    sin = jnp.sin(ang).astype(x.dtype).reshape(T, half)
    ones = jnp.ones((T, QK_NOPE), F32)
    zeros_n = jnp.zeros((T, QK_NOPE), F32)
    pad_q = jnp.zeros((T, LANES - QK_NOPE - QK_ROPE), F32)
    cos_t = jnp.concatenate([ones, cos, cos, pad_q], axis=1)
    sin_t = jnp.concatenate([zeros_n, -sin, sin, pad_q], axis=1)

    lane = jnp.arange(LANES)
    col = jnp.arange(MLA_HEADS * LANES)
    rope_lane = (lane >= QK_NOPE) & (lane < QK_NOPE + QK_ROPE)
    eplace = ((col[None, :] % LANES == lane[:, None]) & rope_lane[:, None]).astype(BF16)
    mix_col = jnp.arange(D)
    gsum = (mix_col[:, None] // GROUP_DIM == lane[None, :]).astype(BF16)
    gexp = (lane[:, None] == mix_col[None, :] // GROUP_DIM).astype(BF16)

    h = x.reshape(T, D)
    for l in range(depth):
        o1 = Q_RANK
        o2 = o1 + KV_RANK
        o3 = o2 + QK_ROPE
        o4 = o3 + CONV_DIM
        o5 = o4 + CONV_DIM
        wl = w_in[l]
        w_kr = wl[:, o2:o3]
        w_krr = jnp.concatenate([w_kr[:, half:], w_kr[:, :half]], axis=1)
        pad_lo = jnp.zeros((D, QK_NOPE), wl.dtype)
        pad_hi = jnp.zeros((D, LANES - QK_NOPE - QK_ROPE), wl.dtype)
        w1 = jnp.concatenate([wl[:, :o1], wl[:, o1:o2], pad_lo, w_kr, pad_hi, pad_lo, w_krr, pad_hi,
                              wl[:, o3:o4], wl[:, o4:o5], wl[:, o5:]], axis=1).astype(BF16)
        qw = QK_NOPE + QK_ROPE
        wq = _head_blocks(w_uq[l], MLA_HEADS, qw, [(0, qw, 0)]).astype(BF16)
        wqr = _head_blocks(w_uq[l], MLA_HEADS, qw,
                           [(QK_NOPE + half, qw, QK_NOPE), (QK_NOPE, QK_NOPE + half, QK_NOPE + half)]).astype(BF16)
        kvw = QK_NOPE + V_HEAD
        wk = _head_blocks(w_ukv[l], MLA_HEADS, kvw, [(0, QK_NOPE, 0)]).astype(BF16)
        v_cols = w_ukv[l].reshape(KV_RANK, MLA_HEADS, kvw)[:, :, QK_NOPE:]
        v_pad = jnp.zeros_like(v_cols)
        odd_head = (jnp.arange(MLA_HEADS) % 2 == 1)[None, :, None]
        wv = jnp.where(odd_head, jnp.concatenate([v_pad, v_cols], axis=-1),
                       jnp.concatenate([v_cols, v_pad], axis=-1)).reshape(KV_RANK, MLA_HEADS * LANES).astype(BF16)
        vone = jnp.stack([(lane == _ones_lane(hd)).astype(F32) for hd in range(MLA_HEADS)]).reshape(1, -1)

        q, k, v, z, gb = _mixer_in(h, g_mix[l][None, :], w1, g_q[l][None, :], wq, wqr, g_kv[l][None, :], wk, wv,
                                   vone, eplace, cos_t, sin_t)
        o = _mla_attn(q.reshape(B, S, -1), k.reshape(B, S, -1), v.reshape(B, S, -1))
        h = _mixer_out(o.reshape(T, -1), z, gb, h, conv_w[l], g_out[l][None, :], gsum, gexp,
                       w_o[l].astype(BF16), S)

        kx, vx = _mem_kv(mem, g_mem[l][None, :], w_xkv[l].astype(BF16))
        h = _xattn(h.reshape(B, S, D), g_x[l][None, :], w_xq[l].astype(BF16), kx, vx,
                   w_xo[l].astype(BF16)).reshape(T, D)

        sk = sub_keys[l]
        zk = jnp.zeros_like(sk[:, 0])
        keys_bd = jnp.concatenate([jnp.concatenate([sk[:, 0], zk], axis=-1),
                                   jnp.concatenate([zk, sk[:, 1]], axis=-1)], axis=1)
        xn, r2, p, n, coef = _peer_route(h, g_ffn[l][None, :], w_pq[l].astype(BF16), keys_bd.astype(BF16))
        u_pack, vt_pack = _pack_experts(u_experts[l], v_experts[l])
        h = _peer_ffn(xn, u_pack, vt_pack, r2, p, n, coef, h,
                      g_final[None, :], final_norm=(l == depth - 1))
    return h.reshape(B, S, D)
```

```python
import functools
import math

import jax
import jax.numpy as jnp
from jax import lax
from jax.experimental import pallas as pl
from jax.experimental.pallas import tpu as pltpu

F32 = jnp.float32
BF16 = jnp.bfloat16

EPS = 1e-6
LANES = 128
SUBLANES = 8
VMEM_LIMIT = 56 * 1024 * 1024

MLA_HEADS = 8
QK_NOPE = 64
QK_ROPE = 32
V_HEAD = 64
Q_RANK = 384
KV_RANK = 256
CONV_DIM = 512
GROUP_DIM = 64
ROPE_THETA = 10000.0
X_HEADS = 4
PEER_HEADS = 8
PEER_KEYS = 128
PEER_TOPK = 16

TS_IN = 512
ROW_SPLIT = 2
TQ = 1024
TS_ROUTE = 1024
TT_FFN = 512
EB_FFN = 2048
SUB_FFN = 1024
GATE_UNIT_SPLIT = (8, 16, 40, 0)
NT_DIMS = (((1,), (1,)), ((), ()))


def _rms(x, g):
    return x * lax.rsqrt(jnp.mean(x * x, axis=-1, keepdims=True) + EPS) * g


def _split_bf16(x):
    hi = x.astype(BF16)
    lo = (x - hi.astype(F32)).astype(BF16)
    return hi, lo


def _gelu_tanh(x):
    c0 = math.sqrt(2.0 / math.pi)
    half_x = 0.5 * x
    return half_x + half_x * jnp.tanh(x * (c0 + (c0 * 0.044715) * (x * x)))


def _dot(a, b):
    return jnp.dot(a, b, preferred_element_type=F32)


def _dot_nt(a, b):
    return lax.dot_general(a, b, NT_DIMS, preferred_element_type=F32)


_C_CQ = 0
_C_CKV = _C_CQ + Q_RANK
_C_KR = _C_CKV + KV_RANK
_C_KRR = _C_KR + LANES
_C_GB = _C_KRR + LANES
_C_GC = _C_GB + CONV_DIM
_C_HX = _C_GC + CONV_DIM
_C_END = _C_HX + CONV_DIM


def _mixer_in_kernel(x_ref, gmix_ref, w1_ref, gq_ref, wq_ref, wqr_ref, gkv_ref, wk_ref, wv_ref, vone_ref,
                     eplace_ref, cos_ref, sin_ref,
                     q_out, k_out, v_out, z_out, gb_out):
    xn = _rms(x_ref[...], gmix_ref[...]).astype(BF16)
    proj = _dot(xn, w1_ref[...])
    cq = proj[:, _C_CQ:_C_CKV]
    ckv = proj[:, _C_CKV:_C_KR]
    kr = proj[:, _C_KR:_C_KRR]
    krr = proj[:, _C_KRR:_C_GB]
    gb_out[...] = proj[:, _C_GB:_C_GC]
    z_out[...] = proj[:, _C_GC:_C_HX] * proj[:, _C_HX:_C_END]

    cqn = _rms(cq, gq_ref[...]).astype(BF16)
    q_raw = _dot(cqn, wq_ref[...])
    q_rot = _dot(cqn, wqr_ref[...])
    cos_t = cos_ref[...]
    sin_t = sin_ref[...]
    q_scale = math.log2(math.e) / math.sqrt(QK_NOPE + QK_ROPE)
    for h in range(MLA_HEADS):
        sl = slice(h * LANES, (h + 1) * LANES)
        q_out[:, sl] = ((q_raw[:, sl] * cos_t + q_rot[:, sl] * sin_t) * q_scale).astype(BF16)

    ckvn = _rms(ckv, gkv_ref[...]).astype(BF16)
    kr_roped = (kr * cos_t + krr * sin_t).astype(BF16)
    k_out[...] = (_dot(ckvn, wk_ref[...]) + _dot(kr_roped, eplace_ref[...])).astype(BF16)
    v_out[...] = (_dot(ckvn, wv_ref[...]) + vone_ref[...]).astype(BF16)


def _mixer_in(x2, gmix, w1, gq, wq, wqr, gkv, wk, wv, vone, eplace, cos_t, sin_t):
    T, D = x2.shape
    ts = min(TS_IN, T)
    row = lambda i: (i, 0)
    fixed = lambda i: (0, 0)
    full = lambda a: pl.BlockSpec(a.shape, fixed)
    return pl.pallas_call(
        _mixer_in_kernel,
        grid=(T // ts,),
        in_specs=[pl.BlockSpec((ts, D), row), full(gmix), full(w1), full(gq), full(wq), full(wqr),
                  full(gkv), full(wk), full(wv), full(vone), full(eplace),
                  pl.BlockSpec((ts, LANES), row), pl.BlockSpec((ts, LANES), row)],
        out_specs=[pl.BlockSpec((ts, MLA_HEADS * LANES), row), pl.BlockSpec((ts, MLA_HEADS * LANES), row),
                   pl.BlockSpec((ts, MLA_HEADS * LANES), row), pl.BlockSpec((ts, CONV_DIM), row),
                   pl.BlockSpec((ts, CONV_DIM), row)],
        out_shape=[jax.ShapeDtypeStruct((T, MLA_HEADS * LANES), BF16),
                   jax.ShapeDtypeStruct((T, MLA_HEADS * LANES), BF16),
                   jax.ShapeDtypeStruct((T, MLA_HEADS * LANES), BF16),
                   jax.ShapeDtypeStruct((T, CONV_DIM), F32),
                   jax.ShapeDtypeStruct((T, CONV_DIM), F32)],
        compiler_params=pltpu.CompilerParams(dimension_semantics=("arbitrary",), vmem_limit_bytes=VMEM_LIMIT),
        name="mixer_in",
    )(x2, gmix, w1, gq, wq, wqr, gkv, wk, wv, vone, eplace, cos_t, sin_t)


def _ones_lane(head):
    return V_HEAD if head % 2 == 0 else 0


def _mla_attn_kernel(q_ref, k_ref, v_ref, o_ref, *, tq):
    seq = q_ref.shape[0]
    causal = (lax.broadcasted_iota(jnp.int32, (tq, tq), 1) <= lax.broadcasted_iota(jnp.int32, (tq, tq), 0))
    lane = lax.broadcasted_iota(jnp.int32, (tq, LANES), 1)
    for qi in range(seq // tq):
        rows = slice(qi * tq, (qi + 1) * tq)
        keys = slice(0, (qi + 1) * tq)
        outs = []
        for hh in range(2):
            hl = slice(hh * LANES, (hh + 1) * LANES)
            s = _dot_nt(q_ref[rows, hl], k_ref[keys, hl])
            s_diag = jnp.where(causal, s[:, qi * tq:], -jnp.inf)
            s = s_diag if qi == 0 else jnp.concatenate([s[:, :qi * tq], s_diag], axis=1)
            m = jnp.max(s, axis=-1, keepdims=True)
            acc = _dot(jnp.exp2(s - m).astype(BF16), v_ref[keys, hl])
            one = _ones_lane(hh)
            outs.append(acc * (1.0 / acc[:, one:one + 1]))
        o_ref[rows, :] = jnp.where(lane < V_HEAD, outs[0], outs[1])


def _mla_attn(q3, k3, v3):
    B, S, _ = q3.shape
    tq = min(TQ, S)
    pair = lambda b, g: (b, 0, g)
    return pl.pallas_call(
        functools.partial(_mla_attn_kernel, tq=tq),
        grid=(B, MLA_HEADS // 2),
        in_specs=[pl.BlockSpec((None, S, 2 * LANES), pair), pl.BlockSpec((None, S, 2 * LANES), pair),
                  pl.BlockSpec((None, S, 2 * LANES), pair)],
        out_specs=pl.BlockSpec((None, S, 2 * V_HEAD), pair),
        out_shape=jax.ShapeDtypeStruct((B, S, MLA_HEADS * V_HEAD), F32),
        compiler_params=pltpu.CompilerParams(dimension_semantics=("arbitrary", "arbitrary"),
                                             vmem_limit_bytes=VMEM_LIMIT),
        name="mla_attn",
    )(q3, k3, v3)


def _mixer_out_kernel(o_ref, z_ref, zh_ref, gb_ref, x_ref, cw_ref, gout_ref, gsum_ref, gexp_ref, wo_ref,
                      h_out, *, tiles_per_seq):
    i = pl.program_id(0)
    ts = z_ref.shape[0]
    z = z_ref[...]
    halo = jnp.where(i % tiles_per_seq == 0, 0.0, zh_ref[...])
    row = lax.broadcasted_iota(jnp.int32, z.shape, 0)
    z1 = jnp.where(row == 0, halo[7:8, :], pltpu.roll(z, 1, axis=0))
    z2 = jnp.where(row == 0, halo[6:7, :], jnp.where(row == 1, halo[7:8, :], pltpu.roll(z, 2, axis=0)))
    cw = cw_ref[...]
    y_conv = gb_ref[...] * (cw[0:1, :] * z2 + cw[1:2, :] * z1 + cw[2:3, :] * z)
    y_all = jnp.concatenate([o_ref[...], y_conv], axis=-1)
    for part in range(ROW_SPLIT):
        rows = slice(part * ts // ROW_SPLIT, (part + 1) * ts // ROW_SPLIT)
        y = y_all[rows]
        sq_hi, sq_lo = _split_bf16(y * y)
        gs = _dot(sq_hi, gsum_ref[...]) + _dot(sq_lo, gsum_ref[...])
        r = lax.rsqrt(gs * (1.0 / GROUP_DIM) + EPS)
        r_hi, r_lo = _split_bf16(r)
        r_full = _dot(r_hi, gexp_ref[...]) + _dot(r_lo, gexp_ref[...])
        yn = (y * r_full * gout_ref[...]).astype(BF16)
        h_out[rows, :] = x_ref[rows, :] + _dot(yn, wo_ref[...])


def _mixer_out(o2, z, gb, x2, conv_w, gout, gsum, gexp, wo, seq):
    T, D = x2.shape
    ts = min(TS_IN, seq)
    row = lambda i: (i, 0)
    fixed = lambda i: (0, 0)
    full = lambda a: pl.BlockSpec(a.shape, fixed)
    halo_blocks = ts // SUBLANES
    return pl.pallas_call(
        functools.partial(_mixer_out_kernel, tiles_per_seq=seq // ts),
        grid=(T // ts,),
        in_specs=[pl.BlockSpec((ts, MLA_HEADS * V_HEAD), row), pl.BlockSpec((ts, CONV_DIM), row),
                  pl.BlockSpec((SUBLANES, CONV_DIM), lambda i: (jnp.maximum(i * halo_blocks - 1, 0), 0)),
                  pl.BlockSpec((ts, CONV_DIM), row), pl.BlockSpec((ts, D), row),
                  full(conv_w), full(gout), full(gsum), full(gexp), full(wo)],
        out_specs=pl.BlockSpec((ts, D), row),
        out_shape=jax.ShapeDtypeStruct((T, D), F32),
        compiler_params=pltpu.CompilerParams(dimension_semantics=("arbitrary",), vmem_limit_bytes=VMEM_LIMIT),
        name="mixer_out",
    )(o2, z, z, gb, x2, conv_w, gout, gsum, gexp, wo)


def _mem_kv_kernel(mem_ref, g_ref, w_ref, k_out, v_out):
    d = mem_ref.shape[-1]
    mn = _rms(mem_ref[...], g_ref[...]).astype(BF16)
    kv = _dot(mn, w_ref[...])
    k_out[...] = kv[:, :d].astype(BF16)
    v_out[...] = kv[:, d:].astype(BF16)


def _mem_kv(mem, g, w):
    B, M, D = mem.shape
    return pl.pallas_call(
        _mem_kv_kernel,
        grid=(B,),
        in_specs=[pl.BlockSpec((None, M, D), lambda b: (b, 0, 0)), pl.BlockSpec(g.shape, lambda b: (0, 0)),
                  pl.BlockSpec(w.shape, lambda b: (0, 0))],
        out_specs=[pl.BlockSpec((None, M, D), lambda b: (b, 0, 0)), pl.BlockSpec((None, M, D), lambda b: (b, 0, 0))],
        out_shape=[jax.ShapeDtypeStruct((B, M, D), BF16), jax.ShapeDtypeStruct((B, M, D), BF16)],
        compiler_params=pltpu.CompilerParams(dimension_semantics=("arbitrary",), vmem_limit_bytes=VMEM_LIMIT),
        name="mem_kv",
    )(mem, g, w)


def _xattn_kernel(h_ref, g_ref, wq_ref, k_ref, v_ref, wo_ref, h_out):
    h = h_ref[...]
    d = h.shape[-1]
    hd = d // X_HEADS
    hn = _rms(h, g_ref[...]).astype(BF16)
    q = _dot(hn, wq_ref[...]).astype(BF16)
    outs = []
    for hh in range(X_HEADS):
        sl = slice(hh * hd, (hh + 1) * hd)
        s = _dot_nt(q[:, sl], k_ref[:, sl]) * (1.0 / math.sqrt(hd))
        m = jnp.max(s, axis=-1, keepdims=True)
        p = jnp.exp(s - m)
        p = p * (1.0 / jnp.sum(p, axis=-1, keepdims=True))
        outs.append(_dot(p.astype(BF16), v_ref[:, sl]))
    o = jnp.concatenate(outs, axis=-1).astype(BF16)
    h_out[...] = h + _dot(o, wo_ref[...])


def _xattn(h3, g, wq, kx, vx, wo):
    B, S, D = h3.shape
    M = kx.shape[1]
    ts = min(TS_IN, S)
    fixed = lambda b, i: (0, 0)
    return pl.pallas_call(
        _xattn_kernel,
        grid=(B, S // ts),
        in_specs=[pl.BlockSpec((None, ts, D), lambda b, i: (b, i, 0)), pl.BlockSpec(g.shape, fixed),
                  pl.BlockSpec(wq.shape, fixed), pl.BlockSpec((None, M, D), lambda b, i: (b, 0, 0)),
                  pl.BlockSpec((None, M, D), lambda b, i: (b, 0, 0)), pl.BlockSpec(wo.shape, fixed)],
        out_specs=pl.BlockSpec((None, ts, D), lambda b, i: (b, i, 0)),
        out_shape=jax.ShapeDtypeStruct((B, S, D), F32),
        compiler_params=pltpu.CompilerParams(dimension_semantics=("arbitrary", "arbitrary"),
                                             vmem_limit_bytes=VMEM_LIMIT),
        name="xattn",
    )(h3, g, wq, kx, vx, wo)


def _batcher_pairs(n):
    pairs = []
    p = 1
    while p < n:
        k = p
        while k >= 1:
            for j in range(k % p, n - k, 2 * k):
                for i in range(min(k, n - j - k)):
                    if (i + j) // (2 * p) == (i + j + k) // (2 * p):
                        pairs.append((i + j, i + j + k))
            k //= 2
        p *= 2
    return pairs


_SORT16 = _batcher_pairs(PEER_TOPK)
_ROW_LEN = [PEER_TOPK // (a + 1) for a in range(PEER_TOPK)]


def _sort_desc(v):
    v = list(v)
    for i, j in _SORT16:
        hi = jnp.maximum(v[i], v[j])
        lo = jnp.minimum(v[i], v[j])
        v[i], v[j] = hi, lo
    return v


def _bitonic_desc(v):
    v = list(v)
    n = len(v)
    d = n // 2
    while d >= 1:
        for k in range(n):
            if k & d == 0:
                hi = jnp.maximum(v[k], v[k + d])
                lo = jnp.minimum(v[k], v[k + d])
                v[k], v[k + d] = hi, lo
        d //= 2
    return v


def _merge_top(cur, other):
    n = len(cur)
    c = list(cur)
    for r, val in enumerate(other):
        c[n - 1 - r] = jnp.maximum(c[n - 1 - r], val)
    return _bitonic_desc(c)


def _top16_sorted(s):
    v = _sort_desc([s[k] for k in range(PEER_TOPK)])
    for shift in (4, 2, 1):
        other = [pltpu.roll(v[k], shift, axis=0) for k in range(PEER_TOPK)]
        v = _merge_top(v, other)
    return v


def _peer_route_kernel(h_ref, g_ref, wq_ref, key_ref,
                       xn_out, r2_out, p_out, n_out, c_out,
                       st_ref, top_ref, res_ref):
    ts = h_ref.shape[0]
    n_chunk = ts // LANES
    hn = _rms(h_ref[...], g_ref[...]).astype(BF16)
    xn_out[...] = hn
    q = _dot(hn, wq_ref[...])
    for h in range(PEER_HEADS):
        st = _dot_nt(key_ref[h], q[:, h * LANES:(h + 1) * LANES].astype(BF16))
        for c in range(n_chunk):
            st_ref[c, h] = st[:, c * LANES:(c + 1) * LANES]


    def chunk_body(c, _):
        def sort_body(h, _):
            for half in range(2):
                s = st_ref[c, h, pl.ds(half * PEER_KEYS, PEER_KEYS), :].reshape(PEER_TOPK, SUBLANES, LANES)
                v = _top16_sorted(s)
                for a in range(PEER_TOPK):
                    top_ref[half, a, pl.ds(h, 1), :] = v[a][0:1, :]
            return 0

        lax.fori_loop(0, PEER_HEADS, sort_body, 0)

        v1 = [top_ref[0, a] for a in range(PEER_TOPK)]
        v2 = [top_ref[1, b] for b in range(PEER_TOPK)]
        sums = [[v1[a] + v2[b] for b in range(_ROW_LEN[a])] for a in range(PEER_TOPK)]
        cur = sums[0]
        a = 1
        while _ROW_LEN[a] > 1:
            cur = _merge_top(cur, sums[a])
            a += 1
        cur = _merge_top(cur, [sums[r][0] for r in range(a, PEER_TOPK)])
        tau = cur[PEER_TOPK - 1]
        top_sum = sums[0][0]
        z = jnp.zeros_like(tau)
        for a in range(PEER_TOPK):
            cnt = jnp.zeros_like(tau)
            for b in range(_ROW_LEN[a]):
                sel = sums[a][b] >= tau
                cnt = cnt + jnp.where(sel, 1.0, 0.0)
                z = z + jnp.where(sel, jnp.exp(sums[a][b] - top_sum), 0.0)
            res_ref[a] = cnt
        res_ref[PEER_TOPK] = 1.0 / z

        def expand_body(h, _):
            s1 = st_ref[c, h, pl.ds(0, PEER_KEYS), :].reshape(PEER_TOPK, SUBLANES, LANES)
            s2 = st_ref[c, h, pl.ds(PEER_KEYS, PEER_KEYS), :].reshape(PEER_TOPK, SUBLANES, LANES)
            n = jnp.zeros(s1.shape, F32)
            r2 = jnp.full(s2.shape, float(PEER_TOPK), F32)
            for a in range(PEER_TOPK - 1, -1, -1):
                v1a = top_ref[0, a, pl.ds(h, 1), :]
                v2a = top_ref[1, a, pl.ds(h, 1), :]
                n = jnp.where(s1 == v1a, res_ref[a, pl.ds(h, 1), :], n)
                r2 = jnp.where(s2 == v2a, float(a), r2)
            m1 = top_ref[0, 0, pl.ds(h, 1), :]
            m2 = top_ref[1, 0, pl.ds(h, 1), :]
            inv_z = res_ref[PEER_TOPK, pl.ds(h, 1), :]
            n_out[c, h] = n.reshape(PEER_KEYS, LANES)
            c_out[c, h] = (jnp.exp(s1 - m1) * inv_z).reshape(PEER_KEYS, LANES)
            r2_out[c, h] = r2.reshape(PEER_KEYS, LANES).astype(BF16)
            p_out[c, h] = jnp.exp(s2 - m2).reshape(PEER_KEYS, LANES).astype(BF16)
            return 0

        lax.fori_loop(0, PEER_HEADS, expand_body, 0)
        return 0

    lax.fori_loop(0, n_chunk, chunk_body, 0)


def _peer_route(h2, g, wq, keys):
    T, D = h2.shape
    ts = min(TS_ROUTE, T)
    nc = ts // LANES
    aux_spec = pl.BlockSpec((nc, PEER_HEADS, PEER_KEYS, LANES), lambda i: (i, 0, 0, 0))
    aux_shape = jax.ShapeDtypeStruct((T // LANES, PEER_HEADS, PEER_KEYS, LANES), F32)
    aux_shape_bf = jax.ShapeDtypeStruct((T // LANES, PEER_HEADS, PEER_KEYS, LANES), BF16)
    fixed2 = lambda i: (0, 0)
    fixed3 = lambda i: (0, 0, 0)
    return pl.pallas_call(
        _peer_route_kernel,
        grid=(T // ts,),
        in_specs=[pl.BlockSpec((ts, D), lambda i: (i, 0)), pl.BlockSpec(g.shape, fixed2),
                  pl.BlockSpec(wq.shape, fixed2), pl.BlockSpec(keys.shape, fixed3)],
        out_specs=[pl.BlockSpec((ts, D), lambda i: (i, 0)), aux_spec, aux_spec, aux_spec, aux_spec],
        out_shape=[jax.ShapeDtypeStruct((T, D), BF16), aux_shape_bf, aux_shape_bf, aux_shape, aux_shape],
        scratch_shapes=[pltpu.VMEM((nc, PEER_HEADS, 2 * PEER_KEYS, LANES), F32),
                        pltpu.VMEM((2, PEER_TOPK, SUBLANES, LANES), F32),
                        pltpu.VMEM((PEER_TOPK + 1, SUBLANES, LANES), F32)],
        compiler_params=pltpu.CompilerParams(dimension_semantics=("arbitrary",), vmem_limit_bytes=VMEM_LIMIT),
        name="peer_route",
    )(h2, g, wq, keys)


def _pack_experts_kernel(u_ref, v_ref, u_out, vt_out):
    u_out[...] = pltpu.bitcast(u_ref[...].astype(BF16), jnp.uint32)
    vt_out[...] = pltpu.bitcast(v_ref[...].T.astype(BF16), jnp.uint32)


def _pack_experts(u, v):
    E, D = u.shape
    eb = EB_FFN
    return pl.pallas_call(
        _pack_experts_kernel,
        grid=(E // eb,),
        in_specs=[pl.BlockSpec((eb, D), lambda e: (e, 0)), pl.BlockSpec((eb, D), lambda e: (e, 0))],
        out_specs=[pl.BlockSpec((eb // 2, D), lambda e: (e, 0)), pl.BlockSpec((D // 2, eb), lambda e: (0, e))],
        out_shape=[jax.ShapeDtypeStruct((E // 2, D), jnp.uint32), jax.ShapeDtypeStruct((D // 2, E), jnp.uint32)],
        compiler_params=pltpu.CompilerParams(dimension_semantics=("arbitrary",), vmem_limit_bytes=VMEM_LIMIT),
        name="pack_experts",
    )(u, v)


def _peer_gate_unit(tc, ii, at_ref, ht_ref, r2s_ref, ps_ref, n_ref, c_ref):
    pack = 2 * SUBLANES
    n_jv = PEER_KEYS // pack
    zero = jnp.zeros((pack, LANES), BF16)
    lanes = slice(tc * LANES, (tc + 1) * LANES)
    g = [None] * n_jv
    for h in range(PEER_HEADS):
        n_b = jnp.broadcast_to(n_ref[tc, h, ii:ii + 1, :], (pack, LANES)).astype(BF16)
        c_b = jnp.broadcast_to(c_ref[tc, h, ii:ii + 1, :], (pack, LANES)).astype(BF16)
        for jv in range(n_jv):
            js = slice(jv * pack, (jv + 1) * pack)
            term = jnp.where(r2s_ref[tc, h, js, :] < n_b, ps_ref[tc, h, js, :], zero) * c_b
            g[jv] = term if g[jv] is None else g[jv] + term
    for jv in range(n_jv):
        rows = slice(ii * PEER_KEYS + jv * pack, ii * PEER_KEYS + (jv + 1) * pack)
        ht_ref[rows, lanes] = _gelu_tanh(at_ref[rows, lanes]).astype(BF16) * g[jv]


def _peer_ffn_kernel(xn_ref, u_ref, vt_ref, r2_ref, p_ref, n_ref, c_ref, h_ref, gfin_ref,
                     out_ref, acc_ref, at0_ref, at1_ref, ht0_ref, ht1_ref, r2s_ref, ps_ref, xs_ref,
                     *, n_e, n_blocks, final_norm):
    g = pl.program_id(0)
    tt = xn_ref.shape[0]
    e_score = g % n_e
    e_gate = jnp.maximum(g - 1, 0) % n_e
    e_down = jnp.maximum(g - 2, 0) % n_e

    @pl.when(g == 0)
    def _():
        at1_ref[...] = jnp.zeros_like(at1_ref)
        ht0_ref[...] = jnp.zeros_like(ht0_ref)
        ht1_ref[...] = jnp.zeros_like(ht1_ref)
        acc_ref[...] = jnp.zeros_like(acc_ref)

    @pl.when((g < n_blocks) & (e_score == 0))
    def _():
        xs_ref[...] = xn_ref[...]

    @pl.when((g <= n_blocks) & (e_gate == 0))
    def _():
        for tc in range(tt // LANES):
            for h in range(PEER_HEADS):
                r2s_ref[tc, h] = r2_ref[tc, h]
                ps_ref[tc, h] = p_ref[tc, h]

    @pl.when((g >= 2) & (e_down == 0))
    def _():
        acc_ref[...] = jnp.zeros_like(acc_ref)

    def stages(at_w, at_r, ht_w, ht_r):
        u_blk = pltpu.bitcast(u_ref[...], BF16)
        vt_blk = pltpu.bitcast(vt_ref[...], BF16)
        eb = u_blk.shape[0]
        subs = [slice(sb * SUB_FFN, (sb + 1) * SUB_FFN) for sb in range(eb // SUB_FFN)]

        def score(ex):
            at_w[ex, :] = _dot_nt(u_blk[ex], xs_ref[...])

        def down():
            acc_ref[...] += _dot(vt_blk, ht_r[...])

        chunks = [functools.partial(score, ex) for ex in subs] + [down]
        units = [(tc, ii) for tc in range(tt // LANES) for ii in range(eb // PEER_KEYS)]
        split = GATE_UNIT_SPLIT
        assert len(split) == len(chunks) + 1 and sum(split) == len(units)
        bounds = [sum(split[:k]) for k in range(len(split) + 1)]

        def gate_units(k):
            for tc, ii in units[bounds[k]:bounds[k + 1]]:
                _peer_gate_unit(tc, ii, at_r, ht_w, r2s_ref, ps_ref, n_ref, c_ref)

        gate_units(0)
        for k, chunk in enumerate(chunks):
            chunk()
            gate_units(k + 1)

    @pl.when(g % 2 == 0)
    def _():
        stages(at0_ref, at1_ref, ht1_ref, ht0_ref)

    @pl.when(g % 2 == 1)
    def _():
        stages(at1_ref, at0_ref, ht0_ref, ht1_ref)

    @pl.when((g >= 2) & (e_down == n_e - 1))
    def _():
        res = h_ref[...] + acc_ref[...].T
        out_ref[...] = _rms(res, gfin_ref[...]) if final_norm else res


def _peer_ffn(xn, u_pack, vt_pack, r2, p, n, coef, h2, gfin, final_norm):
    T, D = h2.shape
    E = vt_pack.shape[1]
    tt = min(TT_FFN, T)
    nc = tt // LANES
    eb = EB_FFN
    n_i = eb // PEER_KEYS
    n_e = E // eb
    n_blocks = (T // tt) * n_e

    def block(lag):
        def split(g):
            b = jnp.clip(g - lag, 0, n_blocks - 1)
            return b // n_e, b % n_e
        return split

    score, gate, down = block(0), block(1), block(2)
    aux_shape = (nc, PEER_HEADS, PEER_KEYS, LANES)
    row_shape = (nc, PEER_HEADS, n_i, LANES)
    return pl.pallas_call(
        functools.partial(_peer_ffn_kernel, n_e=n_e, n_blocks=n_blocks, final_norm=final_norm),
        grid=(n_blocks + 2,),
        in_specs=[pl.BlockSpec((tt, D), lambda g: (score(g)[0], 0)),
                  pl.BlockSpec((eb // 2, D), lambda g: (score(g)[1], 0)),
                  pl.BlockSpec((D // 2, eb), lambda g: (0, down(g)[1])),
                  pl.BlockSpec(aux_shape, lambda g: (gate(g)[0], 0, 0, 0)),
                  pl.BlockSpec(aux_shape, lambda g: (gate(g)[0], 0, 0, 0)),
                  pl.BlockSpec(row_shape, lambda g: (gate(g)[0], 0, gate(g)[1], 0)),
                  pl.BlockSpec(row_shape, lambda g: (gate(g)[0], 0, gate(g)[1], 0)),
                  pl.BlockSpec((tt, D), lambda g: (down(g)[0], 0)),
                  pl.BlockSpec(gfin.shape, lambda g: (0, 0))],
        out_specs=pl.BlockSpec((tt, D), lambda g: (down(g)[0], 0)),
        out_shape=jax.ShapeDtypeStruct((T, D), F32),
        scratch_shapes=[pltpu.VMEM((D, tt), F32),
                        pltpu.VMEM((eb, tt), F32), pltpu.VMEM((eb, tt), F32),
                        pltpu.VMEM((eb, tt), BF16), pltpu.VMEM((eb, tt), BF16),
                        pltpu.VMEM(aux_shape, BF16), pltpu.VMEM(aux_shape, BF16),
                        pltpu.VMEM((tt, D), BF16)],
        compiler_params=pltpu.CompilerParams(dimension_semantics=("arbitrary",), vmem_limit_bytes=VMEM_LIMIT),
        name="peer_ffn",
    )(xn, u_pack, vt_pack, r2, p, n, coef, h2, gfin)


def _head_blocks(w, n_heads, width, pieces):
    w3 = w.reshape(w.shape[0], n_heads, width)
    out = jnp.zeros((w.shape[0], n_heads, LANES), w.dtype)
    for s0, s1, d0 in pieces:
        out = out.at[:, :, d0:d0 + (s1 - s0)].set(w3[:, :, s0:s1])
    return out.reshape(w.shape[0], n_heads * LANES)


def kernel(x, mem, positions, g_mix, w_in, g_q, w_uq, g_kv, w_ukv, conv_w, g_out, w_o, g_x, g_mem, w_xq,
           w_xkv, w_xo, g_ffn, w_pq, sub_keys, u_experts, v_experts, g_final):
    B, S, D = x.shape
    T = B * S
    depth = g_mix.shape[0]
    half = QK_ROPE // 2
    assert S % min(TS_IN, S) == 0 and S % min(TQ, S) == 0 and D == MLA_HEADS * LANES
    assert T % min(TS_ROUTE, T) == 0 and T % min(TT_FFN, T) == 0 and u_experts.shape[1] % EB_FFN == 0

    inv = ROPE_THETA ** (-jnp.arange(0, QK_ROPE, 2, dtype=F32) / QK_ROPE)
    ang = positions.astype(F32)[..., None] * inv
    cos = jnp.cos(ang).astype(x.dtype).reshape(T, half)
    sin = jnp.sin(ang).astype(x.dtype).reshape(T, half)
    ones = jnp.ones((T, QK_NOPE), F32)
    zeros_n = jnp.zeros((T, QK_NOPE), F32)
    pad_q = jnp.zeros((T, LANES - QK_NOPE - QK_ROPE), F32)
    cos_t = jnp.concatenate([ones, cos, cos, pad_q], axis=1)
    sin_t = jnp.concatenate([zeros_n, -sin, sin, pad_q], axis=1)

    lane = jnp.arange(LANES)
    col = jnp.arange(MLA_HEADS * LANES)
    rope_lane = (lane >= QK_NOPE) & (lane < QK_NOPE + QK_ROPE)
    eplace = ((col[None, :] % LANES == lane[:, None]) & rope_lane[:, None]).astype(BF16)
    mix_col = jnp.arange(D)
    gsum = (mix_col[:, None] // GROUP_DIM == lane[None, :]).astype(BF16)
    gexp = (lane[:, None] == mix_col[None, :] // GROUP_DIM).astype(BF16)

    h = x.reshape(T, D)
    for l in range(depth):
        o1 = Q_RANK
        o2 = o1 + KV_RANK
        o3 = o2 + QK_ROPE
        o4 = o3 + CONV_DIM
        o5 = o4 + CONV_DIM
        wl = w_in[l]
        w_kr = wl[:, o2:o3]
        w_krr = jnp.concatenate([w_kr[:, half:], w_kr[:, :half]], axis=1)
        pad_lo = jnp.zeros((D, QK_NOPE), wl.dtype)
        pad_hi = jnp.zeros((D, LANES - QK_NOPE - QK_ROPE), wl.dtype)
        w1 = jnp.concatenate([wl[:, :o1], wl[:, o1:o2], pad_lo, w_kr, pad_hi, pad_lo, w_krr, pad_hi,
                              wl[:, o3:o4], wl[:, o4:o5], wl[:, o5:]], axis=1).astype(BF16)
        qw = QK_NOPE + QK_ROPE
        wq = _head_blocks(w_uq[l], MLA_HEADS, qw, [(0, qw, 0)]).astype(BF16)
        wqr = _head_blocks(w_uq[l], MLA_HEADS, qw,
                           [(QK_NOPE + half, qw, QK_NOPE), (QK_NOPE, QK_NOPE + half, QK_NOPE + half)]).astype(BF16)
        kvw = QK_NOPE + V_HEAD
        wk = _head_blocks(w_ukv[l], MLA_HEADS, kvw, [(0, QK_NOPE, 0)]).astype(BF16)
        v_cols = w_ukv[l].reshape(KV_RANK, MLA_HEADS, kvw)[:, :, QK_NOPE:]
        v_pad = jnp.zeros_like(v_cols)
        odd_head = (jnp.arange(MLA_HEADS) % 2 == 1)[None, :, None]
        wv = jnp.where(odd_head, jnp.concatenate([v_pad, v_cols], axis=-1),
                       jnp.concatenate([v_cols, v_pad], axis=-1)).reshape(KV_RANK, MLA_HEADS * LANES).astype(BF16)
        vone = jnp.stack([(lane == _ones_lane(hd)).astype(F32) for hd in range(MLA_HEADS)]).reshape(1, -1)

        q, k, v, z, gb = _mixer_in(h, g_mix[l][None, :], w1, g_q[l][None, :], wq, wqr, g_kv[l][None, :], wk, wv,
                                   vone, eplace, cos_t, sin_t)
        o = _mla_attn(q.reshape(B, S, -1), k.reshape(B, S, -1), v.reshape(B, S, -1))
        h = _mixer_out(o.reshape(T, -1), z, gb, h, conv_w[l], g_out[l][None, :], gsum, gexp,
                       w_o[l].astype(BF16), S)

        kx, vx = _mem_kv(mem, g_mem[l][None, :], w_xkv[l].astype(BF16))
        h = _xattn(h.reshape(B, S, D), g_x[l][None, :], w_xq[l].astype(BF16), kx, vx,
                   w_xo[l].astype(BF16)).reshape(T, D)

        sk = sub_keys[l]
        zk = jnp.zeros_like(sk[:, 0])
        keys_bd = jnp.concatenate([jnp.concatenate([sk[:, 0], zk], axis=-1),
                                   jnp.concatenate([zk, sk[:, 1]], axis=-1)], axis=1)
        xn, r2, p, n, coef = _peer_route(h, g_ffn[l][None, :], w_pq[l].astype(BF16), keys_bd.astype(BF16))
        u_pack, vt_pack = _pack_experts(u_experts[l], v_experts[l])
        h = _peer_ffn(xn, u_pack, vt_pack, r2, p, n, coef, h,
                      g_final[None, :], final_norm=(l == depth - 1))
    return h.reshape(B, S, D)
```

```python
import functools
import math

import jax
import jax.numpy as jnp
from jax import lax
from jax.experimental import pallas as pl
from jax.experimental.pallas import tpu as pltpu

F32 = jnp.float32
BF16 = jnp.bfloat16

EPS = 1e-6
LANES = 128
SUBLANES = 8
VMEM_LIMIT = 56 * 1024 * 1024

MLA_HEADS = 8
QK_NOPE = 64
QK_ROPE = 32
V_HEAD = 64
Q_RANK = 384
KV_RANK = 256
CONV_DIM = 512
GROUP_DIM = 64
ROPE_THETA = 10000.0
X_HEADS = 4
PEER_HEADS = 8
PEER_KEYS = 128
PEER_TOPK = 16

TS_IN = 1024
TS_X = 1024
ROW_SPLIT = 4
TQ = 512
TS_ROUTE = 1024
TT_FFN = 512
EB_FFN = 2048
SUB_FFN = 1024
GATE_UNIT_SPLIT = (8, 16, 40, 0)
NT_DIMS = (((1,), (1,)), ((), ()))


def _rms(x, g):
    return x * lax.rsqrt(jnp.mean(x * x, axis=-1, keepdims=True) + EPS) * g


def _split_bf16(x):
    hi = x.astype(BF16)
    lo = (x - hi.astype(F32)).astype(BF16)
    return hi, lo


def _gelu_tanh(x):
    c0 = math.sqrt(2.0 / math.pi)
    half_x = 0.5 * x
    return half_x + half_x * jnp.tanh(x * (c0 + (c0 * 0.044715) * (x * x)))


def _dot(a, b):
    return jnp.dot(a, b, preferred_element_type=F32)


def _dot_nt(a, b):
    return lax.dot_general(a, b, NT_DIMS, preferred_element_type=F32)


_C_CQ = 0
_C_CKV = _C_CQ + Q_RANK
_C_KR = _C_CKV + KV_RANK
_C_KRR = _C_KR + LANES
_C_GB = _C_KRR + LANES
_C_GC = _C_GB + CONV_DIM
_C_HX = _C_GC + CONV_DIM
_C_END = _C_HX + CONV_DIM


def _mixer_in_kernel(x_ref, gmix_ref, w1_ref, gq_ref, wq_ref, wqr_ref, gkv_ref, wk_ref, wv_ref, vone_ref,
                     eplace_ref, cos_ref, sin_ref,
                     q_out, k_out, v_out, z_out, gb_out):
    xn = _rms(x_ref[...], gmix_ref[...]).astype(BF16)
    proj = _dot(xn, w1_ref[...])
    cq = proj[:, _C_CQ:_C_CKV]
    ckv = proj[:, _C_CKV:_C_KR]
    kr = proj[:, _C_KR:_C_KRR]
    krr = proj[:, _C_KRR:_C_GB]
    gb_out[...] = proj[:, _C_GB:_C_GC]
    z_out[...] = proj[:, _C_GC:_C_HX] * proj[:, _C_HX:_C_END]

    cqn = _rms(cq, gq_ref[...]).astype(BF16)
    q_raw = _dot(cqn, wq_ref[...])
    q_rot = _dot(cqn, wqr_ref[...])
    cos_t = cos_ref[...]
    sin_t = sin_ref[...]
    q_scale = math.log2(math.e) / math.sqrt(QK_NOPE + QK_ROPE)
    for h in range(MLA_HEADS):
        sl = slice(h * LANES, (h + 1) * LANES)
        q_out[:, sl] = ((q_raw[:, sl] * cos_t + q_rot[:, sl] * sin_t) * q_scale).astype(BF16)

    ckvn = _rms(ckv, gkv_ref[...]).astype(BF16)
    kr_roped = (kr * cos_t + krr * sin_t).astype(BF16)
    k_out[...] = (_dot(ckvn, wk_ref[...]) + _dot(kr_roped, eplace_ref[...])).astype(BF16)
    v_out[...] = (_dot(ckvn, wv_ref[...]) + vone_ref[...]).astype(BF16)


def _mixer_in(x2, gmix, w1, gq, wq, wqr, gkv, wk, wv, vone, eplace, cos_t, sin_t):
    T, D = x2.shape
    ts = min(TS_IN, T)
    row = lambda i: (i, 0)
    fixed = lambda i: (0, 0)
    full = lambda a: pl.BlockSpec(a.shape, fixed)
    return pl.pallas_call(
        _mixer_in_kernel,
        grid=(T // ts,),
        in_specs=[pl.BlockSpec((ts, D), row), full(gmix), full(w1), full(gq), full(wq), full(wqr),
                  full(gkv), full(wk), full(wv), full(vone), full(eplace),
                  pl.BlockSpec((ts, LANES), row), pl.BlockSpec((ts, LANES), row)],
        out_specs=[pl.BlockSpec((ts, MLA_HEADS * LANES), row), pl.BlockSpec((ts, MLA_HEADS * LANES), row),
                   pl.BlockSpec((ts, MLA_HEADS * LANES), row), pl.BlockSpec((ts, CONV_DIM), row),
                   pl.BlockSpec((ts, CONV_DIM), row)],
        out_shape=[jax.ShapeDtypeStruct((T, MLA_HEADS * LANES), BF16),
                   jax.ShapeDtypeStruct((T, MLA_HEADS * LANES), BF16),
                   jax.ShapeDtypeStruct((T, MLA_HEADS * LANES), BF16),
                   jax.ShapeDtypeStruct((T, CONV_DIM), F32),
                   jax.ShapeDtypeStruct((T, CONV_DIM), F32)],
        compiler_params=pltpu.CompilerParams(dimension_semantics=("arbitrary",), vmem_limit_bytes=VMEM_LIMIT),
        name="mixer_in",
    )(x2, gmix, w1, gq, wq, wqr, gkv, wk, wv, vone, eplace, cos_t, sin_t)


def _ones_lane(head):
    return V_HEAD if head % 2 == 0 else 0


def _mla_attn_kernel(q_ref, k_ref, v_ref, o_ref, *, tq):
    seq = q_ref.shape[0]
    causal = (lax.broadcasted_iota(jnp.int32, (tq, tq), 1) <= lax.broadcasted_iota(jnp.int32, (tq, tq), 0))
    lane = lax.broadcasted_iota(jnp.int32, (tq, LANES), 1)
    for qi in range(seq // tq):
        rows = slice(qi * tq, (qi + 1) * tq)
        keys = slice(0, (qi + 1) * tq)
        outs = []
        for hh in range(2):
            hl = slice(hh * LANES, (hh + 1) * LANES)
            s = _dot_nt(q_ref[rows, hl], k_ref[keys, hl])
            s_diag = jnp.where(causal, s[:, qi * tq:], -jnp.inf)
            s = s_diag if qi == 0 else jnp.concatenate([s[:, :qi * tq], s_diag], axis=1)
            m = jnp.max(s, axis=-1, keepdims=True)
            acc = _dot(jnp.exp2(s - m).astype(BF16), v_ref[keys, hl])
            one = _ones_lane(hh)
            outs.append(acc * (1.0 / acc[:, one:one + 1]))
        o_ref[rows, :] = jnp.where(lane < V_HEAD, outs[0], outs[1])


def _mla_attn(q3, k3, v3):
    B, S, _ = q3.shape
    tq = min(TQ, S)
    pair = lambda b, g: (b, 0, g)
    return pl.pallas_call(
        functools.partial(_mla_attn_kernel, tq=tq),
        grid=(B, MLA_HEADS // 2),
        in_specs=[pl.BlockSpec((None, S, 2 * LANES), pair), pl.BlockSpec((None, S, 2 * LANES), pair),
                  pl.BlockSpec((None, S, 2 * LANES), pair)],
        out_specs=pl.BlockSpec((None, S, 2 * V_HEAD), pair),
        out_shape=jax.ShapeDtypeStruct((B, S, MLA_HEADS * V_HEAD), F32),
        compiler_params=pltpu.CompilerParams(dimension_semantics=("arbitrary", "arbitrary"),
                                             vmem_limit_bytes=VMEM_LIMIT),
        name="mla_attn",
    )(q3, k3, v3)


def _mixer_out_kernel(o_ref, z_ref, zh_ref, gb_ref, x_ref, cw_ref, gout_ref, gsum_ref, gexp_ref, wo_ref,
                      h_out, *, tiles_per_seq):
    i = pl.program_id(0)
    ts = z_ref.shape[0]
    z = z_ref[...]
    halo = jnp.where(i % tiles_per_seq == 0, 0.0, zh_ref[...])
    row = lax.broadcasted_iota(jnp.int32, z.shape, 0)
    z1 = jnp.where(row == 0, halo[7:8, :], pltpu.roll(z, 1, axis=0))
    z2 = jnp.where(row == 0, halo[6:7, :], jnp.where(row == 1, halo[7:8, :], pltpu.roll(z, 2, axis=0)))
    cw = cw_ref[...]
    y_conv = gb_ref[...] * (cw[0:1, :] * z2 + cw[1:2, :] * z1 + cw[2:3, :] * z)
    y_all = jnp.concatenate([o_ref[...], y_conv], axis=-1)
    for part in range(ROW_SPLIT):
        rows = slice(part * ts // ROW_SPLIT, (part + 1) * ts // ROW_SPLIT)
        y = y_all[rows]
        sq_hi, sq_lo = _split_bf16(y * y)
        gs = _dot(sq_hi, gsum_ref[...]) + _dot(sq_lo, gsum_ref[...])
        r = lax.rsqrt(gs * (1.0 / GROUP_DIM) + EPS)
        r_hi, r_lo = _split_bf16(r)
        r_full = _dot(r_hi, gexp_ref[...]) + _dot(r_lo, gexp_ref[...])
        yn = (y * r_full * gout_ref[...]).astype(BF16)
        h_out[rows, :] = x_ref[rows, :] + _dot(yn, wo_ref[...])


def _mixer_out(o2, z, gb, x2, conv_w, gout, gsum, gexp, wo, seq):
    T, D = x2.shape
    ts = min(TS_IN, seq)
    row = lambda i: (i, 0)
    fixed = lambda i: (0, 0)
    full = lambda a: pl.BlockSpec(a.shape, fixed)
    halo_blocks = ts // SUBLANES
    return pl.pallas_call(
        functools.partial(_mixer_out_kernel, tiles_per_seq=seq // ts),
        grid=(T // ts,),
        in_specs=[pl.BlockSpec((ts, MLA_HEADS * V_HEAD), row), pl.BlockSpec((ts, CONV_DIM), row),
                  pl.BlockSpec((SUBLANES, CONV_DIM), lambda i: (jnp.maximum(i * halo_blocks - 1, 0), 0)),
                  pl.BlockSpec((ts, CONV_DIM), row), pl.BlockSpec((ts, D), row),
                  full(conv_w), full(gout), full(gsum), full(gexp), full(wo)],
        out_specs=pl.BlockSpec((ts, D), row),
        out_shape=jax.ShapeDtypeStruct((T, D), F32),
        compiler_params=pltpu.CompilerParams(dimension_semantics=("arbitrary",), vmem_limit_bytes=VMEM_LIMIT),
        name="mixer_out",
    )(o2, z, z, gb, x2, conv_w, gout, gsum, gexp, wo)


def _mem_kv_kernel(mem_ref, g_ref, w_ref, k_out, v_out):
    d = mem_ref.shape[-1]
    mn = _rms(mem_ref[...], g_ref[...]).astype(BF16)
    kv = _dot(mn, w_ref[...])
    k_out[...] = kv[:, :d].astype(BF16)
    v_out[...] = kv[:, d:].astype(BF16)


def _mem_kv(mem, g, w):
    B, M, D = mem.shape
    return pl.pallas_call(
        _mem_kv_kernel,
        grid=(B,),
        in_specs=[pl.BlockSpec((None, M, D), lambda b: (b, 0, 0)), pl.BlockSpec(g.shape, lambda b: (0, 0)),
                  pl.BlockSpec(w.shape, lambda b: (0, 0))],
        out_specs=[pl.BlockSpec((None, M, D), lambda b: (b, 0, 0)), pl.BlockSpec((None, M, D), lambda b: (b, 0, 0))],
        out_shape=[jax.ShapeDtypeStruct((B, M, D), BF16), jax.ShapeDtypeStruct((B, M, D), BF16)],
        compiler_params=pltpu.CompilerParams(dimension_semantics=("arbitrary",), vmem_limit_bytes=VMEM_LIMIT),
        name="mem_kv",
    )(mem, g, w)


def _xattn_kernel(h_ref, g_ref, wq_ref, k_ref, v_ref, wo_ref, h_out):
    h = h_ref[...]
    d = h.shape[-1]
    hd = d // X_HEADS
    hn = _rms(h, g_ref[...]).astype(BF16)
    q = _dot(hn, wq_ref[...]).astype(BF16)
    outs = []
    for hh in range(X_HEADS):
        sl = slice(hh * hd, (hh + 1) * hd)
        s = _dot_nt(q[:, sl], k_ref[:, sl]) * (1.0 / math.sqrt(hd))
        m = jnp.max(s, axis=-1, keepdims=True)
        p = jnp.exp(s - m)
        p = p * (1.0 / jnp.sum(p, axis=-1, keepdims=True))
        outs.append(_dot(p.astype(BF16), v_ref[:, sl]))
    o = jnp.concatenate(outs, axis=-1).astype(BF16)
    h_out[...] = h + _dot(o, wo_ref[...])


def _xattn(h3, g, wq, kx, vx, wo):
    B, S, D = h3.shape
    M = kx.shape[1]
    ts = min(TS_X, S)
    fixed = lambda b, i: (0, 0)
    return pl.pallas_call(
        _xattn_kernel,
        grid=(B, S // ts),
        in_specs=[pl.BlockSpec((None, ts, D), lambda b, i: (b, i, 0)), pl.BlockSpec(g.shape, fixed),
                  pl.BlockSpec(wq.shape, fixed), pl.BlockSpec((None, M, D), lambda b, i: (b, 0, 0)),
                  pl.BlockSpec((None, M, D), lambda b, i: (b, 0, 0)), pl.BlockSpec(wo.shape, fixed)],
        out_specs=pl.BlockSpec((None, ts, D), lambda b, i: (b, i, 0)),
        out_shape=jax.ShapeDtypeStruct((B, S, D), F32),
        compiler_params=pltpu.CompilerParams(dimension_semantics=("arbitrary", "arbitrary"),
                                             vmem_limit_bytes=VMEM_LIMIT),
        name="xattn",
    )(h3, g, wq, kx, vx, wo)


def _batcher_pairs(n):
    pairs = []
    p = 1
    while p < n:
        k = p
        while k >= 1:
            for j in range(k % p, n - k, 2 * k):
                for i in range(min(k, n - j - k)):
                    if (i + j) // (2 * p) == (i + j + k) // (2 * p):
                        pairs.append((i + j, i + j + k))
            k //= 2
        p *= 2
    return pairs


_SORT16 = _batcher_pairs(PEER_TOPK)
_ROW_LEN = [PEER_TOPK // (a + 1) for a in range(PEER_TOPK)]


def _sort_desc(v):
    v = list(v)
    for i, j in _SORT16:
        hi = jnp.maximum(v[i], v[j])
        lo = jnp.minimum(v[i], v[j])
        v[i], v[j] = hi, lo
    return v


def _bitonic_desc(v):
    v = list(v)
    n = len(v)
    d = n // 2
    while d >= 1:
        for k in range(n):
            if k & d == 0:
                hi = jnp.maximum(v[k], v[k + d])
                lo = jnp.minimum(v[k], v[k + d])
                v[k], v[k + d] = hi, lo
        d //= 2
    return v


def _merge_top(cur, other):
    n = len(cur)
    c = list(cur)
    for r, val in enumerate(other):
        c[n - 1 - r] = jnp.maximum(c[n - 1 - r], val)
    return _bitonic_desc(c)


def _top16_sorted(s):
    v = _sort_desc([s[k] for k in range(PEER_TOPK)])
    for shift in (4, 2, 1):
        other = [pltpu.roll(v[k], shift, axis=0) for k in range(PEER_TOPK)]
        v = _merge_top(v, other)
    return v


def _peer_route_kernel(h_ref, g_ref, wq_ref, key_ref,
                       xn_out, r2_out, p_out, n_out, c_out,
                       st_ref, top_ref, res_ref):
    ts = h_ref.shape[0]
    n_chunk = ts // LANES
    hn = _rms(h_ref[...], g_ref[...]).astype(BF16)
    xn_out[...] = hn
    q = _dot(hn, wq_ref[...])
    for h in range(PEER_HEADS):
        st = _dot_nt(key_ref[h], q[:, h * LANES:(h + 1) * LANES].astype(BF16))
        for c in range(n_chunk):
            st_ref[c, h] = st[:, c * LANES:(c + 1) * LANES]


    def chunk_body(c, _):
        def sort_body(h, _):
            for half in range(2):
                s = st_ref[c, h, pl.ds(half * PEER_KEYS, PEER_KEYS), :].reshape(PEER_TOPK, SUBLANES, LANES)
                v = _top16_sorted(s)
                for a in range(PEER_TOPK):
                    top_ref[half, a, pl.ds(h, 1), :] = v[a][0:1, :]
            return 0

        lax.fori_loop(0, PEER_HEADS, sort_body, 0)

        v1 = [top_ref[0, a] for a in range(PEER_TOPK)]
        v2 = [top_ref[1, b] for b in range(PEER_TOPK)]
        sums = [[v1[a] + v2[b] for b in range(_ROW_LEN[a])] for a in range(PEER_TOPK)]
        cur = sums[0]
        a = 1
        while _ROW_LEN[a] > 1:
            cur = _merge_top(cur, sums[a])
            a += 1
        cur = _merge_top(cur, [sums[r][0] for r in range(a, PEER_TOPK)])
        tau = cur[PEER_TOPK - 1]
        top_sum = sums[0][0]
        z = jnp.zeros_like(tau)
        for a in range(PEER_TOPK):
            cnt = jnp.zeros_like(tau)
            for b in range(_ROW_LEN[a]):
                sel = sums[a][b] >= tau
                cnt = cnt + jnp.where(sel, 1.0, 0.0)
                z = z + jnp.where(sel, jnp.exp(sums[a][b] - top_sum), 0.0)
            res_ref[a] = cnt
        res_ref[PEER_TOPK] = 1.0 / z

        def expand_body(h, _):
            s1 = st_ref[c, h, pl.ds(0, PEER_KEYS), :].reshape(PEER_TOPK, SUBLANES, LANES)
            s2 = st_ref[c, h, pl.ds(PEER_KEYS, PEER_KEYS), :].reshape(PEER_TOPK, SUBLANES, LANES)
            n = jnp.zeros(s1.shape, F32)
            r2 = jnp.full(s2.shape, float(PEER_TOPK), F32)
            for a in range(PEER_TOPK - 1, -1, -1):
                v1a = top_ref[0, a, pl.ds(h, 1), :]
                v2a = top_ref[1, a, pl.ds(h, 1), :]
                n = jnp.where(s1 == v1a, res_ref[a, pl.ds(h, 1), :], n)
                r2 = jnp.where(s2 == v2a, float(a), r2)
            m1 = top_ref[0, 0, pl.ds(h, 1), :]
            m2 = top_ref[1, 0, pl.ds(h, 1), :]
            inv_z = res_ref[PEER_TOPK, pl.ds(h, 1), :]
            n_out[c, h] = n.reshape(PEER_KEYS, LANES)
            c_out[c, h] = (jnp.exp(s1 - m1) * inv_z).reshape(PEER_KEYS, LANES)
            r2_out[c, h] = r2.reshape(PEER_KEYS, LANES).astype(BF16)
            p_out[c, h] = jnp.exp(s2 - m2).reshape(PEER_KEYS, LANES).astype(BF16)
            return 0

        lax.fori_loop(0, PEER_HEADS, expand_body, 0)
        return 0

    lax.fori_loop(0, n_chunk, chunk_body, 0)


def _peer_route(h2, g, wq, keys):
    T, D = h2.shape
    ts = min(TS_ROUTE, T)
    nc = ts // LANES
    aux_spec = pl.BlockSpec((nc, PEER_HEADS, PEER_KEYS, LANES), lambda i: (i, 0, 0, 0))
    aux_shape = jax.ShapeDtypeStruct((T // LANES, PEER_HEADS, PEER_KEYS, LANES), F32)
    aux_shape_bf = jax.ShapeDtypeStruct((T // LANES, PEER_HEADS, PEER_KEYS, LANES), BF16)
    fixed2 = lambda i: (0, 0)
    fixed3 = lambda i: (0, 0, 0)
    return pl.pallas_call(
        _peer_route_kernel,
        grid=(T // ts,),
        in_specs=[pl.BlockSpec((ts, D), lambda i: (i, 0)), pl.BlockSpec(g.shape, fixed2),
                  pl.BlockSpec(wq.shape, fixed2), pl.BlockSpec(keys.shape, fixed3)],
        out_specs=[pl.BlockSpec((ts, D), lambda i: (i, 0)), aux_spec, aux_spec, aux_spec, aux_spec],
        out_shape=[jax.ShapeDtypeStruct((T, D), BF16), aux_shape_bf, aux_shape_bf, aux_shape, aux_shape],
        scratch_shapes=[pltpu.VMEM((nc, PEER_HEADS, 2 * PEER_KEYS, LANES), F32),
                        pltpu.VMEM((2, PEER_TOPK, SUBLANES, LANES), F32),
                        pltpu.VMEM((PEER_TOPK + 1, SUBLANES, LANES), F32)],
        compiler_params=pltpu.CompilerParams(dimension_semantics=("arbitrary",), vmem_limit_bytes=VMEM_LIMIT),
        name="peer_route",
    )(h2, g, wq, keys)


def _pack_experts_kernel(u_ref, v_ref, u_out, vt_out):
    u_out[...] = pltpu.bitcast(u_ref[...].astype(BF16), jnp.uint32)
    vt_out[...] = pltpu.bitcast(v_ref[...].T.astype(BF16), jnp.uint32)


def _pack_experts(u, v):
    E, D = u.shape
    eb = EB_FFN
    return pl.pallas_call(
        _pack_experts_kernel,
        grid=(E // eb,),
        in_specs=[pl.BlockSpec((eb, D), lambda e: (e, 0)), pl.BlockSpec((eb, D), lambda e: (e, 0))],
        out_specs=[pl.BlockSpec((eb // 2, D), lambda e: (e, 0)), pl.BlockSpec((D // 2, eb), lambda e: (0, e))],
        out_shape=[jax.ShapeDtypeStruct((E // 2, D), jnp.uint32), jax.ShapeDtypeStruct((D // 2, E), jnp.uint32)],
        compiler_params=pltpu.CompilerParams(dimension_semantics=("arbitrary",), vmem_limit_bytes=VMEM_LIMIT),
        name="pack_experts",
    )(u, v)


def _peer_gate_unit(tc, ii, at_ref, ht_ref, r2s_ref, ps_ref, n_ref, c_ref):
    pack = 2 * SUBLANES
    n_jv = PEER_KEYS // pack
    zero = jnp.zeros((pack, LANES), BF16)
    lanes = slice(tc * LANES, (tc + 1) * LANES)
    g = [None] * n_jv
    for h in range(PEER_HEADS):
        n_b = jnp.broadcast_to(n_ref[tc, h, ii:ii + 1, :], (pack, LANES)).astype(BF16)
        c_b = jnp.broadcast_to(c_ref[tc, h, ii:ii + 1, :], (pack, LANES)).astype(BF16)
        for jv in range(n_jv):
            js = slice(jv * pack, (jv + 1) * pack)
            term = jnp.where(r2s_ref[tc, h, js, :] < n_b, ps_ref[tc, h, js, :], zero) * c_b
            g[jv] = term if g[jv] is None else g[jv] + term
    for jv in range(n_jv):
        rows = slice(ii * PEER_KEYS + jv * pack, ii * PEER_KEYS + (jv + 1) * pack)
        ht_ref[rows, lanes] = _gelu_tanh(at_ref[rows, lanes]).astype(BF16) * g[jv]


def _peer_ffn_kernel(xn_ref, u_ref, vt_ref, r2_ref, p_ref, n_ref, c_ref, h_ref, gfin_ref,
                     out_ref, acc_ref, at0_ref, at1_ref, ht0_ref, ht1_ref, r2s_ref, ps_ref, xs_ref,
                     *, n_e, n_blocks, final_norm):
    g = pl.program_id(0)
    tt = xn_ref.shape[0]
    e_score = g % n_e
    e_gate = jnp.maximum(g - 1, 0) % n_e
    e_down = jnp.maximum(g - 2, 0) % n_e

    @pl.when(g == 0)
    def _():
        at1_ref[...] = jnp.zeros_like(at1_ref)
        ht0_ref[...] = jnp.zeros_like(ht0_ref)
        ht1_ref[...] = jnp.zeros_like(ht1_ref)
        acc_ref[...] = jnp.zeros_like(acc_ref)

    @pl.when((g < n_blocks) & (e_score == 0))
    def _():
        xs_ref[...] = xn_ref[...]

    @pl.when((g <= n_blocks) & (e_gate == 0))
    def _():
        for tc in range(tt // LANES):
            for h in range(PEER_HEADS):
                r2s_ref[tc, h] = r2_ref[tc, h]
                ps_ref[tc, h] = p_ref[tc, h]

    @pl.when((g >= 2) & (e_down == 0))
    def _():
        acc_ref[...] = jnp.zeros_like(acc_ref)

    def stages(at_w, at_r, ht_w, ht_r):
        u_blk = pltpu.bitcast(u_ref[...], BF16)
        vt_blk = pltpu.bitcast(vt_ref[...], BF16)
        eb = u_blk.shape[0]
        subs = [slice(sb * SUB_FFN, (sb + 1) * SUB_FFN) for sb in range(eb // SUB_FFN)]

        def score(ex):
            at_w[ex, :] = _dot_nt(u_blk[ex], xs_ref[...])

        def down():
            acc_ref[...] += _dot(vt_blk, ht_r[...])

        chunks = [functools.partial(score, ex) for ex in subs] + [down]
        units = [(tc, ii) for tc in range(tt // LANES) for ii in range(eb // PEER_KEYS)]
        split = GATE_UNIT_SPLIT
        assert len(split) == len(chunks) + 1 and sum(split) == len(units)
        bounds = [sum(split[:k]) for k in range(len(split) + 1)]

        def gate_units(k):
            for tc, ii in units[bounds[k]:bounds[k + 1]]:
                _peer_gate_unit(tc, ii, at_r, ht_w, r2s_ref, ps_ref, n_ref, c_ref)

        gate_units(0)
        for k, chunk in enumerate(chunks):
            chunk()
            gate_units(k + 1)

    @pl.when(g % 2 == 0)
    def _():
        stages(at0_ref, at1_ref, ht1_ref, ht0_ref)

    @pl.when(g % 2 == 1)
    def _():
        stages(at1_ref, at0_ref, ht0_ref, ht1_ref)

    @pl.when((g >= 2) & (e_down == n_e - 1))
    def _():
        res = h_ref[...] + acc_ref[...].T
        out_ref[...] = _rms(res, gfin_ref[...]) if final_norm else res


def _peer_ffn(xn, u_pack, vt_pack, r2, p, n, coef, h2, gfin, final_norm):
    T, D = h2.shape
    E = vt_pack.shape[1]
    tt = min(TT_FFN, T)
    nc = tt // LANES
    eb = EB_FFN
    n_i = eb // PEER_KEYS
    n_e = E // eb
    n_blocks = (T // tt) * n_e

    def block(lag):
        def split(g):
            b = jnp.clip(g - lag, 0, n_blocks - 1)
            return b // n_e, b % n_e
        return split

    score, gate, down = block(0), block(1), block(2)
    aux_shape = (nc, PEER_HEADS, PEER_KEYS, LANES)
    row_shape = (nc, PEER_HEADS, n_i, LANES)
    return pl.pallas_call(
        functools.partial(_peer_ffn_kernel, n_e=n_e, n_blocks=n_blocks, final_norm=final_norm),
        grid=(n_blocks + 2,),
        in_specs=[pl.BlockSpec((tt, D), lambda g: (score(g)[0], 0)),
                  pl.BlockSpec((eb // 2, D), lambda g: (score(g)[1], 0)),
                  pl.BlockSpec((D // 2, eb), lambda g: (0, down(g)[1])),
                  pl.BlockSpec(aux_shape, lambda g: (gate(g)[0], 0, 0, 0)),
                  pl.BlockSpec(aux_shape, lambda g: (gate(g)[0], 0, 0, 0)),
                  pl.BlockSpec(row_shape, lambda g: (gate(g)[0], 0, gate(g)[1], 0)),
                  pl.BlockSpec(row_shape, lambda g: (gate(g)[0], 0, gate(g)[1], 0)),
                  pl.BlockSpec((tt, D), lambda g: (down(g)[0], 0)),
                  pl.BlockSpec(gfin.shape, lambda g: (0, 0))],
        out_specs=pl.BlockSpec((tt, D), lambda g: (down(g)[0], 0)),
        out_shape=jax.ShapeDtypeStruct((T, D), F32),
        scratch_shapes=[pltpu.VMEM((D, tt), F32),
                        pltpu.VMEM((eb, tt), F32), pltpu.VMEM((eb, tt), F32),
                        pltpu.VMEM((eb, tt), BF16), pltpu.VMEM((eb, tt), BF16),
                        pltpu.VMEM(aux_shape, BF16), pltpu.VMEM(aux_shape, BF16),
                        pltpu.VMEM((tt, D), BF16)],
        compiler_params=pltpu.CompilerParams(dimension_semantics=("arbitrary",), vmem_limit_bytes=VMEM_LIMIT),
        name="peer_ffn",
    )(xn, u_pack, vt_pack, r2, p, n, coef, h2, gfin)


def _head_blocks(w, n_heads, width, pieces):
    w3 = w.reshape(w.shape[0], n_heads, width)
    out = jnp.zeros((w.shape[0], n_heads, LANES), w.dtype)
    for s0, s1, d0 in pieces:
        out = out.at[:, :, d0:d0 + (s1 - s0)].set(w3[:, :, s0:s1])
    return out.reshape(w.shape[0], n_heads * LANES)


def kernel(x, mem, positions, g_mix, w_in, g_q, w_uq, g_kv, w_ukv, conv_w, g_out, w_o, g_x, g_mem, w_xq,
           w_xkv, w_xo, g_ffn, w_pq, sub_keys, u_experts, v_experts, g_final):
    B, S, D = x.shape
    T = B * S
    depth = g_mix.shape[0]
    half = QK_ROPE // 2
    assert S % min(TS_IN, S) == 0 and S % min(TS_X, S) == 0 and S % min(TQ, S) == 0 and D == MLA_HEADS * LANES
    assert T % min(TS_ROUTE, T) == 0 and T % min(TT_FFN, T) == 0 and u_experts.shape[1] % EB_FFN == 0

    inv = ROPE_THETA ** (-jnp.arange(0, QK_ROPE, 2, dtype=F32) / QK_ROPE)
    ang = positions.astype(F32)[..., None] * inv
    cos = jnp.cos(ang).astype(x.dtype).reshape(T, half)
    sin = jnp.sin(ang).astype(x.dtype).reshape(T, half)
    ones = jnp.ones((T, QK_NOPE), F32)
    zeros_n = jnp.zeros((T, QK_NOPE), F32)
    pad_q = jnp.zeros((T, LANES - QK_NOPE - QK_ROPE), F32)
    cos_t = jnp.concatenate([ones, cos, cos, pad_q], axis=1)
    sin_t = jnp.concatenate([zeros_n, -sin, sin, pad_q], axis=1)

    lane = jnp.arange(LANES)
    col = jnp.arange(MLA_HEADS * LANES)
    rope_lane = (lane >= QK_NOPE) & (lane < QK_NOPE + QK_ROPE)
    eplace = ((col[None, :] % LANES == lane[:, None]) & rope_lane[:, None]).astype(BF16)
    mix_col = jnp.arange(D)
    gsum = (mix_col[:, None] // GROUP_DIM == lane[None, :]).astype(BF16)
    gexp = (lane[:, None] == mix_col[None, :] // GROUP_DIM).astype(BF16)

    h = x.reshape(T, D)
    for l in range(depth):
        o1 = Q_RANK
        o2 = o1 + KV_RANK
        o3 = o2 + QK_ROPE
        o4 = o3 + CONV_DIM
        o5 = o4 + CONV_DIM
        wl = w_in[l]
        w_kr = wl[:, o2:o3]
        w_krr = jnp.concatenate([w_kr[:, half:], w_kr[:, :half]], axis=1)
        pad_lo = jnp.zeros((D, QK_NOPE), wl.dtype)
        pad_hi = jnp.zeros((D, LANES - QK_NOPE - QK_ROPE), wl.dtype)
        w1 = jnp.concatenate([wl[:, :o1], wl[:, o1:o2], pad_lo, w_kr, pad_hi, pad_lo, w_krr, pad_hi,
                              wl[:, o3:o4], wl[:, o4:o5], wl[:, o5:]], axis=1).astype(BF16)
        qw = QK_NOPE + QK_ROPE
        wq = _head_blocks(w_uq[l], MLA_HEADS, qw, [(0, qw, 0)]).astype(BF16)
        wqr = _head_blocks(w_uq[l], MLA_HEADS, qw,
                           [(QK_NOPE + half, qw, QK_NOPE), (QK_NOPE, QK_NOPE + half, QK_NOPE + half)]).astype(BF16)
        kvw = QK_NOPE + V_HEAD
        wk = _head_blocks(w_ukv[l], MLA_HEADS, kvw, [(0, QK_NOPE, 0)]).astype(BF16)
        v_cols = w_ukv[l].reshape(KV_RANK, MLA_HEADS, kvw)[:, :, QK_NOPE:]
        v_pad = jnp.zeros_like(v_cols)
        odd_head = (jnp.arange(MLA_HEADS) % 2 == 1)[None, :, None]
        wv = jnp.where(odd_head, jnp.concatenate([v_pad, v_cols], axis=-1),
                       jnp.concatenate([v_cols, v_pad], axis=-1)).reshape(KV_RANK, MLA_HEADS * LANES).astype(BF16)
        vone = jnp.stack([(lane == _ones_lane(hd)).astype(F32) for hd in range(MLA_HEADS)]).reshape(1, -1)

        q, k, v, z, gb = _mixer_in(h, g_mix[l][None, :], w1, g_q[l][None, :], wq, wqr, g_kv[l][None, :], wk, wv,
                                   vone, eplace, cos_t, sin_t)
        o = _mla_attn(q.reshape(B, S, -1), k.reshape(B, S, -1), v.reshape(B, S, -1))
        h = _mixer_out(o.reshape(T, -1), z, gb, h, conv_w[l], g_out[l][None, :], gsum, gexp,
                       w_o[l].astype(BF16), S)

        kx, vx = _mem_kv(mem, g_mem[l][None, :], w_xkv[l].astype(BF16))
        h = _xattn(h.reshape(B, S, D), g_x[l][None, :], w_xq[l].astype(BF16), kx, vx,
                   w_xo[l].astype(BF16)).reshape(T, D)

        sk = sub_keys[l]
        zk = jnp.zeros_like(sk[:, 0])
        keys_bd = jnp.concatenate([jnp.concatenate([sk[:, 0], zk], axis=-1),
                                   jnp.concatenate([zk, sk[:, 1]], axis=-1)], axis=1)
        xn, r2, p, n, coef = _peer_route(h, g_ffn[l][None, :], w_pq[l].astype(BF16), keys_bd.astype(BF16))
        u_pack, vt_pack = _pack_experts(u_experts[l], v_experts[l])
        h = _peer_ffn(xn, u_pack, vt_pack, r2, p, n, coef, h,
                      g_final[None, :], final_norm=(l == depth - 1))
    return h.reshape(B, S, D)
```

```python
import functools
import math

import jax
import jax.numpy as jnp
from jax import lax
from jax.experimental import pallas as pl
from jax.experimental.pallas import tpu as pltpu

F32 = jnp.float32
BF16 = jnp.bfloat16

EPS = 1e-6
LANES = 128
SUBLANES = 8
VMEM_LIMIT = 56 * 1024 * 1024

MLA_HEADS = 8
QK_NOPE = 64
QK_ROPE = 32
V_HEAD = 64
Q_RANK = 384
KV_RANK = 256
CONV_DIM = 512
GROUP_DIM = 64
ROPE_THETA = 10000.0
X_HEADS = 4
PEER_HEADS = 8
PEER_KEYS = 128
PEER_TOPK = 16

TS_IN = 1024
TS_X = 1024
ROW_SPLIT = 4
TQ = 512
TS_ROUTE = 1024
TT_FFN = 512
EB_FFN = 2048
SUB_FFN = 1024
GATE_UNIT_SPLIT = (8, 16, 40, 0)
NT_DIMS = (((1,), (1,)), ((), ()))


def _rms(x, g):
    return x * lax.rsqrt(jnp.mean(x * x, axis=-1, keepdims=True) + EPS) * g


def _split_bf16(x):
    hi = x.astype(BF16)
    lo = (x - hi.astype(F32)).astype(BF16)
    return hi, lo


def _gelu_tanh(x):
    c0 = math.sqrt(2.0 / math.pi)
    half_x = 0.5 * x
    return half_x + half_x * jnp.tanh(x * (c0 + (c0 * 0.044715) * (x * x)))


def _dot(a, b):
    return jnp.dot(a, b, preferred_element_type=F32)


def _dot_nt(a, b):
    return lax.dot_general(a, b, NT_DIMS, preferred_element_type=F32)


_C_CQ = 0
_C_CKV = _C_CQ + Q_RANK
_C_KR = _C_CKV + KV_RANK
_C_KRR = _C_KR + LANES
_C_GB = _C_KRR + LANES
_C_GC = _C_GB + CONV_DIM
_C_HX = _C_GC + CONV_DIM
_C_END = _C_HX + CONV_DIM


def _mixer_in_kernel(x_ref, gmix_ref, w1_ref, gq_ref, wq_ref, wqr_ref, gkv_ref, wk_ref, wv_ref, vone_ref,
                     eplace_ref, cos_ref, sin_ref,
                     q_out, k_out, v_out, z_out, gb_out):
    xn = _rms(x_ref[...], gmix_ref[...]).astype(BF16)
    proj = _dot(xn, w1_ref[...])
    cq = proj[:, _C_CQ:_C_CKV]
    ckv = proj[:, _C_CKV:_C_KR]
    kr = proj[:, _C_KR:_C_KRR]
    krr = proj[:, _C_KRR:_C_GB]
    gb_out[...] = proj[:, _C_GB:_C_GC]
    z_out[...] = proj[:, _C_GC:_C_HX] * proj[:, _C_HX:_C_END]

    cqn = _rms(cq, gq_ref[...]).astype(BF16)
    q_raw = _dot(cqn, wq_ref[...])
    q_rot = _dot(cqn, wqr_ref[...])
    cos_t = cos_ref[...]
    sin_t = sin_ref[...]
    q_scale = math.log2(math.e) / math.sqrt(QK_NOPE + QK_ROPE)
    for h in range(MLA_HEADS):
        sl = slice(h * LANES, (h + 1) * LANES)
        q_out[:, sl] = ((q_raw[:, sl] * cos_t + q_rot[:, sl] * sin_t) * q_scale).astype(BF16)

    ckvn = _rms(ckv, gkv_ref[...]).astype(BF16)
    kr_roped = (kr * cos_t + krr * sin_t).astype(BF16)
    k_out[...] = (_dot(ckvn, wk_ref[...]) + _dot(kr_roped, eplace_ref[...])).astype(BF16)
    v_out[...] = (_dot(ckvn, wv_ref[...]) + vone_ref[...]).astype(BF16)


def _mixer_in(x2, gmix, w1, gq, wq, wqr, gkv, wk, wv, vone, eplace, cos_t, sin_t):
    T, D = x2.shape
    ts = min(TS_IN, T)
    row = lambda i: (i, 0)
    fixed = lambda i: (0, 0)
    full = lambda a: pl.BlockSpec(a.shape, fixed)
    return pl.pallas_call(
        _mixer_in_kernel,
        grid=(T // ts,),
        in_specs=[pl.BlockSpec((ts, D), row), full(gmix), full(w1), full(gq), full(wq), full(wqr),
                  full(gkv), full(wk), full(wv), full(vone), full(eplace),
                  pl.BlockSpec((ts, LANES), row), pl.BlockSpec((ts, LANES), row)],
        out_specs=[pl.BlockSpec((ts, MLA_HEADS * LANES), row), pl.BlockSpec((ts, MLA_HEADS * LANES), row),
                   pl.BlockSpec((ts, MLA_HEADS * LANES), row), pl.BlockSpec((ts, CONV_DIM), row),
                   pl.BlockSpec((ts, CONV_DIM), row)],
        out_shape=[jax.ShapeDtypeStruct((T, MLA_HEADS * LANES), BF16),
                   jax.ShapeDtypeStruct((T, MLA_HEADS * LANES), BF16),
                   jax.ShapeDtypeStruct((T, MLA_HEADS * LANES), BF16),
                   jax.ShapeDtypeStruct((T, CONV_DIM), F32),
                   jax.ShapeDtypeStruct((T, CONV_DIM), F32)],
        compiler_params=pltpu.CompilerParams(dimension_semantics=("arbitrary",), vmem_limit_bytes=VMEM_LIMIT),
        name="mixer_in",
    )(x2, gmix, w1, gq, wq, wqr, gkv, wk, wv, vone, eplace, cos_t, sin_t)


def _ones_lane(head):
    return V_HEAD if head % 2 == 0 else 0


def _mla_attn_kernel(q_ref, k_ref, v_ref, o_ref, *, tq):
    seq = q_ref.shape[0]
    causal = (lax.broadcasted_iota(jnp.int32, (tq, tq), 1) <= lax.broadcasted_iota(jnp.int32, (tq, tq), 0))
    lane = lax.broadcasted_iota(jnp.int32, (tq, LANES), 1)
    for qi in range(seq // tq):
        rows = slice(qi * tq, (qi + 1) * tq)
        keys = slice(0, (qi + 1) * tq)
        outs = []
        for hh in range(2):
            hl = slice(hh * LANES, (hh + 1) * LANES)
            s = _dot_nt(q_ref[rows, hl], k_ref[keys, hl])
            s_diag = jnp.where(causal, s[:, qi * tq:], -jnp.inf)
            s = s_diag if qi == 0 else jnp.concatenate([s[:, :qi * tq], s_diag], axis=1)
            m = jnp.max(s, axis=-1, keepdims=True)
            acc = _dot(jnp.exp2(s - m).astype(BF16), v_ref[keys, hl])
            one = _ones_lane(hh)
            outs.append(acc * (1.0 / acc[:, one:one + 1]))
        o_ref[rows, :] = jnp.where(lane < V_HEAD, outs[0], outs[1])


def _mla_attn(q3, k3, v3):
    B, S, _ = q3.shape
    tq = min(TQ, S)
    pair = lambda b, g: (b, 0, g)
    return pl.pallas_call(
        functools.partial(_mla_attn_kernel, tq=tq),
        grid=(B, MLA_HEADS // 2),
        in_specs=[pl.BlockSpec((None, S, 2 * LANES), pair), pl.BlockSpec((None, S, 2 * LANES), pair),
                  pl.BlockSpec((None, S, 2 * LANES), pair)],
        out_specs=pl.BlockSpec((None, S, 2 * V_HEAD), pair),
        out_shape=jax.ShapeDtypeStruct((B, S, MLA_HEADS * V_HEAD), F32),
        compiler_params=pltpu.CompilerParams(dimension_semantics=("arbitrary", "arbitrary"),
                                             vmem_limit_bytes=VMEM_LIMIT),
        name="mla_attn",
    )(q3, k3, v3)


def _mixer_out_kernel(o_ref, z_ref, zh_ref, gb_ref, x_ref, cw_ref, gout_ref, gsum_ref, gexp_ref, wo_ref,
                      h_out, *, tiles_per_seq):
    i = pl.program_id(0)
    ts = z_ref.shape[0]
    z = z_ref[...]
    halo = jnp.where(i % tiles_per_seq == 0, 0.0, zh_ref[...])
    row = lax.broadcasted_iota(jnp.int32, z.shape, 0)
    z1 = jnp.where(row == 0, halo[7:8, :], pltpu.roll(z, 1, axis=0))
    z2 = jnp.where(row == 0, halo[6:7, :], jnp.where(row == 1, halo[7:8, :], pltpu.roll(z, 2, axis=0)))
    cw = cw_ref[...]
    y_conv = gb_ref[...] * (cw[0:1, :] * z2 + cw[1:2, :] * z1 + cw[2:3, :] * z)
    y_all = jnp.concatenate([o_ref[...], y_conv], axis=-1)
    for part in range(ROW_SPLIT):
        rows = slice(part * ts // ROW_SPLIT, (part + 1) * ts // ROW_SPLIT)
        y = y_all[rows]
        sq_hi, sq_lo = _split_bf16(y * y)
        gs = _dot(sq_hi, gsum_ref[...]) + _dot(sq_lo, gsum_ref[...])
        r = lax.rsqrt(gs * (1.0 / GROUP_DIM) + EPS)
        r_hi, r_lo = _split_bf16(r)
        r_full = _dot(r_hi, gexp_ref[...]) + _dot(r_lo, gexp_ref[...])
        yn = (y * r_full * gout_ref[...]).astype(BF16)
        h_out[rows, :] = x_ref[rows, :] + _dot(yn, wo_ref[...])


def _mixer_out(o2, z, gb, x2, conv_w, gout, gsum, gexp, wo, seq):
    T, D = x2.shape
    ts = min(TS_IN, seq)
    row = lambda i: (i, 0)
    fixed = lambda i: (0, 0)
    full = lambda a: pl.BlockSpec(a.shape, fixed)
    halo_blocks = ts // SUBLANES
    return pl.pallas_call(
        functools.partial(_mixer_out_kernel, tiles_per_seq=seq // ts),
        grid=(T // ts,),
        in_specs=[pl.BlockSpec((ts, MLA_HEADS * V_HEAD), row), pl.BlockSpec((ts, CONV_DIM), row),
                  pl.BlockSpec((SUBLANES, CONV_DIM), lambda i: (jnp.maximum(i * halo_blocks - 1, 0), 0)),
                  pl.BlockSpec((ts, CONV_DIM), row), pl.BlockSpec((ts, D), row),
                  full(conv_w), full(gout), full(gsum), full(gexp), full(wo)],
        out_specs=pl.BlockSpec((ts, D), row),
        out_shape=jax.ShapeDtypeStruct((T, D), F32),
        compiler_params=pltpu.CompilerParams(dimension_semantics=("arbitrary",), vmem_limit_bytes=VMEM_LIMIT),
        name="mixer_out",
    )(o2, z, z, gb, x2, conv_w, gout, gsum, gexp, wo)


def _mem_kv_kernel(mem_ref, g_ref, w_ref, k_out, v_out):
    d = mem_ref.shape[-1]
    mn = _rms(mem_ref[...], g_ref[...]).astype(BF16)
    kv = _dot(mn, w_ref[...])
    k_out[...] = kv[:, :d].astype(BF16)
    v_out[...] = kv[:, d:].astype(BF16)


def _mem_kv(mem, g, w):
    B, M, D = mem.shape
    return pl.pallas_call(
        _mem_kv_kernel,
        grid=(B,),
        in_specs=[pl.BlockSpec((None, M, D), lambda b: (b, 0, 0)), pl.BlockSpec(g.shape, lambda b: (0, 0)),
                  pl.BlockSpec(w.shape, lambda b: (0, 0))],
        out_specs=[pl.BlockSpec((None, M, D), lambda b: (b, 0, 0)), pl.BlockSpec((None, M, D), lambda b: (b, 0, 0))],
        out_shape=[jax.ShapeDtypeStruct((B, M, D), BF16), jax.ShapeDtypeStruct((B, M, D), BF16)],
        compiler_params=pltpu.CompilerParams(dimension_semantics=("arbitrary",), vmem_limit_bytes=VMEM_LIMIT),
        name="mem_kv",
    )(mem, g, w)


def _xattn_kernel(h_ref, g_ref, wq_ref, k_ref, v_ref, wo_ref, h_out):
    h = h_ref[...]
    d = h.shape[-1]
    hd = d // X_HEADS
    hn = _rms(h, g_ref[...]).astype(BF16)
    q = _dot(hn, wq_ref[...]).astype(BF16)
    outs = []
    for hh in range(X_HEADS):
        sl = slice(hh * hd, (hh + 1) * hd)
        s = _dot_nt(q[:, sl], k_ref[:, sl]) * (1.0 / math.sqrt(hd))
        m = jnp.max(s, axis=-1, keepdims=True)
        p = jnp.exp(s - m)
        p = p * (1.0 / jnp.sum(p, axis=-1, keepdims=True))
        outs.append(_dot(p.astype(BF16), v_ref[:, sl]))
    o = jnp.concatenate(outs, axis=-1).astype(BF16)
    h_out[...] = h + _dot(o, wo_ref[...])


def _xattn(h3, g, wq, kx, vx, wo):
    B, S, D = h3.shape
    M = kx.shape[1]
    ts = min(TS_X, S)
    fixed = lambda b, i: (0, 0)
    return pl.pallas_call(
        _xattn_kernel,
        grid=(B, S // ts),
        in_specs=[pl.BlockSpec((None, ts, D), lambda b, i: (b, i, 0)), pl.BlockSpec(g.shape, fixed),
                  pl.BlockSpec(wq.shape, fixed), pl.BlockSpec((None, M, D), lambda b, i: (b, 0, 0)),
                  pl.BlockSpec((None, M, D), lambda b, i: (b, 0, 0)), pl.BlockSpec(wo.shape, fixed)],
        out_specs=pl.BlockSpec((None, ts, D), lambda b, i: (b, i, 0)),
        out_shape=jax.ShapeDtypeStruct((B, S, D), F32),
        compiler_params=pltpu.CompilerParams(dimension_semantics=("arbitrary", "arbitrary"),
                                             vmem_limit_bytes=VMEM_LIMIT),
        name="xattn",
    )(h3, g, wq, kx, vx, wo)


def _batcher_pairs(n):
    pairs = []
    p = 1
    while p < n:
        k = p
        while k >= 1:
            for j in range(k % p, n - k, 2 * k):
                for i in range(min(k, n - j - k)):
                    if (i + j) // (2 * p) == (i + j + k) // (2 * p):
                        pairs.append((i + j, i + j + k))
            k //= 2
        p *= 2
    return pairs


_SORT16 = _batcher_pairs(PEER_TOPK)
_ROW_LEN = [PEER_TOPK // (a + 1) for a in range(PEER_TOPK)]


def _sort_desc(v):
    v = list(v)
    for i, j in _SORT16:
        hi = jnp.maximum(v[i], v[j])
        lo = jnp.minimum(v[i], v[j])
        v[i], v[j] = hi, lo
    return v


def _bitonic_desc(v):
    v = list(v)
    n = len(v)
    d = n // 2
    while d >= 1:
        for k in range(n):
            if k & d == 0:
                hi = jnp.maximum(v[k], v[k + d])
                lo = jnp.minimum(v[k], v[k + d])
                v[k], v[k + d] = hi, lo
        d //= 2
    return v


def _merge_top(cur, other):
    n = len(cur)
    c = list(cur)
    for r, val in enumerate(other):
        c[n - 1 - r] = jnp.maximum(c[n - 1 - r], val)
    return _bitonic_desc(c)


def _peer_route_kernel(h_ref, g_ref, wq_ref, key_ref,
                       xn_out, r2_out, p_out, n_out, c_out,
                       st_ref, top_ref, sort_ref, res_ref):
    ts = h_ref.shape[0]
    n_chunk = ts // LANES
    hn = _rms(h_ref[...], g_ref[...]).astype(BF16)
    xn_out[...] = hn
    q = _dot(hn, wq_ref[...])
    for h in range(PEER_HEADS):
        st = _dot_nt(key_ref[h], q[:, h * LANES:(h + 1) * LANES].astype(BF16))
        for c in range(n_chunk):
            st_ref[c, h] = st[:, c * LANES:(c + 1) * LANES]


    def chunk_body(c, _):
        for half in range(2):
            def sort_body(h, _, half=half):
                s = st_ref[c, h, pl.ds(half * PEER_KEYS, PEER_KEYS), :].reshape(PEER_TOPK, SUBLANES, LANES)
                v = _sort_desc([s[k] for k in range(PEER_TOPK)])
                row0 = pl.multiple_of(h * SUBLANES, SUBLANES)
                for k in range(PEER_TOPK):
                    sort_ref[k, pl.ds(row0, SUBLANES), :] = v[k]
                return 0

            lax.fori_loop(0, PEER_HEADS, sort_body, 0)
            lists = [[sort_ref[k, pl.ds(s, PEER_HEADS, stride=SUBLANES), :] for k in range(PEER_TOPK)]
                     for s in range(SUBLANES)]
            while len(lists) > 1:
                lists = [_merge_top(lists[i], lists[i + 1]) for i in range(0, len(lists), 2)]
            for a in range(PEER_TOPK):
                top_ref[half, a] = lists[0][a]

        v1 = [top_ref[0, a] for a in range(PEER_TOPK)]
        v2 = [top_ref[1, b] for b in range(PEER_TOPK)]
        sums = [[v1[a] + v2[b] for b in range(_ROW_LEN[a])] for a in range(PEER_TOPK)]
        cur = sums[0]
        a = 1
        while _ROW_LEN[a] > 1:
            cur = _merge_top(cur, sums[a])
            a += 1
        cur = _merge_top(cur, [sums[r][0] for r in range(a, PEER_TOPK)])
        tau = cur[PEER_TOPK - 1]
        top_sum = sums[0][0]
        z = jnp.zeros_like(tau)
        for a in range(PEER_TOPK):
            cnt = jnp.zeros_like(tau)
            for b in range(_ROW_LEN[a]):
                sel = sums[a][b] >= tau
                cnt = cnt + jnp.where(sel, 1.0, 0.0)
                z = z + jnp.where(sel, jnp.exp(sums[a][b] - top_sum), 0.0)
            res_ref[a] = cnt
        res_ref[PEER_TOPK] = 1.0 / z

        def expand_body(h, _):
            s1 = st_ref[c, h, pl.ds(0, PEER_KEYS), :].reshape(PEER_TOPK, SUBLANES, LANES)
            s2 = st_ref[c, h, pl.ds(PEER_KEYS, PEER_KEYS), :].reshape(PEER_TOPK, SUBLANES, LANES)
            n = jnp.zeros(s1.shape, F32)
            r2 = jnp.full(s2.shape, float(PEER_TOPK), F32)
            for a in range(PEER_TOPK - 1, -1, -1):
                v1a = top_ref[0, a, pl.ds(h, 1), :]
                v2a = top_ref[1, a, pl.ds(h, 1), :]
                n = jnp.where(s1 == v1a, res_ref[a, pl.ds(h, 1), :], n)
                r2 = jnp.where(s2 == v2a, float(a), r2)
            m1 = top_ref[0, 0, pl.ds(h, 1), :]
            m2 = top_ref[1, 0, pl.ds(h, 1), :]
            inv_z = res_ref[PEER_TOPK, pl.ds(h, 1), :]
            n_out[c, h] = n.reshape(PEER_KEYS, LANES)
            c_out[c, h] = (jnp.exp(s1 - m1) * inv_z).reshape(PEER_KEYS, LANES)
            r2_out[c, h] = r2.reshape(PEER_KEYS, LANES).astype(BF16)
            p_out[c, h] = jnp.exp(s2 - m2).reshape(PEER_KEYS, LANES).astype(BF16)
            return 0

        lax.fori_loop(0, PEER_HEADS, expand_body, 0)
        return 0

    lax.fori_loop(0, n_chunk, chunk_body, 0)


def _peer_route(h2, g, wq, keys):
    T, D = h2.shape
    ts = min(TS_ROUTE, T)
    nc = ts // LANES
    aux_spec = pl.BlockSpec((nc, PEER_HEADS, PEER_KEYS, LANES), lambda i: (i, 0, 0, 0))
    aux_shape = jax.ShapeDtypeStruct((T // LANES, PEER_HEADS, PEER_KEYS, LANES), F32)
    aux_shape_bf = jax.ShapeDtypeStruct((T // LANES, PEER_HEADS, PEER_KEYS, LANES), BF16)
    fixed2 = lambda i: (0, 0)
    fixed3 = lambda i: (0, 0, 0)
    return pl.pallas_call(
        _peer_route_kernel,
        grid=(T // ts,),
        in_specs=[pl.BlockSpec((ts, D), lambda i: (i, 0)), pl.BlockSpec(g.shape, fixed2),
                  pl.BlockSpec(wq.shape, fixed2), pl.BlockSpec(keys.shape, fixed3)],
        out_specs=[pl.BlockSpec((ts, D), lambda i: (i, 0)), aux_spec, aux_spec, aux_spec, aux_spec],
        out_shape=[jax.ShapeDtypeStruct((T, D), BF16), aux_shape_bf, aux_shape_bf, aux_shape, aux_shape],
        scratch_shapes=[pltpu.VMEM((nc, PEER_HEADS, 2 * PEER_KEYS, LANES), F32),
                        pltpu.VMEM((2, PEER_TOPK, SUBLANES, LANES), F32),
                        pltpu.VMEM((PEER_TOPK, PEER_HEADS * SUBLANES, LANES), F32),
                        pltpu.VMEM((PEER_TOPK + 1, SUBLANES, LANES), F32)],
        compiler_params=pltpu.CompilerParams(dimension_semantics=("arbitrary",), vmem_limit_bytes=VMEM_LIMIT),
        name="peer_route",
    )(h2, g, wq, keys)


def _pack_experts_kernel(u_ref, v_ref, u_out, vt_out):
    u_out[...] = pltpu.bitcast(u_ref[...].astype(BF16), jnp.uint32)
    vt_out[...] = pltpu.bitcast(v_ref[...].T.astype(BF16), jnp.uint32)


def _pack_experts(u, v):
    E, D = u.shape
    eb = EB_FFN
    return pl.pallas_call(
        _pack_experts_kernel,
        grid=(E // eb,),
        in_specs=[pl.BlockSpec((eb, D), lambda e: (e, 0)), pl.BlockSpec((eb, D), lambda e: (e, 0))],
        out_specs=[pl.BlockSpec((eb // 2, D), lambda e: (e, 0)), pl.BlockSpec((D // 2, eb), lambda e: (0, e))],
        out_shape=[jax.ShapeDtypeStruct((E // 2, D), jnp.uint32), jax.ShapeDtypeStruct((D // 2, E), jnp.uint32)],
        compiler_params=pltpu.CompilerParams(dimension_semantics=("arbitrary",), vmem_limit_bytes=VMEM_LIMIT),
        name="pack_experts",
    )(u, v)


def _peer_gate_unit(tc, ii, at_ref, ht_ref, r2s_ref, ps_ref, n_ref, c_ref):
    pack = 2 * SUBLANES
    n_jv = PEER_KEYS // pack
    zero = jnp.zeros((pack, LANES), BF16)
    lanes = slice(tc * LANES, (tc + 1) * LANES)
    g = [None] * n_jv
    for h in range(PEER_HEADS):
        n_b = jnp.broadcast_to(n_ref[tc, h, ii:ii + 1, :], (pack, LANES)).astype(BF16)
        c_b = jnp.broadcast_to(c_ref[tc, h, ii:ii + 1, :], (pack, LANES)).astype(BF16)
        for jv in range(n_jv):
            js = slice(jv * pack, (jv + 1) * pack)
            term = jnp.where(r2s_ref[tc, h, js, :] < n_b, ps_ref[tc, h, js, :], zero) * c_b
            g[jv] = term if g[jv] is None else g[jv] + term
    for jv in range(n_jv):
        rows = slice(ii * PEER_KEYS + jv * pack, ii * PEER_KEYS + (jv + 1) * pack)
        ht_ref[rows, lanes] = _gelu_tanh(at_ref[rows, lanes]).astype(BF16) * g[jv]


def _peer_ffn_kernel(xn_ref, u_ref, vt_ref, r2_ref, p_ref, n_ref, c_ref, h_ref, gfin_ref,
                     out_ref, acc_ref, at0_ref, at1_ref, ht0_ref, ht1_ref, r2s_ref, ps_ref, xs_ref,
                     *, n_e, n_blocks, final_norm):
    g = pl.program_id(0)
    tt = xn_ref.shape[0]
    e_score = g % n_e
    e_gate = jnp.maximum(g - 1, 0) % n_e
    e_down = jnp.maximum(g - 2, 0) % n_e

    @pl.when(g == 0)
    def _():
        at1_ref[...] = jnp.zeros_like(at1_ref)
        ht0_ref[...] = jnp.zeros_like(ht0_ref)
        ht1_ref[...] = jnp.zeros_like(ht1_ref)
        acc_ref[...] = jnp.zeros_like(acc_ref)

    @pl.when((g < n_blocks) & (e_score == 0))
    def _():
        xs_ref[...] = xn_ref[...]

    @pl.when((g <= n_blocks) & (e_gate == 0))
    def _():
        for tc in range(tt // LANES):
            for h in range(PEER_HEADS):
                r2s_ref[tc, h] = r2_ref[tc, h]
                ps_ref[tc, h] = p_ref[tc, h]

    @pl.when((g >= 2) & (e_down == 0))
    def _():
        acc_ref[...] = jnp.zeros_like(acc_ref)

    def stages(at_w, at_r, ht_w, ht_r):
        u_blk = pltpu.bitcast(u_ref[...], BF16)
        vt_blk = pltpu.bitcast(vt_ref[...], BF16)
        eb = u_blk.shape[0]
        subs = [slice(sb * SUB_FFN, (sb + 1) * SUB_FFN) for sb in range(eb // SUB_FFN)]

        def score(ex):
            at_w[ex, :] = _dot_nt(u_blk[ex], xs_ref[...])

        def down():
            acc_ref[...] += _dot(vt_blk, ht_r[...])

        chunks = [functools.partial(score, ex) for ex in subs] + [down]
        units = [(tc, ii) for tc in range(tt // LANES) for ii in range(eb // PEER_KEYS)]
        split = GATE_UNIT_SPLIT
        assert len(split) == len(chunks) + 1 and sum(split) == len(units)
        bounds = [sum(split[:k]) for k in range(len(split) + 1)]

        def gate_units(k):
            for tc, ii in units[bounds[k]:bounds[k + 1]]:
                _peer_gate_unit(tc, ii, at_r, ht_w, r2s_ref, ps_ref, n_ref, c_ref)

        gate_units(0)
        for k, chunk in enumerate(chunks):
            chunk()
            gate_units(k + 1)

    @pl.when(g % 2 == 0)
    def _():
        stages(at0_ref, at1_ref, ht1_ref, ht0_ref)

    @pl.when(g % 2 == 1)
    def _():
        stages(at1_ref, at0_ref, ht0_ref, ht1_ref)

    @pl.when((g >= 2) & (e_down == n_e - 1))
    def _():
        res = h_ref[...] + acc_ref[...].T
        out_ref[...] = _rms(res, gfin_ref[...]) if final_norm else res


def _peer_ffn(xn, u_pack, vt_pack, r2, p, n, coef, h2, gfin, final_norm):
    T, D = h2.shape
    E = vt_pack.shape[1]
    tt = min(TT_FFN, T)
    nc = tt // LANES
    eb = EB_FFN
    n_i = eb // PEER_KEYS
    n_e = E // eb
    n_blocks = (T // tt) * n_e

    def block(lag):
        def split(g):
            b = jnp.clip(g - lag, 0, n_blocks - 1)
            return b // n_e, b % n_e
        return split

    score, gate, down = block(0), block(1), block(2)
    aux_shape = (nc, PEER_HEADS, PEER_KEYS, LANES)
    row_shape = (nc, PEER_HEADS, n_i, LANES)
    return pl.pallas_call(
        functools.partial(_peer_ffn_kernel, n_e=n_e, n_blocks=n_blocks, final_norm=final_norm),
        grid=(n_blocks + 2,),
        in_specs=[pl.BlockSpec((tt, D), lambda g: (score(g)[0], 0)),
                  pl.BlockSpec((eb // 2, D), lambda g: (score(g)[1], 0)),
                  pl.BlockSpec((D // 2, eb), lambda g: (0, down(g)[1])),
                  pl.BlockSpec(aux_shape, lambda g: (gate(g)[0], 0, 0, 0)),
                  pl.BlockSpec(aux_shape, lambda g: (gate(g)[0], 0, 0, 0)),
                  pl.BlockSpec(row_shape, lambda g: (gate(g)[0], 0, gate(g)[1], 0)),
                  pl.BlockSpec(row_shape, lambda g: (gate(g)[0], 0, gate(g)[1], 0)),
                  pl.BlockSpec((tt, D), lambda g: (down(g)[0], 0)),
                  pl.BlockSpec(gfin.shape, lambda g: (0, 0))],
        out_specs=pl.BlockSpec((tt, D), lambda g: (down(g)[0], 0)),
        out_shape=jax.ShapeDtypeStruct((T, D), F32),
        scratch_shapes=[pltpu.VMEM((D, tt), F32),
                        pltpu.VMEM((eb, tt), F32), pltpu.VMEM((eb, tt), F32),
                        pltpu.VMEM((eb, tt), BF16), pltpu.VMEM((eb, tt), BF16),
                        pltpu.VMEM(aux_shape, BF16), pltpu.VMEM(aux_shape, BF16),
                        pltpu.VMEM((tt, D), BF16)],
        compiler_params=pltpu.CompilerParams(dimension_semantics=("arbitrary",), vmem_limit_bytes=VMEM_LIMIT),
        name="peer_ffn",
    )(xn, u_pack, vt_pack, r2, p, n, coef, h2, gfin)


def _head_blocks(w, n_heads, width, pieces):
    w3 = w.reshape(w.shape[0], n_heads, width)
    out = jnp.zeros((w.shape[0], n_heads, LANES), w.dtype)
    for s0, s1, d0 in pieces:
        out = out.at[:, :, d0:d0 + (s1 - s0)].set(w3[:, :, s0:s1])
    return out.reshape(w.shape[0], n_heads * LANES)


def kernel(x, mem, positions, g_mix, w_in, g_q, w_uq, g_kv, w_ukv, conv_w, g_out, w_o, g_x, g_mem, w_xq,
           w_xkv, w_xo, g_ffn, w_pq, sub_keys, u_experts, v_experts, g_final):
    B, S, D = x.shape
    T = B * S
    depth = g_mix.shape[0]
    half = QK_ROPE // 2
    assert S % min(TS_IN, S) == 0 and S % min(TS_X, S) == 0 and S % min(TQ, S) == 0 and D == MLA_HEADS * LANES
    assert T % min(TS_ROUTE, T) == 0 and T % min(TT_FFN, T) == 0 and u_experts.shape[1] % EB_FFN == 0

    inv = ROPE_THETA ** (-jnp.arange(0, QK_ROPE, 2, dtype=F32) / QK_ROPE)
    ang = positions.astype(F32)[..., None] * inv
    cos = jnp.cos(ang).astype(x.dtype).reshape(T, half)
    sin = jnp.sin(ang).astype(x.dtype).reshape(T, half)
    ones = jnp.ones((T, QK_NOPE), F32)
    zeros_n = jnp.zeros((T, QK_NOPE), F32)
    pad_q = jnp.zeros((T, LANES - QK_NOPE - QK_ROPE), F32)
    cos_t = jnp.concatenate([ones, cos, cos, pad_q], axis=1)
    sin_t = jnp.concatenate([zeros_n, -sin, sin, pad_q], axis=1)

    lane = jnp.arange(LANES)
    col = jnp.arange(MLA_HEADS * LANES)
    rope_lane = (lane >= QK_NOPE) & (lane < QK_NOPE + QK_ROPE)
    eplace = ((col[None, :] % LANES == lane[:, None]) & rope_lane[:, None]).astype(BF16)
    mix_col = jnp.arange(D)
    gsum = (mix_col[:, None] // GROUP_DIM == lane[None, :]).astype(BF16)
    gexp = (lane[:, None] == mix_col[None, :] // GROUP_DIM).astype(BF16)

    h = x.reshape(T, D)
    for l in range(depth):
        o1 = Q_RANK
        o2 = o1 + KV_RANK
        o3 = o2 + QK_ROPE
        o4 = o3 + CONV_DIM
        o5 = o4 + CONV_DIM
        wl = w_in[l]
        w_kr = wl[:, o2:o3]
        w_krr = jnp.concatenate([w_kr[:, half:], w_kr[:, :half]], axis=1)
        pad_lo = jnp.zeros((D, QK_NOPE), wl.dtype)
        pad_hi = jnp.zeros((D, LANES - QK_NOPE - QK_ROPE), wl.dtype)
        w1 = jnp.concatenate([wl[:, :o1], wl[:, o1:o2], pad_lo, w_kr, pad_hi, pad_lo, w_krr, pad_hi,
                              wl[:, o3:o4], wl[:, o4:o5], wl[:, o5:]], axis=1).astype(BF16)
        qw = QK_NOPE + QK_ROPE
        wq = _head_blocks(w_uq[l], MLA_HEADS, qw, [(0, qw, 0)]).astype(BF16)
        wqr = _head_blocks(w_uq[l], MLA_HEADS, qw,
                           [(QK_NOPE + half, qw, QK_NOPE), (QK_NOPE, QK_NOPE + half, QK_NOPE + half)]).astype(BF16)
        kvw = QK_NOPE + V_HEAD
        wk = _head_blocks(w_ukv[l], MLA_HEADS, kvw, [(0, QK_NOPE, 0)]).astype(BF16)
        v_cols = w_ukv[l].reshape(KV_RANK, MLA_HEADS, kvw)[:, :, QK_NOPE:]
        v_pad = jnp.zeros_like(v_cols)
        odd_head = (jnp.arange(MLA_HEADS) % 2 == 1)[None, :, None]
        wv = jnp.where(odd_head, jnp.concatenate([v_pad, v_cols], axis=-1),
                       jnp.concatenate([v_cols, v_pad], axis=-1)).reshape(KV_RANK, MLA_HEADS * LANES).astype(BF16)
        vone = jnp.stack([(lane == _ones_lane(hd)).astype(F32) for hd in range(MLA_HEADS)]).reshape(1, -1)

        q, k, v, z, gb = _mixer_in(h, g_mix[l][None, :], w1, g_q[l][None, :], wq, wqr, g_kv[l][None, :], wk, wv,
                                   vone, eplace, cos_t, sin_t)
        o = _mla_attn(q.reshape(B, S, -1), k.reshape(B, S, -1), v.reshape(B, S, -1))
        h = _mixer_out(o.reshape(T, -1), z, gb, h, conv_w[l], g_out[l][None, :], gsum, gexp,
                       w_o[l].astype(BF16), S)

        kx, vx = _mem_kv(mem, g_mem[l][None, :], w_xkv[l].astype(BF16))
        h = _xattn(h.reshape(B, S, D), g_x[l][None, :], w_xq[l].astype(BF16), kx, vx,
                   w_xo[l].astype(BF16)).reshape(T, D)

        sk = sub_keys[l]
        zk = jnp.zeros_like(sk[:, 0])
        keys_bd = jnp.concatenate([jnp.concatenate([sk[:, 0], zk], axis=-1),
                                   jnp.concatenate([zk, sk[:, 1]], axis=-1)], axis=1)
        xn, r2, p, n, coef = _peer_route(h, g_ffn[l][None, :], w_pq[l].astype(BF16), keys_bd.astype(BF16))
        u_pack, vt_pack = _pack_experts(u_experts[l], v_experts[l])
        h = _peer_ffn(xn, u_pack, vt_pack, r2, p, n, coef, h,
                      g_final[None, :], final_norm=(l == depth - 1))
    return h.reshape(B, S, D)
```

```python
import functools
import math

import jax
import jax.numpy as jnp
from jax import lax
from jax.experimental import pallas as pl
from jax.experimental.pallas import tpu as pltpu

F32 = jnp.float32
BF16 = jnp.bfloat16

EPS = 1e-6
LANES = 128
SUBLANES = 8
VMEM_LIMIT = 56 * 1024 * 1024

MLA_HEADS = 8
QK_NOPE = 64
QK_ROPE = 32
V_HEAD = 64
Q_RANK = 384
KV_RANK = 256
CONV_DIM = 512
GROUP_DIM = 64
ROPE_THETA = 10000.0
X_HEADS = 4
PEER_HEADS = 8
PEER_KEYS = 128
PEER_TOPK = 16

TS_IN = 1024
TS_X = 1024
ROW_SPLIT = 4
TQ = 512
TS_ROUTE = 1024
TT_FFN = 512
EB_FFN = 2048
SUB_FFN = 1024
GATE_UNIT_SPLIT = (8, 16, 40, 0)
NT_DIMS = (((1,), (1,)), ((), ()))


def _rms(x, g):
    return x * lax.rsqrt(jnp.mean(x * x, axis=-1, keepdims=True) + EPS) * g


def _split_bf16(x):
    hi = x.astype(BF16)
    lo = (x - hi.astype(F32)).astype(BF16)
    return hi, lo


def _gelu_tanh(x):
    c0 = math.sqrt(2.0 / math.pi)
    half_x = 0.5 * x
    return half_x + half_x * jnp.tanh(x * (c0 + (c0 * 0.044715) * (x * x)))


def _dot(a, b):
    return jnp.dot(a, b, preferred_element_type=F32)


def _dot_nt(a, b):
    return lax.dot_general(a, b, NT_DIMS, preferred_element_type=F32)


_C_CQ = 0
_C_CKV = _C_CQ + Q_RANK
_C_KR = _C_CKV + KV_RANK
_C_KRR = _C_KR + LANES
_C_GB = _C_KRR + LANES
_C_GC = _C_GB + CONV_DIM
_C_HX = _C_GC + CONV_DIM
_C_END = _C_HX + CONV_DIM


def _mixer_in_kernel(x_ref, gmix_ref, w1_ref, gq_ref, wq_ref, wqr_ref, gkv_ref, wk_ref, wv_ref, vone_ref,
                     eplace_ref, cos_ref, sin_ref,
                     q_out, k_out, v_out, z_out, gb_out):
    xn = _rms(x_ref[...], gmix_ref[...]).astype(BF16)
    proj = _dot(xn, w1_ref[...])
    cq = proj[:, _C_CQ:_C_CKV]
    ckv = proj[:, _C_CKV:_C_KR]
    kr = proj[:, _C_KR:_C_KRR]
    krr = proj[:, _C_KRR:_C_GB]
    gb_out[...] = proj[:, _C_GB:_C_GC]
    z_out[...] = proj[:, _C_GC:_C_HX] * proj[:, _C_HX:_C_END]

    cqn = _rms(cq, gq_ref[...]).astype(BF16)
    q_raw = _dot(cqn, wq_ref[...])
    q_rot = _dot(cqn, wqr_ref[...])
    cos_t = cos_ref[...]
    sin_t = sin_ref[...]
    q_scale = math.log2(math.e) / math.sqrt(QK_NOPE + QK_ROPE)
    for h in range(MLA_HEADS):
        sl = slice(h * LANES, (h + 1) * LANES)
        q_out[:, sl] = ((q_raw[:, sl] * cos_t + q_rot[:, sl] * sin_t) * q_scale).astype(BF16)

    ckvn = _rms(ckv, gkv_ref[...]).astype(BF16)
    kr_roped = (kr * cos_t + krr * sin_t).astype(BF16)
    k_out[...] = (_dot(ckvn, wk_ref[...]) + _dot(kr_roped, eplace_ref[...])).astype(BF16)
    v_out[...] = (_dot(ckvn, wv_ref[...]) + vone_ref[...]).astype(BF16)


def _mixer_in(x2, gmix, w1, gq, wq, wqr, gkv, wk, wv, vone, eplace, cos_t, sin_t):
    T, D = x2.shape
    ts = min(TS_IN, T)
    row = lambda i: (i, 0)
    fixed = lambda i: (0, 0)
    full = lambda a: pl.BlockSpec(a.shape, fixed)
    return pl.pallas_call(
        _mixer_in_kernel,
        grid=(T // ts,),
        in_specs=[pl.BlockSpec((ts, D), row), full(gmix), full(w1), full(gq), full(wq), full(wqr),
                  full(gkv), full(wk), full(wv), full(vone), full(eplace),
                  pl.BlockSpec((ts, LANES), row), pl.BlockSpec((ts, LANES), row)],
        out_specs=[pl.BlockSpec((ts, MLA_HEADS * LANES), row), pl.BlockSpec((ts, MLA_HEADS * LANES), row),
                   pl.BlockSpec((ts, MLA_HEADS * LANES), row), pl.BlockSpec((ts, CONV_DIM), row),
                   pl.BlockSpec((ts, CONV_DIM), row)],
        out_shape=[jax.ShapeDtypeStruct((T, MLA_HEADS * LANES), BF16),
                   jax.ShapeDtypeStruct((T, MLA_HEADS * LANES), BF16),
                   jax.ShapeDtypeStruct((T, MLA_HEADS * LANES), BF16),
                   jax.ShapeDtypeStruct((T, CONV_DIM), F32),
                   jax.ShapeDtypeStruct((T, CONV_DIM), F32)],
        compiler_params=pltpu.CompilerParams(dimension_semantics=("arbitrary",), vmem_limit_bytes=VMEM_LIMIT),
        name="mixer_in",
    )(x2, gmix, w1, gq, wq, wqr, gkv, wk, wv, vone, eplace, cos_t, sin_t)


def _ones_lane(head):
    return V_HEAD if head % 2 == 0 else 0


def _mla_attn_kernel(q_ref, k_ref, v_ref, o_ref, *, tq):
    seq = q_ref.shape[0]
    causal = (lax.broadcasted_iota(jnp.int32, (tq, tq), 1) <= lax.broadcasted_iota(jnp.int32, (tq, tq), 0))
    lane = lax.broadcasted_iota(jnp.int32, (tq, LANES), 1)
    n_q = seq // tq
    head_lanes = [slice(hh * LANES, (hh + 1) * LANES) for hh in range(2)]

    def qk(qi):
        return [_dot_nt(q_ref[qi * tq:(qi + 1) * tq, hl], k_ref[0:(qi + 1) * tq, hl]) for hl in head_lanes]

    scores = qk(0)
    for qi in range(n_q):
        rows = slice(qi * tq, (qi + 1) * tq)
        keys = slice(0, (qi + 1) * tq)
        cur = scores
        if qi + 1 < n_q:
            scores = qk(qi + 1)
        outs = []
        for hh, hl in enumerate(head_lanes):
            s = cur[hh]
            s_diag = jnp.where(causal, s[:, qi * tq:], -jnp.inf)
            s = s_diag if qi == 0 else jnp.concatenate([s[:, :qi * tq], s_diag], axis=1)
            m = jnp.max(s, axis=-1, keepdims=True)
            acc = _dot(jnp.exp2(s - m).astype(BF16), v_ref[keys, hl])
            one = _ones_lane(hh)
            outs.append(acc * (1.0 / acc[:, one:one + 1]))
        o_ref[rows, :] = jnp.where(lane < V_HEAD, outs[0], outs[1])


def _mla_attn(q3, k3, v3):
    B, S, _ = q3.shape
    tq = min(TQ, S)
    pair = lambda b, g: (b, 0, g)
    return pl.pallas_call(
        functools.partial(_mla_attn_kernel, tq=tq),
        grid=(B, MLA_HEADS // 2),
        in_specs=[pl.BlockSpec((None, S, 2 * LANES), pair), pl.BlockSpec((None, S, 2 * LANES), pair),
                  pl.BlockSpec((None, S, 2 * LANES), pair)],
        out_specs=pl.BlockSpec((None, S, 2 * V_HEAD), pair),
        out_shape=jax.ShapeDtypeStruct((B, S, MLA_HEADS * V_HEAD), F32),
        compiler_params=pltpu.CompilerParams(dimension_semantics=("arbitrary", "arbitrary"),
                                             vmem_limit_bytes=VMEM_LIMIT),
        name="mla_attn",
    )(q3, k3, v3)


def _mixer_out_kernel(o_ref, z_ref, zh_ref, gb_ref, x_ref, cw_ref, gout_ref, gsum_ref, gexp_ref, wo_ref,
                      h_out, *, tiles_per_seq):
    i = pl.program_id(0)
    ts = z_ref.shape[0]
    z = z_ref[...]
    halo = jnp.where(i % tiles_per_seq == 0, 0.0, zh_ref[...])
    row = lax.broadcasted_iota(jnp.int32, z.shape, 0)
    z1 = jnp.where(row == 0, halo[7:8, :], pltpu.roll(z, 1, axis=0))
    z2 = jnp.where(row == 0, halo[6:7, :], jnp.where(row == 1, halo[7:8, :], pltpu.roll(z, 2, axis=0)))
    cw = cw_ref[...]
    y_conv = gb_ref[...] * (cw[0:1, :] * z2 + cw[1:2, :] * z1 + cw[2:3, :] * z)
    y_all = jnp.concatenate([o_ref[...], y_conv], axis=-1)
    for part in range(ROW_SPLIT):
        rows = slice(part * ts // ROW_SPLIT, (part + 1) * ts // ROW_SPLIT)
        y = y_all[rows]
        sq_hi, sq_lo = _split_bf16(y * y)
        gs = _dot(sq_hi, gsum_ref[...]) + _dot(sq_lo, gsum_ref[...])
        r = lax.rsqrt(gs * (1.0 / GROUP_DIM) + EPS)
        r_hi, r_lo = _split_bf16(r)
        r_full = _dot(r_hi, gexp_ref[...]) + _dot(r_lo, gexp_ref[...])
        yn = (y * r_full * gout_ref[...]).astype(BF16)
        h_out[rows, :] = x_ref[rows, :] + _dot(yn, wo_ref[...])


def _mixer_out(o2, z, gb, x2, conv_w, gout, gsum, gexp, wo, seq):
    T, D = x2.shape
    ts = min(TS_IN, seq)
    row = lambda i: (i, 0)
    fixed = lambda i: (0, 0)
    full = lambda a: pl.BlockSpec(a.shape, fixed)
    halo_blocks = ts // SUBLANES
    return pl.pallas_call(
        functools.partial(_mixer_out_kernel, tiles_per_seq=seq // ts),
        grid=(T // ts,),
        in_specs=[pl.BlockSpec((ts, MLA_HEADS * V_HEAD), row), pl.BlockSpec((ts, CONV_DIM), row),
                  pl.BlockSpec((SUBLANES, CONV_DIM), lambda i: (jnp.maximum(i * halo_blocks - 1, 0), 0)),
                  pl.BlockSpec((ts, CONV_DIM), row), pl.BlockSpec((ts, D), row),
                  full(conv_w), full(gout), full(gsum), full(gexp), full(wo)],
        out_specs=pl.BlockSpec((ts, D), row),
        out_shape=jax.ShapeDtypeStruct((T, D), F32),
        compiler_params=pltpu.CompilerParams(dimension_semantics=("arbitrary",), vmem_limit_bytes=VMEM_LIMIT),
        name="mixer_out",
    )(o2, z, z, gb, x2, conv_w, gout, gsum, gexp, wo)


def _mem_kv_kernel(mem_ref, g_ref, w_ref, k_out, v_out):
    d = mem_ref.shape[-1]
    mn = _rms(mem_ref[...], g_ref[...]).astype(BF16)
    kv = _dot(mn, w_ref[...])
    k_out[...] = kv[:, :d].astype(BF16)
    v_out[...] = kv[:, d:].astype(BF16)


def _mem_kv(mem, g, w):
    B, M, D = mem.shape
    return pl.pallas_call(
        _mem_kv_kernel,
        grid=(B,),
        in_specs=[pl.BlockSpec((None, M, D), lambda b: (b, 0, 0)), pl.BlockSpec(g.shape, lambda b: (0, 0)),
                  pl.BlockSpec(w.shape, lambda b: (0, 0))],
        out_specs=[pl.BlockSpec((None, M, D), lambda b: (b, 0, 0)), pl.BlockSpec((None, M, D), lambda b: (b, 0, 0))],
        out_shape=[jax.ShapeDtypeStruct((B, M, D), BF16), jax.ShapeDtypeStruct((B, M, D), BF16)],
        compiler_params=pltpu.CompilerParams(dimension_semantics=("arbitrary",), vmem_limit_bytes=VMEM_LIMIT),
        name="mem_kv",
    )(mem, g, w)


def _xattn_kernel(h_ref, g_ref, wq_ref, k_ref, v_ref, wo_ref, h_out):
    h = h_ref[...]
    d = h.shape[-1]
    hd = d // X_HEADS
    hn = _rms(h, g_ref[...]).astype(BF16)
    q = _dot(hn, wq_ref[...]).astype(BF16)
    outs = []
    for hh in range(X_HEADS):
        sl = slice(hh * hd, (hh + 1) * hd)
        s = _dot_nt(q[:, sl], k_ref[:, sl]) * (1.0 / math.sqrt(hd))
        m = jnp.max(s, axis=-1, keepdims=True)
        p = jnp.exp(s - m)
        p = p * (1.0 / jnp.sum(p, axis=-1, keepdims=True))
        outs.append(_dot(p.astype(BF16), v_ref[:, sl]))
    o = jnp.concatenate(outs, axis=-1).astype(BF16)
    h_out[...] = h + _dot(o, wo_ref[...])


def _xattn(h3, g, wq, kx, vx, wo):
    B, S, D = h3.shape
    M = kx.shape[1]
    ts = min(TS_X, S)
    fixed = lambda b, i: (0, 0)
    return pl.pallas_call(
        _xattn_kernel,
        grid=(B, S // ts),
        in_specs=[pl.BlockSpec((None, ts, D), lambda b, i: (b, i, 0)), pl.BlockSpec(g.shape, fixed),
                  pl.BlockSpec(wq.shape, fixed), pl.BlockSpec((None, M, D), lambda b, i: (b, 0, 0)),
                  pl.BlockSpec((None, M, D), lambda b, i: (b, 0, 0)), pl.BlockSpec(wo.shape, fixed)],
        out_specs=pl.BlockSpec((None, ts, D), lambda b, i: (b, i, 0)),
        out_shape=jax.ShapeDtypeStruct((B, S, D), F32),
        compiler_params=pltpu.CompilerParams(dimension_semantics=("arbitrary", "arbitrary"),
                                             vmem_limit_bytes=VMEM_LIMIT),
        name="xattn",
    )(h3, g, wq, kx, vx, wo)


def _batcher_pairs(n):
    pairs = []
    p = 1
    while p < n:
        k = p
        while k >= 1:
            for j in range(k % p, n - k, 2 * k):
                for i in range(min(k, n - j - k)):
                    if (i + j) // (2 * p) == (i + j + k) // (2 * p):
                        pairs.append((i + j, i + j + k))
            k //= 2
        p *= 2
    return pairs


_SORT16 = _batcher_pairs(PEER_TOPK)
_ROW_LEN = [PEER_TOPK // (a + 1) for a in range(PEER_TOPK)]


def _sort_desc(v):
    v = list(v)
    for i, j in _SORT16:
        hi = jnp.maximum(v[i], v[j])
        lo = jnp.minimum(v[i], v[j])
        v[i], v[j] = hi, lo
    return v


def _bitonic_desc(v):
    v = list(v)
    n = len(v)
    d = n // 2
    while d >= 1:
        for k in range(n):
            if k & d == 0:
                hi = jnp.maximum(v[k], v[k + d])
                lo = jnp.minimum(v[k], v[k + d])
                v[k], v[k + d] = hi, lo
        d //= 2
    return v


def _merge_top(cur, other):
    n = len(cur)
    c = list(cur)
    for r, val in enumerate(other):
        c[n - 1 - r] = jnp.maximum(c[n - 1 - r], val)
    return _bitonic_desc(c)


def _peer_route_kernel(h_ref, g_ref, wq_ref, key_ref,
                       xn_out, r2_out, p_out, n_out, c_out,
                       st_ref, top_ref, sort_ref, res_ref):
    ts = h_ref.shape[0]
    n_chunk = ts // LANES
    hn = _rms(h_ref[...], g_ref[...]).astype(BF16)
    xn_out[...] = hn
    q = _dot(hn, wq_ref[...])
    for h in range(PEER_HEADS):
        st = _dot_nt(key_ref[h], q[:, h * LANES:(h + 1) * LANES].astype(BF16))
        for c in range(n_chunk):
            st_ref[c, h] = st[:, c * LANES:(c + 1) * LANES]


    def chunk_body(c, _):
        for half in range(2):
            def sort_body(h, _, half=half):
                s = st_ref[c, h, pl.ds(half * PEER_KEYS, PEER_KEYS), :].reshape(PEER_TOPK, SUBLANES, LANES)
                v = _sort_desc([s[k] for k in range(PEER_TOPK)])
                row0 = pl.multiple_of(h * SUBLANES, SUBLANES)
                for k in range(PEER_TOPK):
                    sort_ref[k, pl.ds(row0, SUBLANES), :] = v[k]
                return 0

            lax.fori_loop(0, PEER_HEADS, sort_body, 0)
            lists = [[sort_ref[k, pl.ds(s, PEER_HEADS, stride=SUBLANES), :] for k in range(PEER_TOPK)]
                     for s in range(SUBLANES)]
            while len(lists) > 1:
                lists = [_merge_top(lists[i], lists[i + 1]) for i in range(0, len(lists), 2)]
            for a in range(PEER_TOPK):
                top_ref[half, a] = lists[0][a]

        v1 = [top_ref[0, a] for a in range(PEER_TOPK)]
        v2 = [top_ref[1, b] for b in range(PEER_TOPK)]
        sums = [[v1[a] + v2[b] for b in range(_ROW_LEN[a])] for a in range(PEER_TOPK)]
        cur = sums[0]
        a = 1
        while _ROW_LEN[a] > 1:
            cur = _merge_top(cur, sums[a])
            a += 1
        cur = _merge_top(cur, [sums[r][0] for r in range(a, PEER_TOPK)])
        tau = cur[PEER_TOPK - 1]
        top_sum = sums[0][0]
        z = jnp.zeros_like(tau)
        for a in range(PEER_TOPK):
            cnt = jnp.zeros_like(tau)
            for b in range(_ROW_LEN[a]):
                sel = sums[a][b] >= tau
                cnt = cnt + jnp.where(sel, 1.0, 0.0)
                z = z + jnp.where(sel, jnp.exp(sums[a][b] - top_sum), 0.0)
            res_ref[a] = cnt
        res_ref[PEER_TOPK] = 1.0 / z

        def expand_body(h, _):
            s1 = st_ref[c, h, pl.ds(0, PEER_KEYS), :].reshape(PEER_TOPK, SUBLANES, LANES)
            s2 = st_ref[c, h, pl.ds(PEER_KEYS, PEER_KEYS), :].reshape(PEER_TOPK, SUBLANES, LANES)
            n = jnp.zeros(s1.shape, F32)
            r2 = jnp.full(s2.shape, float(PEER_TOPK), F32)
            for a in range(PEER_TOPK - 1, -1, -1):
                v1a = top_ref[0, a, pl.ds(h, 1), :]
                v2a = top_ref[1, a, pl.ds(h, 1), :]
                n = jnp.where(s1 == v1a, res_ref[a, pl.ds(h, 1), :], n)
                r2 = jnp.where(s2 == v2a, float(a), r2)
            m1 = top_ref[0, 0, pl.ds(h, 1), :]
            m2 = top_ref[1, 0, pl.ds(h, 1), :]
            inv_z = res_ref[PEER_TOPK, pl.ds(h, 1), :]
            n_out[c, h] = n.reshape(PEER_KEYS, LANES)
            c_out[c, h] = (jnp.exp(s1 - m1) * inv_z).reshape(PEER_KEYS, LANES)
            r2_out[c, h] = r2.reshape(PEER_KEYS, LANES).astype(BF16)
            p_out[c, h] = jnp.exp(s2 - m2).reshape(PEER_KEYS, LANES).astype(BF16)
            return 0

        lax.fori_loop(0, PEER_HEADS, expand_body, 0)
        return 0

    lax.fori_loop(0, n_chunk, chunk_body, 0)


def _peer_route(h2, g, wq, keys):
    T, D = h2.shape
    ts = min(TS_ROUTE, T)
    nc = ts // LANES
    aux_spec = pl.BlockSpec((nc, PEER_HEADS, PEER_KEYS, LANES), lambda i: (i, 0, 0, 0))
    aux_shape = jax.ShapeDtypeStruct((T // LANES, PEER_HEADS, PEER_KEYS, LANES), F32)
    aux_shape_bf = jax.ShapeDtypeStruct((T // LANES, PEER_HEADS, PEER_KEYS, LANES), BF16)
    fixed2 = lambda i: (0, 0)
    fixed3 = lambda i: (0, 0, 0)
    return pl.pallas_call(
        _peer_route_kernel,
        grid=(T // ts,),
        in_specs=[pl.BlockSpec((ts, D), lambda i: (i, 0)), pl.BlockSpec(g.shape, fixed2),
                  pl.BlockSpec(wq.shape, fixed2), pl.BlockSpec(keys.shape, fixed3)],
        out_specs=[pl.BlockSpec((ts, D), lambda i: (i, 0)), aux_spec, aux_spec, aux_spec, aux_spec],
        out_shape=[jax.ShapeDtypeStruct((T, D), BF16), aux_shape_bf, aux_shape_bf, aux_shape, aux_shape],
        scratch_shapes=[pltpu.VMEM((nc, PEER_HEADS, 2 * PEER_KEYS, LANES), F32),
                        pltpu.VMEM((2, PEER_TOPK, SUBLANES, LANES), F32),
                        pltpu.VMEM((PEER_TOPK, PEER_HEADS * SUBLANES, LANES), F32),
                        pltpu.VMEM((PEER_TOPK + 1, SUBLANES, LANES), F32)],
        compiler_params=pltpu.CompilerParams(dimension_semantics=("arbitrary",), vmem_limit_bytes=VMEM_LIMIT),
        name="peer_route",
    )(h2, g, wq, keys)


def _pack_experts_kernel(u_ref, v_ref, u_out, vt_out):
    u_out[...] = pltpu.bitcast(u_ref[...].astype(BF16), jnp.uint32)
    vt_out[...] = pltpu.bitcast(v_ref[...].T.astype(BF16), jnp.uint32)


def _pack_experts(u, v):
    E, D = u.shape
    eb = EB_FFN
    return pl.pallas_call(
        _pack_experts_kernel,
        grid=(E // eb,),
        in_specs=[pl.BlockSpec((eb, D), lambda e: (e, 0)), pl.BlockSpec((eb, D), lambda e: (e, 0))],
        out_specs=[pl.BlockSpec((eb // 2, D), lambda e: (e, 0)), pl.BlockSpec((D // 2, eb), lambda e: (0, e))],
        out_shape=[jax.ShapeDtypeStruct((E // 2, D), jnp.uint32), jax.ShapeDtypeStruct((D // 2, E), jnp.uint32)],
        compiler_params=pltpu.CompilerParams(dimension_semantics=("arbitrary",), vmem_limit_bytes=VMEM_LIMIT),
        name="pack_experts",
    )(u, v)


def _peer_gate_unit(tc, ii, at_ref, ht_ref, r2s_ref, ps_ref, n_ref, c_ref):
    pack = 2 * SUBLANES
    n_jv = PEER_KEYS // pack
    zero = jnp.zeros((pack, LANES), BF16)
    lanes = slice(tc * LANES, (tc + 1) * LANES)
    g = [None] * n_jv
    for h in range(PEER_HEADS):
        n_b = jnp.broadcast_to(n_ref[tc, h, ii:ii + 1, :], (pack, LANES)).astype(BF16)
        c_b = jnp.broadcast_to(c_ref[tc, h, ii:ii + 1, :], (pack, LANES)).astype(BF16)
        for jv in range(n_jv):
            js = slice(jv * pack, (jv + 1) * pack)
            term = jnp.where(r2s_ref[tc, h, js, :] < n_b, ps_ref[tc, h, js, :], zero) * c_b
            g[jv] = term if g[jv] is None else g[jv] + term
    for jv in range(n_jv):
        rows = slice(ii * PEER_KEYS + jv * pack, ii * PEER_KEYS + (jv + 1) * pack)
        ht_ref[rows, lanes] = _gelu_tanh(at_ref[rows, lanes]).astype(BF16) * g[jv]


def _peer_ffn_kernel(xn_ref, u_ref, vt_ref, r2_ref, p_ref, n_ref, c_ref, h_ref, gfin_ref,
                     out_ref, acc_ref, at0_ref, at1_ref, ht0_ref, ht1_ref, r2s_ref, ps_ref, xs_ref,
                     *, n_e, n_blocks, final_norm):
    g = pl.program_id(0)
    tt = xn_ref.shape[0]
    e_score = g % n_e
    e_gate = jnp.maximum(g - 1, 0) % n_e
    e_down = jnp.maximum(g - 2, 0) % n_e

    @pl.when(g == 0)
    def _():
        at1_ref[...] = jnp.zeros_like(at1_ref)
        ht0_ref[...] = jnp.zeros_like(ht0_ref)
        ht1_ref[...] = jnp.zeros_like(ht1_ref)
        acc_ref[...] = jnp.zeros_like(acc_ref)

    @pl.when((g < n_blocks) & (e_score == 0))
    def _():
        xs_ref[...] = xn_ref[...]

    @pl.when((g <= n_blocks) & (e_gate == 0))
    def _():
        for tc in range(tt // LANES):
            for h in range(PEER_HEADS):
                r2s_ref[tc, h] = r2_ref[tc, h]
                ps_ref[tc, h] = p_ref[tc, h]

    @pl.when((g >= 2) & (e_down == 0))
    def _():
        acc_ref[...] = jnp.zeros_like(acc_ref)

    def stages(at_w, at_r, ht_w, ht_r):
        u_blk = pltpu.bitcast(u_ref[...], BF16)
        vt_blk = pltpu.bitcast(vt_ref[...], BF16)
        eb = u_blk.shape[0]
        subs = [slice(sb * SUB_FFN, (sb + 1) * SUB_FFN) for sb in range(eb // SUB_FFN)]

        def score(ex):
            at_w[ex, :] = _dot_nt(u_blk[ex], xs_ref[...])

        def down():
            acc_ref[...] += _dot(vt_blk, ht_r[...])

        chunks = [functools.partial(score, ex) for ex in subs] + [down]
        units = [(tc, ii) for tc in range(tt // LANES) for ii in range(eb // PEER_KEYS)]
        split = GATE_UNIT_SPLIT
        assert len(split) == len(chunks) + 1 and sum(split) == len(units)
        bounds = [sum(split[:k]) for k in range(len(split) + 1)]

        def gate_units(k):
            for tc, ii in units[bounds[k]:bounds[k + 1]]:
                _peer_gate_unit(tc, ii, at_r, ht_w, r2s_ref, ps_ref, n_ref, c_ref)

        gate_units(0)
        for k, chunk in enumerate(chunks):
            chunk()
            gate_units(k + 1)

    @pl.when(g % 2 == 0)
    def _():
        stages(at0_ref, at1_ref, ht1_ref, ht0_ref)

    @pl.when(g % 2 == 1)
    def _():
        stages(at1_ref, at0_ref, ht0_ref, ht1_ref)

    @pl.when((g >= 2) & (e_down == n_e - 1))
    def _():
        res = h_ref[...] + acc_ref[...].T
        out_ref[...] = _rms(res, gfin_ref[...]) if final_norm else res


def _peer_ffn(xn, u_pack, vt_pack, r2, p, n, coef, h2, gfin, final_norm):
    T, D = h2.shape
    E = vt_pack.shape[1]
    tt = min(TT_FFN, T)
    nc = tt // LANES
    eb = EB_FFN
    n_i = eb // PEER_KEYS
    n_e = E // eb
    n_blocks = (T // tt) * n_e

    def block(lag):
        def split(g):
            b = jnp.clip(g - lag, 0, n_blocks - 1)
            return b // n_e, b % n_e
        return split

    score, gate, down = block(0), block(1), block(2)
    aux_shape = (nc, PEER_HEADS, PEER_KEYS, LANES)
    row_shape = (nc, PEER_HEADS, n_i, LANES)
    return pl.pallas_call(
        functools.partial(_peer_ffn_kernel, n_e=n_e, n_blocks=n_blocks, final_norm=final_norm),
        grid=(n_blocks + 2,),
        in_specs=[pl.BlockSpec((tt, D), lambda g: (score(g)[0], 0)),
                  pl.BlockSpec((eb // 2, D), lambda g: (score(g)[1], 0)),
                  pl.BlockSpec((D // 2, eb), lambda g: (0, down(g)[1])),
                  pl.BlockSpec(aux_shape, lambda g: (gate(g)[0], 0, 0, 0)),
                  pl.BlockSpec(aux_shape, lambda g: (gate(g)[0], 0, 0, 0)),
                  pl.BlockSpec(row_shape, lambda g: (gate(g)[0], 0, gate(g)[1], 0)),
                  pl.BlockSpec(row_shape, lambda g: (gate(g)[0], 0, gate(g)[1], 0)),
                  pl.BlockSpec((tt, D), lambda g: (down(g)[0], 0)),
                  pl.BlockSpec(gfin.shape, lambda g: (0, 0))],
        out_specs=pl.BlockSpec((tt, D), lambda g: (down(g)[0], 0)),
        out_shape=jax.ShapeDtypeStruct((T, D), F32),
        scratch_shapes=[pltpu.VMEM((D, tt), F32),
                        pltpu.VMEM((eb, tt), F32), pltpu.VMEM((eb, tt), F32),
                        pltpu.VMEM((eb, tt), BF16), pltpu.VMEM((eb, tt), BF16),
                        pltpu.VMEM(aux_shape, BF16), pltpu.VMEM(aux_shape, BF16),
                        pltpu.VMEM((tt, D), BF16)],
        compiler_params=pltpu.CompilerParams(dimension_semantics=("arbitrary",), vmem_limit_bytes=VMEM_LIMIT),
        name="peer_ffn",
    )(xn, u_pack, vt_pack, r2, p, n, coef, h2, gfin)


def _head_blocks(w, n_heads, width, pieces):
    w3 = w.reshape(w.shape[0], n_heads, width)
    out = jnp.zeros((w.shape[0], n_heads, LANES), w.dtype)
    for s0, s1, d0 in pieces:
        out = out.at[:, :, d0:d0 + (s1 - s0)].set(w3[:, :, s0:s1])
    return out.reshape(w.shape[0], n_heads * LANES)


def kernel(x, mem, positions, g_mix, w_in, g_q, w_uq, g_kv, w_ukv, conv_w, g_out, w_o, g_x, g_mem, w_xq,
           w_xkv, w_xo, g_ffn, w_pq, sub_keys, u_experts, v_experts, g_final):
    B, S, D = x.shape
    T = B * S
    depth = g_mix.shape[0]
    half = QK_ROPE // 2
    assert S % min(TS_IN, S) == 0 and S % min(TS_X, S) == 0 and S % min(TQ, S) == 0 and D == MLA_HEADS * LANES
    assert T % min(TS_ROUTE, T) == 0 and T % min(TT_FFN, T) == 0 and u_experts.shape[1] % EB_FFN == 0

    inv = ROPE_THETA ** (-jnp.arange(0, QK_ROPE, 2, dtype=F32) / QK_ROPE)
    ang = positions.astype(F32)[..., None] * inv
    cos = jnp.cos(ang).astype(x.dtype).reshape(T, half)
    sin = jnp.sin(ang).astype(x.dtype).reshape(T, half)
    ones = jnp.ones((T, QK_NOPE), F32)
    zeros_n = jnp.zeros((T, QK_NOPE), F32)
    pad_q = jnp.zeros((T, LANES - QK_NOPE - QK_ROPE), F32)
    cos_t = jnp.concatenate([ones, cos, cos, pad_q], axis=1)
    sin_t = jnp.concatenate([zeros_n, -sin, sin, pad_q], axis=1)

    lane = jnp.arange(LANES)
    col = jnp.arange(MLA_HEADS * LANES)
    rope_lane = (lane >= QK_NOPE) & (lane < QK_NOPE + QK_ROPE)
    eplace = ((col[None, :] % LANES == lane[:, None]) & rope_lane[:, None]).astype(BF16)
    mix_col = jnp.arange(D)
    gsum = (mix_col[:, None] // GROUP_DIM == lane[None, :]).astype(BF16)
    gexp = (lane[:, None] == mix_col[None, :] // GROUP_DIM).astype(BF16)

    h = x.reshape(T, D)
    for l in range(depth):
        o1 = Q_RANK
        o2 = o1 + KV_RANK
        o3 = o2 + QK_ROPE
        o4 = o3 + CONV_DIM
        o5 = o4 + CONV_DIM
        wl = w_in[l]
        w_kr = wl[:, o2:o3]
        w_krr = jnp.concatenate([w_kr[:, half:], w_kr[:, :half]], axis=1)
        pad_lo = jnp.zeros((D, QK_NOPE), wl.dtype)
        pad_hi = jnp.zeros((D, LANES - QK_NOPE - QK_ROPE), wl.dtype)
        w1 = jnp.concatenate([wl[:, :o1], wl[:, o1:o2], pad_lo, w_kr, pad_hi, pad_lo, w_krr, pad_hi,
                              wl[:, o3:o4], wl[:, o4:o5], wl[:, o5:]], axis=1).astype(BF16)
        qw = QK_NOPE + QK_ROPE
        wq = _head_blocks(w_uq[l], MLA_HEADS, qw, [(0, qw, 0)]).astype(BF16)
        wqr = _head_blocks(w_uq[l], MLA_HEADS, qw,
                           [(QK_NOPE + half, qw, QK_NOPE), (QK_NOPE, QK_NOPE + half, QK_NOPE + half)]).astype(BF16)
        kvw = QK_NOPE + V_HEAD
        wk = _head_blocks(w_ukv[l], MLA_HEADS, kvw, [(0, QK_NOPE, 0)]).astype(BF16)
        v_cols = w_ukv[l].reshape(KV_RANK, MLA_HEADS, kvw)[:, :, QK_NOPE:]
        v_pad = jnp.zeros_like(v_cols)
        odd_head = (jnp.arange(MLA_HEADS) % 2 == 1)[None, :, None]
        wv = jnp.where(odd_head, jnp.concatenate([v_pad, v_cols], axis=-1),
                       jnp.concatenate([v_cols, v_pad], axis=-1)).reshape(KV_RANK, MLA_HEADS * LANES).astype(BF16)
        vone = jnp.stack([(lane == _ones_lane(hd)).astype(F32) for hd in range(MLA_HEADS)]).reshape(1, -1)

        q, k, v, z, gb = _mixer_in(h, g_mix[l][None, :], w1, g_q[l][None, :], wq, wqr, g_kv[l][None, :], wk, wv,
                                   vone, eplace, cos_t, sin_t)
        o = _mla_attn(q.reshape(B, S, -1), k.reshape(B, S, -1), v.reshape(B, S, -1))
        h = _mixer_out(o.reshape(T, -1), z, gb, h, conv_w[l], g_out[l][None, :], gsum, gexp,
                       w_o[l].astype(BF16), S)

        kx, vx = _mem_kv(mem, g_mem[l][None, :], w_xkv[l].astype(BF16))
        h = _xattn(h.reshape(B, S, D), g_x[l][None, :], w_xq[l].astype(BF16), kx, vx,
                   w_xo[l].astype(BF16)).reshape(T, D)

        sk = sub_keys[l]
        zk = jnp.zeros_like(sk[:, 0])
        keys_bd = jnp.concatenate([jnp.concatenate([sk[:, 0], zk], axis=-1),
                                   jnp.concatenate([zk, sk[:, 1]], axis=-1)], axis=1)
        xn, r2, p, n, coef = _peer_route(h, g_ffn[l][None, :], w_pq[l].astype(BF16), keys_bd.astype(BF16))
        u_pack, vt_pack = _pack_experts(u_experts[l], v_experts[l])
        h = _peer_ffn(xn, u_pack, vt_pack, r2, p, n, coef, h,
                      g_final[None, :], final_norm=(l == depth - 1))
    return h.reshape(B, S, D)
```

```python
import functools
import math

import jax
import jax.numpy as jnp
from jax import lax
from jax.experimental import pallas as pl
from jax.experimental.pallas import tpu as pltpu

F32 = jnp.float32
BF16 = jnp.bfloat16

EPS = 1e-6
LANES = 128
SUBLANES = 8
VMEM_LIMIT = 56 * 1024 * 1024

MLA_HEADS = 8
QK_NOPE = 64
QK_ROPE = 32
V_HEAD = 64
Q_RANK = 384
KV_RANK = 256
CONV_DIM = 512
GROUP_DIM = 64
ROPE_THETA = 10000.0
X_HEADS = 4
PEER_HEADS = 8
PEER_KEYS = 128
PEER_TOPK = 16

TS_IN = 1024
TS_X = 1024
ROW_SPLIT = 4
ROW_SPLIT_X = 2
TQ = 512
TS_ROUTE = 1024
TT_FFN = 512
EB_FFN = 2048
SUB_FFN = 1024
GATE_UNIT_SPLIT = (8, 16, 40, 0)
NT_DIMS = (((1,), (1,)), ((), ()))


def _rms(x, g):
    return x * lax.rsqrt(jnp.mean(x * x, axis=-1, keepdims=True) + EPS) * g


def _split_bf16(x):
    hi = x.astype(BF16)
    lo = (x - hi.astype(F32)).astype(BF16)
    return hi, lo


def _gelu_tanh(x):
    c0 = math.sqrt(2.0 / math.pi)
    half_x = 0.5 * x
    return half_x + half_x * jnp.tanh(x * (c0 + (c0 * 0.044715) * (x * x)))


def _dot(a, b):
    return jnp.dot(a, b, preferred_element_type=F32)


def _dot_nt(a, b):
    return lax.dot_general(a, b, NT_DIMS, preferred_element_type=F32)


_C_CQ = 0
_C_CKV = _C_CQ + Q_RANK
_C_KR = _C_CKV + KV_RANK
_C_KRR = _C_KR + LANES
_C_GB = _C_KRR + LANES
_C_GC = _C_GB + CONV_DIM
_C_HX = _C_GC + CONV_DIM
_C_END = _C_HX + CONV_DIM


def _mixer_in_kernel(x_ref, gmix_ref, w1_ref, gq_ref, wq_ref, wqr_ref, gkv_ref, wk_ref, wv_ref, vone_ref,
                     eplace_ref, cos_ref, sin_ref,
                     q_out, k_out, v_out, z_out, gb_out):
    xn = _rms(x_ref[...], gmix_ref[...]).astype(BF16)
    proj = _dot(xn, w1_ref[...])
    cq = proj[:, _C_CQ:_C_CKV]
    ckv = proj[:, _C_CKV:_C_KR]
    kr = proj[:, _C_KR:_C_KRR]
    krr = proj[:, _C_KRR:_C_GB]
    gb_out[...] = proj[:, _C_GB:_C_GC]
    z_out[...] = proj[:, _C_GC:_C_HX] * proj[:, _C_HX:_C_END]

    cqn = _rms(cq, gq_ref[...]).astype(BF16)
    q_raw = _dot(cqn, wq_ref[...])
    q_rot = _dot(cqn, wqr_ref[...])
    cos_t = cos_ref[...]
    sin_t = sin_ref[...]
    q_scale = math.log2(math.e) / math.sqrt(QK_NOPE + QK_ROPE)
    for h in range(MLA_HEADS):
        sl = slice(h * LANES, (h + 1) * LANES)
        q_out[:, sl] = ((q_raw[:, sl] * cos_t + q_rot[:, sl] * sin_t) * q_scale).astype(BF16)

    ckvn = _rms(ckv, gkv_ref[...]).astype(BF16)
    kr_roped = (kr * cos_t + krr * sin_t).astype(BF16)
    k_out[...] = (_dot(ckvn, wk_ref[...]) + _dot(kr_roped, eplace_ref[...])).astype(BF16)
    v_out[...] = (_dot(ckvn, wv_ref[...]) + vone_ref[...]).astype(BF16)


def _mixer_in(x2, gmix, w1, gq, wq, wqr, gkv, wk, wv, vone, eplace, cos_t, sin_t):
    T, D = x2.shape
    ts = min(TS_IN, T)
    row = lambda i: (i, 0)
    fixed = lambda i: (0, 0)
    full = lambda a: pl.BlockSpec(a.shape, fixed)
    return pl.pallas_call(
        _mixer_in_kernel,
        grid=(T // ts,),
        in_specs=[pl.BlockSpec((ts, D), row), full(gmix), full(w1), full(gq), full(wq), full(wqr),
                  full(gkv), full(wk), full(wv), full(vone), full(eplace),
                  pl.BlockSpec((ts, LANES), row), pl.BlockSpec((ts, LANES), row)],
        out_specs=[pl.BlockSpec((ts, MLA_HEADS * LANES), row), pl.BlockSpec((ts, MLA_HEADS * LANES), row),
                   pl.BlockSpec((ts, MLA_HEADS * LANES), row), pl.BlockSpec((ts, CONV_DIM), row),
                   pl.BlockSpec((ts, CONV_DIM), row)],
        out_shape=[jax.ShapeDtypeStruct((T, MLA_HEADS * LANES), BF16),
                   jax.ShapeDtypeStruct((T, MLA_HEADS * LANES), BF16),
                   jax.ShapeDtypeStruct((T, MLA_HEADS * LANES), BF16),
                   jax.ShapeDtypeStruct((T, CONV_DIM), F32),
                   jax.ShapeDtypeStruct((T, CONV_DIM), F32)],
        compiler_params=pltpu.CompilerParams(dimension_semantics=("arbitrary",), vmem_limit_bytes=VMEM_LIMIT),
        name="mixer_in",
    )(x2, gmix, w1, gq, wq, wqr, gkv, wk, wv, vone, eplace, cos_t, sin_t)


def _ones_lane(head):
    return V_HEAD if head % 2 == 0 else 0


def _mla_attn_kernel(q_ref, k_ref, v_ref, o_ref, *, tq):
    seq = q_ref.shape[0]
    causal = (lax.broadcasted_iota(jnp.int32, (tq, tq), 1) <= lax.broadcasted_iota(jnp.int32, (tq, tq), 0))
    lane = lax.broadcasted_iota(jnp.int32, (tq, LANES), 1)
    n_q = seq // tq
    head_lanes = [slice(hh * LANES, (hh + 1) * LANES) for hh in range(2)]

    def qk(qi):
        return [_dot_nt(q_ref[qi * tq:(qi + 1) * tq, hl], k_ref[0:(qi + 1) * tq, hl]) for hl in head_lanes]

    scores = qk(0)
    for qi in range(n_q):
        rows = slice(qi * tq, (qi + 1) * tq)
        keys = slice(0, (qi + 1) * tq)
        cur = scores
        if qi + 1 < n_q:
            scores = qk(qi + 1)
        outs = []
        for hh, hl in enumerate(head_lanes):
            s = cur[hh]
            s_diag = jnp.where(causal, s[:, qi * tq:], -jnp.inf)
            s = s_diag if qi == 0 else jnp.concatenate([s[:, :qi * tq], s_diag], axis=1)
            m = jnp.max(s, axis=-1, keepdims=True)
            acc = _dot(jnp.exp2(s - m).astype(BF16), v_ref[keys, hl])
            one = _ones_lane(hh)
            outs.append(acc * (1.0 / acc[:, one:one + 1]))
        o_ref[rows, :] = jnp.where(lane < V_HEAD, outs[0], outs[1])


def _mla_attn(q3, k3, v3):
    B, S, _ = q3.shape
    tq = min(TQ, S)
    pair = lambda b, g: (b, 0, g)
    return pl.pallas_call(
        functools.partial(_mla_attn_kernel, tq=tq),
        grid=(B, MLA_HEADS // 2),
        in_specs=[pl.BlockSpec((None, S, 2 * LANES), pair), pl.BlockSpec((None, S, 2 * LANES), pair),
                  pl.BlockSpec((None, S, 2 * LANES), pair)],
        out_specs=pl.BlockSpec((None, S, 2 * V_HEAD), pair),
        out_shape=jax.ShapeDtypeStruct((B, S, MLA_HEADS * V_HEAD), F32),
        compiler_params=pltpu.CompilerParams(dimension_semantics=("arbitrary", "arbitrary"),
                                             vmem_limit_bytes=VMEM_LIMIT),
        name="mla_attn",
    )(q3, k3, v3)


def _mixer_out_kernel(o_ref, z_ref, zh_ref, gb_ref, x_ref, cw_ref, gout_ref, gsum_ref, gexp_ref, wo_ref,
                      h_out, *, tiles_per_seq):
    i = pl.program_id(0)
    ts = z_ref.shape[0]
    z = z_ref[...]
    halo = jnp.where(i % tiles_per_seq == 0, 0.0, zh_ref[...])
    row = lax.broadcasted_iota(jnp.int32, z.shape, 0)
    z1 = jnp.where(row == 0, halo[7:8, :], pltpu.roll(z, 1, axis=0))
    z2 = jnp.where(row == 0, halo[6:7, :], jnp.where(row == 1, halo[7:8, :], pltpu.roll(z, 2, axis=0)))
    cw = cw_ref[...]
    y_conv = gb_ref[...] * (cw[0:1, :] * z2 + cw[1:2, :] * z1 + cw[2:3, :] * z)
    y_all = jnp.concatenate([o_ref[...], y_conv], axis=-1)
    for part in range(ROW_SPLIT):
        rows = slice(part * ts // ROW_SPLIT, (part + 1) * ts // ROW_SPLIT)
        y = y_all[rows]
        sq_hi, sq_lo = _split_bf16(y * y)
        gs = _dot(sq_hi, gsum_ref[...]) + _dot(sq_lo, gsum_ref[...])
        r = lax.rsqrt(gs * (1.0 / GROUP_DIM) + EPS)
        r_hi, r_lo = _split_bf16(r)
        r_full = _dot(r_hi, gexp_ref[...]) + _dot(r_lo, gexp_ref[...])
        yn = (y * r_full * gout_ref[...]).astype(BF16)
        h_out[rows, :] = x_ref[rows, :] + _dot(yn, wo_ref[...])


def _mixer_out(o2, z, gb, x2, conv_w, gout, gsum, gexp, wo, seq):
    T, D = x2.shape
    ts = min(TS_IN, seq)
    row = lambda i: (i, 0)
    fixed = lambda i: (0, 0)
    full = lambda a: pl.BlockSpec(a.shape, fixed)
    halo_blocks = ts // SUBLANES
    return pl.pallas_call(
        functools.partial(_mixer_out_kernel, tiles_per_seq=seq // ts),
        grid=(T // ts,),
        in_specs=[pl.BlockSpec((ts, MLA_HEADS * V_HEAD), row), pl.BlockSpec((ts, CONV_DIM), row),
                  pl.BlockSpec((SUBLANES, CONV_DIM), lambda i: (jnp.maximum(i * halo_blocks - 1, 0), 0)),
                  pl.BlockSpec((ts, CONV_DIM), row), pl.BlockSpec((ts, D), row),
                  full(conv_w), full(gout), full(gsum), full(gexp), full(wo)],
        out_specs=pl.BlockSpec((ts, D), row),
        out_shape=jax.ShapeDtypeStruct((T, D), F32),
        compiler_params=pltpu.CompilerParams(dimension_semantics=("arbitrary",), vmem_limit_bytes=VMEM_LIMIT),
        name="mixer_out",
    )(o2, z, z, gb, x2, conv_w, gout, gsum, gexp, wo)


def _mem_kv_kernel(mem_ref, g_ref, w_ref, k_out, v_out):
    d = mem_ref.shape[-1]
    mn = _rms(mem_ref[...], g_ref[...]).astype(BF16)
    kv = _dot(mn, w_ref[...])
    k_out[...] = kv[:, :d].astype(BF16)
    v_out[...] = kv[:, d:].astype(BF16)


def _mem_kv(mem, g, w):
    B, M, D = mem.shape
    return pl.pallas_call(
        _mem_kv_kernel,
        grid=(B,),
        in_specs=[pl.BlockSpec((None, M, D), lambda b: (b, 0, 0)), pl.BlockSpec(g.shape, lambda b: (0, 0)),
                  pl.BlockSpec(w.shape, lambda b: (0, 0))],
        out_specs=[pl.BlockSpec((None, M, D), lambda b: (b, 0, 0)), pl.BlockSpec((None, M, D), lambda b: (b, 0, 0))],
        out_shape=[jax.ShapeDtypeStruct((B, M, D), BF16), jax.ShapeDtypeStruct((B, M, D), BF16)],
        compiler_params=pltpu.CompilerParams(dimension_semantics=("arbitrary",), vmem_limit_bytes=VMEM_LIMIT),
        name="mem_kv",
    )(mem, g, w)


def _xattn_kernel(h_ref, g_ref, wq_ref, k_ref, v_ref, wo_ref, h_out):
    ts, d = h_ref.shape
    hd = d // X_HEADS
    head_cols = [slice(hh * hd, (hh + 1) * hd) for hh in range(X_HEADS)]
    groups = [slice(part * ts // ROW_SPLIT_X, (part + 1) * ts // ROW_SPLIT_X) for part in range(ROW_SPLIT_X)]

    def scores_of(rows):
        q = _dot(_rms(h_ref[rows, :], g_ref[...]).astype(BF16), wq_ref[...]).astype(BF16)
        return [_dot_nt(q[:, sl], k_ref[:, sl]) for sl in head_cols]

    scores = scores_of(groups[0])
    for part, rows in enumerate(groups):
        cur = scores
        if part + 1 < len(groups):
            scores = scores_of(groups[part + 1])
        outs = []
        for hh, sl in enumerate(head_cols):
            s = cur[hh] * (1.0 / math.sqrt(hd))
            m = jnp.max(s, axis=-1, keepdims=True)
            p = jnp.exp(s - m)
            p = p * (1.0 / jnp.sum(p, axis=-1, keepdims=True))
            outs.append(_dot(p.astype(BF16), v_ref[:, sl]))
        o = jnp.concatenate(outs, axis=-1).astype(BF16)
        h_out[rows, :] = h_ref[rows, :] + _dot(o, wo_ref[...])


def _xattn(h3, g, wq, kx, vx, wo):
    B, S, D = h3.shape
    M = kx.shape[1]
    ts = min(TS_X, S)
    fixed = lambda b, i: (0, 0)
    return pl.pallas_call(
        _xattn_kernel,
        grid=(B, S // ts),
        in_specs=[pl.BlockSpec((None, ts, D), lambda b, i: (b, i, 0)), pl.BlockSpec(g.shape, fixed),
                  pl.BlockSpec(wq.shape, fixed), pl.BlockSpec((None, M, D), lambda b, i: (b, 0, 0)),
                  pl.BlockSpec((None, M, D), lambda b, i: (b, 0, 0)), pl.BlockSpec(wo.shape, fixed)],
        out_specs=pl.BlockSpec((None, ts, D), lambda b, i: (b, i, 0)),
        out_shape=jax.ShapeDtypeStruct((B, S, D), F32),
        compiler_params=pltpu.CompilerParams(dimension_semantics=("arbitrary", "arbitrary"),
                                             vmem_limit_bytes=VMEM_LIMIT),
        name="xattn",
    )(h3, g, wq, kx, vx, wo)


def _batcher_pairs(n):
    pairs = []
    p = 1
    while p < n:
        k = p
        while k >= 1:
            for j in range(k % p, n - k, 2 * k):
                for i in range(min(k, n - j - k)):
                    if (i + j) // (2 * p) == (i + j + k) // (2 * p):
                        pairs.append((i + j, i + j + k))
            k //= 2
        p *= 2
    return pairs


_SORT16 = _batcher_pairs(PEER_TOPK)
_ROW_LEN = [PEER_TOPK // (a + 1) for a in range(PEER_TOPK)]


def _sort_desc(v):
    v = list(v)
    for i, j in _SORT16:
        hi = jnp.maximum(v[i], v[j])
        lo = jnp.minimum(v[i], v[j])
        v[i], v[j] = hi, lo
    return v


def _bitonic_desc(v):
    v = list(v)
    n = len(v)
    d = n // 2
    while d >= 1:
        for k in range(n):
            if k & d == 0:
                hi = jnp.maximum(v[k], v[k + d])
                lo = jnp.minimum(v[k], v[k + d])
                v[k], v[k + d] = hi, lo
        d //= 2
    return v


def _merge_top(cur, other):
    n = len(cur)
    c = list(cur)
    for r, val in enumerate(other):
        c[n - 1 - r] = jnp.maximum(c[n - 1 - r], val)
    return _bitonic_desc(c)


def _peer_route_kernel(h_ref, g_ref, wq_ref, key_ref,
                       xn_out, r2_out, p_out, n_out, c_out,
                       st_ref, top_ref, sort_ref, res_ref):
    ts = h_ref.shape[0]
    n_chunk = ts // LANES
    hn = _rms(h_ref[...], g_ref[...]).astype(BF16)
    xn_out[...] = hn
    q = _dot(hn, wq_ref[...])
    for h in range(PEER_HEADS):
        st = _dot_nt(key_ref[h], q[:, h * LANES:(h + 1) * LANES].astype(BF16))
        for c in range(n_chunk):
            st_ref[c, h] = st[:, c * LANES:(c + 1) * LANES]


    def chunk_body(c, _):
        for half in range(2):
            def sort_body(h, _, half=half):
                s = st_ref[c, h, pl.ds(half * PEER_KEYS, PEER_KEYS), :].reshape(PEER_TOPK, SUBLANES, LANES)
                v = _sort_desc([s[k] for k in range(PEER_TOPK)])
                row0 = pl.multiple_of(h * SUBLANES, SUBLANES)
                for k in range(PEER_TOPK):
                    sort_ref[k, pl.ds(row0, SUBLANES), :] = v[k]
                return 0

            lax.fori_loop(0, PEER_HEADS, sort_body, 0)
            lists = [[sort_ref[k, pl.ds(s, PEER_HEADS, stride=SUBLANES), :] for k in range(PEER_TOPK)]
                     for s in range(SUBLANES)]
            while len(lists) > 1:
                lists = [_merge_top(lists[i], lists[i + 1]) for i in range(0, len(lists), 2)]
            for a in range(PEER_TOPK):
                top_ref[half, a] = lists[0][a]

        v1 = [top_ref[0, a] for a in range(PEER_TOPK)]
        v2 = [top_ref[1, b] for b in range(PEER_TOPK)]
        sums = [[v1[a] + v2[b] for b in range(_ROW_LEN[a])] for a in range(PEER_TOPK)]
        cur = sums[0]
        a = 1
        while _ROW_LEN[a] > 1:
            cur = _merge_top(cur, sums[a])
            a += 1
        cur = _merge_top(cur, [sums[r][0] for r in range(a, PEER_TOPK)])
        tau = cur[PEER_TOPK - 1]
        top_sum = sums[0][0]
        z = jnp.zeros_like(tau)
        for a in range(PEER_TOPK):
            cnt = jnp.zeros_like(tau)
            for b in range(_ROW_LEN[a]):
                sel = sums[a][b] >= tau
                cnt = cnt + jnp.where(sel, 1.0, 0.0)
                z = z + jnp.where(sel, jnp.exp(sums[a][b] - top_sum), 0.0)
            res_ref[a] = cnt
        res_ref[PEER_TOPK] = 1.0 / z

        def expand_body(h, _):
            s1 = st_ref[c, h, pl.ds(0, PEER_KEYS), :].reshape(PEER_TOPK, SUBLANES, LANES)
            s2 = st_ref[c, h, pl.ds(PEER_KEYS, PEER_KEYS), :].reshape(PEER_TOPK, SUBLANES, LANES)
            n = jnp.zeros(s1.shape, F32)
            r2 = jnp.full(s2.shape, float(PEER_TOPK), F32)
            for a in range(PEER_TOPK - 1, -1, -1):
                v1a = top_ref[0, a, pl.ds(h, 1), :]
                v2a = top_ref[1, a, pl.ds(h, 1), :]
                n = jnp.where(s1 == v1a, res_ref[a, pl.ds(h, 1), :], n)
                r2 = jnp.where(s2 == v2a, float(a), r2)
            m1 = top_ref[0, 0, pl.ds(h, 1), :]
            m2 = top_ref[1, 0, pl.ds(h, 1), :]
            inv_z = res_ref[PEER_TOPK, pl.ds(h, 1), :]
            n_out[c, h] = n.reshape(PEER_KEYS, LANES)
            c_out[c, h] = (jnp.exp(s1 - m1) * inv_z).reshape(PEER_KEYS, LANES)
            r2_out[c, h] = r2.reshape(PEER_KEYS, LANES).astype(BF16)
            p_out[c, h] = jnp.exp(s2 - m2).reshape(PEER_KEYS, LANES).astype(BF16)
            return 0

        lax.fori_loop(0, PEER_HEADS, expand_body, 0)
        return 0

    lax.fori_loop(0, n_chunk, chunk_body, 0)


def _peer_route(h2, g, wq, keys):
    T, D = h2.shape
    ts = min(TS_ROUTE, T)
    nc = ts // LANES
    aux_spec = pl.BlockSpec((nc, PEER_HEADS, PEER_KEYS, LANES), lambda i: (i, 0, 0, 0))
    aux_shape = jax.ShapeDtypeStruct((T // LANES, PEER_HEADS, PEER_KEYS, LANES), F32)
    aux_shape_bf = jax.ShapeDtypeStruct((T // LANES, PEER_HEADS, PEER_KEYS, LANES), BF16)
    fixed2 = lambda i: (0, 0)
    fixed3 = lambda i: (0, 0, 0)
    return pl.pallas_call(
        _peer_route_kernel,
        grid=(T // ts,),
        in_specs=[pl.BlockSpec((ts, D), lambda i: (i, 0)), pl.BlockSpec(g.shape, fixed2),
                  pl.BlockSpec(wq.shape, fixed2), pl.BlockSpec(keys.shape, fixed3)],
        out_specs=[pl.BlockSpec((ts, D), lambda i: (i, 0)), aux_spec, aux_spec, aux_spec, aux_spec],
        out_shape=[jax.ShapeDtypeStruct((T, D), BF16), aux_shape_bf, aux_shape_bf, aux_shape, aux_shape],
        scratch_shapes=[pltpu.VMEM((nc, PEER_HEADS, 2 * PEER_KEYS, LANES), F32),
                        pltpu.VMEM((2, PEER_TOPK, SUBLANES, LANES), F32),
                        pltpu.VMEM((PEER_TOPK, PEER_HEADS * SUBLANES, LANES), F32),
                        pltpu.VMEM((PEER_TOPK + 1, SUBLANES, LANES), F32)],
        compiler_params=pltpu.CompilerParams(dimension_semantics=("arbitrary",), vmem_limit_bytes=VMEM_LIMIT),
        name="peer_route",
    )(h2, g, wq, keys)


def _pack_experts_kernel(u_ref, v_ref, u_out, vt_out):
    u_out[...] = pltpu.bitcast(u_ref[...].astype(BF16), jnp.uint32)
    vt_out[...] = pltpu.bitcast(v_ref[...].T.astype(BF16), jnp.uint32)


def _pack_experts(u, v):
    E, D = u.shape
    eb = EB_FFN
    return pl.pallas_call(
        _pack_experts_kernel,
        grid=(E // eb,),
        in_specs=[pl.BlockSpec((eb, D), lambda e: (e, 0)), pl.BlockSpec((eb, D), lambda e: (e, 0))],
        out_specs=[pl.BlockSpec((eb // 2, D), lambda e: (e, 0)), pl.BlockSpec((D // 2, eb), lambda e: (0, e))],
        out_shape=[jax.ShapeDtypeStruct((E // 2, D), jnp.uint32), jax.ShapeDtypeStruct((D // 2, E), jnp.uint32)],
        compiler_params=pltpu.CompilerParams(dimension_semantics=("arbitrary",), vmem_limit_bytes=VMEM_LIMIT),
        name="pack_experts",
    )(u, v)


def _peer_gate_unit(tc, ii, at_ref, ht_ref, r2s_ref, ps_ref, n_ref, c_ref):
    pack = 2 * SUBLANES
    n_jv = PEER_KEYS // pack
    zero = jnp.zeros((pack, LANES), BF16)
    lanes = slice(tc * LANES, (tc + 1) * LANES)
    g = [None] * n_jv
    for h in range(PEER_HEADS):
        n_b = jnp.broadcast_to(n_ref[tc, h, ii:ii + 1, :], (pack, LANES)).astype(BF16)
        c_b = jnp.broadcast_to(c_ref[tc, h, ii:ii + 1, :], (pack, LANES)).astype(BF16)
        for jv in range(n_jv):
            js = slice(jv * pack, (jv + 1) * pack)
            term = jnp.where(r2s_ref[tc, h, js, :] < n_b, ps_ref[tc, h, js, :], zero) * c_b
            g[jv] = term if g[jv] is None else g[jv] + term
    for jv in range(n_jv):
        rows = slice(ii * PEER_KEYS + jv * pack, ii * PEER_KEYS + (jv + 1) * pack)
        ht_ref[rows, lanes] = _gelu_tanh(at_ref[rows, lanes]).astype(BF16) * g[jv]


def _peer_ffn_kernel(xn_ref, u_ref, vt_ref, r2_ref, p_ref, n_ref, c_ref, h_ref, gfin_ref,
                     out_ref, acc_ref, at0_ref, at1_ref, ht0_ref, ht1_ref, r2s_ref, ps_ref, xs_ref,
                     *, n_e, n_blocks, final_norm):
    g = pl.program_id(0)
    tt = xn_ref.shape[0]
    e_score = g % n_e
    e_gate = jnp.maximum(g - 1, 0) % n_e
    e_down = jnp.maximum(g - 2, 0) % n_e

    @pl.when(g == 0)
    def _():
        at1_ref[...] = jnp.zeros_like(at1_ref)
        ht0_ref[...] = jnp.zeros_like(ht0_ref)
        ht1_ref[...] = jnp.zeros_like(ht1_ref)
        acc_ref[...] = jnp.zeros_like(acc_ref)

    @pl.when((g < n_blocks) & (e_score == 0))
    def _():
        xs_ref[...] = xn_ref[...]

    @pl.when((g <= n_blocks) & (e_gate == 0))
    def _():
        for tc in range(tt // LANES):
            for h in range(PEER_HEADS):
                r2s_ref[tc, h] = r2_ref[tc, h]
                ps_ref[tc, h] = p_ref[tc, h]

    @pl.when((g >= 2) & (e_down == 0))
    def _():
        acc_ref[...] = jnp.zeros_like(acc_ref)

    def stages(at_w, at_r, ht_w, ht_r):
        u_blk = pltpu.bitcast(u_ref[...], BF16)
        vt_blk = pltpu.bitcast(vt_ref[...], BF16)
        eb = u_blk.shape[0]
        subs = [slice(sb * SUB_FFN, (sb + 1) * SUB_FFN) for sb in range(eb // SUB_FFN)]

        def score(ex):
            at_w[ex, :] = _dot_nt(u_blk[ex], xs_ref[...])

        def down():
            acc_ref[...] += _dot(vt_blk, ht_r[...])

        chunks = [functools.partial(score, ex) for ex in subs] + [down]
        units = [(tc, ii) for tc in range(tt // LANES) for ii in range(eb // PEER_KEYS)]
        split = GATE_UNIT_SPLIT
        assert len(split) == len(chunks) + 1 and sum(split) == len(units)
        bounds = [sum(split[:k]) for k in range(len(split) + 1)]

        def gate_units(k):
            for tc, ii in units[bounds[k]:bounds[k + 1]]:
                _peer_gate_unit(tc, ii, at_r, ht_w, r2s_ref, ps_ref, n_ref, c_ref)

        gate_units(0)
        for k, chunk in enumerate(chunks):
            chunk()
            gate_units(k + 1)

    @pl.when(g % 2 == 0)
    def _():
        stages(at0_ref, at1_ref, ht1_ref, ht0_ref)

    @pl.when(g % 2 == 1)
    def _():
        stages(at1_ref, at0_ref, ht0_ref, ht1_ref)

    @pl.when((g >= 2) & (e_down == n_e - 1))
    def _():
        res = h_ref[...] + acc_ref[...].T
        out_ref[...] = _rms(res, gfin_ref[...]) if final_norm else res


def _peer_ffn(xn, u_pack, vt_pack, r2, p, n, coef, h2, gfin, final_norm):
    T, D = h2.shape
    E = vt_pack.shape[1]
    tt = min(TT_FFN, T)
    nc = tt // LANES
    eb = EB_FFN
    n_i = eb // PEER_KEYS
    n_e = E // eb
    n_blocks = (T // tt) * n_e

    def block(lag):
        def split(g):
            b = jnp.clip(g - lag, 0, n_blocks - 1)
            return b // n_e, b % n_e
        return split

    score, gate, down = block(0), block(1), block(2)
    aux_shape = (nc, PEER_HEADS, PEER_KEYS, LANES)
    row_shape = (nc, PEER_HEADS, n_i, LANES)
    return pl.pallas_call(
        functools.partial(_peer_ffn_kernel, n_e=n_e, n_blocks=n_blocks, final_norm=final_norm),
        grid=(n_blocks + 2,),
        in_specs=[pl.BlockSpec((tt, D), lambda g: (score(g)[0], 0)),
                  pl.BlockSpec((eb // 2, D), lambda g: (score(g)[1], 0)),
                  pl.BlockSpec((D // 2, eb), lambda g: (0, down(g)[1])),
                  pl.BlockSpec(aux_shape, lambda g: (gate(g)[0], 0, 0, 0)),
                  pl.BlockSpec(aux_shape, lambda g: (gate(g)[0], 0, 0, 0)),
                  pl.BlockSpec(row_shape, lambda g: (gate(g)[0], 0, gate(g)[1], 0)),
                  pl.BlockSpec(row_shape, lambda g: (gate(g)[0], 0, gate(g)[1], 0)),
                  pl.BlockSpec((tt, D), lambda g: (down(g)[0], 0)),
                  pl.BlockSpec(gfin.shape, lambda g: (0, 0))],
        out_specs=pl.BlockSpec((tt, D), lambda g: (down(g)[0], 0)),
        out_shape=jax.ShapeDtypeStruct((T, D), F32),
        scratch_shapes=[pltpu.VMEM((D, tt), F32),
                        pltpu.VMEM((eb, tt), F32), pltpu.VMEM((eb, tt), F32),
                        pltpu.VMEM((eb, tt), BF16), pltpu.VMEM((eb, tt), BF16),
                        pltpu.VMEM(aux_shape, BF16), pltpu.VMEM(aux_shape, BF16),
                        pltpu.VMEM((tt, D), BF16)],
        compiler_params=pltpu.CompilerParams(dimension_semantics=("arbitrary",), vmem_limit_bytes=VMEM_LIMIT),
        name="peer_ffn",
    )(xn, u_pack, vt_pack, r2, p, n, coef, h2, gfin)


def _head_blocks(w, n_heads, width, pieces):
    w3 = w.reshape(w.shape[0], n_heads, width)
    out = jnp.zeros((w.shape[0], n_heads, LANES), w.dtype)
    for s0, s1, d0 in pieces:
        out = out.at[:, :, d0:d0 + (s1 - s0)].set(w3[:, :, s0:s1])
    return out.reshape(w.shape[0], n_heads * LANES)


def kernel(x, mem, positions, g_mix, w_in, g_q, w_uq, g_kv, w_ukv, conv_w, g_out, w_o, g_x, g_mem, w_xq,
           w_xkv, w_xo, g_ffn, w_pq, sub_keys, u_experts, v_experts, g_final):
    B, S, D = x.shape
    T = B * S
    depth = g_mix.shape[0]
    half = QK_ROPE // 2
    assert S % min(TS_IN, S) == 0 and S % min(TS_X, S) == 0 and S % min(TQ, S) == 0 and D == MLA_HEADS * LANES
    assert T % min(TS_ROUTE, T) == 0 and T % min(TT_FFN, T) == 0 and u_experts.shape[1] % EB_FFN == 0

    inv = ROPE_THETA ** (-jnp.arange(0, QK_ROPE, 2, dtype=F32) / QK_ROPE)
    ang = positions.astype(F32)[..., None] * inv
    cos = jnp.cos(ang).astype(x.dtype).reshape(T, half)
    sin = jnp.sin(ang).astype(x.dtype).reshape(T, half)
    ones = jnp.ones((T, QK_NOPE), F32)
    zeros_n = jnp.zeros((T, QK_NOPE), F32)
    pad_q = jnp.zeros((T, LANES - QK_NOPE - QK_ROPE), F32)
    cos_t = jnp.concatenate([ones, cos, cos, pad_q], axis=1)
    sin_t = jnp.concatenate([zeros_n, -sin, sin, pad_q], axis=1)

    lane = jnp.arange(LANES)
    col = jnp.arange(MLA_HEADS * LANES)
    rope_lane = (lane >= QK_NOPE) & (lane < QK_NOPE + QK_ROPE)
    eplace = ((col[None, :] % LANES == lane[:, None]) & rope_lane[:, None]).astype(BF16)
    mix_col = jnp.arange(D)
    gsum = (mix_col[:, None] // GROUP_DIM == lane[None, :]).astype(BF16)
    gexp = (lane[:, None] == mix_col[None, :] // GROUP_DIM).astype(BF16)

    h = x.reshape(T, D)
    for l in range(depth):
        o1 = Q_RANK
        o2 = o1 + KV_RANK
        o3 = o2 + QK_ROPE
        o4 = o3 + CONV_DIM
        o5 = o4 + CONV_DIM
        wl = w_in[l]
        w_kr = wl[:, o2:o3]
        w_krr = jnp.concatenate([w_kr[:, half:], w_kr[:, :half]], axis=1)
        pad_lo = jnp.zeros((D, QK_NOPE), wl.dtype)
        pad_hi = jnp.zeros((D, LANES - QK_NOPE - QK_ROPE), wl.dtype)
        w1 = jnp.concatenate([wl[:, :o1], wl[:, o1:o2], pad_lo, w_kr, pad_hi, pad_lo, w_krr, pad_hi,
                              wl[:, o3:o4], wl[:, o4:o5], wl[:, o5:]], axis=1).astype(BF16)
        qw = QK_NOPE + QK_ROPE
        wq = _head_blocks(w_uq[l], MLA_HEADS, qw, [(0, qw, 0)]).astype(BF16)
        wqr = _head_blocks(w_uq[l], MLA_HEADS, qw,
                           [(QK_NOPE + half, qw, QK_NOPE), (QK_NOPE, QK_NOPE + half, QK_NOPE + half)]).astype(BF16)
        kvw = QK_NOPE + V_HEAD
        wk = _head_blocks(w_ukv[l], MLA_HEADS, kvw, [(0, QK_NOPE, 0)]).astype(BF16)
        v_cols = w_ukv[l].reshape(KV_RANK, MLA_HEADS, kvw)[:, :, QK_NOPE:]
        v_pad = jnp.zeros_like(v_cols)
        odd_head = (jnp.arange(MLA_HEADS) % 2 == 1)[None, :, None]
        wv = jnp.where(odd_head, jnp.concatenate([v_pad, v_cols], axis=-1),
                       jnp.concatenate([v_cols, v_pad], axis=-1)).reshape(KV_RANK, MLA_HEADS * LANES).astype(BF16)
        vone = jnp.stack([(lane == _ones_lane(hd)).astype(F32) for hd in range(MLA_HEADS)]).reshape(1, -1)

        q, k, v, z, gb = _mixer_in(h, g_mix[l][None, :], w1, g_q[l][None, :], wq, wqr, g_kv[l][None, :], wk, wv,
                                   vone, eplace, cos_t, sin_t)
        o = _mla_attn(q.reshape(B, S, -1), k.reshape(B, S, -1), v.reshape(B, S, -1))
        h = _mixer_out(o.reshape(T, -1), z, gb, h, conv_w[l], g_out[l][None, :], gsum, gexp,
                       w_o[l].astype(BF16), S)

        kx, vx = _mem_kv(mem, g_mem[l][None, :], w_xkv[l].astype(BF16))
        h = _xattn(h.reshape(B, S, D), g_x[l][None, :], w_xq[l].astype(BF16), kx, vx,
                   w_xo[l].astype(BF16)).reshape(T, D)

        sk = sub_keys[l]
        zk = jnp.zeros_like(sk[:, 0])
        keys_bd = jnp.concatenate([jnp.concatenate([sk[:, 0], zk], axis=-1),
                                   jnp.concatenate([zk, sk[:, 1]], axis=-1)], axis=1)
        xn, r2, p, n, coef = _peer_route(h, g_ffn[l][None, :], w_pq[l].astype(BF16), keys_bd.astype(BF16))
        u_pack, vt_pack = _pack_experts(u_experts[l], v_experts[l])
        h = _peer_ffn(xn, u_pack, vt_pack, r2, p, n, coef, h,
                      g_final[None, :], final_norm=(l == depth - 1))
    return h.reshape(B, S, D)
```

```python
import functools
import math

import jax
import jax.numpy as jnp
from jax import lax
from jax.experimental import pallas as pl
from jax.experimental.pallas import tpu as pltpu

F32 = jnp.float32
BF16 = jnp.bfloat16

EPS = 1e-6
LANES = 128
SUBLANES = 8
VMEM_LIMIT = 56 * 1024 * 1024

MLA_HEADS = 8
QK_NOPE = 64
QK_ROPE = 32
V_HEAD = 64
Q_RANK = 384
KV_RANK = 256
CONV_DIM = 512
GROUP_DIM = 64
ROPE_THETA = 10000.0
X_HEADS = 4
PEER_HEADS = 8
PEER_KEYS = 128
PEER_TOPK = 16

TS_IN = 1024
TS_X = 1024
ROW_SPLIT = 4
ROW_SPLIT_X = 2
TQ = 512
TS_ROUTE = 1024
TT_FFN = 512
EB_FFN = 2048
SUB_FFN = 1024
GATE_UNIT_SPLIT = (8, 16, 40, 0)
NT_DIMS = (((1,), (1,)), ((), ()))


def _rms(x, g):
    return x * lax.rsqrt(jnp.mean(x * x, axis=-1, keepdims=True) + EPS) * g


def _split_bf16(x):
    hi = x.astype(BF16)
    lo = (x - hi.astype(F32)).astype(BF16)
    return hi, lo


def _two_gelu_tanh(x):
    c0 = math.sqrt(2.0 / math.pi)
    return x + x * jnp.tanh(x * (c0 + (c0 * 0.044715) * (x * x)))


def _dot(a, b):
    return jnp.dot(a, b, preferred_element_type=F32)


def _dot_nt(a, b):
    return lax.dot_general(a, b, NT_DIMS, preferred_element_type=F32)


_C_CQ = 0
_C_CKV = _C_CQ + Q_RANK
_C_KR = _C_CKV + KV_RANK
_C_KRR = _C_KR + LANES
_C_GB = _C_KRR + LANES
_C_GC = _C_GB + CONV_DIM
_C_HX = _C_GC + CONV_DIM
_C_END = _C_HX + CONV_DIM


def _mixer_in_kernel(x_ref, gmix_ref, w1_ref, gq_ref, wq_ref, wqr_ref, gkv_ref, wk_ref, wv_ref, vone_ref,
                     eplace_ref, cos_ref, sin_ref,
                     q_out, k_out, v_out, z_out, gb_out):
    xn = _rms(x_ref[...], gmix_ref[...]).astype(BF16)
    proj = _dot(xn, w1_ref[...])
    cq = proj[:, _C_CQ:_C_CKV]
    ckv = proj[:, _C_CKV:_C_KR]
    kr = proj[:, _C_KR:_C_KRR]
    krr = proj[:, _C_KRR:_C_GB]
    gb_out[...] = proj[:, _C_GB:_C_GC]
    z_out[...] = proj[:, _C_GC:_C_HX] * proj[:, _C_HX:_C_END]

    cqn = _rms(cq, gq_ref[...]).astype(BF16)
    q_raw = _dot(cqn, wq_ref[...])
    q_rot = _dot(cqn, wqr_ref[...])
    cos_t = cos_ref[...]
    sin_t = sin_ref[...]
    q_scale = math.log2(math.e) / math.sqrt(QK_NOPE + QK_ROPE)
    for h in range(MLA_HEADS):
        sl = slice(h * LANES, (h + 1) * LANES)
        q_out[:, sl] = ((q_raw[:, sl] * cos_t + q_rot[:, sl] * sin_t) * q_scale).astype(BF16)

    ckvn = _rms(ckv, gkv_ref[...]).astype(BF16)
    kr_roped = (kr * cos_t + krr * sin_t).astype(BF16)
    k_out[...] = (_dot(ckvn, wk_ref[...]) + _dot(kr_roped, eplace_ref[...])).astype(BF16)
    v_out[...] = (_dot(ckvn, wv_ref[...]) + vone_ref[...]).astype(BF16)


def _mixer_in(x2, gmix, w1, gq, wq, wqr, gkv, wk, wv, vone, eplace, cos_t, sin_t):
    T, D = x2.shape
    ts = min(TS_IN, T)
    row = lambda i: (i, 0)
    fixed = lambda i: (0, 0)
    full = lambda a: pl.BlockSpec(a.shape, fixed)
    return pl.pallas_call(
        _mixer_in_kernel,
        grid=(T // ts,),
        in_specs=[pl.BlockSpec((ts, D), row), full(gmix), full(w1), full(gq), full(wq), full(wqr),
                  full(gkv), full(wk), full(wv), full(vone), full(eplace),
                  pl.BlockSpec((ts, LANES), row), pl.BlockSpec((ts, LANES), row)],
        out_specs=[pl.BlockSpec((ts, MLA_HEADS * LANES), row), pl.BlockSpec((ts, MLA_HEADS * LANES), row),
                   pl.BlockSpec((ts, MLA_HEADS * LANES), row), pl.BlockSpec((ts, CONV_DIM), row),
                   pl.BlockSpec((ts, CONV_DIM), row)],
        out_shape=[jax.ShapeDtypeStruct((T, MLA_HEADS * LANES), BF16),
                   jax.ShapeDtypeStruct((T, MLA_HEADS * LANES), BF16),
                   jax.ShapeDtypeStruct((T, MLA_HEADS * LANES), BF16),
                   jax.ShapeDtypeStruct((T, CONV_DIM), F32),
                   jax.ShapeDtypeStruct((T, CONV_DIM), F32)],
        compiler_params=pltpu.CompilerParams(dimension_semantics=("arbitrary",), vmem_limit_bytes=VMEM_LIMIT),
        name="mixer_in",
    )(x2, gmix, w1, gq, wq, wqr, gkv, wk, wv, vone, eplace, cos_t, sin_t)


def _ones_lane(head):
    return V_HEAD if head % 2 == 0 else 0


def _mla_attn_kernel(q_ref, k_ref, v_ref, o_ref, *, tq):
    seq = q_ref.shape[0]
    causal = (lax.broadcasted_iota(jnp.int32, (tq, tq), 1) <= lax.broadcasted_iota(jnp.int32, (tq, tq), 0))
    lane = lax.broadcasted_iota(jnp.int32, (tq, LANES), 1)
    n_q = seq // tq
    head_lanes = [slice(hh * LANES, (hh + 1) * LANES) for hh in range(2)]

    def qk(qi):
        return [_dot_nt(q_ref[qi * tq:(qi + 1) * tq, hl], k_ref[0:(qi + 1) * tq, hl]) for hl in head_lanes]

    scores = qk(0)
    for qi in range(n_q):
        rows = slice(qi * tq, (qi + 1) * tq)
        keys = slice(0, (qi + 1) * tq)
        cur = scores
        if qi + 1 < n_q:
            scores = qk(qi + 1)
        outs = []
        for hh, hl in enumerate(head_lanes):
            s = cur[hh]
            s_diag = jnp.where(causal, s[:, qi * tq:], -jnp.inf)
            s = s_diag if qi == 0 else jnp.concatenate([s[:, :qi * tq], s_diag], axis=1)
            m = jnp.max(s, axis=-1, keepdims=True)
            acc = _dot(jnp.exp2(s - m).astype(BF16), v_ref[keys, hl])
            one = _ones_lane(hh)
            outs.append(acc * (1.0 / acc[:, one:one + 1]))
        o_ref[rows, :] = jnp.where(lane < V_HEAD, outs[0], outs[1])


def _mla_attn(q3, k3, v3):
    B, S, _ = q3.shape
    tq = min(TQ, S)
    pair = lambda b, g: (b, 0, g)
    return pl.pallas_call(
        functools.partial(_mla_attn_kernel, tq=tq),
        grid=(B, MLA_HEADS // 2),
        in_specs=[pl.BlockSpec((None, S, 2 * LANES), pair), pl.BlockSpec((None, S, 2 * LANES), pair),
                  pl.BlockSpec((None, S, 2 * LANES), pair)],
        out_specs=pl.BlockSpec((None, S, 2 * V_HEAD), pair),
        out_shape=jax.ShapeDtypeStruct((B, S, MLA_HEADS * V_HEAD), F32),
        compiler_params=pltpu.CompilerParams(dimension_semantics=("arbitrary", "arbitrary"),
                                             vmem_limit_bytes=VMEM_LIMIT),
        name="mla_attn",
    )(q3, k3, v3)


def _mixer_out_kernel(o_ref, z_ref, zh_ref, gb_ref, x_ref, cw_ref, gout_ref, gsum_ref, gexp_ref, wo_ref,
                      h_out, *, tiles_per_seq):
    i = pl.program_id(0)
    ts = z_ref.shape[0]
    z = z_ref[...]
    halo = jnp.where(i % tiles_per_seq == 0, 0.0, zh_ref[...])
    row = lax.broadcasted_iota(jnp.int32, z.shape, 0)
    z1 = jnp.where(row == 0, halo[7:8, :], pltpu.roll(z, 1, axis=0))
    z2 = jnp.where(row == 0, halo[6:7, :], jnp.where(row == 1, halo[7:8, :], pltpu.roll(z, 2, axis=0)))
    cw = cw_ref[...]
    y_conv = gb_ref[...] * (cw[0:1, :] * z2 + cw[1:2, :] * z1 + cw[2:3, :] * z)
    y_all = jnp.concatenate([o_ref[...], y_conv], axis=-1)
    for part in range(ROW_SPLIT):
        rows = slice(part * ts // ROW_SPLIT, (part + 1) * ts // ROW_SPLIT)
        y = y_all[rows]
        sq_hi, sq_lo = _split_bf16(y * y)
        gs = _dot(sq_hi, gsum_ref[...]) + _dot(sq_lo, gsum_ref[...])
        r = lax.rsqrt(gs * (1.0 / GROUP_DIM) + EPS)
        r_hi, r_lo = _split_bf16(r)
        r_full = _dot(r_hi, gexp_ref[...]) + _dot(r_lo, gexp_ref[...])
        yn = (y * r_full * gout_ref[...]).astype(BF16)
        h_out[rows, :] = x_ref[rows, :] + _dot(yn, wo_ref[...])


def _mixer_out(o2, z, gb, x2, conv_w, gout, gsum, gexp, wo, seq):
    T, D = x2.shape
    ts = min(TS_IN, seq)
    row = lambda i: (i, 0)
    fixed = lambda i: (0, 0)
    full = lambda a: pl.BlockSpec(a.shape, fixed)
    halo_blocks = ts // SUBLANES
    return pl.pallas_call(
        functools.partial(_mixer_out_kernel, tiles_per_seq=seq // ts),
        grid=(T // ts,),
        in_specs=[pl.BlockSpec((ts, MLA_HEADS * V_HEAD), row), pl.BlockSpec((ts, CONV_DIM), row),
                  pl.BlockSpec((SUBLANES, CONV_DIM), lambda i: (jnp.maximum(i * halo_blocks - 1, 0), 0)),
                  pl.BlockSpec((ts, CONV_DIM), row), pl.BlockSpec((ts, D), row),
                  full(conv_w), full(gout), full(gsum), full(gexp), full(wo)],
        out_specs=pl.BlockSpec((ts, D), row),
        out_shape=jax.ShapeDtypeStruct((T, D), F32),
        compiler_params=pltpu.CompilerParams(dimension_semantics=("arbitrary",), vmem_limit_bytes=VMEM_LIMIT),
        name="mixer_out",
    )(o2, z, z, gb, x2, conv_w, gout, gsum, gexp, wo)


def _mem_kv_kernel(mem_ref, g_ref, w_ref, k_out, v_out):
    d = mem_ref.shape[-1]
    mn = _rms(mem_ref[...], g_ref[...]).astype(BF16)
    kv = _dot(mn, w_ref[...])
    k_out[...] = kv[:, :d].astype(BF16)
    v_out[...] = kv[:, d:].astype(BF16)


def _mem_kv(mem, g, w):
    B, M, D = mem.shape
    return pl.pallas_call(
        _mem_kv_kernel,
        grid=(B,),
        in_specs=[pl.BlockSpec((None, M, D), lambda b: (b, 0, 0)), pl.BlockSpec(g.shape, lambda b: (0, 0)),
                  pl.BlockSpec(w.shape, lambda b: (0, 0))],
        out_specs=[pl.BlockSpec((None, M, D), lambda b: (b, 0, 0)), pl.BlockSpec((None, M, D), lambda b: (b, 0, 0))],
        out_shape=[jax.ShapeDtypeStruct((B, M, D), BF16), jax.ShapeDtypeStruct((B, M, D), BF16)],
        compiler_params=pltpu.CompilerParams(dimension_semantics=("arbitrary",), vmem_limit_bytes=VMEM_LIMIT),
        name="mem_kv",
    )(mem, g, w)


def _xattn_kernel(h_ref, g_ref, wq_ref, k_ref, v_ref, wo_ref, h_out):
    ts, d = h_ref.shape
    hd = d // X_HEADS
    head_cols = [slice(hh * hd, (hh + 1) * hd) for hh in range(X_HEADS)]
    groups = [slice(part * ts // ROW_SPLIT_X, (part + 1) * ts // ROW_SPLIT_X) for part in range(ROW_SPLIT_X)]

    def scores_of(rows):
        q = _dot(_rms(h_ref[rows, :], g_ref[...]).astype(BF16), wq_ref[...]).astype(BF16)
        return [_dot_nt(q[:, sl], k_ref[:, sl]) for sl in head_cols]

    scores = scores_of(groups[0])
    for part, rows in enumerate(groups):
        cur = scores
        if part + 1 < len(groups):
            scores = scores_of(groups[part + 1])
        outs = []
        for hh, sl in enumerate(head_cols):
            s = cur[hh] * (1.0 / math.sqrt(hd))
            m = jnp.max(s, axis=-1, keepdims=True)
            p = jnp.exp(s - m)
            p = p * (1.0 / jnp.sum(p, axis=-1, keepdims=True))
            outs.append(_dot(p.astype(BF16), v_ref[:, sl]))
        o = jnp.concatenate(outs, axis=-1).astype(BF16)
        h_out[rows, :] = h_ref[rows, :] + _dot(o, wo_ref[...])


def _xattn(h3, g, wq, kx, vx, wo):
    B, S, D = h3.shape
    M = kx.shape[1]
    ts = min(TS_X, S)
    fixed = lambda b, i: (0, 0)
    return pl.pallas_call(
        _xattn_kernel,
        grid=(B, S // ts),
        in_specs=[pl.BlockSpec((None, ts, D), lambda b, i: (b, i, 0)), pl.BlockSpec(g.shape, fixed),
                  pl.BlockSpec(wq.shape, fixed), pl.BlockSpec((None, M, D), lambda b, i: (b, 0, 0)),
                  pl.BlockSpec((None, M, D), lambda b, i: (b, 0, 0)), pl.BlockSpec(wo.shape, fixed)],
        out_specs=pl.BlockSpec((None, ts, D), lambda b, i: (b, i, 0)),
        out_shape=jax.ShapeDtypeStruct((B, S, D), F32),
        compiler_params=pltpu.CompilerParams(dimension_semantics=("arbitrary", "arbitrary"),
                                             vmem_limit_bytes=VMEM_LIMIT),
        name="xattn",
    )(h3, g, wq, kx, vx, wo)


def _batcher_pairs(n):
    pairs = []
    p = 1
    while p < n:
        k = p
        while k >= 1:
            for j in range(k % p, n - k, 2 * k):
                for i in range(min(k, n - j - k)):
                    if (i + j) // (2 * p) == (i + j + k) // (2 * p):
                        pairs.append((i + j, i + j + k))
            k //= 2
        p *= 2
    return pairs


_SORT16 = _batcher_pairs(PEER_TOPK)
_ROW_LEN = [PEER_TOPK // (a + 1) for a in range(PEER_TOPK)]


def _sort_desc(v):
    v = list(v)
    for i, j in _SORT16:
        hi = jnp.maximum(v[i], v[j])
        lo = jnp.minimum(v[i], v[j])
        v[i], v[j] = hi, lo
    return v


def _bitonic_desc(v):
    v = list(v)
    n = len(v)
    d = n // 2
    while d >= 1:
        for k in range(n):
            if k & d == 0:
                hi = jnp.maximum(v[k], v[k + d])
                lo = jnp.minimum(v[k], v[k + d])
                v[k], v[k + d] = hi, lo
        d //= 2
    return v


def _merge_top(cur, other):
    n = len(cur)
    c = list(cur)
    for r, val in enumerate(other):
        c[n - 1 - r] = jnp.maximum(c[n - 1 - r], val)
    return _bitonic_desc(c)


def _peer_route_kernel(h_ref, g_ref, wq_ref, key_ref,
                       xn_out, r2_out, p_out, n_out, c_out,
                       st_ref, top_ref, sort_ref, res_ref):
    ts = h_ref.shape[0]
    n_chunk = ts // LANES
    hn = _rms(h_ref[...], g_ref[...]).astype(BF16)
    xn_out[...] = hn
    q = _dot(hn, wq_ref[...])
    for h in range(PEER_HEADS):
        st = _dot_nt(key_ref[h], q[:, h * LANES:(h + 1) * LANES].astype(BF16))
        for c in range(n_chunk):
            st_ref[c, h] = st[:, c * LANES:(c + 1) * LANES]


    def chunk_body(c, _):
        for half in range(2):
            def sort_body(h, _, half=half):
                s = st_ref[c, h, pl.ds(half * PEER_KEYS, PEER_KEYS), :].reshape(PEER_TOPK, SUBLANES, LANES)
                v = _sort_desc([s[k] for k in range(PEER_TOPK)])
                row0 = pl.multiple_of(h * SUBLANES, SUBLANES)
                for k in range(PEER_TOPK):
                    sort_ref[k, pl.ds(row0, SUBLANES), :] = v[k]
                return 0

            lax.fori_loop(0, PEER_HEADS, sort_body, 0)
            lists = [[sort_ref[k, pl.ds(s, PEER_HEADS, stride=SUBLANES), :] for k in range(PEER_TOPK)]
                     for s in range(SUBLANES)]
            while len(lists) > 1:
                lists = [_merge_top(lists[i], lists[i + 1]) for i in range(0, len(lists), 2)]
            for a in range(PEER_TOPK):
                top_ref[half, a] = lists[0][a]

        v1 = [top_ref[0, a] for a in range(PEER_TOPK)]
        v2 = [top_ref[1, b] for b in range(PEER_TOPK)]
        sums = [[v1[a] + v2[b] for b in range(_ROW_LEN[a])] for a in range(PEER_TOPK)]
        cur = sums[0]
        a = 1
        while _ROW_LEN[a] > 1:
            cur = _merge_top(cur, sums[a])
            a += 1
        cur = _merge_top(cur, [sums[r][0] for r in range(a, PEER_TOPK)])
        tau = cur[PEER_TOPK - 1]
        top_sum = sums[0][0]
        z = jnp.zeros_like(tau)
        for a in range(PEER_TOPK):
            cnt = jnp.zeros_like(tau)
            for b in range(_ROW_LEN[a]):
                sel = sums[a][b] >= tau
                cnt = cnt + jnp.where(sel, 1.0, 0.0)
                z = z + jnp.where(sel, jnp.exp(sums[a][b] - top_sum), 0.0)
            res_ref[a] = cnt
        res_ref[PEER_TOPK] = 0.5 / z

        def expand_body(h, _):
            s1 = st_ref[c, h, pl.ds(0, PEER_KEYS), :].reshape(PEER_TOPK, SUBLANES, LANES)
            s2 = st_ref[c, h, pl.ds(PEER_KEYS, PEER_KEYS), :].reshape(PEER_TOPK, SUBLANES, LANES)
            n = jnp.zeros(s1.shape, F32)
            r2 = jnp.full(s2.shape, float(PEER_TOPK), F32)
            for a in range(PEER_TOPK - 1, -1, -1):
                v1a = top_ref[0, a, pl.ds(h, 1), :]
                v2a = top_ref[1, a, pl.ds(h, 1), :]
                n = jnp.where(s1 == v1a, res_ref[a, pl.ds(h, 1), :], n)
                r2 = jnp.where(s2 == v2a, float(a), r2)
            m1 = top_ref[0, 0, pl.ds(h, 1), :]
            m2 = top_ref[1, 0, pl.ds(h, 1), :]
            inv_z = res_ref[PEER_TOPK, pl.ds(h, 1), :]
            n_out[c, h] = n.reshape(PEER_KEYS, LANES)
            c_out[c, h] = (jnp.exp(s1 - m1) * inv_z).reshape(PEER_KEYS, LANES)
            r2_out[c, h] = r2.reshape(PEER_KEYS, LANES).astype(BF16)
            p_out[c, h] = jnp.exp(s2 - m2).reshape(PEER_KEYS, LANES).astype(BF16)
            return 0

        lax.fori_loop(0, PEER_HEADS, expand_body, 0)
        return 0

    lax.fori_loop(0, n_chunk, chunk_body, 0)


def _peer_route(h2, g, wq, keys):
    T, D = h2.shape
    ts = min(TS_ROUTE, T)
    nc = ts // LANES
    aux_spec = pl.BlockSpec((nc, PEER_HEADS, PEER_KEYS, LANES), lambda i: (i, 0, 0, 0))
    aux_shape = jax.ShapeDtypeStruct((T // LANES, PEER_HEADS, PEER_KEYS, LANES), F32)
    aux_shape_bf = jax.ShapeDtypeStruct((T // LANES, PEER_HEADS, PEER_KEYS, LANES), BF16)
    fixed2 = lambda i: (0, 0)
    fixed3 = lambda i: (0, 0, 0)
    return pl.pallas_call(
        _peer_route_kernel,
        grid=(T // ts,),
        in_specs=[pl.BlockSpec((ts, D), lambda i: (i, 0)), pl.BlockSpec(g.shape, fixed2),
                  pl.BlockSpec(wq.shape, fixed2), pl.BlockSpec(keys.shape, fixed3)],
        out_specs=[pl.BlockSpec((ts, D), lambda i: (i, 0)), aux_spec, aux_spec, aux_spec, aux_spec],
        out_shape=[jax.ShapeDtypeStruct((T, D), BF16), aux_shape_bf, aux_shape_bf, aux_shape, aux_shape],
        scratch_shapes=[pltpu.VMEM((nc, PEER_HEADS, 2 * PEER_KEYS, LANES), F32),
                        pltpu.VMEM((2, PEER_TOPK, SUBLANES, LANES), F32),
                        pltpu.VMEM((PEER_TOPK, PEER_HEADS * SUBLANES, LANES), F32),
                        pltpu.VMEM((PEER_TOPK + 1, SUBLANES, LANES), F32)],
        compiler_params=pltpu.CompilerParams(dimension_semantics=("arbitrary",), vmem_limit_bytes=VMEM_LIMIT),
        name="peer_route",
    )(h2, g, wq, keys)


def _pack_experts_kernel(u_ref, v_ref, u_out, vt_out):
    u_out[...] = pltpu.bitcast(u_ref[...].astype(BF16), jnp.uint32)
    vt_out[...] = pltpu.bitcast(v_ref[...].T.astype(BF16), jnp.uint32)


def _pack_experts(u, v):
    E, D = u.shape
    eb = EB_FFN
    return pl.pallas_call(
        _pack_experts_kernel,
        grid=(E // eb,),
        in_specs=[pl.BlockSpec((eb, D), lambda e: (e, 0)), pl.BlockSpec((eb, D), lambda e: (e, 0))],
        out_specs=[pl.BlockSpec((eb // 2, D), lambda e: (e, 0)), pl.BlockSpec((D // 2, eb), lambda e: (0, e))],
        out_shape=[jax.ShapeDtypeStruct((E // 2, D), jnp.uint32), jax.ShapeDtypeStruct((D // 2, E), jnp.uint32)],
        compiler_params=pltpu.CompilerParams(dimension_semantics=("arbitrary",), vmem_limit_bytes=VMEM_LIMIT),
        name="pack_experts",
    )(u, v)


def _peer_gate_unit(tc, ii, at_ref, ht_ref, r2s_ref, ps_ref, n_ref, c_ref):
    pack = 2 * SUBLANES
    n_jv = PEER_KEYS // pack
    zero = jnp.zeros((pack, LANES), BF16)
    lanes = slice(tc * LANES, (tc + 1) * LANES)
    g = [None] * n_jv
    for h in range(PEER_HEADS):
        n_b = jnp.broadcast_to(n_ref[tc, h, ii:ii + 1, :], (pack, LANES)).astype(BF16)
        c_b = jnp.broadcast_to(c_ref[tc, h, ii:ii + 1, :], (pack, LANES)).astype(BF16)
        for jv in range(n_jv):
            js = slice(jv * pack, (jv + 1) * pack)
            term = jnp.where(r2s_ref[tc, h, js, :] < n_b, ps_ref[tc, h, js, :], zero) * c_b
            g[jv] = term if g[jv] is None else g[jv] + term
    for jv in range(n_jv):
        rows = slice(ii * PEER_KEYS + jv * pack, ii * PEER_KEYS + (jv + 1) * pack)
        ht_ref[rows, lanes] = _two_gelu_tanh(at_ref[rows, lanes]).astype(BF16) * g[jv]


def _peer_ffn_kernel(xn_ref, u_ref, vt_ref, r2_ref, p_ref, n_ref, c_ref, h_ref, gfin_ref,
                     out_ref, acc_ref, at0_ref, at1_ref, ht0_ref, ht1_ref, r2s_ref, ps_ref, xs_ref,
                     *, n_e, n_blocks, final_norm):
    g = pl.program_id(0)
    tt = xn_ref.shape[0]
    e_score = g % n_e
    e_gate = jnp.maximum(g - 1, 0) % n_e
    e_down = jnp.maximum(g - 2, 0) % n_e

    @pl.when(g == 0)
    def _():
        at1_ref[...] = jnp.zeros_like(at1_ref)
        ht0_ref[...] = jnp.zeros_like(ht0_ref)
        ht1_ref[...] = jnp.zeros_like(ht1_ref)
        acc_ref[...] = jnp.zeros_like(acc_ref)

    @pl.when((g < n_blocks) & (e_score == 0))
    def _():
        xs_ref[...] = xn_ref[...]

    @pl.when((g <= n_blocks) & (e_gate == 0))
    def _():
        for tc in range(tt // LANES):
            for h in range(PEER_HEADS):
                r2s_ref[tc, h] = r2_ref[tc, h]
                ps_ref[tc, h] = p_ref[tc, h]

    @pl.when((g >= 2) & (e_down == 0))
    def _():
        acc_ref[...] = jnp.zeros_like(acc_ref)

    def stages(at_w, at_r, ht_w, ht_r):
        u_blk = pltpu.bitcast(u_ref[...], BF16)
        vt_blk = pltpu.bitcast(vt_ref[...], BF16)
        eb = u_blk.shape[0]
        subs = [slice(sb * SUB_FFN, (sb + 1) * SUB_FFN) for sb in range(eb // SUB_FFN)]

        def score(ex):
            at_w[ex, :] = _dot_nt(u_blk[ex], xs_ref[...])

        def down():
            acc_ref[...] += _dot(vt_blk, ht_r[...])

        chunks = [functools.partial(score, ex) for ex in subs] + [down]
        units = [(tc, ii) for tc in range(tt // LANES) for ii in range(eb // PEER_KEYS)]
        split = GATE_UNIT_SPLIT
        assert len(split) == len(chunks) + 1 and sum(split) == len(units)
        bounds = [sum(split[:k]) for k in range(len(split) + 1)]

        def gate_units(k):
            for tc, ii in units[bounds[k]:bounds[k + 1]]:
                _peer_gate_unit(tc, ii, at_r, ht_w, r2s_ref, ps_ref, n_ref, c_ref)

        gate_units(0)
        for k, chunk in enumerate(chunks):
            chunk()
            gate_units(k + 1)

    @pl.when(g % 2 == 0)
    def _():
        stages(at0_ref, at1_ref, ht1_ref, ht0_ref)

    @pl.when(g % 2 == 1)
    def _():
        stages(at1_ref, at0_ref, ht0_ref, ht1_ref)

    @pl.when((g >= 2) & (e_down == n_e - 1))
    def _():
        res = h_ref[...] + acc_ref[...].T
        out_ref[...] = _rms(res, gfin_ref[...]) if final_norm else res


def _peer_ffn(xn, u_pack, vt_pack, r2, p, n, coef, h2, gfin, final_norm):
    T, D = h2.shape
    E = vt_pack.shape[1]
    tt = min(TT_FFN, T)
    nc = tt // LANES
    eb = EB_FFN
    n_i = eb // PEER_KEYS
    n_e = E // eb
    n_blocks = (T // tt) * n_e

    def block(lag):
        def split(g):
            b = jnp.clip(g - lag, 0, n_blocks - 1)
            return b // n_e, b % n_e
        return split

    score, gate, down = block(0), block(1), block(2)
    aux_shape = (nc, PEER_HEADS, PEER_KEYS, LANES)
    row_shape = (nc, PEER_HEADS, n_i, LANES)
    return pl.pallas_call(
        functools.partial(_peer_ffn_kernel, n_e=n_e, n_blocks=n_blocks, final_norm=final_norm),
        grid=(n_blocks + 2,),
        in_specs=[pl.BlockSpec((tt, D), lambda g: (score(g)[0], 0)),
                  pl.BlockSpec((eb // 2, D), lambda g: (score(g)[1], 0)),
                  pl.BlockSpec((D // 2, eb), lambda g: (0, down(g)[1])),
                  pl.BlockSpec(aux_shape, lambda g: (gate(g)[0], 0, 0, 0)),
                  pl.BlockSpec(aux_shape, lambda g: (gate(g)[0], 0, 0, 0)),
                  pl.BlockSpec(row_shape, lambda g: (gate(g)[0], 0, gate(g)[1], 0)),
                  pl.BlockSpec(row_shape, lambda g: (gate(g)[0], 0, gate(g)[1], 0)),
                  pl.BlockSpec((tt, D), lambda g: (down(g)[0], 0)),
                  pl.BlockSpec(gfin.shape, lambda g: (0, 0))],
        out_specs=pl.BlockSpec((tt, D), lambda g: (down(g)[0], 0)),
        out_shape=jax.ShapeDtypeStruct((T, D), F32),
        scratch_shapes=[pltpu.VMEM((D, tt), F32),
                        pltpu.VMEM((eb, tt), F32), pltpu.VMEM((eb, tt), F32),
                        pltpu.VMEM((eb, tt), BF16), pltpu.VMEM((eb, tt), BF16),
                        pltpu.VMEM(aux_shape, BF16), pltpu.VMEM(aux_shape, BF16),
                        pltpu.VMEM((tt, D), BF16)],
        compiler_params=pltpu.CompilerParams(dimension_semantics=("arbitrary",), vmem_limit_bytes=VMEM_LIMIT),
        name="peer_ffn",
    )(xn, u_pack, vt_pack, r2, p, n, coef, h2, gfin)


def _head_blocks(w, n_heads, width, pieces):
    w3 = w.reshape(w.shape[0], n_heads, width)
    out = jnp.zeros((w.shape[0], n_heads, LANES), w.dtype)
    for s0, s1, d0 in pieces:
        out = out.at[:, :, d0:d0 + (s1 - s0)].set(w3[:, :, s0:s1])
    return out.reshape(w.shape[0], n_heads * LANES)


def kernel(x, mem, positions, g_mix, w_in, g_q, w_uq, g_kv, w_ukv, conv_w, g_out, w_o, g_x, g_mem, w_xq,
           w_xkv, w_xo, g_ffn, w_pq, sub_keys, u_experts, v_experts, g_final):
    B, S, D = x.shape
    T = B * S
    depth = g_mix.shape[0]
    half = QK_ROPE // 2
    assert S % min(TS_IN, S) == 0 and S % min(TS_X, S) == 0 and S % min(TQ, S) == 0 and D == MLA_HEADS * LANES
    assert T % min(TS_ROUTE, T) == 0 and T % min(TT_FFN, T) == 0 and u_experts.shape[1] % EB_FFN == 0

    inv = ROPE_THETA ** (-jnp.arange(0, QK_ROPE, 2, dtype=F32) / QK_ROPE)
    ang = positions.astype(F32)[..., None] * inv
    cos = jnp.cos(ang).astype(x.dtype).reshape(T, half)
    sin = jnp.sin(ang).astype(x.dtype).reshape(T, half)
    ones = jnp.ones((T, QK_NOPE), F32)
    zeros_n = jnp.zeros((T, QK_NOPE), F32)
    pad_q = jnp.zeros((T, LANES - QK_NOPE - QK_ROPE), F32)
    cos_t = jnp.concatenate([ones, cos, cos, pad_q], axis=1)
    sin_t = jnp.concatenate([zeros_n, -sin, sin, pad_q], axis=1)

    lane = jnp.arange(LANES)
    col = jnp.arange(MLA_HEADS * LANES)
    rope_lane = (lane >= QK_NOPE) & (lane < QK_NOPE + QK_ROPE)
    eplace = ((col[None, :] % LANES == lane[:, None]) & rope_lane[:, None]).astype(BF16)
    mix_col = jnp.arange(D)
    gsum = (mix_col[:, None] // GROUP_DIM == lane[None, :]).astype(BF16)
    gexp = (lane[:, None] == mix_col[None, :] // GROUP_DIM).astype(BF16)

    h = x.reshape(T, D)
    for l in range(depth):
        o1 = Q_RANK
        o2 = o1 + KV_RANK
        o3 = o2 + QK_ROPE
        o4 = o3 + CONV_DIM
        o5 = o4 + CONV_DIM
        wl = w_in[l]
        w_kr = wl[:, o2:o3]
        w_krr = jnp.concatenate([w_kr[:, half:], w_kr[:, :half]], axis=1)
        pad_lo = jnp.zeros((D, QK_NOPE), wl.dtype)
        pad_hi = jnp.zeros((D, LANES - QK_NOPE - QK_ROPE), wl.dtype)
        w1 = jnp.concatenate([wl[:, :o1], wl[:, o1:o2], pad_lo, w_kr, pad_hi, pad_lo, w_krr, pad_hi,
                              wl[:, o3:o4], wl[:, o4:o5], wl[:, o5:]], axis=1).astype(BF16)
        qw = QK_NOPE + QK_ROPE
        wq = _head_blocks(w_uq[l], MLA_HEADS, qw, [(0, qw, 0)]).astype(BF16)
        wqr = _head_blocks(w_uq[l], MLA_HEADS, qw,
                           [(QK_NOPE + half, qw, QK_NOPE), (QK_NOPE, QK_NOPE + half, QK_NOPE + half)]).astype(BF16)
        kvw = QK_NOPE + V_HEAD
        wk = _head_blocks(w_ukv[l], MLA_HEADS, kvw, [(0, QK_NOPE, 0)]).astype(BF16)
        v_cols = w_ukv[l].reshape(KV_RANK, MLA_HEADS, kvw)[:, :, QK_NOPE:]
        v_pad = jnp.zeros_like(v_cols)
        odd_head = (jnp.arange(MLA_HEADS) % 2 == 1)[None, :, None]
        wv = jnp.where(odd_head, jnp.concatenate([v_pad, v_cols], axis=-1),
                       jnp.concatenate([v_cols, v_pad], axis=-1)).reshape(KV_RANK, MLA_HEADS * LANES).astype(BF16)
        vone = jnp.stack([(lane == _ones_lane(hd)).astype(F32) for hd in range(MLA_HEADS)]).reshape(1, -1)

        q, k, v, z, gb = _mixer_in(h, g_mix[l][None, :], w1, g_q[l][None, :], wq, wqr, g_kv[l][None, :], wk, wv,
                                   vone, eplace, cos_t, sin_t)
        o = _mla_attn(q.reshape(B, S, -1), k.reshape(B, S, -1), v.reshape(B, S, -1))
        h = _mixer_out(o.reshape(T, -1), z, gb, h, conv_w[l], g_out[l][None, :], gsum, gexp,
                       w_o[l].astype(BF16), S)

        kx, vx = _mem_kv(mem, g_mem[l][None, :], w_xkv[l].astype(BF16))
        h = _xattn(h.reshape(B, S, D), g_x[l][None, :], w_xq[l].astype(BF16), kx, vx,
                   w_xo[l].astype(BF16)).reshape(T, D)

        sk = sub_keys[l]
        zk = jnp.zeros_like(sk[:, 0])
        keys_bd = jnp.concatenate([jnp.concatenate([sk[:, 0], zk], axis=-1),
                                   jnp.concatenate([zk, sk[:, 1]], axis=-1)], axis=1)
        xn, r2, p, n, coef = _peer_route(h, g_ffn[l][None, :], w_pq[l].astype(BF16), keys_bd.astype(BF16))
        u_pack, vt_pack = _pack_experts(u_experts[l], v_experts[l])
        h = _peer_ffn(xn, u_pack, vt_pack, r2, p, n, coef, h,
                      g_final[None, :], final_norm=(l == depth - 1))
    return h.reshape(B, S, D)
```

```python
import functools
import math

import jax
import jax.numpy as jnp
from jax import lax
from jax.experimental import pallas as pl
from jax.experimental.pallas import tpu as pltpu

F32 = jnp.float32
BF16 = jnp.bfloat16

EPS = 1e-6
LANES = 128
SUBLANES = 8
VMEM_LIMIT = 56 * 1024 * 1024

MLA_HEADS = 8
QK_NOPE = 64
QK_ROPE = 32
V_HEAD = 64
Q_RANK = 384
KV_RANK = 256
CONV_DIM = 512
GROUP_DIM = 64
ROPE_THETA = 10000.0
X_HEADS = 4
PEER_HEADS = 8
PEER_KEYS = 128
PEER_TOPK = 16

TS_IN = 1024
TS_X = 1024
ROW_SPLIT = 4
ROW_SPLIT_X = 2
TQ = 512
TS_ROUTE = 1024
TT_FFN = 512
EB_FFN = 2048
SUB_FFN = 1024
GATE_UNIT_SPLIT = (8, 16, 40, 0)
NT_DIMS = (((1,), (1,)), ((), ()))


def _rms(x, g):
    return x * lax.rsqrt(jnp.mean(x * x, axis=-1, keepdims=True) + EPS) * g


def _split_bf16(x):
    hi = x.astype(BF16)
    lo = (x - hi.astype(F32)).astype(BF16)
    return hi, lo


def _gelu_tanh(x):
    c0 = math.sqrt(2.0 / math.pi)
    half_x = 0.5 * x
    return half_x + half_x * jnp.tanh(x * (c0 + (c0 * 0.044715) * (x * x)))


def _dot(a, b):
    return jnp.dot(a, b, preferred_element_type=F32)


def _dot_nt(a, b):
    return lax.dot_general(a, b, NT_DIMS, preferred_element_type=F32)


_C_CQ = 0
_C_CKV = _C_CQ + Q_RANK
_C_KR = _C_CKV + KV_RANK
_C_KRR = _C_KR + LANES
_C_GB = _C_KRR + LANES
_C_GC = _C_GB + CONV_DIM
_C_HX = _C_GC + CONV_DIM
_C_END = _C_HX + CONV_DIM


def _mixer_in_kernel(x_ref, gmix_ref, w1_ref, gq_ref, wq_ref, wqr_ref, gkv_ref, wk_ref, wv_ref, vone_ref,
                     eplace_ref, cos_ref, sin_ref,
                     q_out, k_out, v_out, z_out, gb_out):
    xn = _rms(x_ref[...], gmix_ref[...]).astype(BF16)
    proj = _dot(xn, w1_ref[...])
    cq = proj[:, _C_CQ:_C_CKV]
    ckv = proj[:, _C_CKV:_C_KR]
    kr = proj[:, _C_KR:_C_KRR]
    krr = proj[:, _C_KRR:_C_GB]
    gb_out[...] = proj[:, _C_GB:_C_GC]
    z_out[...] = proj[:, _C_GC:_C_HX] * proj[:, _C_HX:_C_END]

    cqn = _rms(cq, gq_ref[...]).astype(BF16)
    q_raw = _dot(cqn, wq_ref[...])
    q_rot = _dot(cqn, wqr_ref[...])
    cos_t = cos_ref[...]
    sin_t = sin_ref[...]
    q_scale = math.log2(math.e) / math.sqrt(QK_NOPE + QK_ROPE)
    for h in range(MLA_HEADS):
        sl = slice(h * LANES, (h + 1) * LANES)
        q_out[:, sl] = ((q_raw[:, sl] * cos_t + q_rot[:, sl] * sin_t) * q_scale).astype(BF16)

    ckvn = _rms(ckv, gkv_ref[...]).astype(BF16)
    kr_roped = (kr * cos_t + krr * sin_t).astype(BF16)
    k_out[...] = (_dot(ckvn, wk_ref[...]) + _dot(kr_roped, eplace_ref[...])).astype(BF16)
    v_out[...] = (_dot(ckvn, wv_ref[...]) + vone_ref[...]).astype(BF16)


def _mixer_in(x2, gmix, w1, gq, wq, wqr, gkv, wk, wv, vone, eplace, cos_t, sin_t):
    T, D = x2.shape
    ts = min(TS_IN, T)
    row = lambda i: (i, 0)
    fixed = lambda i: (0, 0)
    full = lambda a: pl.BlockSpec(a.shape, fixed)
    return pl.pallas_call(
        _mixer_in_kernel,
        grid=(T // ts,),
        in_specs=[pl.BlockSpec((ts, D), row), full(gmix), full(w1), full(gq), full(wq), full(wqr),
                  full(gkv), full(wk), full(wv), full(vone), full(eplace),
                  pl.BlockSpec((ts, LANES), row), pl.BlockSpec((ts, LANES), row)],
        out_specs=[pl.BlockSpec((ts, MLA_HEADS * LANES), row), pl.BlockSpec((ts, MLA_HEADS * LANES), row),
                   pl.BlockSpec((ts, MLA_HEADS * LANES), row), pl.BlockSpec((ts, CONV_DIM), row),
                   pl.BlockSpec((ts, CONV_DIM), row)],
        out_shape=[jax.ShapeDtypeStruct((T, MLA_HEADS * LANES), BF16),
                   jax.ShapeDtypeStruct((T, MLA_HEADS * LANES), BF16),
                   jax.ShapeDtypeStruct((T, MLA_HEADS * LANES), BF16),
                   jax.ShapeDtypeStruct((T, CONV_DIM), F32),
                   jax.ShapeDtypeStruct((T, CONV_DIM), F32)],
        compiler_params=pltpu.CompilerParams(dimension_semantics=("arbitrary",), vmem_limit_bytes=VMEM_LIMIT),
        name="mixer_in",
    )(x2, gmix, w1, gq, wq, wqr, gkv, wk, wv, vone, eplace, cos_t, sin_t)


def _ones_lane(head):
    return V_HEAD if head % 2 == 0 else 0


def _mla_attn_kernel(q_ref, k_ref, v_ref, o_ref, *, tq):
    seq = q_ref.shape[0]
    causal = (lax.broadcasted_iota(jnp.int32, (tq, tq), 1) <= lax.broadcasted_iota(jnp.int32, (tq, tq), 0))
    lane = lax.broadcasted_iota(jnp.int32, (tq, LANES), 1)
    n_q = seq // tq
    head_lanes = [slice(hh * LANES, (hh + 1) * LANES) for hh in range(2)]

    def qk(qi):
        return [_dot_nt(q_ref[qi * tq:(qi + 1) * tq, hl], k_ref[0:(qi + 1) * tq, hl]) for hl in head_lanes]

    scores = qk(0)
    for qi in range(n_q):
        rows = slice(qi * tq, (qi + 1) * tq)
        keys = slice(0, (qi + 1) * tq)
        cur = scores
        if qi + 1 < n_q:
            scores = qk(qi + 1)
        outs = []
        for hh, hl in enumerate(head_lanes):
            s = cur[hh]
            s_diag = jnp.where(causal, s[:, qi * tq:], -jnp.inf)
            s = s_diag if qi == 0 else jnp.concatenate([s[:, :qi * tq], s_diag], axis=1)
            m = jnp.max(s, axis=-1, keepdims=True)
            acc = _dot(jnp.exp2(s - m).astype(BF16), v_ref[keys, hl])
            one = _ones_lane(hh)
            outs.append(acc * (1.0 / acc[:, one:one + 1]))
        o_ref[rows, :] = jnp.where(lane < V_HEAD, outs[0], outs[1])


def _mla_attn(q3, k3, v3):
    B, S, _ = q3.shape
    tq = min(TQ, S)
    pair = lambda b, g: (b, 0, g)
    return pl.pallas_call(
        functools.partial(_mla_attn_kernel, tq=tq),
        grid=(B, MLA_HEADS // 2),
        in_specs=[pl.BlockSpec((None, S, 2 * LANES), pair), pl.BlockSpec((None, S, 2 * LANES), pair),
                  pl.BlockSpec((None, S, 2 * LANES), pair)],
        out_specs=pl.BlockSpec((None, S, 2 * V_HEAD), pair),
        out_shape=jax.ShapeDtypeStruct((B, S, MLA_HEADS * V_HEAD), F32),
        compiler_params=pltpu.CompilerParams(dimension_semantics=("arbitrary", "arbitrary"),
                                             vmem_limit_bytes=VMEM_LIMIT),
        name="mla_attn",
    )(q3, k3, v3)


def _mixer_out_kernel(o_ref, z_ref, zh_ref, gb_ref, x_ref, cw_ref, gout_ref, gsum_ref, gexp_ref, wo_ref,
                      h_out, *, tiles_per_seq):
    i = pl.program_id(0)
    ts = z_ref.shape[0]
    z = z_ref[...]
    halo = jnp.where(i % tiles_per_seq == 0, 0.0, zh_ref[...])
    row = lax.broadcasted_iota(jnp.int32, z.shape, 0)
    z1 = jnp.where(row == 0, halo[7:8, :], pltpu.roll(z, 1, axis=0))
    z2 = jnp.where(row == 0, halo[6:7, :], jnp.where(row == 1, halo[7:8, :], pltpu.roll(z, 2, axis=0)))
    cw = cw_ref[...]
    y_conv = gb_ref[...] * (cw[0:1, :] * z2 + cw[1:2, :] * z1 + cw[2:3, :] * z)
    y_all = jnp.concatenate([o_ref[...], y_conv], axis=-1)
    for part in range(ROW_SPLIT):
        rows = slice(part * ts // ROW_SPLIT, (part + 1) * ts // ROW_SPLIT)
        y = y_all[rows]
        sq_hi, sq_lo = _split_bf16(y * y)
        gs = _dot(sq_hi, gsum_ref[...]) + _dot(sq_lo, gsum_ref[...])
        r = lax.rsqrt(gs * (1.0 / GROUP_DIM) + EPS)
        r_hi, r_lo = _split_bf16(r)
        r_full = _dot(r_hi, gexp_ref[...]) + _dot(r_lo, gexp_ref[...])
        yn = (y * r_full * gout_ref[...]).astype(BF16)
        h_out[rows, :] = x_ref[rows, :] + _dot(yn, wo_ref[...])


def _mixer_out(o2, z, gb, x2, conv_w, gout, gsum, gexp, wo, seq):
    T, D = x2.shape
    ts = min(TS_IN, seq)
    row = lambda i: (i, 0)
    fixed = lambda i: (0, 0)
    full = lambda a: pl.BlockSpec(a.shape, fixed)
    halo_blocks = ts // SUBLANES
    return pl.pallas_call(
        functools.partial(_mixer_out_kernel, tiles_per_seq=seq // ts),
        grid=(T // ts,),
        in_specs=[pl.BlockSpec((ts, MLA_HEADS * V_HEAD), row), pl.BlockSpec((ts, CONV_DIM), row),
                  pl.BlockSpec((SUBLANES, CONV_DIM), lambda i: (jnp.maximum(i * halo_blocks - 1, 0), 0)),
                  pl.BlockSpec((ts, CONV_DIM), row), pl.BlockSpec((ts, D), row),
                  full(conv_w), full(gout), full(gsum), full(gexp), full(wo)],
        out_specs=pl.BlockSpec((ts, D), row),
        out_shape=jax.ShapeDtypeStruct((T, D), F32),
        compiler_params=pltpu.CompilerParams(dimension_semantics=("arbitrary",), vmem_limit_bytes=VMEM_LIMIT),
        name="mixer_out",
    )(o2, z, z, gb, x2, conv_w, gout, gsum, gexp, wo)


def _mem_kv_kernel(mem_ref, g_ref, w_ref, k_out, v_out):
    d = mem_ref.shape[-1]
    mn = _rms(mem_ref[...], g_ref[...]).astype(BF16)
    kv = _dot(mn, w_ref[...])
    k_out[...] = kv[:, :d].astype(BF16)
    v_out[...] = kv[:, d:].astype(BF16)


def _mem_kv(mem, g, w):
    B, M, D = mem.shape
    return pl.pallas_call(
        _mem_kv_kernel,
        grid=(B,),
        in_specs=[pl.BlockSpec((None, M, D), lambda b: (b, 0, 0)), pl.BlockSpec(g.shape, lambda b: (0, 0)),
                  pl.BlockSpec(w.shape, lambda b: (0, 0))],
        out_specs=[pl.BlockSpec((None, M, D), lambda b: (b, 0, 0)), pl.BlockSpec((None, M, D), lambda b: (b, 0, 0))],
        out_shape=[jax.ShapeDtypeStruct((B, M, D), BF16), jax.ShapeDtypeStruct((B, M, D), BF16)],
        compiler_params=pltpu.CompilerParams(dimension_semantics=("arbitrary",), vmem_limit_bytes=VMEM_LIMIT),
        name="mem_kv",
    )(mem, g, w)


def _xattn_kernel(h_ref, g_ref, wq_ref, k_ref, v_ref, wo_ref, h_out):
    ts, d = h_ref.shape
    hd = d // X_HEADS
    head_cols = [slice(hh * hd, (hh + 1) * hd) for hh in range(X_HEADS)]
    groups = [slice(part * ts // ROW_SPLIT_X, (part + 1) * ts // ROW_SPLIT_X) for part in range(ROW_SPLIT_X)]

    def scores_of(rows):
        q = _dot(_rms(h_ref[rows, :], g_ref[...]).astype(BF16), wq_ref[...]).astype(BF16)
        return [_dot_nt(q[:, sl], k_ref[:, sl]) for sl in head_cols]

    scores = scores_of(groups[0])
    for part, rows in enumerate(groups):
        cur = scores
        if part + 1 < len(groups):
            scores = scores_of(groups[part + 1])
        outs = []
        for hh, sl in enumerate(head_cols):
            s = cur[hh] * (1.0 / math.sqrt(hd))
            m = jnp.max(s, axis=-1, keepdims=True)
            p = jnp.exp(s - m)
            p = p * (1.0 / jnp.sum(p, axis=-1, keepdims=True))
            outs.append(_dot(p.astype(BF16), v_ref[:, sl]))
        o = jnp.concatenate(outs, axis=-1).astype(BF16)
        h_out[rows, :] = h_ref[rows, :] + _dot(o, wo_ref[...])


def _xattn(h3, g, wq, kx, vx, wo):
    B, S, D = h3.shape
    M = kx.shape[1]
    ts = min(TS_X, S)
    fixed = lambda b, i: (0, 0)
    return pl.pallas_call(
        _xattn_kernel,
        grid=(B, S // ts),
        in_specs=[pl.BlockSpec((None, ts, D), lambda b, i: (b, i, 0)), pl.BlockSpec(g.shape, fixed),
                  pl.BlockSpec(wq.shape, fixed), pl.BlockSpec((None, M, D), lambda b, i: (b, 0, 0)),
                  pl.BlockSpec((None, M, D), lambda b, i: (b, 0, 0)), pl.BlockSpec(wo.shape, fixed)],
        out_specs=pl.BlockSpec((None, ts, D), lambda b, i: (b, i, 0)),
        out_shape=jax.ShapeDtypeStruct((B, S, D), F32),
        compiler_params=pltpu.CompilerParams(dimension_semantics=("arbitrary", "arbitrary"),
                                             vmem_limit_bytes=VMEM_LIMIT),
        name="xattn",
    )(h3, g, wq, kx, vx, wo)


def _batcher_pairs(n):
    pairs = []
    p = 1
    while p < n:
        k = p
        while k >= 1:
            for j in range(k % p, n - k, 2 * k):
                for i in range(min(k, n - j - k)):
                    if (i + j) // (2 * p) == (i + j + k) // (2 * p):
                        pairs.append((i + j, i + j + k))
            k //= 2
        p *= 2
    return pairs


_SORT16 = _batcher_pairs(PEER_TOPK)
_ROW_LEN = [PEER_TOPK // (a + 1) for a in range(PEER_TOPK)]


def _sort_desc(v):
    v = list(v)
    for i, j in _SORT16:
        hi = jnp.maximum(v[i], v[j])
        lo = jnp.minimum(v[i], v[j])
        v[i], v[j] = hi, lo
    return v


def _bitonic_desc(v):
    v = list(v)
    n = len(v)
    d = n // 2
    while d >= 1:
        for k in range(n):
            if k & d == 0:
                hi = jnp.maximum(v[k], v[k + d])
                lo = jnp.minimum(v[k], v[k + d])
                v[k], v[k + d] = hi, lo
        d //= 2
    return v


def _merge_top(cur, other):
    n = len(cur)
    c = list(cur)
    for r, val in enumerate(other):
        c[n - 1 - r] = jnp.maximum(c[n - 1 - r], val)
    return _bitonic_desc(c)


def _peer_route_kernel(h_ref, g_ref, wq_ref, key_ref,
                       xn_out, r2_out, p_out, n_out, c_out,
                       st_ref, top_ref, sort_ref, res_ref):
    ts = h_ref.shape[0]
    n_chunk = ts // LANES
    hn = _rms(h_ref[...], g_ref[...]).astype(BF16)
    xn_out[...] = hn
    q = _dot(hn, wq_ref[...])
    for h in range(PEER_HEADS):
        st = _dot_nt(key_ref[h], q[:, h * LANES:(h + 1) * LANES].astype(BF16))
        for c in range(n_chunk):
            st_ref[c, h] = st[:, c * LANES:(c + 1) * LANES]


    def chunk_body(c, _):
        for half in range(2):
            def sort_body(h, _, half=half):
                s = st_ref[c, h, pl.ds(half * PEER_KEYS, PEER_KEYS), :].reshape(PEER_TOPK, SUBLANES, LANES)
                v = _sort_desc([s[k] for k in range(PEER_TOPK)])
                row0 = pl.multiple_of(h * SUBLANES, SUBLANES)
                for k in range(PEER_TOPK):
                    sort_ref[k, pl.ds(row0, SUBLANES), :] = v[k]
                return 0

            lax.fori_loop(0, PEER_HEADS, sort_body, 0)
            lists = [[sort_ref[k, pl.ds(s, PEER_HEADS, stride=SUBLANES), :] for k in range(PEER_TOPK)]
                     for s in range(SUBLANES)]
            while len(lists) > 1:
                lists = [_merge_top(lists[i], lists[i + 1]) for i in range(0, len(lists), 2)]
            for a in range(PEER_TOPK):
                top_ref[half, a] = lists[0][a]

        v1 = [top_ref[0, a] for a in range(PEER_TOPK)]
        v2 = [top_ref[1, b] for b in range(PEER_TOPK)]
        sums = [[v1[a] + v2[b] for b in range(_ROW_LEN[a])] for a in range(PEER_TOPK)]
        cur = sums[0]
        a = 1
        while _ROW_LEN[a] > 1:
            cur = _merge_top(cur, sums[a])
            a += 1
        cur = _merge_top(cur, [sums[r][0] for r in range(a, PEER_TOPK)])
        tau = cur[PEER_TOPK - 1]
        top_sum = sums[0][0]
        z = jnp.zeros_like(tau)
        for a in range(PEER_TOPK):
            cnt = jnp.zeros_like(tau)
            for b in range(_ROW_LEN[a]):
                sel = sums[a][b] >= tau
                cnt = cnt + jnp.where(sel, 1.0, 0.0)
                z = z + jnp.where(sel, jnp.exp(sums[a][b] - top_sum), 0.0)
            res_ref[a] = cnt
        res_ref[PEER_TOPK] = 1.0 / z

        def expand_body(h, _):
            s1 = st_ref[c, h, pl.ds(0, PEER_KEYS), :].reshape(PEER_TOPK, SUBLANES, LANES)
            s2 = st_ref[c, h, pl.ds(PEER_KEYS, PEER_KEYS), :].reshape(PEER_TOPK, SUBLANES, LANES)
            n = jnp.zeros(s1.shape, F32)
            r2 = jnp.full(s2.shape, float(PEER_TOPK), F32)
            for a in range(PEER_TOPK - 1, -1, -1):
                v1a = top_ref[0, a, pl.ds(h, 1), :]
                v2a = top_ref[1, a, pl.ds(h, 1), :]
                n = jnp.where(s1 == v1a, res_ref[a, pl.ds(h, 1), :], n)
                r2 = jnp.where(s2 == v2a, float(a), r2)
            m1 = top_ref[0, 0, pl.ds(h, 1), :]
            m2 = top_ref[1, 0, pl.ds(h, 1), :]
            inv_z = res_ref[PEER_TOPK, pl.ds(h, 1), :]
            n_out[c, h] = n.reshape(PEER_KEYS, LANES)
            c_out[c, h] = (jnp.exp(s1 - m1) * inv_z).reshape(PEER_KEYS, LANES)
            r2_out[c, h] = r2.reshape(PEER_KEYS, LANES).astype(BF16)
            p_out[c, h] = jnp.exp(s2 - m2).reshape(PEER_KEYS, LANES).astype(BF16)
            return 0

        lax.fori_loop(0, PEER_HEADS, expand_body, 0)
        return 0

    lax.fori_loop(0, n_chunk, chunk_body, 0)


def _peer_route(h2, g, wq, keys):
    T, D = h2.shape
    ts = min(TS_ROUTE, T)
    nc = ts // LANES
    aux_spec = pl.BlockSpec((nc, PEER_HEADS, PEER_KEYS, LANES), lambda i: (i, 0, 0, 0))
    aux_shape = jax.ShapeDtypeStruct((T // LANES, PEER_HEADS, PEER_KEYS, LANES), F32)
    aux_shape_bf = jax.ShapeDtypeStruct((T // LANES, PEER_HEADS, PEER_KEYS, LANES), BF16)
    fixed2 = lambda i: (0, 0)
    fixed3 = lambda i: (0, 0, 0)
    return pl.pallas_call(
        _peer_route_kernel,
        grid=(T // ts,),
        in_specs=[pl.BlockSpec((ts, D), lambda i: (i, 0)), pl.BlockSpec(g.shape, fixed2),
                  pl.BlockSpec(wq.shape, fixed2), pl.BlockSpec(keys.shape, fixed3)],
        out_specs=[pl.BlockSpec((ts, D), lambda i: (i, 0)), aux_spec, aux_spec, aux_spec, aux_spec],
        out_shape=[jax.ShapeDtypeStruct((T, D), BF16), aux_shape_bf, aux_shape_bf, aux_shape, aux_shape],
        scratch_shapes=[pltpu.VMEM((nc, PEER_HEADS, 2 * PEER_KEYS, LANES), F32),
                        pltpu.VMEM((2, PEER_TOPK, SUBLANES, LANES), F32),
                        pltpu.VMEM((PEER_TOPK, PEER_HEADS * SUBLANES, LANES), F32),
                        pltpu.VMEM((PEER_TOPK + 1, SUBLANES, LANES), F32)],
        compiler_params=pltpu.CompilerParams(dimension_semantics=("arbitrary",), vmem_limit_bytes=VMEM_LIMIT),
        name="peer_route",
    )(h2, g, wq, keys)


def _pack_experts_kernel(u_ref, v_ref, u_out, vt_out):
    u_out[...] = pltpu.bitcast(u_ref[...].astype(BF16), jnp.uint32)
    vt_out[...] = pltpu.bitcast(v_ref[...].T.astype(BF16), jnp.uint32)


def _pack_experts(u, v):
    E, D = u.shape
    eb = EB_FFN
    return pl.pallas_call(
        _pack_experts_kernel,
        grid=(E // eb,),
        in_specs=[pl.BlockSpec((eb, D), lambda e: (e, 0)), pl.BlockSpec((eb, D), lambda e: (e, 0))],
        out_specs=[pl.BlockSpec((eb // 2, D), lambda e: (e, 0)), pl.BlockSpec((D // 2, eb), lambda e: (0, e))],
        out_shape=[jax.ShapeDtypeStruct((E // 2, D), jnp.uint32), jax.ShapeDtypeStruct((D // 2, E), jnp.uint32)],
        compiler_params=pltpu.CompilerParams(dimension_semantics=("arbitrary",), vmem_limit_bytes=VMEM_LIMIT),
        name="pack_experts",
    )(u, v)


def _peer_gate_unit(tc, ii, at_ref, ht_ref, r2s_ref, ps_ref, n_ref, c_ref):
    pack = 2 * SUBLANES
    n_jv = PEER_KEYS // pack
    zero = jnp.zeros((pack, LANES), BF16)
    lanes = slice(tc * LANES, (tc + 1) * LANES)
    g = [None] * n_jv
    for h in range(PEER_HEADS):
        n_b = jnp.broadcast_to(n_ref[tc, h, ii:ii + 1, :], (pack, LANES)).astype(BF16)
        c_b = jnp.broadcast_to(c_ref[tc, h, ii:ii + 1, :], (pack, LANES)).astype(BF16)
        for jv in range(n_jv):
            js = slice(jv * pack, (jv + 1) * pack)
            term = jnp.where(r2s_ref[tc, h, js, :] < n_b, ps_ref[tc, h, js, :], zero) * c_b
            g[jv] = term if g[jv] is None else g[jv] + term
    for jv in range(n_jv):
        rows = slice(ii * PEER_KEYS + jv * pack, ii * PEER_KEYS + (jv + 1) * pack)
        ht_ref[rows, lanes] = _gelu_tanh(at_ref[rows, lanes]).astype(BF16) * g[jv]


def _peer_ffn_kernel(xn_ref, u_ref, vt_ref, r2_ref, p_ref, n_ref, c_ref, h_ref, gfin_ref,
                     out_ref, acc_ref, at0_ref, at1_ref, ht0_ref, ht1_ref, r2s_ref, ps_ref, xs_ref,
                     *, n_e, n_blocks, final_norm):
    g = pl.program_id(0)
    tt = xn_ref.shape[0]
    e_score = g % n_e
    e_gate = jnp.maximum(g - 1, 0) % n_e
    e_down = jnp.maximum(g - 2, 0) % n_e

    @pl.when(g == 0)
    def _():
        at1_ref[...] = jnp.zeros_like(at1_ref)
        ht0_ref[...] = jnp.zeros_like(ht0_ref)
        ht1_ref[...] = jnp.zeros_like(ht1_ref)
        acc_ref[...] = jnp.zeros_like(acc_ref)

    @pl.when((g < n_blocks) & (e_score == 0))
    def _():
        xs_ref[...] = xn_ref[...]

    @pl.when((g <= n_blocks) & (e_gate == 0))
    def _():
        for tc in range(tt // LANES):
            for h in range(PEER_HEADS):
                r2s_ref[tc, h] = r2_ref[tc, h]
                ps_ref[tc, h] = p_ref[tc, h]

    @pl.when((g >= 2) & (e_down == 0))
    def _():
        acc_ref[...] = jnp.zeros_like(acc_ref)

    def stages(at_w, at_r, ht_w, ht_r):
        u_blk = pltpu.bitcast(u_ref[...], BF16)
        vt_blk = pltpu.bitcast(vt_ref[...], BF16)
        eb = u_blk.shape[0]
        subs = [slice(sb * SUB_FFN, (sb + 1) * SUB_FFN) for sb in range(eb // SUB_FFN)]

        def score(ex):
            at_w[ex, :] = _dot_nt(u_blk[ex], xs_ref[...])

        def down():
            acc_ref[...] += _dot(vt_blk, ht_r[...])

        chunks = [functools.partial(score, ex) for ex in subs] + [down]
        units = [(tc, ii) for tc in range(tt // LANES) for ii in range(eb // PEER_KEYS)]
        split = GATE_UNIT_SPLIT
        assert len(split) == len(chunks) + 1 and sum(split) == len(units)
        bounds = [sum(split[:k]) for k in range(len(split) + 1)]

        def gate_units(k):
            for tc, ii in units[bounds[k]:bounds[k + 1]]:
                _peer_gate_unit(tc, ii, at_r, ht_w, r2s_ref, ps_ref, n_ref, c_ref)

        gate_units(0)
        for k, chunk in enumerate(chunks):
            chunk()
            gate_units(k + 1)

    @pl.when(g % 2 == 0)
    def _():
        stages(at0_ref, at1_ref, ht1_ref, ht0_ref)

    @pl.when(g % 2 == 1)
    def _():
        stages(at1_ref, at0_ref, ht0_ref, ht1_ref)

    @pl.when((g >= 2) & (e_down == n_e - 1))
    def _():
        res = h_ref[...] + acc_ref[...].T
        out_ref[...] = _rms(res, gfin_ref[...]) if final_norm else res


def _peer_ffn(xn, u_pack, vt_pack, r2, p, n, coef, h2, gfin, final_norm):
    T, D = h2.shape
    E = vt_pack.shape[1]
    tt = min(TT_FFN, T)
    nc = tt // LANES
    eb = EB_FFN
    n_i = eb // PEER_KEYS
    n_e = E // eb
    n_blocks = (T // tt) * n_e

    def block(lag):
        def split(g):
            b = jnp.clip(g - lag, 0, n_blocks - 1)
            return b // n_e, b % n_e
        return split

    score, gate, down = block(0), block(1), block(2)
    aux_shape = (nc, PEER_HEADS, PEER_KEYS, LANES)
    row_shape = (nc, PEER_HEADS, n_i, LANES)
    return pl.pallas_call(
        functools.partial(_peer_ffn_kernel, n_e=n_e, n_blocks=n_blocks, final_norm=final_norm),
        grid=(n_blocks + 2,),
        in_specs=[pl.BlockSpec((tt, D), lambda g: (score(g)[0], 0)),
                  pl.BlockSpec((eb // 2, D), lambda g: (score(g)[1], 0)),
                  pl.BlockSpec((D // 2, eb), lambda g: (0, down(g)[1])),
                  pl.BlockSpec(aux_shape, lambda g: (gate(g)[0], 0, 0, 0)),
                  pl.BlockSpec(aux_shape, lambda g: (gate(g)[0], 0, 0, 0)),
                  pl.BlockSpec(row_shape, lambda g: (gate(g)[0], 0, gate(g)[1], 0)),
                  pl.BlockSpec(row_shape, lambda g: (gate(g)[0], 0, gate(g)[1], 0)),
                  pl.BlockSpec((tt, D), lambda g: (down(g)[0], 0)),
                  pl.BlockSpec(gfin.shape, lambda g: (0, 0))],
        out_specs=pl.BlockSpec((tt, D), lambda g: (down(g)[0], 0)),
        out_shape=jax.ShapeDtypeStruct((T, D), F32),
        scratch_shapes=[pltpu.VMEM((D, tt), F32),
                        pltpu.VMEM((eb, tt), F32), pltpu.VMEM((eb, tt), F32),
                        pltpu.VMEM((eb, tt), BF16), pltpu.VMEM((eb, tt), BF16),
                        pltpu.VMEM(aux_shape, BF16), pltpu.VMEM(aux_shape, BF16),
                        pltpu.VMEM((tt, D), BF16)],
        compiler_params=pltpu.CompilerParams(dimension_semantics=("arbitrary",), vmem_limit_bytes=VMEM_LIMIT),
        name="peer_ffn",
    )(xn, u_pack, vt_pack, r2, p, n, coef, h2, gfin)


def _head_blocks(w, n_heads, width, pieces):
    w3 = w.reshape(w.shape[0], n_heads, width)
    out = jnp.zeros((w.shape[0], n_heads, LANES), w.dtype)
    for s0, s1, d0 in pieces:
        out = out.at[:, :, d0:d0 + (s1 - s0)].set(w3[:, :, s0:s1])
    return out.reshape(w.shape[0], n_heads * LANES)


def kernel(x, mem, positions, g_mix, w_in, g_q, w_uq, g_kv, w_ukv, conv_w, g_out, w_o, g_x, g_mem, w_xq,
           w_xkv, w_xo, g_ffn, w_pq, sub_keys, u_experts, v_experts, g_final):
    B, S, D = x.shape
    T = B * S
    depth = g_mix.shape[0]
    half = QK_ROPE // 2
    assert S % min(TS_IN, S) == 0 and S % min(TS_X, S) == 0 and S % min(TQ, S) == 0 and D == MLA_HEADS * LANES
    assert T % min(TS_ROUTE, T) == 0 and T % min(TT_FFN, T) == 0 and u_experts.shape[1] % EB_FFN == 0

    inv = ROPE_THETA ** (-jnp.arange(0, QK_ROPE, 2, dtype=F32) / QK_ROPE)
    lane = jnp.arange(LANES)
    rope_lane = (lane >= QK_NOPE) & (lane < QK_NOPE + QK_ROPE)
    inv_lane = jnp.where(rope_lane, inv[(lane - QK_NOPE) % half], 0.0)
    sign_lane = jnp.where(rope_lane, jnp.where(lane < QK_NOPE + half, -1.0, 1.0), 0.0)
    ang = positions.astype(F32).reshape(T, 1) * inv_lane[None, :]
    cos_t = jnp.where(lane[None, :] < QK_NOPE + QK_ROPE, jnp.cos(ang).astype(x.dtype), 0.0)
    sin_t = jnp.sin(ang).astype(x.dtype) * sign_lane[None, :]

    col = jnp.arange(MLA_HEADS * LANES)
    eplace = ((col[None, :] % LANES == lane[:, None]) & rope_lane[:, None]).astype(BF16)
    mix_col = jnp.arange(D)
    gsum = (mix_col[:, None] // GROUP_DIM == lane[None, :]).astype(BF16)
    gexp = (lane[:, None] == mix_col[None, :] // GROUP_DIM).astype(BF16)

    h = x.reshape(T, D)
    for l in range(depth):
        o1 = Q_RANK
        o2 = o1 + KV_RANK
        o3 = o2 + QK_ROPE
        o4 = o3 + CONV_DIM
        o5 = o4 + CONV_DIM
        wl = w_in[l]
        w_kr = wl[:, o2:o3]
        w_krr = jnp.concatenate([w_kr[:, half:], w_kr[:, :half]], axis=1)
        pad_lo = jnp.zeros((D, QK_NOPE), wl.dtype)
        pad_hi = jnp.zeros((D, LANES - QK_NOPE - QK_ROPE), wl.dtype)
        w1 = jnp.concatenate([wl[:, :o1], wl[:, o1:o2], pad_lo, w_kr, pad_hi, pad_lo, w_krr, pad_hi,
                              wl[:, o3:o4], wl[:, o4:o5], wl[:, o5:]], axis=1).astype(BF16)
        qw = QK_NOPE + QK_ROPE
        wq = _head_blocks(w_uq[l], MLA_HEADS, qw, [(0, qw, 0)]).astype(BF16)
        wqr = _head_blocks(w_uq[l], MLA_HEADS, qw,
                           [(QK_NOPE + half, qw, QK_NOPE), (QK_NOPE, QK_NOPE + half, QK_NOPE + half)]).astype(BF16)
        kvw = QK_NOPE + V_HEAD
        wk = _head_blocks(w_ukv[l], MLA_HEADS, kvw, [(0, QK_NOPE, 0)]).astype(BF16)
        v_cols = w_ukv[l].reshape(KV_RANK, MLA_HEADS, kvw)[:, :, QK_NOPE:]
        v_pad = jnp.zeros_like(v_cols)
        odd_head = (jnp.arange(MLA_HEADS) % 2 == 1)[None, :, None]
        wv = jnp.where(odd_head, jnp.concatenate([v_pad, v_cols], axis=-1),
                       jnp.concatenate([v_cols, v_pad], axis=-1)).reshape(KV_RANK, MLA_HEADS * LANES).astype(BF16)
        vone = jnp.stack([(lane == _ones_lane(hd)).astype(F32) for hd in range(MLA_HEADS)]).reshape(1, -1)

        q, k, v, z, gb = _mixer_in(h, g_mix[l][None, :], w1, g_q[l][None, :], wq, wqr, g_kv[l][None, :], wk, wv,
                                   vone, eplace, cos_t, sin_t)
        o = _mla_attn(q.reshape(B, S, -1), k.reshape(B, S, -1), v.reshape(B, S, -1))
        h = _mixer_out(o.reshape(T, -1), z, gb, h, conv_w[l], g_out[l][None, :], gsum, gexp,
                       w_o[l].astype(BF16), S)

        kx, vx = _mem_kv(mem, g_mem[l][None, :], w_xkv[l].astype(BF16))
        h = _xattn(h.reshape(B, S, D), g_x[l][None, :], w_xq[l].astype(BF16), kx, vx,
                   w_xo[l].astype(BF16)).reshape(T, D)

        sk = sub_keys[l]
        zk = jnp.zeros_like(sk[:, 0])
        keys_bd = jnp.concatenate([jnp.concatenate([sk[:, 0], zk], axis=-1),
                                   jnp.concatenate([zk, sk[:, 1]], axis=-1)], axis=1)
        xn, r2, p, n, coef = _peer_route(h, g_ffn[l][None, :], w_pq[l].astype(BF16), keys_bd.astype(BF16))
        u_pack, vt_pack = _pack_experts(u_experts[l], v_experts[l])
        h = _peer_ffn(xn, u_pack, vt_pack, r2, p, n, coef, h,
                      g_final[None, :], final_norm=(l == depth - 1))
    return h.reshape(B, S, D)
```

```python
import functools
import math

import jax
import jax.numpy as jnp
from jax import lax
from jax.experimental import pallas as pl
from jax.experimental.pallas import tpu as pltpu

F32 = jnp.float32
BF16 = jnp.bfloat16

EPS = 1e-6
LANES = 128
SUBLANES = 8
VMEM_LIMIT = 56 * 1024 * 1024

MLA_HEADS = 8
QK_NOPE = 64
QK_ROPE = 32
V_HEAD = 64
Q_RANK = 384
KV_RANK = 256
CONV_DIM = 512
GROUP_DIM = 64
ROPE_THETA = 10000.0
X_HEADS = 4
PEER_HEADS = 8
PEER_KEYS = 128
PEER_TOPK = 16

TS_IN = 1024
TS_X = 1024
ROW_SPLIT = 4
ROW_SPLIT_X = 2
TQ = 512
TS_ROUTE = 1024
TT_FFN = 512
EB_FFN = 2048
SUB_FFN = 1024
GATE_UNIT_SPLIT = (8, 16, 40, 0)
NT_DIMS = (((1,), (1,)), ((), ()))


def _rms(x, g):
    return x * lax.rsqrt(jnp.mean(x * x, axis=-1, keepdims=True) + EPS) * g


def _split_bf16(x):
    hi = x.astype(BF16)
    lo = (x - hi.astype(F32)).astype(BF16)
    return hi, lo


def _gelu_tanh(x):
    c0 = math.sqrt(2.0 / math.pi)
    half_x = 0.5 * x
    return half_x + half_x * jnp.tanh(x * (c0 + (c0 * 0.044715) * (x * x)))


def _dot(a, b):
    return jnp.dot(a, b, preferred_element_type=F32)


def _dot_nt(a, b):
    return lax.dot_general(a, b, NT_DIMS, preferred_element_type=F32)


_C_CQ = 0
_C_CKV = _C_CQ + Q_RANK
_C_KR = _C_CKV + KV_RANK
_C_KRR = _C_KR + LANES
_C_GB = _C_KRR + LANES
_C_GC = _C_GB + CONV_DIM
_C_HX = _C_GC + CONV_DIM
_C_END = _C_HX + CONV_DIM


def _mixer_in_kernel(x_ref, gmix_ref, w1_ref, gq_ref, wq_ref, wqr_ref, gkv_ref, wk_ref, wv_ref, vone_ref,
                     eplace_ref, cos_ref, sin_ref,
                     q_out, k_out, v_out, z_out, gb_out):
    xn = _rms(x_ref[...], gmix_ref[...]).astype(BF16)
    proj = _dot(xn, w1_ref[...])
    cq = proj[:, _C_CQ:_C_CKV]
    ckv = proj[:, _C_CKV:_C_KR]
    kr = proj[:, _C_KR:_C_KRR]
    krr = proj[:, _C_KRR:_C_GB]
    gb_out[...] = proj[:, _C_GB:_C_GC]
    z_out[...] = proj[:, _C_GC:_C_HX] * proj[:, _C_HX:_C_END]

    cqn = _rms(cq, gq_ref[...]).astype(BF16)
    q_raw = _dot(cqn, wq_ref[...])
    q_rot = _dot(cqn, wqr_ref[...])
    cos_t = cos_ref[...]
    sin_t = sin_ref[...]
    q_scale = math.log2(math.e) / math.sqrt(QK_NOPE + QK_ROPE)
    for h in range(MLA_HEADS):
        sl = slice(h * LANES, (h + 1) * LANES)
        q_out[:, sl] = ((q_raw[:, sl] * cos_t + q_rot[:, sl] * sin_t) * q_scale).astype(BF16)

    ckvn = _rms(ckv, gkv_ref[...]).astype(BF16)
    kr_roped = (kr * cos_t + krr * sin_t).astype(BF16)
    k_out[...] = (_dot(ckvn, wk_ref[...]) + _dot(kr_roped, eplace_ref[...])).astype(BF16)
    v_out[...] = (_dot(ckvn, wv_ref[...]) + vone_ref[...]).astype(BF16)


def _mixer_in(x2, gmix, w1, gq, wq, wqr, gkv, wk, wv, vone, eplace, cos_t, sin_t):
    T, D = x2.shape
    ts = min(TS_IN, T)
    row = lambda i: (i, 0)
    fixed = lambda i: (0, 0)
    full = lambda a: pl.BlockSpec(a.shape, fixed)
    return pl.pallas_call(
        _mixer_in_kernel,
        grid=(T // ts,),
        in_specs=[pl.BlockSpec((ts, D), row), full(gmix), full(w1), full(gq), full(wq), full(wqr),
                  full(gkv), full(wk), full(wv), full(vone), full(eplace),
                  pl.BlockSpec((ts, LANES), row), pl.BlockSpec((ts, LANES), row)],
        out_specs=[pl.BlockSpec((ts, MLA_HEADS * LANES), row), pl.BlockSpec((ts, MLA_HEADS * LANES), row),
                   pl.BlockSpec((ts, MLA_HEADS * LANES), row), pl.BlockSpec((ts, CONV_DIM), row),
                   pl.BlockSpec((ts, CONV_DIM), row)],
        out_shape=[jax.ShapeDtypeStruct((T, MLA_HEADS * LANES), BF16),
                   jax.ShapeDtypeStruct((T, MLA_HEADS * LANES), BF16),
                   jax.ShapeDtypeStruct((T, MLA_HEADS * LANES), BF16),
                   jax.ShapeDtypeStruct((T, CONV_DIM), F32),
                   jax.ShapeDtypeStruct((T, CONV_DIM), F32)],
        compiler_params=pltpu.CompilerParams(dimension_semantics=("arbitrary",), vmem_limit_bytes=VMEM_LIMIT),
        name="mixer_in",
    )(x2, gmix, w1, gq, wq, wqr, gkv, wk, wv, vone, eplace, cos_t, sin_t)


def _ones_lane(head):
    return V_HEAD if head % 2 == 0 else 0


def _mla_attn_kernel(q_ref, k_ref, v_ref, o_ref, *, tq):
    seq = q_ref.shape[0]
    causal = (lax.broadcasted_iota(jnp.int32, (tq, tq), 1) <= lax.broadcasted_iota(jnp.int32, (tq, tq), 0))
    lane = lax.broadcasted_iota(jnp.int32, (tq, LANES), 1)
    n_q = seq // tq
    head_lanes = [slice(hh * LANES, (hh + 1) * LANES) for hh in range(2)]

    def qk(qi):
        return [_dot_nt(q_ref[qi * tq:(qi + 1) * tq, hl], k_ref[0:(qi + 1) * tq, hl]) for hl in head_lanes]

    scores = qk(0)
    for qi in range(n_q):
        rows = slice(qi * tq, (qi + 1) * tq)
        keys = slice(0, (qi + 1) * tq)
        cur = scores
        if qi + 1 < n_q:
            scores = qk(qi + 1)
        outs = []
        for hh, hl in enumerate(head_lanes):
            s = cur[hh]
            s_diag = jnp.where(causal, s[:, qi * tq:], -jnp.inf)
            s = s_diag if qi == 0 else jnp.concatenate([s[:, :qi * tq], s_diag], axis=1)
            m = jnp.max(s, axis=-1, keepdims=True)
            acc = _dot(jnp.exp2(s - m).astype(BF16), v_ref[keys, hl])
            one = _ones_lane(hh)
            outs.append(acc * (1.0 / acc[:, one:one + 1]))
        o_ref[rows, :] = jnp.where(lane < V_HEAD, outs[0], outs[1])


def _mla_attn(q3, k3, v3):
    B, S, _ = q3.shape
    tq = min(TQ, S)
    pair = lambda b, g: (b, 0, g)
    return pl.pallas_call(
        functools.partial(_mla_attn_kernel, tq=tq),
        grid=(B, MLA_HEADS // 2),
        in_specs=[pl.BlockSpec((None, S, 2 * LANES), pair), pl.BlockSpec((None, S, 2 * LANES), pair),
                  pl.BlockSpec((None, S, 2 * LANES), pair)],
        out_specs=pl.BlockSpec((None, S, 2 * V_HEAD), pair),
        out_shape=jax.ShapeDtypeStruct((B, S, MLA_HEADS * V_HEAD), F32),
        compiler_params=pltpu.CompilerParams(dimension_semantics=("arbitrary", "arbitrary"),
                                             vmem_limit_bytes=VMEM_LIMIT),
        name="mla_attn",
    )(q3, k3, v3)


def _mixer_out_kernel(o_ref, z_ref, zh_ref, gb_ref, x_ref, cw_ref, gout_ref, gsum_ref, gexp_ref, wo_ref,
                      h_out, *, tiles_per_seq):
    i = pl.program_id(0)
    ts = z_ref.shape[0]
    z = z_ref[...]
    halo = jnp.where(i % tiles_per_seq == 0, 0.0, zh_ref[...])
    row = lax.broadcasted_iota(jnp.int32, z.shape, 0)
    z1 = jnp.where(row == 0, halo[7:8, :], pltpu.roll(z, 1, axis=0))
    z2 = jnp.where(row == 0, halo[6:7, :], jnp.where(row == 1, halo[7:8, :], pltpu.roll(z, 2, axis=0)))
    cw = cw_ref[...]
    y_conv = gb_ref[...] * (cw[0:1, :] * z2 + cw[1:2, :] * z1 + cw[2:3, :] * z)
    y_all = jnp.concatenate([o_ref[...], y_conv], axis=-1)
    for part in range(ROW_SPLIT):
        rows = slice(part * ts // ROW_SPLIT, (part + 1) * ts // ROW_SPLIT)
        y = y_all[rows]
        sq_hi, sq_lo = _split_bf16(y * y)
        gs = _dot(sq_hi, gsum_ref[...]) + _dot(sq_lo, gsum_ref[...])
        r = lax.rsqrt(gs * (1.0 / GROUP_DIM) + EPS)
        r_hi, r_lo = _split_bf16(r)
        r_full = _dot(r_hi, gexp_ref[...]) + _dot(r_lo, gexp_ref[...])
        yn = (y * r_full * gout_ref[...]).astype(BF16)
        h_out[rows, :] = x_ref[rows, :] + _dot(yn, wo_ref[...])


def _mixer_out(o2, z, gb, x2, conv_w, gout, gsum, gexp, wo, seq):
    T, D = x2.shape
    ts = min(TS_IN, seq)
    row = lambda i: (i, 0)
    fixed = lambda i: (0, 0)
    full = lambda a: pl.BlockSpec(a.shape, fixed)
    halo_blocks = ts // SUBLANES
    return pl.pallas_call(
        functools.partial(_mixer_out_kernel, tiles_per_seq=seq // ts),
        grid=(T // ts,),
        in_specs=[pl.BlockSpec((ts, MLA_HEADS * V_HEAD), row), pl.BlockSpec((ts, CONV_DIM), row),
                  pl.BlockSpec((SUBLANES, CONV_DIM), lambda i: (jnp.maximum(i * halo_blocks - 1, 0), 0)),
                  pl.BlockSpec((ts, CONV_DIM), row), pl.BlockSpec((ts, D), row),
                  full(conv_w), full(gout), full(gsum), full(gexp), full(wo)],
        out_specs=pl.BlockSpec((ts, D), row),
        out_shape=jax.ShapeDtypeStruct((T, D), F32),
        compiler_params=pltpu.CompilerParams(dimension_semantics=("arbitrary",), vmem_limit_bytes=VMEM_LIMIT),
        name="mixer_out",
    )(o2, z, z, gb, x2, conv_w, gout, gsum, gexp, wo)


def _mem_kv_kernel(mem_ref, g_ref, w_ref, k_out, v_out):
    d = mem_ref.shape[-1]
    mn = _rms(mem_ref[...], g_ref[...]).astype(BF16)
    kv = _dot(mn, w_ref[...])
    k_out[...] = kv[:, :d].astype(BF16)
    v_out[...] = kv[:, d:].astype(BF16)


def _mem_kv(mem, g, w):
    B, M, D = mem.shape
    return pl.pallas_call(
        _mem_kv_kernel,
        grid=(B,),
        in_specs=[pl.BlockSpec((None, M, D), lambda b: (b, 0, 0)), pl.BlockSpec(g.shape, lambda b: (0, 0)),
                  pl.BlockSpec(w.shape, lambda b: (0, 0))],
        out_specs=[pl.BlockSpec((None, M, D), lambda b: (b, 0, 0)), pl.BlockSpec((None, M, D), lambda b: (b, 0, 0))],
        out_shape=[jax.ShapeDtypeStruct((B, M, D), BF16), jax.ShapeDtypeStruct((B, M, D), BF16)],
        compiler_params=pltpu.CompilerParams(dimension_semantics=("arbitrary",), vmem_limit_bytes=VMEM_LIMIT),
        name="mem_kv",
    )(mem, g, w)


def _xattn_kernel(h_ref, g_ref, wq_ref, k_ref, v_ref, wo_ref, h_out):
    ts, d = h_ref.shape
    hd = d // X_HEADS
    head_cols = [slice(hh * hd, (hh + 1) * hd) for hh in range(X_HEADS)]
    groups = [slice(part * ts // ROW_SPLIT_X, (part + 1) * ts // ROW_SPLIT_X) for part in range(ROW_SPLIT_X)]

    def scores_of(rows):
        q = _dot(_rms(h_ref[rows, :], g_ref[...]).astype(BF16), wq_ref[...]).astype(BF16)
        return [_dot_nt(q[:, sl], k_ref[:, sl]) for sl in head_cols]

    scores = scores_of(groups[0])
    for part, rows in enumerate(groups):
        cur = scores
        if part + 1 < len(groups):
            scores = scores_of(groups[part + 1])
        outs = []
        for hh, sl in enumerate(head_cols):
            s = cur[hh] * (1.0 / math.sqrt(hd))
            m = jnp.max(s, axis=-1, keepdims=True)
            p = jnp.exp(s - m)
            p = p * (1.0 / jnp.sum(p, axis=-1, keepdims=True))
            outs.append(_dot(p.astype(BF16), v_ref[:, sl]))
        o = jnp.concatenate(outs, axis=-1).astype(BF16)
        h_out[rows, :] = h_ref[rows, :] + _dot(o, wo_ref[...])


def _xattn(h3, g, wq, kx, vx, wo):
    B, S, D = h3.shape
    M = kx.shape[1]
    ts = min(TS_X, S)
    fixed = lambda b, i: (0, 0)
    return pl.pallas_call(
        _xattn_kernel,
        grid=(B, S // ts),
        in_specs=[pl.BlockSpec((None, ts, D), lambda b, i: (b, i, 0)), pl.BlockSpec(g.shape, fixed),
                  pl.BlockSpec(wq.shape, fixed), pl.BlockSpec((None, M, D), lambda b, i: (b, 0, 0)),
                  pl.BlockSpec((None, M, D), lambda b, i: (b, 0, 0)), pl.BlockSpec(wo.shape, fixed)],
        out_specs=pl.BlockSpec((None, ts, D), lambda b, i: (b, i, 0)),
        out_shape=jax.ShapeDtypeStruct((B, S, D), F32),
        compiler_params=pltpu.CompilerParams(dimension_semantics=("arbitrary", "arbitrary"),
                                             vmem_limit_bytes=VMEM_LIMIT),
        name="xattn",
    )(h3, g, wq, kx, vx, wo)


def _batcher_pairs(n):
    pairs = []
    p = 1
    while p < n:
        k = p
        while k >= 1:
            for j in range(k % p, n - k, 2 * k):
                for i in range(min(k, n - j - k)):
                    if (i + j) // (2 * p) == (i + j + k) // (2 * p):
                        pairs.append((i + j, i + j + k))
            k //= 2
        p *= 2
    return pairs


_SORT16 = _batcher_pairs(PEER_TOPK)
_ROW_LEN = [PEER_TOPK // (a + 1) for a in range(PEER_TOPK)]


def _sort_desc(v):
    v = list(v)
    for i, j in _SORT16:
        hi = jnp.maximum(v[i], v[j])
        lo = jnp.minimum(v[i], v[j])
        v[i], v[j] = hi, lo
    return v


def _bitonic_desc(v):
    v = list(v)
    n = len(v)
    d = n // 2
    while d >= 1:
        for k in range(n):
            if k & d == 0:
                hi = jnp.maximum(v[k], v[k + d])
                lo = jnp.minimum(v[k], v[k + d])
                v[k], v[k + d] = hi, lo
        d //= 2
    return v


def _merge_top(cur, other):
    n = len(cur)
    c = list(cur)
    for r, val in enumerate(other):
        c[n - 1 - r] = jnp.maximum(c[n - 1 - r], val)
    return _bitonic_desc(c)


def _peer_route_kernel(h_ref, g_ref, wq_ref, key_ref,
                       xn_out, r2_out, p_out, n_out, c_out,
                       st_ref, top_ref, sort_ref, res_ref):
    ts = h_ref.shape[0]
    n_chunk = ts // LANES
    hn = _rms(h_ref[...], g_ref[...]).astype(BF16)
    xn_out[...] = hn
    q = _dot(hn, wq_ref[...])
    for h in range(PEER_HEADS):
        st = _dot_nt(key_ref[h], q[:, h * LANES:(h + 1) * LANES].astype(BF16))
        for c in range(n_chunk):
            st_ref[c, h] = st[:, c * LANES:(c + 1) * LANES]


    def chunk_body(c, _):
        for half in range(2):
            def sort_body(h, _, half=half):
                s = st_ref[c, h, pl.ds(half * PEER_KEYS, PEER_KEYS), :].reshape(PEER_TOPK, SUBLANES, LANES)
                v = _sort_desc([s[k] for k in range(PEER_TOPK)])
                row0 = pl.multiple_of(h * SUBLANES, SUBLANES)
                for k in range(PEER_TOPK):
                    sort_ref[k, pl.ds(row0, SUBLANES), :] = v[k]
                return 0

            lax.fori_loop(0, PEER_HEADS, sort_body, 0)
            lists = [[sort_ref[k, pl.ds(s, PEER_HEADS, stride=SUBLANES), :] for k in range(PEER_TOPK)]
                     for s in range(SUBLANES)]
            while len(lists) > 1:
                lists = [_merge_top(lists[i], lists[i + 1]) for i in range(0, len(lists), 2)]
            for a in range(PEER_TOPK):
                top_ref[half, a] = lists[0][a]

        v1 = [top_ref[0, a] for a in range(PEER_TOPK)]
        v2 = [top_ref[1, b] for b in range(PEER_TOPK)]
        sums = [[v1[a] + v2[b] for b in range(_ROW_LEN[a])] for a in range(PEER_TOPK)]
        cur = sums[0]
        a = 1
        while _ROW_LEN[a] > 1:
            cur = _merge_top(cur, sums[a])
            a += 1
        cur = _merge_top(cur, [sums[r][0] for r in range(a, PEER_TOPK)])
        tau = cur[PEER_TOPK - 1]
        top_sum = sums[0][0]
        z = jnp.zeros_like(tau)
        for a in range(PEER_TOPK):
            cnt = jnp.zeros_like(tau)
            for b in range(_ROW_LEN[a]):
                sel = sums[a][b] >= tau
                cnt = cnt + jnp.where(sel, 1.0, 0.0)
                z = z + jnp.where(sel, jnp.exp(sums[a][b] - top_sum), 0.0)
            res_ref[a] = cnt
        res_ref[PEER_TOPK] = 1.0 / z

        def expand_body(h, _):
            s1 = st_ref[c, h, pl.ds(0, PEER_KEYS), :].reshape(PEER_TOPK, SUBLANES, LANES)
            s2 = st_ref[c, h, pl.ds(PEER_KEYS, PEER_KEYS), :].reshape(PEER_TOPK, SUBLANES, LANES)
            n = jnp.zeros(s1.shape, F32)
            r2 = jnp.full(s2.shape, float(PEER_TOPK), F32)
            for a in range(PEER_TOPK - 1, -1, -1):
                v1a = top_ref[0, a, pl.ds(h, 1), :]
                v2a = top_ref[1, a, pl.ds(h, 1), :]
                n = jnp.where(s1 == v1a, res_ref[a, pl.ds(h, 1), :], n)
                r2 = jnp.where(s2 == v2a, float(a), r2)
            m1 = top_ref[0, 0, pl.ds(h, 1), :]
            m2 = top_ref[1, 0, pl.ds(h, 1), :]
            inv_z = res_ref[PEER_TOPK, pl.ds(h, 1), :]
            n_out[c, h] = n.reshape(PEER_KEYS, LANES)
            c_out[c, h] = (jnp.exp(s1 - m1) * inv_z).reshape(PEER_KEYS, LANES)
            r2_out[c, h] = r2.reshape(PEER_KEYS, LANES).astype(BF16)
            p_out[c, h] = jnp.exp(s2 - m2).reshape(PEER_KEYS, LANES).astype(BF16)
            return 0

        lax.fori_loop(0, PEER_HEADS, expand_body, 0)
        return 0

    lax.fori_loop(0, n_chunk, chunk_body, 0)


def _peer_route(h2, g, wq, keys):
    T, D = h2.shape
    ts = min(TS_ROUTE, T)
    nc = ts // LANES
    aux_spec = pl.BlockSpec((nc, PEER_HEADS, PEER_KEYS, LANES), lambda i: (i, 0, 0, 0))
    aux_shape = jax.ShapeDtypeStruct((T // LANES, PEER_HEADS, PEER_KEYS, LANES), F32)
    aux_shape_bf = jax.ShapeDtypeStruct((T // LANES, PEER_HEADS, PEER_KEYS, LANES), BF16)
    fixed2 = lambda i: (0, 0)
    fixed3 = lambda i: (0, 0, 0)
    return pl.pallas_call(
        _peer_route_kernel,
        grid=(T // ts,),
        in_specs=[pl.BlockSpec((ts, D), lambda i: (i, 0)), pl.BlockSpec(g.shape, fixed2),
                  pl.BlockSpec(wq.shape, fixed2), pl.BlockSpec(keys.shape, fixed3)],
        out_specs=[pl.BlockSpec((ts, D), lambda i: (i, 0)), aux_spec, aux_spec, aux_spec, aux_spec],
        out_shape=[jax.ShapeDtypeStruct((T, D), BF16), aux_shape_bf, aux_shape_bf, aux_shape, aux_shape],
        scratch_shapes=[pltpu.VMEM((nc, PEER_HEADS, 2 * PEER_KEYS, LANES), F32),
                        pltpu.VMEM((2, PEER_TOPK, SUBLANES, LANES), F32),
                        pltpu.VMEM((PEER_TOPK, PEER_HEADS * SUBLANES, LANES), F32),
                        pltpu.VMEM((PEER_TOPK + 1, SUBLANES, LANES), F32)],
        compiler_params=pltpu.CompilerParams(dimension_semantics=("arbitrary",), vmem_limit_bytes=VMEM_LIMIT),
        name="peer_route",
    )(h2, g, wq, keys)


def _pack_experts_kernel(u_ref, v_ref, u_out, vt_out):
    u_out[...] = pltpu.bitcast(u_ref[...].astype(BF16), jnp.uint32)
    vt_out[...] = pltpu.bitcast(v_ref[...].T.astype(BF16), jnp.uint32)


def _pack_experts(u, v):
    E, D = u.shape
    eb = EB_FFN
    return pl.pallas_call(
        _pack_experts_kernel,
        grid=(E // eb,),
        in_specs=[pl.BlockSpec((eb, D), lambda e: (e, 0)), pl.BlockSpec((eb, D), lambda e: (e, 0))],
        out_specs=[pl.BlockSpec((eb // 2, D), lambda e: (e, 0)), pl.BlockSpec((D // 2, eb), lambda e: (0, e))],
        out_shape=[jax.ShapeDtypeStruct((E // 2, D), jnp.uint32), jax.ShapeDtypeStruct((D // 2, E), jnp.uint32)],
        compiler_params=pltpu.CompilerParams(dimension_semantics=("arbitrary",), vmem_limit_bytes=VMEM_LIMIT),
        name="pack_experts",
    )(u, v)


def _peer_gate_unit(tc, ii, at_ref, ht_ref, r2s_ref, ps_ref, n_ref, c_ref):
    pack = 2 * SUBLANES
    n_jv = PEER_KEYS // pack
    zero = jnp.zeros((pack, LANES), BF16)
    lanes = slice(tc * LANES, (tc + 1) * LANES)
    g = [None] * n_jv
    for h in range(PEER_HEADS):
        n_b = jnp.broadcast_to(n_ref[tc, h, ii:ii + 1, :], (pack, LANES)).astype(BF16)
        c_b = jnp.broadcast_to(c_ref[tc, h, ii:ii + 1, :], (pack, LANES)).astype(BF16)
        for jv in range(n_jv):
            js = slice(jv * pack, (jv + 1) * pack)
            term = jnp.where(r2s_ref[tc, h, js, :] < n_b, ps_ref[tc, h, js, :], zero) * c_b
            g[jv] = term if g[jv] is None else g[jv] + term
    for jv in range(n_jv):
        rows = slice(ii * PEER_KEYS + jv * pack, ii * PEER_KEYS + (jv + 1) * pack)
        ht_ref[rows, lanes] = _gelu_tanh(at_ref[rows, lanes]).astype(BF16) * g[jv]


def _peer_ffn_kernel(xn_ref, u_ref, vt_ref, r2_ref, p_ref, n_ref, c_ref, h_ref, gfin_ref,
                     out_ref, acc_ref, at0_ref, at1_ref, ht0_ref, ht1_ref, r2s_ref, ps_ref, xs_ref,
                     *, n_e, n_blocks, final_norm):
    g = pl.program_id(0)
    tt = xn_ref.shape[0]
    e_score = g % n_e
    e_gate = jnp.maximum(g - 1, 0) % n_e
    e_down = jnp.maximum(g - 2, 0) % n_e

    @pl.when(g == 0)
    def _():
        at1_ref[...] = jnp.zeros_like(at1_ref)
        ht0_ref[...] = jnp.zeros_like(ht0_ref)
        ht1_ref[...] = jnp.zeros_like(ht1_ref)
        acc_ref[...] = jnp.zeros_like(acc_ref)

    @pl.when((g < n_blocks) & (e_score == 0))
    def _():
        xs_ref[...] = xn_ref[...]

    @pl.when((g <= n_blocks) & (e_gate == 0))
    def _():
        for tc in range(tt // LANES):
            for h in range(PEER_HEADS):
                r2s_ref[tc, h] = r2_ref[tc, h]
                ps_ref[tc, h] = p_ref[tc, h]

    @pl.when((g >= 2) & (e_down == 0))
    def _():
        acc_ref[...] = jnp.zeros_like(acc_ref)

    def stages(at_w, at_r, ht_w, ht_r):
        u_blk = pltpu.bitcast(u_ref[...], BF16)
        vt_blk = pltpu.bitcast(vt_ref[...], BF16)
        eb = u_blk.shape[0]
        subs = [slice(sb * SUB_FFN, (sb + 1) * SUB_FFN) for sb in range(eb // SUB_FFN)]

        def score(ex):
            at_w[ex, :] = _dot_nt(u_blk[ex], xs_ref[...])

        def down():
            acc_ref[...] += _dot(vt_blk, ht_r[...])

        chunks = [functools.partial(score, ex) for ex in subs] + [down]
        units = [(tc, ii) for tc in range(tt // LANES) for ii in range(eb // PEER_KEYS)]
        split = GATE_UNIT_SPLIT
        assert len(split) == len(chunks) + 1 and sum(split) == len(units)
        bounds = [sum(split[:k]) for k in range(len(split) + 1)]

        def gate_units(k):
            for tc, ii in units[bounds[k]:bounds[k + 1]]:
                _peer_gate_unit(tc, ii, at_r, ht_w, r2s_ref, ps_ref, n_ref, c_ref)

        gate_units(0)
        for k, chunk in enumerate(chunks):
            chunk()
            gate_units(k + 1)

    @pl.when(g % 2 == 0)
    def _():
        stages(at0_ref, at1_ref, ht1_ref, ht0_ref)

    @pl.when(g % 2 == 1)
    def _():
        stages(at1_ref, at0_ref, ht0_ref, ht1_ref)

    @pl.when((g >= 2) & (e_down == n_e - 1))
    def _():
        res = h_ref[...] + acc_ref[...].T
        out_ref[...] = _rms(res, gfin_ref[...]) if final_norm else res


def _peer_ffn(xn, u_pack, vt_pack, r2, p, n, coef, h2, gfin, final_norm):
    T, D = h2.shape
    E = vt_pack.shape[1]
    tt = min(TT_FFN, T)
    nc = tt // LANES
    eb = EB_FFN
    n_i = eb // PEER_KEYS
    n_e = E // eb
    n_blocks = (T // tt) * n_e

    def block(lag):
        def split(g):
            b = jnp.clip(g - lag, 0, n_blocks - 1)
            return b // n_e, b % n_e
        return split

    score, gate, down = block(0), block(1), block(2)
    aux_shape = (nc, PEER_HEADS, PEER_KEYS, LANES)
    row_shape = (nc, PEER_HEADS, n_i, LANES)
    return pl.pallas_call(
        functools.partial(_peer_ffn_kernel, n_e=n_e, n_blocks=n_blocks, final_norm=final_norm),
        grid=(n_blocks + 2,),
        in_specs=[pl.BlockSpec((tt, D), lambda g: (score(g)[0], 0)),
                  pl.BlockSpec((eb // 2, D), lambda g: (score(g)[1], 0)),
                  pl.BlockSpec((D // 2, eb), lambda g: (0, down(g)[1])),
                  pl.BlockSpec(aux_shape, lambda g: (gate(g)[0], 0, 0, 0)),
                  pl.BlockSpec(aux_shape, lambda g: (gate(g)[0], 0, 0, 0)),
                  pl.BlockSpec(row_shape, lambda g: (gate(g)[0], 0, gate(g)[1], 0)),
                  pl.BlockSpec(row_shape, lambda g: (gate(g)[0], 0, gate(g)[1], 0)),
                  pl.BlockSpec((tt, D), lambda g: (down(g)[0], 0)),
                  pl.BlockSpec(gfin.shape, lambda g: (0, 0))],
        out_specs=pl.BlockSpec((tt, D), lambda g: (down(g)[0], 0)),
        out_shape=jax.ShapeDtypeStruct((T, D), F32),
        scratch_shapes=[pltpu.VMEM((D, tt), F32),
                        pltpu.VMEM((eb, tt), F32), pltpu.VMEM((eb, tt), F32),
                        pltpu.VMEM((eb, tt), BF16), pltpu.VMEM((eb, tt), BF16),
                        pltpu.VMEM(aux_shape, BF16), pltpu.VMEM(aux_shape, BF16),
                        pltpu.VMEM((tt, D), BF16)],
        compiler_params=pltpu.CompilerParams(dimension_semantics=("arbitrary",), vmem_limit_bytes=VMEM_LIMIT),
        name="peer_ffn",
    )(xn, u_pack, vt_pack, r2, p, n, coef, h2, gfin)


def _head_blocks(w, n_heads, width, pieces):
    w3 = w.reshape(w.shape[0], n_heads, width)
    out = jnp.zeros((w.shape[0], n_heads, LANES), w.dtype)
    for s0, s1, d0 in pieces:
        out = out.at[:, :, d0:d0 + (s1 - s0)].set(w3[:, :, s0:s1])
    return out.reshape(w.shape[0], n_heads * LANES)


def kernel(x, mem, positions, g_mix, w_in, g_q, w_uq, g_kv, w_ukv, conv_w, g_out, w_o, g_x, g_mem, w_xq,
           w_xkv, w_xo, g_ffn, w_pq, sub_keys, u_experts, v_experts, g_final):
    B, S, D = x.shape
    T = B * S
    depth = g_mix.shape[0]
    half = QK_ROPE // 2
    assert S % min(TS_IN, S) == 0 and S % min(TS_X, S) == 0 and S % min(TQ, S) == 0 and D == MLA_HEADS * LANES
    assert T % min(TS_ROUTE, T) == 0 and T % min(TT_FFN, T) == 0 and u_experts.shape[1] % EB_FFN == 0

    inv = ROPE_THETA ** (-jnp.arange(0, QK_ROPE, 2, dtype=F32) / QK_ROPE)
    lane = jnp.arange(LANES)
    rope_lane = (lane >= QK_NOPE) & (lane < QK_NOPE + QK_ROPE)
    inv_lane = jnp.concatenate([jnp.zeros((QK_NOPE,), F32), inv, inv, jnp.zeros((LANES - QK_NOPE - QK_ROPE,), F32)])
    sign_lane = jnp.where(rope_lane, jnp.where(lane < QK_NOPE + half, -1.0, 1.0), 0.0)
    ang = positions.astype(F32).reshape(T, 1) * inv_lane[None, :]
    cos_t = jnp.where(lane[None, :] < QK_NOPE + QK_ROPE, jnp.cos(ang).astype(x.dtype), 0.0)
    sin_t = jnp.sin(ang).astype(x.dtype) * sign_lane[None, :]

    col = jnp.arange(MLA_HEADS * LANES)
    eplace = ((col[None, :] % LANES == lane[:, None]) & rope_lane[:, None]).astype(BF16)
    mix_col = jnp.arange(D)
    gsum = (mix_col[:, None] // GROUP_DIM == lane[None, :]).astype(BF16)
    gexp = (lane[:, None] == mix_col[None, :] // GROUP_DIM).astype(BF16)

    h = x.reshape(T, D)
    for l in range(depth):
        o1 = Q_RANK
        o2 = o1 + KV_RANK
        o3 = o2 + QK_ROPE
        o4 = o3 + CONV_DIM
        o5 = o4 + CONV_DIM
        wl = w_in[l]
        w_kr = wl[:, o2:o3]
        w_krr = jnp.concatenate([w_kr[:, half:], w_kr[:, :half]], axis=1)
        pad_lo = jnp.zeros((D, QK_NOPE), wl.dtype)
        pad_hi = jnp.zeros((D, LANES - QK_NOPE - QK_ROPE), wl.dtype)
        w1 = jnp.concatenate([wl[:, :o1], wl[:, o1:o2], pad_lo, w_kr, pad_hi, pad_lo, w_krr, pad_hi,
                              wl[:, o3:o4], wl[:, o4:o5], wl[:, o5:]], axis=1).astype(BF16)
        qw = QK_NOPE + QK_ROPE
        wq = _head_blocks(w_uq[l], MLA_HEADS, qw, [(0, qw, 0)]).astype(BF16)
        wqr = _head_blocks(w_uq[l], MLA_HEADS, qw,
                           [(QK_NOPE + half, qw, QK_NOPE), (QK_NOPE, QK_NOPE + half, QK_NOPE + half)]).astype(BF16)
        kvw = QK_NOPE + V_HEAD
        wk = _head_blocks(w_ukv[l], MLA_HEADS, kvw, [(0, QK_NOPE, 0)]).astype(BF16)
        v_cols = w_ukv[l].reshape(KV_RANK, MLA_HEADS, kvw)[:, :, QK_NOPE:]
        v_pad = jnp.zeros_like(v_cols)
        odd_head = (jnp.arange(MLA_HEADS) % 2 == 1)[None, :, None]
        wv = jnp.where(odd_head, jnp.concatenate([v_pad, v_cols], axis=-1),
                       jnp.concatenate([v_cols, v_pad], axis=-1)).reshape(KV_RANK, MLA_HEADS * LANES).astype(BF16)
        vone = jnp.stack([(lane == _ones_lane(hd)).astype(F32) for hd in range(MLA_HEADS)]).reshape(1, -1)

        q, k, v, z, gb = _mixer_in(h, g_mix[l][None, :], w1, g_q[l][None, :], wq, wqr, g_kv[l][None, :], wk, wv,
                                   vone, eplace, cos_t, sin_t)
        o = _mla_attn(q.reshape(B, S, -1), k.reshape(B, S, -1), v.reshape(B, S, -1))
        h = _mixer_out(o.reshape(T, -1), z, gb, h, conv_w[l], g_out[l][None, :], gsum, gexp,
                       w_o[l].astype(BF16), S)

        kx, vx = _mem_kv(mem, g_mem[l][None, :], w_xkv[l].astype(BF16))
        h = _xattn(h.reshape(B, S, D), g_x[l][None, :], w_xq[l].astype(BF16), kx, vx,
                   w_xo[l].astype(BF16)).reshape(T, D)

        sk = sub_keys[l]
        zk = jnp.zeros_like(sk[:, 0])
        keys_bd = jnp.concatenate([jnp.concatenate([sk[:, 0], zk], axis=-1),
                                   jnp.concatenate([zk, sk[:, 1]], axis=-1)], axis=1)
        xn, r2, p, n, coef = _peer_route(h, g_ffn[l][None, :], w_pq[l].astype(BF16), keys_bd.astype(BF16))
        u_pack, vt_pack = _pack_experts(u_experts[l], v_experts[l])
        h = _peer_ffn(xn, u_pack, vt_pack, r2, p, n, coef, h,
                      g_final[None, :], final_norm=(l == depth - 1))
    return h.reshape(B, S, D)
```

```python
import functools
import math

import jax
import jax.numpy as jnp
from jax import lax
from jax.experimental import pallas as pl
from jax.experimental.pallas import tpu as pltpu

F32 = jnp.float32
BF16 = jnp.bfloat16

EPS = 1e-6
LANES = 128
SUBLANES = 8
VMEM_LIMIT = 56 * 1024 * 1024

MLA_HEADS = 8
QK_NOPE = 64
QK_ROPE = 32
V_HEAD = 64
Q_RANK = 384
KV_RANK = 256
CONV_DIM = 512
GROUP_DIM = 64
ROPE_THETA = 10000.0
X_HEADS = 4
PEER_HEADS = 8
PEER_KEYS = 128
PEER_TOPK = 16

TS_IN = 1024
TS_X = 1024
ROW_SPLIT = 4
ROW_SPLIT_X = 2
TQ = 512
TS_ROUTE = 1024
TT_FFN = 512
EB_FFN = 2048
SUB_FFN = 1024
GATE_UNIT_SPLIT = (8, 16, 40, 0)
NT_DIMS = (((1,), (1,)), ((), ()))


def _rms(x, g):
    return x * lax.rsqrt(jnp.mean(x * x, axis=-1, keepdims=True) + EPS) * g


def _split_bf16(x):
    hi = x.astype(BF16)
    lo = (x - hi.astype(F32)).astype(BF16)
    return hi, lo


def _gelu_tanh(x):
    c0 = math.sqrt(2.0 / math.pi)
    half_x = 0.5 * x
    return half_x + half_x * jnp.tanh(x * (c0 + (c0 * 0.044715) * (x * x)))


def _dot(a, b):
    return jnp.dot(a, b, preferred_element_type=F32)


def _dot_nt(a, b):
    return lax.dot_general(a, b, NT_DIMS, preferred_element_type=F32)


_C_CQ = 0
_C_CKV = _C_CQ + Q_RANK
_C_KR = _C_CKV + KV_RANK
_C_KRR = _C_KR + LANES
_C_GB = _C_KRR + LANES
_C_GC = _C_GB + CONV_DIM
_C_HX = _C_GC + CONV_DIM
_C_END = _C_HX + CONV_DIM


def _mixer_in_kernel(x_ref, gmix_ref, w1_ref, gq_ref, wq_ref, wqr_ref, gkv_ref, wk_ref, wv_ref, vone_ref,
                     eplace_ref, cos_ref, sin_ref,
                     q_out, k_out, v_out, z_out, gb_out):
    xn = _rms(x_ref[...], gmix_ref[...]).astype(BF16)
    proj = _dot(xn, w1_ref[...])
    cq = proj[:, _C_CQ:_C_CKV]
    ckv = proj[:, _C_CKV:_C_KR]
    kr = proj[:, _C_KR:_C_KRR]
    krr = proj[:, _C_KRR:_C_GB]
    gb_out[...] = proj[:, _C_GB:_C_GC]
    z_out[...] = proj[:, _C_GC:_C_HX] * proj[:, _C_HX:_C_END]

    cqn = _rms(cq, gq_ref[...]).astype(BF16)
    q_raw = _dot(cqn, wq_ref[...])
    q_rot = _dot(cqn, wqr_ref[...])
    cos_t = cos_ref[...]
    sin_t = sin_ref[...]
    q_scale = math.log2(math.e) / math.sqrt(QK_NOPE + QK_ROPE)
    for h in range(MLA_HEADS):
        sl = slice(h * LANES, (h + 1) * LANES)
        q_out[:, sl] = ((q_raw[:, sl] * cos_t + q_rot[:, sl] * sin_t) * q_scale).astype(BF16)

    ckvn = _rms(ckv, gkv_ref[...]).astype(BF16)
    kr_roped = (kr * cos_t + krr * sin_t).astype(BF16)
    k_out[...] = (_dot(ckvn, wk_ref[...]) + _dot(kr_roped, eplace_ref[...])).astype(BF16)
    v_out[...] = (_dot(ckvn, wv_ref[...]) + vone_ref[...]).astype(BF16)


def _mixer_in(x2, gmix, w1, gq, wq, wqr, gkv, wk, wv, vone, eplace, cos_t, sin_t):
    T, D = x2.shape
    ts = min(TS_IN, T)
    row = lambda i: (i, 0)
    fixed = lambda i: (0, 0)
    full = lambda a: pl.BlockSpec(a.shape, fixed)
    return pl.pallas_call(
        _mixer_in_kernel,
        grid=(T // ts,),
        in_specs=[pl.BlockSpec((ts, D), row), full(gmix), full(w1), full(gq), full(wq), full(wqr),
                  full(gkv), full(wk), full(wv), full(vone), full(eplace),
                  pl.BlockSpec((ts, LANES), row), pl.BlockSpec((ts, LANES), row)],
        out_specs=[pl.BlockSpec((ts, MLA_HEADS * LANES), row), pl.BlockSpec((ts, MLA_HEADS * LANES), row),
                   pl.BlockSpec((ts, MLA_HEADS * LANES), row), pl.BlockSpec((ts, CONV_DIM), row),
                   pl.BlockSpec((ts, CONV_DIM), row)],
        out_shape=[jax.ShapeDtypeStruct((T, MLA_HEADS * LANES), BF16),
                   jax.ShapeDtypeStruct((T, MLA_HEADS * LANES), BF16),
                   jax.ShapeDtypeStruct((T, MLA_HEADS * LANES), BF16),
                   jax.ShapeDtypeStruct((T, CONV_DIM), F32),
                   jax.ShapeDtypeStruct((T, CONV_DIM), F32)],
        compiler_params=pltpu.CompilerParams(dimension_semantics=("arbitrary",), vmem_limit_bytes=VMEM_LIMIT),
        name="mixer_in",
    )(x2, gmix, w1, gq, wq, wqr, gkv, wk, wv, vone, eplace, cos_t, sin_t)


def _ones_lane(head):
    return V_HEAD if head % 2 == 0 else 0


def _mla_attn_kernel(q_ref, k_ref, v_ref, o_ref, *, tq):
    seq = q_ref.shape[0]
    causal = (lax.broadcasted_iota(jnp.int32, (tq, tq), 1) <= lax.broadcasted_iota(jnp.int32, (tq, tq), 0))
    lane = lax.broadcasted_iota(jnp.int32, (tq, LANES), 1)
    n_q = seq // tq
    head_lanes = [slice(hh * LANES, (hh + 1) * LANES) for hh in range(2)]

    def qk(qi):
        return [_dot_nt(q_ref[qi * tq:(qi + 1) * tq, hl], k_ref[0:(qi + 1) * tq, hl]) for hl in head_lanes]

    scores = qk(0)
    for qi in range(n_q):
        rows = slice(qi * tq, (qi + 1) * tq)
        keys = slice(0, (qi + 1) * tq)
        cur = scores
        if qi + 1 < n_q:
            scores = qk(qi + 1)
        outs = []
        for hh, hl in enumerate(head_lanes):
            s = cur[hh]
            s_diag = jnp.where(causal, s[:, qi * tq:], -jnp.inf)
            s = s_diag if qi == 0 else jnp.concatenate([s[:, :qi * tq], s_diag], axis=1)
            m = jnp.max(s, axis=-1, keepdims=True)
            acc = _dot(jnp.exp2(s - m).astype(BF16), v_ref[keys, hl])
            one = _ones_lane(hh)
            outs.append(acc * (1.0 / acc[:, one:one + 1]))
        o_ref[rows, :] = jnp.where(lane < V_HEAD, outs[0], outs[1])


def _mla_attn(q3, k3, v3):
    B, S, _ = q3.shape
    tq = min(TQ, S)
    pair = lambda b, g: (b, 0, g)
    return pl.pallas_call(
        functools.partial(_mla_attn_kernel, tq=tq),
        grid=(B, MLA_HEADS // 2),
        in_specs=[pl.BlockSpec((None, S, 2 * LANES), pair), pl.BlockSpec((None, S, 2 * LANES), pair),
                  pl.BlockSpec((None, S, 2 * LANES), pair)],
        out_specs=pl.BlockSpec((None, S, 2 * V_HEAD), pair),
        out_shape=jax.ShapeDtypeStruct((B, S, MLA_HEADS * V_HEAD), F32),
        compiler_params=pltpu.CompilerParams(dimension_semantics=("arbitrary", "arbitrary"),
                                             vmem_limit_bytes=VMEM_LIMIT),
        name="mla_attn",
    )(q3, k3, v3)


def _mixer_out_kernel(o_ref, z_ref, zh_ref, gb_ref, x_ref, cw_ref, gout_ref, gsum_ref, gexp_ref, wo_ref,
                      h_out, *, tiles_per_seq):
    i = pl.program_id(0)
    ts = z_ref.shape[0]
    z = z_ref[...]
    halo = jnp.where(i % tiles_per_seq == 0, 0.0, zh_ref[...])
    row = lax.broadcasted_iota(jnp.int32, z.shape, 0)
    z1 = jnp.where(row == 0, halo[7:8, :], pltpu.roll(z, 1, axis=0))
    z2 = jnp.where(row == 0, halo[6:7, :], jnp.where(row == 1, halo[7:8, :], pltpu.roll(z, 2, axis=0)))
    cw = cw_ref[...]
    y_conv = gb_ref[...] * (cw[0:1, :] * z2 + cw[1:2, :] * z1 + cw[2:3, :] * z)
    y_all = jnp.concatenate([o_ref[...], y_conv], axis=-1)
    for part in range(ROW_SPLIT):
        rows = slice(part * ts // ROW_SPLIT, (part + 1) * ts // ROW_SPLIT)
        y = y_all[rows]
        sq_hi, sq_lo = _split_bf16(y * y)
        gs = _dot(sq_hi, gsum_ref[...]) + _dot(sq_lo, gsum_ref[...])
        r = lax.rsqrt(gs * (1.0 / GROUP_DIM) + EPS)
        r_hi, r_lo = _split_bf16(r)
        r_full = _dot(r_hi, gexp_ref[...]) + _dot(r_lo, gexp_ref[...])
        yn = (y * r_full * gout_ref[...]).astype(BF16)
        h_out[rows, :] = x_ref[rows, :] + _dot(yn, wo_ref[...])


def _mixer_out(o2, z, gb, x2, conv_w, gout, gsum, gexp, wo, seq):
    T, D = x2.shape
    ts = min(TS_IN, seq)
    row = lambda i: (i, 0)
    fixed = lambda i: (0, 0)
    full = lambda a: pl.BlockSpec(a.shape, fixed)
    halo_blocks = ts // SUBLANES
    return pl.pallas_call(
        functools.partial(_mixer_out_kernel, tiles_per_seq=seq // ts),
        grid=(T // ts,),
        in_specs=[pl.BlockSpec((ts, MLA_HEADS * V_HEAD), row), pl.BlockSpec((ts, CONV_DIM), row),
                  pl.BlockSpec((SUBLANES, CONV_DIM), lambda i: (jnp.maximum(i * halo_blocks - 1, 0), 0)),
                  pl.BlockSpec((ts, CONV_DIM), row), pl.BlockSpec((ts, D), row),
                  full(conv_w), full(gout), full(gsum), full(gexp), full(wo)],
        out_specs=pl.BlockSpec((ts, D), row),
        out_shape=jax.ShapeDtypeStruct((T, D), F32),
        compiler_params=pltpu.CompilerParams(dimension_semantics=("arbitrary",), vmem_limit_bytes=VMEM_LIMIT),
        name="mixer_out",
    )(o2, z, z, gb, x2, conv_w, gout, gsum, gexp, wo)


def _xattn_kernel(h_ref, g_ref, wq_ref, mem_ref, gmem_ref, wkv_ref, wo_ref, h_out, k_ref, v_ref):
    ts, d = h_ref.shape
    hd = d // X_HEADS

    @pl.when(pl.program_id(1) == 0)
    def _():
        kv = _dot(_rms(mem_ref[...], gmem_ref[...]).astype(BF16), wkv_ref[...])
        k_ref[...] = kv[:, :d].astype(BF16)
        v_ref[...] = kv[:, d:].astype(BF16)

    head_cols = [slice(hh * hd, (hh + 1) * hd) for hh in range(X_HEADS)]
    groups = [slice(part * ts // ROW_SPLIT_X, (part + 1) * ts // ROW_SPLIT_X) for part in range(ROW_SPLIT_X)]

    def scores_of(rows):
        q = _dot(_rms(h_ref[rows, :], g_ref[...]).astype(BF16), wq_ref[...]).astype(BF16)
        return [_dot_nt(q[:, sl], k_ref[:, sl]) for sl in head_cols]

    scores = scores_of(groups[0])
    for part, rows in enumerate(groups):
        cur = scores
        if part + 1 < len(groups):
            scores = scores_of(groups[part + 1])
        outs = []
        for hh, sl in enumerate(head_cols):
            s = cur[hh] * (1.0 / math.sqrt(hd))
            m = jnp.max(s, axis=-1, keepdims=True)
            p = jnp.exp(s - m)
            p = p * (1.0 / jnp.sum(p, axis=-1, keepdims=True))
            outs.append(_dot(p.astype(BF16), v_ref[:, sl]))
        o = jnp.concatenate(outs, axis=-1).astype(BF16)
        h_out[rows, :] = h_ref[rows, :] + _dot(o, wo_ref[...])


def _xattn(h3, g, wq, mem, gmem, wkv, wo):
    B, S, D = h3.shape
    M = mem.shape[1]
    ts = min(TS_X, S)
    fixed = lambda b, i: (0, 0)
    return pl.pallas_call(
        _xattn_kernel,
        grid=(B, S // ts),
        in_specs=[pl.BlockSpec((None, ts, D), lambda b, i: (b, i, 0)), pl.BlockSpec(g.shape, fixed),
                  pl.BlockSpec(wq.shape, fixed), pl.BlockSpec((None, M, D), lambda b, i: (b, 0, 0)),
                  pl.BlockSpec(gmem.shape, fixed), pl.BlockSpec(wkv.shape, fixed), pl.BlockSpec(wo.shape, fixed)],
        out_specs=pl.BlockSpec((None, ts, D), lambda b, i: (b, i, 0)),
        out_shape=jax.ShapeDtypeStruct((B, S, D), F32),
        scratch_shapes=[pltpu.VMEM((M, D), BF16), pltpu.VMEM((M, D), BF16)],
        compiler_params=pltpu.CompilerParams(dimension_semantics=("arbitrary", "arbitrary"),
                                             vmem_limit_bytes=VMEM_LIMIT),
        name="xattn",
    )(h3, g, wq, mem, gmem, wkv, wo)


def _batcher_pairs(n):
    pairs = []
    p = 1
    while p < n:
        k = p
        while k >= 1:
            for j in range(k % p, n - k, 2 * k):
                for i in range(min(k, n - j - k)):
                    if (i + j) // (2 * p) == (i + j + k) // (2 * p):
                        pairs.append((i + j, i + j + k))
            k //= 2
        p *= 2
    return pairs


_SORT16 = _batcher_pairs(PEER_TOPK)
_ROW_LEN = [PEER_TOPK // (a + 1) for a in range(PEER_TOPK)]


def _sort_desc(v):
    v = list(v)
    for i, j in _SORT16:
        hi = jnp.maximum(v[i], v[j])
        lo = jnp.minimum(v[i], v[j])
        v[i], v[j] = hi, lo
    return v


def _bitonic_desc(v):
    v = list(v)
    n = len(v)
    d = n // 2
    while d >= 1:
        for k in range(n):
            if k & d == 0:
                hi = jnp.maximum(v[k], v[k + d])
                lo = jnp.minimum(v[k], v[k + d])
                v[k], v[k + d] = hi, lo
        d //= 2
    return v


def _merge_top(cur, other):
    n = len(cur)
    c = list(cur)
    for r, val in enumerate(other):
        c[n - 1 - r] = jnp.maximum(c[n - 1 - r], val)
    return _bitonic_desc(c)


def _peer_route_kernel(h_ref, g_ref, wq_ref, key_ref,
                       xn_out, r2_out, p_out, n_out, c_out,
                       st_ref, top_ref, sort_ref, res_ref):
    ts = h_ref.shape[0]
    n_chunk = ts // LANES
    hn = _rms(h_ref[...], g_ref[...]).astype(BF16)
    xn_out[...] = hn
    q = _dot(hn, wq_ref[...])
    for h in range(PEER_HEADS):
        st = _dot_nt(key_ref[h], q[:, h * LANES:(h + 1) * LANES].astype(BF16))
        for c in range(n_chunk):
            st_ref[c, h] = st[:, c * LANES:(c + 1) * LANES]


    def chunk_body(c, _):
        for half in range(2):
            def sort_body(h, _, half=half):
                s = st_ref[c, h, pl.ds(half * PEER_KEYS, PEER_KEYS), :].reshape(PEER_TOPK, SUBLANES, LANES)
                v = _sort_desc([s[k] for k in range(PEER_TOPK)])
                row0 = pl.multiple_of(h * SUBLANES, SUBLANES)
                for k in range(PEER_TOPK):
                    sort_ref[k, pl.ds(row0, SUBLANES), :] = v[k]
                return 0

            lax.fori_loop(0, PEER_HEADS, sort_body, 0)
            lists = [[sort_ref[k, pl.ds(s, PEER_HEADS, stride=SUBLANES), :] for k in range(PEER_TOPK)]
                     for s in range(SUBLANES)]
            while len(lists) > 1:
                lists = [_merge_top(lists[i], lists[i + 1]) for i in range(0, len(lists), 2)]
            for a in range(PEER_TOPK):
                top_ref[half, a] = lists[0][a]

        v1 = [top_ref[0, a] for a in range(PEER_TOPK)]
        v2 = [top_ref[1, b] for b in range(PEER_TOPK)]
        sums = [[v1[a] + v2[b] for b in range(_ROW_LEN[a])] for a in range(PEER_TOPK)]
        cur = sums[0]
        a = 1
        while _ROW_LEN[a] > 1:
            cur = _merge_top(cur, sums[a])
            a += 1
        cur = _merge_top(cur, [sums[r][0] for r in range(a, PEER_TOPK)])
        tau = cur[PEER_TOPK - 1]
        top_sum = sums[0][0]
        z = jnp.zeros_like(tau)
        for a in range(PEER_TOPK):
            cnt = jnp.zeros_like(tau)
            for b in range(_ROW_LEN[a]):
                sel = sums[a][b] >= tau
                cnt = cnt + jnp.where(sel, 1.0, 0.0)
                z = z + jnp.where(sel, jnp.exp(sums[a][b] - top_sum), 0.0)
            res_ref[a] = cnt
        res_ref[PEER_TOPK] = 1.0 / z

        def expand_body(h, _):
            s1 = st_ref[c, h, pl.ds(0, PEER_KEYS), :].reshape(PEER_TOPK, SUBLANES, LANES)
            s2 = st_ref[c, h, pl.ds(PEER_KEYS, PEER_KEYS), :].reshape(PEER_TOPK, SUBLANES, LANES)
            n = jnp.zeros(s1.shape, F32)
            r2 = jnp.full(s2.shape, float(PEER_TOPK), F32)
            for a in range(PEER_TOPK - 1, -1, -1):
                v1a = top_ref[0, a, pl.ds(h, 1), :]
                v2a = top_ref[1, a, pl.ds(h, 1), :]
                n = jnp.where(s1 == v1a, res_ref[a, pl.ds(h, 1), :], n)
                r2 = jnp.where(s2 == v2a, float(a), r2)
            m1 = top_ref[0, 0, pl.ds(h, 1), :]
            m2 = top_ref[1, 0, pl.ds(h, 1), :]
            inv_z = res_ref[PEER_TOPK, pl.ds(h, 1), :]
            n_out[c, h] = n.reshape(PEER_KEYS, LANES)
            c_out[c, h] = (jnp.exp(s1 - m1) * inv_z).reshape(PEER_KEYS, LANES)
            r2_out[c, h] = r2.reshape(PEER_KEYS, LANES).astype(BF16)
            p_out[c, h] = jnp.exp(s2 - m2).reshape(PEER_KEYS, LANES).astype(BF16)
            return 0

        lax.fori_loop(0, PEER_HEADS, expand_body, 0)
        return 0

    lax.fori_loop(0, n_chunk, chunk_body, 0)


def _peer_route(h2, g, wq, keys):
    T, D = h2.shape
    ts = min(TS_ROUTE, T)
    nc = ts // LANES
    aux_spec = pl.BlockSpec((nc, PEER_HEADS, PEER_KEYS, LANES), lambda i: (i, 0, 0, 0))
    aux_shape = jax.ShapeDtypeStruct((T // LANES, PEER_HEADS, PEER_KEYS, LANES), F32)
    aux_shape_bf = jax.ShapeDtypeStruct((T // LANES, PEER_HEADS, PEER_KEYS, LANES), BF16)
    fixed2 = lambda i: (0, 0)
    fixed3 = lambda i: (0, 0, 0)
    return pl.pallas_call(
        _peer_route_kernel,
        grid=(T // ts,),
        in_specs=[pl.BlockSpec((ts, D), lambda i: (i, 0)), pl.BlockSpec(g.shape, fixed2),
                  pl.BlockSpec(wq.shape, fixed2), pl.BlockSpec(keys.shape, fixed3)],
        out_specs=[pl.BlockSpec((ts, D), lambda i: (i, 0)), aux_spec, aux_spec, aux_spec, aux_spec],
        out_shape=[jax.ShapeDtypeStruct((T, D), BF16), aux_shape_bf, aux_shape_bf, aux_shape, aux_shape],
        scratch_shapes=[pltpu.VMEM((nc, PEER_HEADS, 2 * PEER_KEYS, LANES), F32),
                        pltpu.VMEM((2, PEER_TOPK, SUBLANES, LANES), F32),
                        pltpu.VMEM((PEER_TOPK, PEER_HEADS * SUBLANES, LANES), F32),
                        pltpu.VMEM((PEER_TOPK + 1, SUBLANES, LANES), F32)],
        compiler_params=pltpu.CompilerParams(dimension_semantics=("arbitrary",), vmem_limit_bytes=VMEM_LIMIT),
        name="peer_route",
    )(h2, g, wq, keys)


def _pack_experts_kernel(u_ref, v_ref, u_out, vt_out):
    u_out[...] = pltpu.bitcast(u_ref[...].astype(BF16), jnp.uint32)
    vt_out[...] = pltpu.bitcast(v_ref[...].T.astype(BF16), jnp.uint32)


def _pack_experts(u, v):
    E, D = u.shape
    eb = EB_FFN
    return pl.pallas_call(
        _pack_experts_kernel,
        grid=(E // eb,),
        in_specs=[pl.BlockSpec((eb, D), lambda e: (e, 0)), pl.BlockSpec((eb, D), lambda e: (e, 0))],
        out_specs=[pl.BlockSpec((eb // 2, D), lambda e: (e, 0)), pl.BlockSpec((D // 2, eb), lambda e: (0, e))],
        out_shape=[jax.ShapeDtypeStruct((E // 2, D), jnp.uint32), jax.ShapeDtypeStruct((D // 2, E), jnp.uint32)],
        compiler_params=pltpu.CompilerParams(dimension_semantics=("arbitrary",), vmem_limit_bytes=VMEM_LIMIT),
        name="pack_experts",
    )(u, v)


def _peer_gate_unit(tc, ii, at_ref, ht_ref, r2s_ref, ps_ref, n_ref, c_ref):
    pack = 2 * SUBLANES
    n_jv = PEER_KEYS // pack
    zero = jnp.zeros((pack, LANES), BF16)
    lanes = slice(tc * LANES, (tc + 1) * LANES)
    g = [None] * n_jv
    for h in range(PEER_HEADS):
        n_b = jnp.broadcast_to(n_ref[tc, h, ii:ii + 1, :], (pack, LANES)).astype(BF16)
        c_b = jnp.broadcast_to(c_ref[tc, h, ii:ii + 1, :], (pack, LANES)).astype(BF16)
        for jv in range(n_jv):
            js = slice(jv * pack, (jv + 1) * pack)
            term = jnp.where(r2s_ref[tc, h, js, :] < n_b, ps_ref[tc, h, js, :], zero) * c_b
            g[jv] = term if g[jv] is None else g[jv] + term
    for jv in range(n_jv):
        rows = slice(ii * PEER_KEYS + jv * pack, ii * PEER_KEYS + (jv + 1) * pack)
        ht_ref[rows, lanes] = _gelu_tanh(at_ref[rows, lanes]).astype(BF16) * g[jv]


def _peer_ffn_kernel(xn_ref, u_ref, vt_ref, r2_ref, p_ref, n_ref, c_ref, h_ref, gfin_ref,
                     out_ref, acc_ref, at0_ref, at1_ref, ht0_ref, ht1_ref, r2s_ref, ps_ref, xs_ref,
                     *, n_e, n_blocks, final_norm):
    g = pl.program_id(0)
    tt = xn_ref.shape[0]
    e_score = g % n_e
    e_gate = jnp.maximum(g - 1, 0) % n_e
    e_down = jnp.maximum(g - 2, 0) % n_e

    @pl.when(g == 0)
    def _():
        at1_ref[...] = jnp.zeros_like(at1_ref)
        ht0_ref[...] = jnp.zeros_like(ht0_ref)
        ht1_ref[...] = jnp.zeros_like(ht1_ref)
        acc_ref[...] = jnp.zeros_like(acc_ref)

    @pl.when((g < n_blocks) & (e_score == 0))
    def _():
        xs_ref[...] = xn_ref[...]

    @pl.when((g <= n_blocks) & (e_gate == 0))
    def _():
        for tc in range(tt // LANES):
            for h in range(PEER_HEADS):
                r2s_ref[tc, h] = r2_ref[tc, h]
                ps_ref[tc, h] = p_ref[tc, h]

    @pl.when((g >= 2) & (e_down == 0))
    def _():
        acc_ref[...] = jnp.zeros_like(acc_ref)

    def stages(at_w, at_r, ht_w, ht_r):
        u_blk = pltpu.bitcast(u_ref[...], BF16)
        vt_blk = pltpu.bitcast(vt_ref[...], BF16)
        eb = u_blk.shape[0]
        subs = [slice(sb * SUB_FFN, (sb + 1) * SUB_FFN) for sb in range(eb // SUB_FFN)]

        def score(ex):
            at_w[ex, :] = _dot_nt(u_blk[ex], xs_ref[...])

        def down():
            acc_ref[...] += _dot(vt_blk, ht_r[...])

        chunks = [functools.partial(score, ex) for ex in subs] + [down]
        units = [(tc, ii) for tc in range(tt // LANES) for ii in range(eb // PEER_KEYS)]
        split = GATE_UNIT_SPLIT
        assert len(split) == len(chunks) + 1 and sum(split) == len(units)
        bounds = [sum(split[:k]) for k in range(len(split) + 1)]

        def gate_units(k):
            for tc, ii in units[bounds[k]:bounds[k + 1]]:
                _peer_gate_unit(tc, ii, at_r, ht_w, r2s_ref, ps_ref, n_ref, c_ref)

        gate_units(0)
        for k, chunk in enumerate(chunks):
            chunk()
            gate_units(k + 1)

    @pl.when(g % 2 == 0)
    def _():
        stages(at0_ref, at1_ref, ht1_ref, ht0_ref)

    @pl.when(g % 2 == 1)
    def _():
        stages(at1_ref, at0_ref, ht0_ref, ht1_ref)

    @pl.when((g >= 2) & (e_down == n_e - 1))
    def _():
        res = h_ref[...] + acc_ref[...].T
        out_ref[...] = _rms(res, gfin_ref[...]) if final_norm else res


def _peer_ffn(xn, u_pack, vt_pack, r2, p, n, coef, h2, gfin, final_norm):
    T, D = h2.shape
    E = vt_pack.shape[1]
    tt = min(TT_FFN, T)
    nc = tt // LANES
    eb = EB_FFN
    n_i = eb // PEER_KEYS
    n_e = E // eb
    n_blocks = (T // tt) * n_e

    def block(lag):
        def split(g):
            b = jnp.clip(g - lag, 0, n_blocks - 1)
            return b // n_e, b % n_e
        return split

    score, gate, down = block(0), block(1), block(2)
    aux_shape = (nc, PEER_HEADS, PEER_KEYS, LANES)
    row_shape = (nc, PEER_HEADS, n_i, LANES)
    return pl.pallas_call(
        functools.partial(_peer_ffn_kernel, n_e=n_e, n_blocks=n_blocks, final_norm=final_norm),
        grid=(n_blocks + 2,),
        in_specs=[pl.BlockSpec((tt, D), lambda g: (score(g)[0], 0)),
                  pl.BlockSpec((eb // 2, D), lambda g: (score(g)[1], 0)),
                  pl.BlockSpec((D // 2, eb), lambda g: (0, down(g)[1])),
                  pl.BlockSpec(aux_shape, lambda g: (gate(g)[0], 0, 0, 0)),
                  pl.BlockSpec(aux_shape, lambda g: (gate(g)[0], 0, 0, 0)),
                  pl.BlockSpec(row_shape, lambda g: (gate(g)[0], 0, gate(g)[1], 0)),
                  pl.BlockSpec(row_shape, lambda g: (gate(g)[0], 0, gate(g)[1], 0)),
                  pl.BlockSpec((tt, D), lambda g: (down(g)[0], 0)),
                  pl.BlockSpec(gfin.shape, lambda g: (0, 0))],
        out_specs=pl.BlockSpec((tt, D), lambda g: (down(g)[0], 0)),
        out_shape=jax.ShapeDtypeStruct((T, D), F32),
        scratch_shapes=[pltpu.VMEM((D, tt), F32),
                        pltpu.VMEM((eb, tt), F32), pltpu.VMEM((eb, tt), F32),
                        pltpu.VMEM((eb, tt), BF16), pltpu.VMEM((eb, tt), BF16),
                        pltpu.VMEM(aux_shape, BF16), pltpu.VMEM(aux_shape, BF16),
                        pltpu.VMEM((tt, D), BF16)],
        compiler_params=pltpu.CompilerParams(dimension_semantics=("arbitrary",), vmem_limit_bytes=VMEM_LIMIT),
        name="peer_ffn",
    )(xn, u_pack, vt_pack, r2, p, n, coef, h2, gfin)


def _head_blocks(w, n_heads, width, pieces):
    w3 = w.reshape(w.shape[0], n_heads, width)
    out = jnp.zeros((w.shape[0], n_heads, LANES), w.dtype)
    for s0, s1, d0 in pieces:
        out = out.at[:, :, d0:d0 + (s1 - s0)].set(w3[:, :, s0:s1])
    return out.reshape(w.shape[0], n_heads * LANES)


def kernel(x, mem, positions, g_mix, w_in, g_q, w_uq, g_kv, w_ukv, conv_w, g_out, w_o, g_x, g_mem, w_xq,
           w_xkv, w_xo, g_ffn, w_pq, sub_keys, u_experts, v_experts, g_final):
    B, S, D = x.shape
    T = B * S
    depth = g_mix.shape[0]
    half = QK_ROPE // 2
    assert S % min(TS_IN, S) == 0 and S % min(TS_X, S) == 0 and S % min(TQ, S) == 0 and D == MLA_HEADS * LANES
    assert T % min(TS_ROUTE, T) == 0 and T % min(TT_FFN, T) == 0 and u_experts.shape[1] % EB_FFN == 0

    inv = ROPE_THETA ** (-jnp.arange(0, QK_ROPE, 2, dtype=F32) / QK_ROPE)
    lane = jnp.arange(LANES)
    rope_lane = (lane >= QK_NOPE) & (lane < QK_NOPE + QK_ROPE)
    inv_lane = jnp.concatenate([jnp.zeros((QK_NOPE,), F32), inv, inv, jnp.zeros((LANES - QK_NOPE - QK_ROPE,), F32)])
    sign_lane = jnp.where(rope_lane, jnp.where(lane < QK_NOPE + half, -1.0, 1.0), 0.0)
    ang = positions.astype(F32).reshape(T, 1) * inv_lane[None, :]
    cos_t = jnp.where(lane[None, :] < QK_NOPE + QK_ROPE, jnp.cos(ang).astype(x.dtype), 0.0)
    sin_t = jnp.sin(ang).astype(x.dtype) * sign_lane[None, :]

    col = jnp.arange(MLA_HEADS * LANES)
    eplace = ((col[None, :] % LANES == lane[:, None]) & rope_lane[:, None]).astype(BF16)
    mix_col = jnp.arange(D)
    gsum = (mix_col[:, None] // GROUP_DIM == lane[None, :]).astype(BF16)
    gexp = (lane[:, None] == mix_col[None, :] // GROUP_DIM).astype(BF16)

    h = x.reshape(T, D)
    for l in range(depth):
        o1 = Q_RANK
        o2 = o1 + KV_RANK
        o3 = o2 + QK_ROPE
        o4 = o3 + CONV_DIM
        o5 = o4 + CONV_DIM
        wl = w_in[l]
        w_kr = wl[:, o2:o3]
        w_krr = jnp.concatenate([w_kr[:, half:], w_kr[:, :half]], axis=1)
        pad_lo = jnp.zeros((D, QK_NOPE), wl.dtype)
        pad_hi = jnp.zeros((D, LANES - QK_NOPE - QK_ROPE), wl.dtype)
        w1 = jnp.concatenate([wl[:, :o1], wl[:, o1:o2], pad_lo, w_kr, pad_hi, pad_lo, w_krr, pad_hi,
                              wl[:, o3:o4], wl[:, o4:o5], wl[:, o5:]], axis=1).astype(BF16)
        qw = QK_NOPE + QK_ROPE
        wq = _head_blocks(w_uq[l], MLA_HEADS, qw, [(0, qw, 0)]).astype(BF16)
        wqr = _head_blocks(w_uq[l], MLA_HEADS, qw,
                           [(QK_NOPE + half, qw, QK_NOPE), (QK_NOPE, QK_NOPE + half, QK_NOPE + half)]).astype(BF16)
        kvw = QK_NOPE + V_HEAD
        wk = _head_blocks(w_ukv[l], MLA_HEADS, kvw, [(0, QK_NOPE, 0)]).astype(BF16)
        v_cols = w_ukv[l].reshape(KV_RANK, MLA_HEADS, kvw)[:, :, QK_NOPE:]
        v_pad = jnp.zeros_like(v_cols)
        odd_head = (jnp.arange(MLA_HEADS) % 2 == 1)[None, :, None]
        wv = jnp.where(odd_head, jnp.concatenate([v_pad, v_cols], axis=-1),
                       jnp.concatenate([v_cols, v_pad], axis=-1)).reshape(KV_RANK, MLA_HEADS * LANES).astype(BF16)
        vone = jnp.stack([(lane == _ones_lane(hd)).astype(F32) for hd in range(MLA_HEADS)]).reshape(1, -1)

        q, k, v, z, gb = _mixer_in(h, g_mix[l][None, :], w1, g_q[l][None, :], wq, wqr, g_kv[l][None, :], wk, wv,
                                   vone, eplace, cos_t, sin_t)
        o = _mla_attn(q.reshape(B, S, -1), k.reshape(B, S, -1), v.reshape(B, S, -1))
        h = _mixer_out(o.reshape(T, -1), z, gb, h, conv_w[l], g_out[l][None, :], gsum, gexp,
                       w_o[l].astype(BF16), S)

        h = _xattn(h.reshape(B, S, D), g_x[l][None, :], w_xq[l].astype(BF16), mem, g_mem[l][None, :],
                   w_xkv[l].astype(BF16), w_xo[l].astype(BF16)).reshape(T, D)

        sk = sub_keys[l]
        zk = jnp.zeros_like(sk[:, 0])
        keys_bd = jnp.concatenate([jnp.concatenate([sk[:, 0], zk], axis=-1),
                                   jnp.concatenate([zk, sk[:, 1]], axis=-1)], axis=1)
        xn, r2, p, n, coef = _peer_route(h, g_ffn[l][None, :], w_pq[l].astype(BF16), keys_bd.astype(BF16))
        u_pack, vt_pack = _pack_experts(u_experts[l], v_experts[l])
        h = _peer_ffn(xn, u_pack, vt_pack, r2, p, n, coef, h,
                      g_final[None, :], final_norm=(l == depth - 1))
    return h.reshape(B, S, D)
```
